```python
import jax, jax.numpy as jnp
from jax import lax
import numpy as np


D_MODEL = 1024
BATCH = 2
SEQ = 8192
DEPTH = 2

CHUNK = 64
HEAD_DIM = 64
H_SB = 8
H_CA = 8
W_SB = H_SB * HEAD_DIM
W_CA = H_CA * HEAD_DIM
N_PAST_CHUNKS = 8
BAND = (N_PAST_CHUNKS + 1) * CHUNK
REL_CLIP = 128
Q_BLOCK = 128
N_GROUPS = 4
EXPERTS_PER_GROUP = 8
TOP_K_IN_GROUP = 2
D_EXPERT = 256
ALPHA = (2.0 * DEPTH) ** 0.25
BETA = (8.0 * DEPTH) ** -0.25
LN_EPS = 1e-5
SPLIT_SIZES = (W_SB, W_SB, W_SB, W_CA, W_CA, W_CA, D_MODEL, D_MODEL)
D_IN = sum(SPLIT_SIZES)

kernel_name = 'hybrid_stickbreak_chunkrel_hmoe_deepnorm'


def _layer_norm(x, g, b):
    xf = x.astype(jnp.float32)
    mu = jnp.mean(xf, axis=-1, keepdims=True)
    var = jnp.mean(jnp.square(xf - mu), axis=-1, keepdims=True)
    y = (xf - mu) * lax.rsqrt(var + LN_EPS)
    return (y * g.astype(jnp.float32) + b.astype(jnp.float32)).astype(x.dtype)


def _stick_breaking_attention(q, k, v):
    b, s, h, d = q.shape
    nb = s // Q_BLOCK
    scale = d ** -0.5
    q_blocks = q.reshape(b, nb, Q_BLOCK, h, d).transpose(1, 0, 3, 2, 4)
    k_t = k.transpose(0, 2, 1, 3)
    v_t = v.transpose(0, 2, 1, 3)
    key_pos = jnp.arange(s, dtype=jnp.int32)

    def one_block(args):
        q_blk, blk = args
        q_pos = blk * Q_BLOCK + jnp.arange(Q_BLOCK, dtype=jnp.int32)
        before = key_pos[None, :] < q_pos[:, None]
        z = jnp.einsum('bhqd,bhkd->bhqk', q_blk, k_t).astype(jnp.float32) * scale
        log_beta = jax.nn.log_sigmoid(z)
        log_one_minus = jnp.where(before, log_beta - z, 0.0)
        log_remaining = lax.cumsum(log_one_minus, axis=3, reverse=True) - log_one_minus
        w = jnp.where(before, jnp.exp(log_beta + log_remaining), 0.0)
        return jnp.einsum('bhqk,bhkd->bhqd', w.astype(v.dtype), v_t)

    out = lax.map(one_block, (q_blocks, jnp.arange(nb, dtype=jnp.int32)))
    return out.transpose(1, 0, 3, 2, 4).reshape(b, s, h * d)


def _chunked_relpos_attention(q, k, v, rel_bias):
    b, s, h, d = q.shape
    nc = s // CHUNK
    q_c = q.reshape(b, nc, CHUNK, h, d)

    def band(x):
        xc = x.reshape(b, nc, CHUNK, h, d)
        xp = jnp.pad(xc, ((0, 0), (N_PAST_CHUNKS, 0), (0, 0), (0, 0), (0, 0)))
        return jnp.concatenate([xp[:, i:i + nc] for i in range(N_PAST_CHUNKS + 1)], axis=2)

    k_b = band(k)
    v_b = band(v)
    scores = jnp.einsum('bcqhd,bckhd->bhcqk', q_c, k_b).astype(jnp.float32) * (d ** -0.5)
    qi = jnp.arange(CHUNK, dtype=jnp.int32)
    kp = jnp.arange(BAND, dtype=jnp.int32)
    rel = jnp.clip(qi[:, None] + N_PAST_CHUNKS * CHUNK - kp[None, :], -REL_CLIP, REL_CLIP) + REL_CLIP
    bias = rel_bias.astype(jnp.float32)[:, rel]
    valid = (jnp.arange(nc, dtype=jnp.int32)[:, None] - N_PAST_CHUNKS + kp[None, :] // CHUNK) >= 0
    scores = scores + bias[None, :, None]
    scores = jnp.where(valid[None, None, :, None, :], scores, -1e30)
    probs = jax.nn.softmax(scores, axis=-1)
    out = jnp.einsum('bhcqk,bckhd->bcqhd', probs.astype(v.dtype), v_b)
    return out.reshape(b, s, h * d)


def _hierarchical_moe(x, w_group, b_group, w_erouter, b_erouter, w_gate, w_up, w_down):
    n = x.shape[0]
    g_logits = jnp.einsum('nd,dg->ng', x, w_group).astype(jnp.float32) + b_group.astype(jnp.float32)
    g_probs = jax.nn.softmax(g_logits, axis=-1)
    g_val, g_idx = lax.top_k(g_probs, 1)
    g_sel = g_idx[:, 0]
    e_logits_all = (jnp.einsum('nd,gde->nge', x, w_erouter).astype(jnp.float32)
                    + b_erouter.astype(jnp.float32)[None])
    e_logits = jnp.take_along_axis(e_logits_all, g_sel[:, None, None], axis=1)[:, 0]
    top_val, top_idx = lax.top_k(e_logits, TOP_K_IN_GROUP)
    top_w = jax.nn.softmax(top_val, axis=-1)
    expert_w = jnp.sum(jax.nn.one_hot(top_idx, EXPERTS_PER_GROUP, dtype=jnp.float32) * top_w[..., None], axis=1)
    combine = (jax.nn.one_hot(g_sel, N_GROUPS, dtype=jnp.float32) * g_val)[:, :, None] * expert_w[:, None, :]
    y = jnp.zeros((n, x.shape[1]), dtype=x.dtype)
    for g in range(N_GROUPS):
        hid = jax.nn.silu(jnp.einsum('nd,edf->nef', x, w_gate[g])) * jnp.einsum('nd,edf->nef', x, w_up[g])
        hid = hid * combine[:, g, :, None].astype(hid.dtype)
        y = y + jnp.einsum('nef,efd->nd', hid, w_down[g])
    return y


def _layer(x, w_in, b_gate, rel_bias, w_br_sb, w_br_ca, w_out, ln1_g, ln1_b,
           w_group, b_group, w_erouter, b_erouter, w_gate, w_up, w_down, ln2_g, ln2_b):
    b, s, d = x.shape
    proj = jnp.einsum('bsd,de->bse', x, w_in)
    idx = [int(i) for i in np.cumsum(SPLIT_SIZES)[:-1]]
    q_sb, k_sb, v_sb, q_ca, k_ca, v_ca, gl_sb, gl_ca = jnp.split(proj, idx, axis=-1)
    heads = lambda t, h: t.reshape(b, s, h, HEAD_DIM)
    y_sb = _stick_breaking_attention(heads(q_sb, H_SB), heads(k_sb, H_SB), heads(v_sb, H_SB))
    y_ca = _chunked_relpos_attention(heads(q_ca, H_CA), heads(k_ca, H_CA), heads(v_ca, H_CA), rel_bias)
    y_sb = jnp.einsum('bsw,wd->bsd', y_sb, w_br_sb)
    y_ca = jnp.einsum('bsw,wd->bsd', y_ca, w_br_ca)
    g_sb = jax.nn.sigmoid(gl_sb + b_gate[0])
    g_ca = jax.nn.sigmoid(gl_ca + b_gate[1])
    mixed = jnp.einsum('bsd,de->bse', g_sb * y_sb + g_ca * y_ca, w_out)
    x = _layer_norm(ALPHA * x + mixed, ln1_g, ln1_b)
    ffn = _hierarchical_moe(x.reshape(b * s, d), w_group, b_group, w_erouter, b_erouter,
                            w_gate, w_up, w_down).reshape(b, s, d)
    return _layer_norm(ALPHA * x + ffn, ln2_g, ln2_b)


def setup_inputs(seed: int = 0) -> dict:
    key = jax.random.key(seed)
    ks = jax.random.split(key, 20)
    f32 = jnp.float32
    d = D_MODEL
    nrm = lambda k, shape, sc: jax.random.normal(k, shape, dtype=f32) * sc
    col_scale = jnp.concatenate([
        jnp.ones((2 * W_SB,), f32), jnp.full((W_SB,), BETA, f32),
        jnp.ones((2 * W_CA,), f32), jnp.full((W_CA,), BETA, f32),
        jnp.ones((2 * d,), f32)])
    x = nrm(ks[0], (BATCH, SEQ, d), 1.0)
    w_in = nrm(ks[1], (DEPTH, d, D_IN), d ** -0.5) * col_scale
    b_gate = nrm(ks[2], (DEPTH, 2, d), 0.1)
    rel_bias = nrm(ks[3], (DEPTH, H_CA, 2 * REL_CLIP + 1), 0.2)
    w_br_sb = nrm(ks[4], (DEPTH, W_SB, d), BETA * W_SB ** -0.5)
    w_br_ca = nrm(ks[5], (DEPTH, W_CA, d), BETA * W_CA ** -0.5)
    w_out = nrm(ks[6], (DEPTH, d, d), BETA * d ** -0.5)
    ln1_g = 1.0 + nrm(ks[7], (DEPTH, d), 0.01)
    ln1_b = nrm(ks[8], (DEPTH, d), 0.01)
    w_group = nrm(ks[9], (DEPTH, d, N_GROUPS), d ** -0.5)
    b_group = nrm(ks[10], (DEPTH, N_GROUPS), 0.01)
    w_erouter = nrm(ks[11], (DEPTH, N_GROUPS, d, EXPERTS_PER_GROUP), d ** -0.5)
    b_erouter = nrm(ks[12], (DEPTH, N_GROUPS, EXPERTS_PER_GROUP), 0.01)
    w_gate = nrm(ks[13], (DEPTH, N_GROUPS, EXPERTS_PER_GROUP, d, D_EXPERT), d ** -0.5)
    w_up = nrm(ks[14], (DEPTH, N_GROUPS, EXPERTS_PER_GROUP, d, D_EXPERT), BETA * d ** -0.5)
    w_down = nrm(ks[15], (DEPTH, N_GROUPS, EXPERTS_PER_GROUP, D_EXPERT, d), BETA * D_EXPERT ** -0.5)
    ln2_g = 1.0 + nrm(ks[16], (DEPTH, d), 0.01)
    ln2_b = nrm(ks[17], (DEPTH, d), 0.01)
    return {'x': x, 'w_in': w_in, 'b_gate': b_gate, 'rel_bias': rel_bias,
            'w_br_sb': w_br_sb, 'w_br_ca': w_br_ca, 'w_out': w_out,
            'ln1_g': ln1_g, 'ln1_b': ln1_b, 'w_group': w_group, 'b_group': b_group,
            'w_erouter': w_erouter, 'b_erouter': b_erouter,
            'w_gate': w_gate, 'w_up': w_up, 'w_down': w_down,
            'ln2_g': ln2_g, 'ln2_b': ln2_b}


def reference(x, w_in, b_gate, rel_bias, w_br_sb, w_br_ca, w_out, ln1_g, ln1_b,
              w_group, b_group, w_erouter, b_erouter, w_gate, w_up, w_down, ln2_g, ln2_b):
    for l in range(DEPTH):
        x = _layer(x, w_in[l], b_gate[l], rel_bias[l], w_br_sb[l], w_br_ca[l], w_out[l],
                   ln1_g[l], ln1_b[l], w_group[l], b_group[l], w_erouter[l], b_erouter[l],
                   w_gate[l], w_up[l], w_down[l], ln2_g[l], ln2_b[l])
    return x
```

```python
import functools

import jax
import jax.numpy as jnp
import numpy as np
from jax import lax
from jax.experimental import pallas as pl
from jax.experimental.pallas import tpu as pltpu

D_MODEL = 1024
DEPTH = 2
CHUNK = 64
HEAD_DIM = 64
H_SB = 8
H_CA = 8
W_SB = H_SB * HEAD_DIM
W_CA = H_CA * HEAD_DIM
N_PAST_CHUNKS = 8
REL_CLIP = 128
N_GROUPS = 4
EXPERTS_PER_GROUP = 8
N_EXPERTS = N_GROUPS * EXPERTS_PER_GROUP
D_EXPERT = 256
ALPHA = (2.0 * DEPTH) ** 0.25
LN_EPS = 1e-5
D_QKV = 3 * W_SB + 3 * W_CA
D_IN = D_QKV + 2 * D_MODEL
NEG_INF = -1e30

LANES = 128
HEADS_PER_TILE = LANES // HEAD_DIM
ROUTER_LANES = LANES
VMEM_LIMIT = 56 * 1024 * 1024

BF16 = jnp.bfloat16
F32 = jnp.float32

_NT = (((1,), (1,)), ((), ()))


def _dot(a, b):
    return jnp.dot(a, b, preferred_element_type=F32)


def _layer_norm(h, g, b):
    mu = jnp.mean(h, axis=-1, keepdims=True)
    hc = h - mu
    var = jnp.mean(hc * hc, axis=-1, keepdims=True)
    return hc * lax.rsqrt(var + LN_EPS) * g + b


def _split_bf16(a):
    hi = a.astype(BF16)
    lo = (a - hi.astype(F32)).astype(BF16)
    return hi, lo


def _in_proj_kernel(x_ref, w_ref, scale_ref, bg_ref, qkv_ref, gate_ref):
    xb = x_ref[...].astype(BF16)
    for c in range(D_QKV // D_MODEL):
        cols = slice(c * D_MODEL, (c + 1) * D_MODEL)
        acc = _dot(xb, w_ref[:, cols])
        qkv_ref[:, cols] = (acc * scale_ref[:, cols]).astype(BF16)
    for c in range(2):
        cols = slice(c * D_MODEL, (c + 1) * D_MODEL)
        wcols = slice(D_QKV + c * D_MODEL, D_QKV + (c + 1) * D_MODEL)
        logit = _dot(xb, w_ref[:, wcols]) + bg_ref[:, cols]
        gate_ref[:, cols] = 1.0 / (1.0 + jnp.exp(-logit))


def _in_proj(x2d, w_in_bf16, qscale, b_gate_row, tm=512):
    n = x2d.shape[0]
    return pl.pallas_call(
        _in_proj_kernel,
        grid=(n // tm,),
        in_specs=[
            pl.BlockSpec((tm, D_MODEL), lambda i: (i, 0)),
            pl.BlockSpec((D_MODEL, D_IN), lambda i: (0, 0)),
            pl.BlockSpec((1, D_QKV), lambda i: (0, 0)),
            pl.BlockSpec((1, 2 * D_MODEL), lambda i: (0, 0)),
        ],
        out_specs=[
            pl.BlockSpec((tm, D_QKV), lambda i: (i, 0)),
            pl.BlockSpec((tm, 2 * D_MODEL), lambda i: (i, 0)),
        ],
        out_shape=[
            jax.ShapeDtypeStruct((n, D_QKV), BF16),
            jax.ShapeDtypeStruct((n, 2 * D_MODEL), F32),
        ],
        compiler_params=pltpu.CompilerParams(
            dimension_semantics=("arbitrary",), vmem_limit_bytes=VMEM_LIMIT),
        name="in_proj",
    )(x2d, w_in_bf16, qscale, b_gate_row)


def _sb_kernel(q_ref, k_ref, v_ref, u_ref, o_ref, acc_ref, car_ref, *, tq):
    i = pl.program_id(2)
    q = q_ref[...]
    lane_q = lax.broadcasted_iota(jnp.int32, (tq, LANES), 1)
    zero_q = jnp.zeros_like(q)
    q_heads = (jnp.where(lane_q < HEAD_DIM, q, zero_q), jnp.where(lane_q < HEAD_DIM, zero_q, q))
    row = lax.broadcasted_iota(jnp.int32, (tq, tq), 0)
    col = lax.broadcasted_iota(jnp.int32, (tq, tq), 1)
    before = col < row

    acc_ref[...] = jnp.zeros_like(acc_ref)
    car_ref[...] = jnp.zeros_like(car_ref)

    def key_block(j, masked):
        start = pl.multiple_of(j * tq, tq)
        k = k_ref[pl.ds(start, tq), :]
        v = v_ref[pl.ds(start, tq), :]
        zero_v = jnp.zeros_like(v)
        v_heads = jnp.concatenate(
            [jnp.where(lane_q < HEAD_DIM, v, zero_v), jnp.where(lane_q < HEAD_DIM, zero_v, v)], axis=0)
        ws = []
        for h in range(HEADS_PER_TILE):
            z = lax.dot_general(q_heads[h], k, _NT, preferred_element_type=F32)
            sp = jnp.maximum(z, 0.0) + jnp.log(1.0 + jnp.exp(-jnp.abs(z)))
            if masked:
                sp = jnp.where(before, sp, 0.0)
            hi, lo = _split_bf16(sp)
            carry = car_ref[h]
            suffix = _dot(jnp.concatenate([hi, lo], axis=1), u_ref[...]) + carry
            w = jnp.exp(z - suffix)
            if masked:
                w = jnp.where(before, w, 0.0)
            ws.append(w.astype(BF16))
            car_ref[h] = carry + jnp.sum(sp, axis=1, keepdims=True)
        acc_ref[...] += _dot(jnp.concatenate(ws, axis=1), v_heads)

    key_block(i, True)

    def body(t, c):
        key_block(i - 1 - t, False)
        return c

    lax.fori_loop(0, i, body, 0)
    o_ref[...] = acc_ref[...].astype(o_ref.dtype)


def _sb_attn(qkv, tq=256):
    b, s, _ = qkv.shape
    n_tiles = W_SB // LANES
    tri = (np.arange(tq)[:, None] >= np.arange(tq)[None, :]).astype(np.float32)
    u = jnp.asarray(np.concatenate([tri, tri], axis=0), dtype=BF16)
    return pl.pallas_call(
        functools.partial(_sb_kernel, tq=tq),
        grid=(b, n_tiles, s // tq),
        in_specs=[
            pl.BlockSpec((None, tq, LANES), lambda bi, hp, i: (bi, i, hp)),
            pl.BlockSpec((None, s, LANES), lambda bi, hp, i: (bi, 0, n_tiles + hp)),
            pl.BlockSpec((None, s, LANES), lambda bi, hp, i: (bi, 0, 2 * n_tiles + hp)),
            pl.BlockSpec((2 * tq, tq), lambda bi, hp, i: (0, 0)),
        ],
        out_specs=pl.BlockSpec((None, tq, LANES), lambda bi, hp, i: (bi, i, hp)),
        out_shape=jax.ShapeDtypeStruct((b, s, W_SB), BF16),
        scratch_shapes=[
            pltpu.VMEM((tq, LANES), F32),
            pltpu.VMEM((HEADS_PER_TILE, tq, 1), F32),
        ],
        compiler_params=pltpu.CompilerParams(
            dimension_semantics=("arbitrary", "arbitrary", "arbitrary"), vmem_limit_bytes=VMEM_LIMIT),
        name="sb_attn",
    )(qkv, qkv, qkv, u)


CA_GROUP = 4
CA_TQ = CA_GROUP * CHUNK
CA_BAND = (CA_GROUP + N_PAST_CHUNKS) * CHUNK
CA_PAD = N_PAST_CHUNKS * CHUNK


def _ca_kernel(q_ref, k_ref, v_ref, bias_ref, o_ref, kp_ref, vp_ref):
    c = pl.program_id(2)
    s = k_ref.shape[0]

    @pl.when(c == 0)
    def _():
        kp_ref[0:CA_PAD, :] = jnp.zeros((CA_PAD, LANES), BF16)
        vp_ref[0:CA_PAD, :] = jnp.zeros((CA_PAD, LANES), BF16)
        kp_ref[CA_PAD:CA_PAD + s, :] = k_ref[...]
        vp_ref[CA_PAD:CA_PAD + s, :] = v_ref[...]

    start = pl.multiple_of(c * CA_TQ, CA_TQ)
    kb = kp_ref[pl.ds(start, CA_BAND), :]
    vb = vp_ref[pl.ds(start, CA_BAND), :]
    q = q_ref[...]
    lane_q = lax.broadcasted_iota(jnp.int32, (CA_TQ, LANES), 1)
    lane_v = lax.broadcasted_iota(jnp.int32, (CA_BAND, LANES), 1)
    zero_q = jnp.zeros_like(q)
    zero_v = jnp.zeros_like(vb)
    q_heads = (jnp.where(lane_q < HEAD_DIM, q, zero_q), jnp.where(lane_q < HEAD_DIM, zero_q, q))
    v_heads = jnp.concatenate(
        [jnp.where(lane_v < HEAD_DIM, vb, zero_v), jnp.where(lane_v < HEAD_DIM, zero_v, vb)], axis=0)
    pos = lax.broadcasted_iota(jnp.int32, (CA_TQ, CA_BAND), 1)
    exists = pos >= CA_PAD - c * CA_TQ
    es = []
    inv = []
    for h in range(HEADS_PER_TILE):
        sc = lax.dot_general(q_heads[h], kb, _NT, preferred_element_type=F32) + bias_ref[h]
        sc = jnp.where(exists, sc, NEG_INF)
        m = jnp.max(sc, axis=1, keepdims=True)
        e = jnp.exp(sc - m)
        inv.append(1.0 / jnp.sum(e, axis=1, keepdims=True))
        es.append(e.astype(BF16))
    out = _dot(jnp.concatenate(es, axis=1), v_heads)
    o_ref[...] = (out * jnp.where(lane_q < HEAD_DIM, inv[0], inv[1])).astype(o_ref.dtype)


def _ca_bias_table(rel_bias):
    r = np.arange(CA_TQ)[:, None]
    p = np.arange(CA_BAND)[None, :]
    qc = r // CHUNK
    kc = p // CHUNK
    in_band = (kc >= qc) & (kc <= qc + N_PAST_CHUNKS)
    rel = np.clip(r + CA_PAD - p, -REL_CLIP, REL_CLIP) + REL_CLIP
    table = rel_bias.astype(F32)[:, rel]
    return jnp.where(jnp.asarray(in_band)[None], table, NEG_INF)


def _ca_attn(qkv, bias_table):
    b, s, _ = qkv.shape
    n_tiles = W_CA // LANES
    base = 3 * W_SB // LANES
    return pl.pallas_call(
        _ca_kernel,
        grid=(b, n_tiles, s // CA_TQ),
        in_specs=[
            pl.BlockSpec((None, CA_TQ, LANES), lambda bi, hp, c: (bi, c, base + hp)),
            pl.BlockSpec((None, s, LANES), lambda bi, hp, c: (bi, 0, base + n_tiles + hp)),
            pl.BlockSpec((None, s, LANES), lambda bi, hp, c: (bi, 0, base + 2 * n_tiles + hp)),
            pl.BlockSpec((HEADS_PER_TILE, CA_TQ, CA_BAND), lambda bi, hp, c: (hp, 0, 0)),
        ],
        out_specs=pl.BlockSpec((None, CA_TQ, LANES), lambda bi, hp, c: (bi, c, hp)),
        out_shape=jax.ShapeDtypeStruct((b, s, W_CA), BF16),
        scratch_shapes=[
            pltpu.VMEM((CA_PAD + s, LANES), BF16),
            pltpu.VMEM((CA_PAD + s, LANES), BF16),
        ],
        compiler_params=pltpu.CompilerParams(
            dimension_semantics=("arbitrary", "arbitrary", "arbitrary"), vmem_limit_bytes=VMEM_LIMIT),
        name="ca_attn",
    )(qkv, qkv, qkv, bias_table)


def _route(lg):
    lane = lax.broadcasted_iota(jnp.int32, lg.shape, 1)
    big = jnp.int32(ROUTER_LANES)
    is_group = lane < N_GROUPS
    g_max = jnp.max(jnp.where(is_group, lg, -jnp.inf), axis=1, keepdims=True)
    g_idx = jnp.min(jnp.where(is_group & (lg == g_max), lane, big), axis=1, keepdims=True)
    g_den = jnp.sum(jnp.where(is_group, jnp.exp(lg - g_max), 0.0), axis=1, keepdims=True)
    g_val = 1.0 / g_den
    lo = N_GROUPS + EXPERTS_PER_GROUP * g_idx
    in_group = (lane >= lo) & (lane < lo + EXPERTS_PER_GROUP)
    v1 = jnp.max(jnp.where(in_group, lg, -jnp.inf), axis=1, keepdims=True)
    i1 = jnp.min(jnp.where(in_group & (lg == v1), lane, big), axis=1, keepdims=True)
    rest = in_group & (lane != i1)
    v2 = jnp.max(jnp.where(rest, lg, -jnp.inf), axis=1, keepdims=True)
    i2 = jnp.min(jnp.where(rest & (lg == v2), lane, big), axis=1, keepdims=True)
    e2 = jnp.exp(v2 - v1)
    w1 = g_val / (1.0 + e2)
    w2 = g_val * e2 / (1.0 + e2)
    return jnp.where(lane == i1, w1, 0.0) + jnp.where(lane == i2, w2, 0.0)


def _post_kernel(ysb_ref, yca_ref, gate_ref, x_ref, wsb_ref, wca_ref, wout_ref, wr_ref, br_ref,
                 g_ref, b_ref, x1_ref, cw_ref):
    a = _dot(ysb_ref[...], wsb_ref[...])
    c = _dot(yca_ref[...], wca_ref[...])
    mix = gate_ref[:, 0:D_MODEL] * a + gate_ref[:, D_MODEL:2 * D_MODEL] * c
    mixed = _dot(mix.astype(BF16), wout_ref[...])
    x1 = _layer_norm(ALPHA * x_ref[...] + mixed, g_ref[...], b_ref[...])
    x1_ref[...] = x1
    x_hi, x_lo = _split_bf16(x1)
    w_hi, w_lo = _split_bf16(wr_ref[...])
    lg = _dot(jnp.concatenate([x_hi, x_lo, x_hi], axis=1),
              jnp.concatenate([w_hi, w_hi, w_lo], axis=0)) + br_ref[...]
    cw_ref[...] = _route(lg)


def _post_attn(y_sb, y_ca, gates, x2d, w_br_sb, w_br_ca, w_out, w_router, b_router, ln_g, ln_b, tm=256):
    n = x2d.shape[0]
    row = lambda i: (i, 0)
    fixed = lambda i: (0, 0)
    return pl.pallas_call(
        _post_kernel,
        grid=(n // tm,),
        in_specs=[
            pl.BlockSpec((tm, W_SB), row),
            pl.BlockSpec((tm, W_CA), row),
            pl.BlockSpec((tm, 2 * D_MODEL), row),
            pl.BlockSpec((tm, D_MODEL), row),
            pl.BlockSpec((W_SB, D_MODEL), fixed),
            pl.BlockSpec((W_CA, D_MODEL), fixed),
            pl.BlockSpec((D_MODEL, D_MODEL), fixed),
            pl.BlockSpec((D_MODEL, ROUTER_LANES), fixed),
            pl.BlockSpec((1, ROUTER_LANES), fixed),
            pl.BlockSpec((1, D_MODEL), fixed),
            pl.BlockSpec((1, D_MODEL), fixed),
        ],
        out_specs=[
            pl.BlockSpec((tm, D_MODEL), row),
            pl.BlockSpec((tm, ROUTER_LANES), row),
        ],
        out_shape=[
            jax.ShapeDtypeStruct((n, D_MODEL), F32),
            jax.ShapeDtypeStruct((n, ROUTER_LANES), F32),
        ],
        compiler_params=pltpu.CompilerParams(
            dimension_semantics=("arbitrary",), vmem_limit_bytes=VMEM_LIMIT),
        name="post_attn",
    )(y_sb, y_ca, gates, x2d, w_br_sb, w_br_ca, w_out, w_router, b_router, ln_g, ln_b)


def _moe_kernel(x1_ref, cw_ref, wgu_ref, wd_ref, g_ref, b_ref, out_ref, xb_ref, acc_ref):
    e = pl.program_id(1)

    @pl.when(e == 0)
    def _():
        xb_ref[...] = x1_ref[...].astype(BF16)
        acc_ref[...] = jnp.zeros_like(acc_ref)

    gu = _dot(xb_ref[...], wgu_ref[...])
    gate = gu[:, 0:D_EXPERT]
    up = gu[:, D_EXPERT:2 * D_EXPERT]
    cw = cw_ref[...]
    lane = lax.broadcasted_iota(jnp.int32, cw.shape, 1)
    cwe = jnp.sum(jnp.where(lane == N_GROUPS + e, cw, 0.0), axis=1, keepdims=True)
    hid = (gate * (1.0 / (1.0 + jnp.exp(-gate)))) * up * cwe
    acc_ref[...] += _dot(hid.astype(BF16), wd_ref[...])

    @pl.when(e == N_EXPERTS - 1)
    def _():
        out_ref[...] = _layer_norm(ALPHA * x1_ref[...] + acc_ref[...], g_ref[...], b_ref[...])


def _moe(x1, cw, w_gu, w_down, ln_g, ln_b, tm=1024):
    n = x1.shape[0]
    return pl.pallas_call(
        _moe_kernel,
        grid=(n // tm, N_EXPERTS),
        in_specs=[
            pl.BlockSpec((tm, D_MODEL), lambda i, e: (i, 0)),
            pl.BlockSpec((tm, ROUTER_LANES), lambda i, e: (i, 0)),
            pl.BlockSpec((None, D_MODEL, 2 * D_EXPERT), lambda i, e: (e, 0, 0)),
            pl.BlockSpec((None, D_EXPERT, D_MODEL), lambda i, e: (e, 0, 0)),
            pl.BlockSpec((1, D_MODEL), lambda i, e: (0, 0)),
            pl.BlockSpec((1, D_MODEL), lambda i, e: (0, 0)),
        ],
        out_specs=pl.BlockSpec((tm, D_MODEL), lambda i, e: (i, 0)),
        out_shape=jax.ShapeDtypeStruct((n, D_MODEL), F32),
        scratch_shapes=[
            pltpu.VMEM((tm, D_MODEL), BF16),
            pltpu.VMEM((tm, D_MODEL), F32),
        ],
        compiler_params=pltpu.CompilerParams(
            dimension_semantics=("arbitrary", "arbitrary"), vmem_limit_bytes=VMEM_LIMIT),
        name="moe",
    )(x1, cw, w_gu, w_down, ln_g, ln_b)


def _layer(x2d, batch, w_in, b_gate, rel_bias, w_br_sb, w_br_ca, w_out, ln1_g, ln1_b,
           w_group, b_group, w_erouter, b_erouter, w_gate, w_up, w_down, ln2_g, ln2_b):
    n = x2d.shape[0]
    seq = n // batch
    scale = HEAD_DIM ** -0.5
    qscale = np.ones((1, D_QKV), np.float32)
    qscale[:, 0:W_SB] = scale
    qscale[:, 3 * W_SB:3 * W_SB + W_CA] = scale
    qkv, gates = _in_proj(x2d, w_in.astype(BF16), jnp.asarray(qscale), b_gate.reshape(1, 2 * D_MODEL))
    qkv = qkv.reshape(batch, seq, D_QKV)
    y_sb = _sb_attn(qkv).reshape(n, W_SB)
    y_ca = _ca_attn(qkv, _ca_bias_table(rel_bias)).reshape(n, W_CA)

    w_router = jnp.concatenate(
        [w_group, w_erouter.transpose(1, 0, 2).reshape(D_MODEL, N_EXPERTS)], axis=1)
    w_router = jnp.pad(w_router, ((0, 0), (0, ROUTER_LANES - N_GROUPS - N_EXPERTS)))
    b_router = jnp.pad(jnp.concatenate([b_group, b_erouter.reshape(N_EXPERTS)]),
                       (0, ROUTER_LANES - N_GROUPS - N_EXPERTS)).reshape(1, ROUTER_LANES)
    x1, cw = _post_attn(y_sb, y_ca, gates, x2d, w_br_sb.astype(BF16), w_br_ca.astype(BF16),
                        w_out.astype(BF16), w_router, b_router,
                        ln1_g.reshape(1, D_MODEL), ln1_b.reshape(1, D_MODEL))

    w_gu = jnp.concatenate([w_gate, w_up], axis=-1).reshape(N_EXPERTS, D_MODEL, 2 * D_EXPERT).astype(BF16)
    w_dn = w_down.reshape(N_EXPERTS, D_EXPERT, D_MODEL).astype(BF16)
    return _moe(x1, cw, w_gu, w_dn, ln2_g.reshape(1, D_MODEL), ln2_b.reshape(1, D_MODEL))


def kernel(x, w_in, b_gate, rel_bias, w_br_sb, w_br_ca, w_out, ln1_g, ln1_b, w_group, b_group,
           w_erouter, b_erouter, w_gate, w_up, w_down, ln2_g, ln2_b):
    batch, seq, d = x.shape
    h = x.reshape(batch * seq, d)
    for l in range(DEPTH):
        h = _layer(h, batch, w_in[l], b_gate[l], rel_bias[l], w_br_sb[l], w_br_ca[l], w_out[l],
                   ln1_g[l], ln1_b[l], w_group[l], b_group[l], w_erouter[l], b_erouter[l],
                   w_gate[l], w_up[l], w_down[l], ln2_g[l], ln2_b[l])
    return h.reshape(batch, seq, d)
```

```python
import functools

import jax
import jax.numpy as jnp
import numpy as np
from jax import lax
from jax.experimental import pallas as pl
from jax.experimental.pallas import tpu as pltpu

D_MODEL = 1024
DEPTH = 2
CHUNK = 64
HEAD_DIM = 64
H_SB = 8
H_CA = 8
W_SB = H_SB * HEAD_DIM
W_CA = H_CA * HEAD_DIM
N_PAST_CHUNKS = 8
REL_CLIP = 128
N_GROUPS = 4
EXPERTS_PER_GROUP = 8
N_EXPERTS = N_GROUPS * EXPERTS_PER_GROUP
D_EXPERT = 256
ALPHA = (2.0 * DEPTH) ** 0.25
LN_EPS = 1e-5
D_QKV = 3 * W_SB + 3 * W_CA
D_IN = D_QKV + 2 * D_MODEL
NEG_INF = -1e30

LANES = 128
HEADS_PER_TILE = LANES // HEAD_DIM
ROUTER_LANES = LANES
VMEM_LIMIT = 56 * 1024 * 1024

BF16 = jnp.bfloat16
F32 = jnp.float32

_NT = (((1,), (1,)), ((), ()))


def _dot(a, b):
    return jnp.dot(a, b, preferred_element_type=F32)


def _layer_norm(h, g, b):
    mu = jnp.mean(h, axis=-1, keepdims=True)
    hc = h - mu
    var = jnp.mean(hc * hc, axis=-1, keepdims=True)
    return hc * lax.rsqrt(var + LN_EPS) * g + b


def _split_bf16(a):
    hi = a.astype(BF16)
    lo = (a - hi.astype(F32)).astype(BF16)
    return hi, lo


def _in_proj_kernel(x_ref, w_ref, scale_ref, bg_ref, qkv_ref, gate_ref):
    xb = x_ref[...].astype(BF16)
    for c in range(D_QKV // D_MODEL):
        cols = slice(c * D_MODEL, (c + 1) * D_MODEL)
        acc = _dot(xb, w_ref[:, cols])
        qkv_ref[:, cols] = (acc * scale_ref[:, cols]).astype(BF16)
    for c in range(2):
        cols = slice(c * D_MODEL, (c + 1) * D_MODEL)
        wcols = slice(D_QKV + c * D_MODEL, D_QKV + (c + 1) * D_MODEL)
        logit = _dot(xb, w_ref[:, wcols]) + bg_ref[:, cols]
        gate_ref[:, cols] = 1.0 / (1.0 + jnp.exp(-logit))


def _in_proj(x2d, w_in_bf16, qscale, b_gate_row, tm=512):
    n = x2d.shape[0]
    return pl.pallas_call(
        _in_proj_kernel,
        grid=(n // tm,),
        in_specs=[
            pl.BlockSpec((tm, D_MODEL), lambda i: (i, 0)),
            pl.BlockSpec((D_MODEL, D_IN), lambda i: (0, 0)),
            pl.BlockSpec((1, D_QKV), lambda i: (0, 0)),
            pl.BlockSpec((1, 2 * D_MODEL), lambda i: (0, 0)),
        ],
        out_specs=[
            pl.BlockSpec((tm, D_QKV), lambda i: (i, 0)),
            pl.BlockSpec((tm, 2 * D_MODEL), lambda i: (i, 0)),
        ],
        out_shape=[
            jax.ShapeDtypeStruct((n, D_QKV), BF16),
            jax.ShapeDtypeStruct((n, 2 * D_MODEL), F32),
        ],
        compiler_params=pltpu.CompilerParams(
            dimension_semantics=("arbitrary",), vmem_limit_bytes=VMEM_LIMIT),
        name="in_proj",
    )(x2d, w_in_bf16, qscale, b_gate_row)


def _neg_abs(z):
    bits = lax.bitcast_convert_type(z, jnp.uint32) | jnp.uint32(0x80000000)
    return lax.bitcast_convert_type(bits, F32)


def _sb_kernel(q_ref, k_ref, v_ref, u_ref, o_ref, z_ref, arg_ref, rs_ref, acc_ref, car_ref, *, tq):
    i = pl.program_id(2)
    q = q_ref[...]
    lane_q = lax.broadcasted_iota(jnp.int32, (tq, LANES), 1)
    zero_q = jnp.zeros_like(q)
    q_heads = (jnp.where(lane_q < HEAD_DIM, q, zero_q), jnp.where(lane_q < HEAD_DIM, zero_q, q))

    def block_rows(n):
        return pl.ds(pl.multiple_of(jnp.maximum(i - n, 0) * tq, tq), tq)

    def scores(n, slot, diagonal):
        k = k_ref[block_rows(n), :]
        for h in range(HEADS_PER_TILE):
            z = lax.dot_general(q_heads[h], k, _NT, preferred_element_type=F32)
            if diagonal:
                row = lax.broadcasted_iota(jnp.int32, (tq, tq), 0)
                col = lax.broadcasted_iota(jnp.int32, (tq, tq), 1)
                z = jnp.where(col < row, z, NEG_INF)
            z_ref[slot, h] = z

    def terms(slot):
        for h in range(HEADS_PER_TILE):
            z = z_ref[slot, h]
            sp = jnp.maximum(z, 0.0) + jnp.log2(1.0 + jnp.exp2(_neg_abs(z)))
            later = _dot(sp.astype(BF16), u_ref[...])
            arg_ref[slot, h] = (z - sp) - later
            rs_ref[slot, h] = jnp.sum(sp, axis=1, keepdims=True)

    def apply(n, slot):
        v = v_ref[block_rows(n), :]
        zero_v = jnp.zeros_like(v)
        v = jnp.where(n <= i, v, zero_v)
        v_heads = jnp.concatenate(
            [jnp.where(lane_q < HEAD_DIM, v, zero_v), jnp.where(lane_q < HEAD_DIM, zero_v, v)], axis=0)
        ws = []
        for h in range(HEADS_PER_TILE):
            carry = car_ref[h]
            ws.append(jnp.exp2(arg_ref[slot, h] - carry).astype(BF16))
            car_ref[h] = carry + rs_ref[slot, h]
        acc_ref[...] += _dot(jnp.concatenate(ws, axis=1), v_heads)

    acc_ref[...] = jnp.zeros_like(acc_ref)
    car_ref[...] = jnp.zeros_like(car_ref)
    scores(0, 0, True)
    scores(1, 1, False)
    terms(0)

    def body(m, c):
        n = 2 * m
        for slot in range(2):
            scores(n + slot + 2, slot, False)
            apply(n + slot, slot)
            terms(1 - slot)
        return c

    lax.fori_loop(0, (i + 2) // 2, body, 0)
    o_ref[...] = acc_ref[...].astype(o_ref.dtype)


def _sb_attn(qkv, tq=256):
    b, s, _ = qkv.shape
    n_tiles = W_SB // LANES
    u = jnp.asarray(np.arange(tq)[:, None] > np.arange(tq)[None, :], dtype=BF16)
    return pl.pallas_call(
        functools.partial(_sb_kernel, tq=tq),
        grid=(b, n_tiles, s // tq),
        in_specs=[
            pl.BlockSpec((None, tq, LANES), lambda bi, hp, i: (bi, i, hp)),
            pl.BlockSpec((None, s, LANES), lambda bi, hp, i: (bi, 0, n_tiles + hp)),
            pl.BlockSpec((None, s, LANES), lambda bi, hp, i: (bi, 0, 2 * n_tiles + hp)),
            pl.BlockSpec((tq, tq), lambda bi, hp, i: (0, 0)),
        ],
        out_specs=pl.BlockSpec((None, tq, LANES), lambda bi, hp, i: (bi, i, hp)),
        out_shape=jax.ShapeDtypeStruct((b, s, W_SB), BF16),
        scratch_shapes=[
            pltpu.VMEM((2, HEADS_PER_TILE, tq, tq), F32),
            pltpu.VMEM((2, HEADS_PER_TILE, tq, tq), F32),
            pltpu.VMEM((2, HEADS_PER_TILE, tq, 1), F32),
            pltpu.VMEM((tq, LANES), F32),
            pltpu.VMEM((HEADS_PER_TILE, tq, 1), F32),
        ],
        compiler_params=pltpu.CompilerParams(
            dimension_semantics=("arbitrary", "arbitrary", "arbitrary"), vmem_limit_bytes=VMEM_LIMIT),
        name="sb_attn",
    )(qkv, qkv, qkv, u)


CA_GROUP = 4
CA_TQ = CA_GROUP * CHUNK
CA_BAND = (CA_GROUP + N_PAST_CHUNKS) * CHUNK
CA_PAD = N_PAST_CHUNKS * CHUNK


def _ca_kernel(q_ref, k_ref, v_ref, bias_ref, o_ref, kp_ref, vp_ref):
    c = pl.program_id(2)
    s = k_ref.shape[0]

    @pl.when(c == 0)
    def _():
        kp_ref[0:CA_PAD, :] = jnp.zeros((CA_PAD, LANES), BF16)
        vp_ref[0:CA_PAD, :] = jnp.zeros((CA_PAD, LANES), BF16)
        kp_ref[CA_PAD:CA_PAD + s, :] = k_ref[...]
        vp_ref[CA_PAD:CA_PAD + s, :] = v_ref[...]

    start = pl.multiple_of(c * CA_TQ, CA_TQ)
    kb = kp_ref[pl.ds(start, CA_BAND), :]
    vb = vp_ref[pl.ds(start, CA_BAND), :]
    q = q_ref[...]
    lane_q = lax.broadcasted_iota(jnp.int32, (CA_TQ, LANES), 1)
    lane_v = lax.broadcasted_iota(jnp.int32, (CA_BAND, LANES), 1)
    zero_q = jnp.zeros_like(q)
    zero_v = jnp.zeros_like(vb)
    q_heads = (jnp.where(lane_q < HEAD_DIM, q, zero_q), jnp.where(lane_q < HEAD_DIM, zero_q, q))
    v_heads = jnp.concatenate(
        [jnp.where(lane_v < HEAD_DIM, vb, zero_v), jnp.where(lane_v < HEAD_DIM, zero_v, vb)], axis=0)
    pos = lax.broadcasted_iota(jnp.int32, (CA_TQ, CA_BAND), 1)
    exists = pos >= CA_PAD - c * CA_TQ
    es = []
    inv = []
    for h in range(HEADS_PER_TILE):
        sc = lax.dot_general(q_heads[h], kb, _NT, preferred_element_type=F32) + bias_ref[h]
        sc = jnp.where(exists, sc, NEG_INF)
        m = jnp.max(sc, axis=1, keepdims=True)
        e = jnp.exp(sc - m)
        inv.append(1.0 / jnp.sum(e, axis=1, keepdims=True))
        es.append(e.astype(BF16))
    out = _dot(jnp.concatenate(es, axis=1), v_heads)
    o_ref[...] = (out * jnp.where(lane_q < HEAD_DIM, inv[0], inv[1])).astype(o_ref.dtype)


def _ca_bias_table(rel_bias):
    h = rel_bias.shape[0]
    r = np.arange(CA_TQ)[:, None]
    p = np.arange(CA_BAND)[None, :]
    qc = r // CHUNK
    kc = p // CHUNK
    in_band = (kc >= qc) & (kc <= qc + N_PAST_CHUNKS)
    n_far = CA_PAD + CA_TQ - 1 - REL_CLIP
    n_neg = CA_BAND - 1 - CA_PAD - REL_CLIP
    rb = rel_bias.astype(F32)
    line = jnp.concatenate([jnp.broadcast_to(rb[:, 2 * REL_CLIP:], (h, n_far)), rb[:, ::-1],
                            jnp.broadcast_to(rb[:, :1], (h, n_neg))], axis=1)
    length = CA_TQ + CA_BAND - 1
    assert line.shape[1] == length
    line = jnp.pad(line, ((0, 0), (0, 1)))
    skew = jnp.tile(line, (1, CA_TQ))[:, :CA_TQ * length].reshape(h, CA_TQ, length)
    table = skew[:, :, CA_TQ - 1:]
    return jnp.where(jnp.asarray(in_band)[None], table, NEG_INF)


def _ca_attn(qkv, bias_table):
    b, s, _ = qkv.shape
    n_tiles = W_CA // LANES
    base = 3 * W_SB // LANES
    return pl.pallas_call(
        _ca_kernel,
        grid=(b, n_tiles, s // CA_TQ),
        in_specs=[
            pl.BlockSpec((None, CA_TQ, LANES), lambda bi, hp, c: (bi, c, base + hp)),
            pl.BlockSpec((None, s, LANES), lambda bi, hp, c: (bi, 0, base + n_tiles + hp)),
            pl.BlockSpec((None, s, LANES), lambda bi, hp, c: (bi, 0, base + 2 * n_tiles + hp)),
            pl.BlockSpec((HEADS_PER_TILE, CA_TQ, CA_BAND), lambda bi, hp, c: (hp, 0, 0)),
        ],
        out_specs=pl.BlockSpec((None, CA_TQ, LANES), lambda bi, hp, c: (bi, c, hp)),
        out_shape=jax.ShapeDtypeStruct((b, s, W_CA), BF16),
        scratch_shapes=[
            pltpu.VMEM((CA_PAD + s, LANES), BF16),
            pltpu.VMEM((CA_PAD + s, LANES), BF16),
        ],
        compiler_params=pltpu.CompilerParams(
            dimension_semantics=("arbitrary", "arbitrary", "arbitrary"), vmem_limit_bytes=VMEM_LIMIT),
        name="ca_attn",
    )(qkv, qkv, qkv, bias_table)


def _route(lg):
    lane = lax.broadcasted_iota(jnp.int32, lg.shape, 1)
    big = jnp.int32(ROUTER_LANES)
    is_group = lane < N_GROUPS
    g_max = jnp.max(jnp.where(is_group, lg, -jnp.inf), axis=1, keepdims=True)
    g_idx = jnp.min(jnp.where(is_group & (lg == g_max), lane, big), axis=1, keepdims=True)
    g_den = jnp.sum(jnp.where(is_group, jnp.exp(lg - g_max), 0.0), axis=1, keepdims=True)
    g_val = 1.0 / g_den
    lo = N_GROUPS + EXPERTS_PER_GROUP * g_idx
    in_group = (lane >= lo) & (lane < lo + EXPERTS_PER_GROUP)
    v1 = jnp.max(jnp.where(in_group, lg, -jnp.inf), axis=1, keepdims=True)
    i1 = jnp.min(jnp.where(in_group & (lg == v1), lane, big), axis=1, keepdims=True)
    rest = in_group & (lane != i1)
    v2 = jnp.max(jnp.where(rest, lg, -jnp.inf), axis=1, keepdims=True)
    i2 = jnp.min(jnp.where(rest & (lg == v2), lane, big), axis=1, keepdims=True)
    e2 = jnp.exp(v2 - v1)
    w1 = g_val / (1.0 + e2)
    w2 = g_val * e2 / (1.0 + e2)
    return jnp.where(lane == i1, w1, 0.0) + jnp.where(lane == i2, w2, 0.0)


def _post_kernel(ysb_ref, yca_ref, gate_ref, x_ref, wsb_ref, wca_ref, wout_ref, wr_ref, br_ref,
                 g_ref, b_ref, x1_ref, cw_ref):
    a = _dot(ysb_ref[...], wsb_ref[...])
    c = _dot(yca_ref[...], wca_ref[...])
    mix = gate_ref[:, 0:D_MODEL] * a + gate_ref[:, D_MODEL:2 * D_MODEL] * c
    mixed = _dot(mix.astype(BF16), wout_ref[...])
    x1 = _layer_norm(ALPHA * x_ref[...] + mixed, g_ref[...], b_ref[...])
    x1_ref[...] = x1
    x_hi, x_lo = _split_bf16(x1)
    w_hi, w_lo = _split_bf16(wr_ref[...])
    lg = _dot(jnp.concatenate([x_hi, x_lo, x_hi], axis=1),
              jnp.concatenate([w_hi, w_hi, w_lo], axis=0)) + br_ref[...]
    cw_ref[...] = _route(lg)


def _post_attn(y_sb, y_ca, gates, x2d, w_br_sb, w_br_ca, w_out, w_router, b_router, ln_g, ln_b, tm=256):
    n = x2d.shape[0]
    row = lambda i: (i, 0)
    fixed = lambda i: (0, 0)
    return pl.pallas_call(
        _post_kernel,
        grid=(n // tm,),
        in_specs=[
            pl.BlockSpec((tm, W_SB), row),
            pl.BlockSpec((tm, W_CA), row),
            pl.BlockSpec((tm, 2 * D_MODEL), row),
            pl.BlockSpec((tm, D_MODEL), row),
            pl.BlockSpec((W_SB, D_MODEL), fixed),
            pl.BlockSpec((W_CA, D_MODEL), fixed),
            pl.BlockSpec((D_MODEL, D_MODEL), fixed),
            pl.BlockSpec((D_MODEL, ROUTER_LANES), fixed),
            pl.BlockSpec((1, ROUTER_LANES), fixed),
            pl.BlockSpec((1, D_MODEL), fixed),
            pl.BlockSpec((1, D_MODEL), fixed),
        ],
        out_specs=[
            pl.BlockSpec((tm, D_MODEL), row),
            pl.BlockSpec((tm, ROUTER_LANES), row),
        ],
        out_shape=[
            jax.ShapeDtypeStruct((n, D_MODEL), F32),
            jax.ShapeDtypeStruct((n, ROUTER_LANES), F32),
        ],
        compiler_params=pltpu.CompilerParams(
            dimension_semantics=("arbitrary",), vmem_limit_bytes=VMEM_LIMIT),
        name="post_attn",
    )(y_sb, y_ca, gates, x2d, w_br_sb, w_br_ca, w_out, w_router, b_router, ln_g, ln_b)


def _moe_kernel(x1_ref, cw_ref, wgu_ref, wd_ref, g_ref, b_ref, out_ref, xb_ref, acc_ref):
    e = pl.program_id(1)

    @pl.when(e == 0)
    def _():
        xb_ref[...] = x1_ref[...].astype(BF16)
        acc_ref[...] = jnp.zeros_like(acc_ref)

    gu = _dot(xb_ref[...], wgu_ref[...])
    gate = gu[:, 0:D_EXPERT]
    up = gu[:, D_EXPERT:2 * D_EXPERT]
    cw = cw_ref[...]
    lane = lax.broadcasted_iota(jnp.int32, cw.shape, 1)
    cwe = jnp.sum(jnp.where(lane == N_GROUPS + e, cw, 0.0), axis=1, keepdims=True)
    hid = (gate * (1.0 / (1.0 + jnp.exp(-gate)))) * up * cwe
    acc_ref[...] += _dot(hid.astype(BF16), wd_ref[...])

    @pl.when(e == N_EXPERTS - 1)
    def _():
        out_ref[...] = _layer_norm(ALPHA * x1_ref[...] + acc_ref[...], g_ref[...], b_ref[...])


def _moe(x1, cw, w_gu, w_down, ln_g, ln_b, tm=1024):
    n = x1.shape[0]
    return pl.pallas_call(
        _moe_kernel,
        grid=(n // tm, N_EXPERTS),
        in_specs=[
            pl.BlockSpec((tm, D_MODEL), lambda i, e: (i, 0)),
            pl.BlockSpec((tm, ROUTER_LANES), lambda i, e: (i, 0)),
            pl.BlockSpec((None, D_MODEL, 2 * D_EXPERT), lambda i, e: (e, 0, 0)),
            pl.BlockSpec((None, D_EXPERT, D_MODEL), lambda i, e: (e, 0, 0)),
            pl.BlockSpec((1, D_MODEL), lambda i, e: (0, 0)),
            pl.BlockSpec((1, D_MODEL), lambda i, e: (0, 0)),
        ],
        out_specs=pl.BlockSpec((tm, D_MODEL), lambda i, e: (i, 0)),
        out_shape=jax.ShapeDtypeStruct((n, D_MODEL), F32),
        scratch_shapes=[
            pltpu.VMEM((tm, D_MODEL), BF16),
            pltpu.VMEM((tm, D_MODEL), F32),
        ],
        compiler_params=pltpu.CompilerParams(
            dimension_semantics=("arbitrary", "arbitrary"), vmem_limit_bytes=VMEM_LIMIT),
        name="moe",
    )(x1, cw, w_gu, w_down, ln_g, ln_b)


def _layer(x2d, batch, w_in, b_gate, rel_bias, w_br_sb, w_br_ca, w_out, ln1_g, ln1_b,
           w_group, b_group, w_erouter, b_erouter, w_gate, w_up, w_down, ln2_g, ln2_b):
    n = x2d.shape[0]
    seq = n // batch
    scale = HEAD_DIM ** -0.5
    qscale = np.ones((1, D_QKV), np.float32)
    qscale[:, 0:W_SB] = scale * np.log2(np.e)
    qscale[:, 3 * W_SB:3 * W_SB + W_CA] = scale
    qkv, gates = _in_proj(x2d, w_in.astype(BF16), jnp.asarray(qscale), b_gate.reshape(1, 2 * D_MODEL))
    qkv = qkv.reshape(batch, seq, D_QKV)
    y_sb = _sb_attn(qkv).reshape(n, W_SB)
    y_ca = _ca_attn(qkv, _ca_bias_table(rel_bias)).reshape(n, W_CA)

    w_router = jnp.concatenate(
        [w_group, w_erouter.transpose(1, 0, 2).reshape(D_MODEL, N_EXPERTS)], axis=1)
    w_router = jnp.pad(w_router, ((0, 0), (0, ROUTER_LANES - N_GROUPS - N_EXPERTS)))
    b_router = jnp.pad(jnp.concatenate([b_group, b_erouter.reshape(N_EXPERTS)]),
                       (0, ROUTER_LANES - N_GROUPS - N_EXPERTS)).reshape(1, ROUTER_LANES)
    x1, cw = _post_attn(y_sb, y_ca, gates, x2d, w_br_sb.astype(BF16), w_br_ca.astype(BF16),
                        w_out.astype(BF16), w_router, b_router,
                        ln1_g.reshape(1, D_MODEL), ln1_b.reshape(1, D_MODEL))

    w_gu = jnp.concatenate([w_gate, w_up], axis=-1).reshape(N_EXPERTS, D_MODEL, 2 * D_EXPERT).astype(BF16)
    w_dn = w_down.reshape(N_EXPERTS, D_EXPERT, D_MODEL).astype(BF16)
    return _moe(x1, cw, w_gu, w_dn, ln2_g.reshape(1, D_MODEL), ln2_b.reshape(1, D_MODEL))


def kernel(x, w_in, b_gate, rel_bias, w_br_sb, w_br_ca, w_out, ln1_g, ln1_b, w_group, b_group,
           w_erouter, b_erouter, w_gate, w_up, w_down, ln2_g, ln2_b):
    batch, seq, d = x.shape
    h = x.reshape(batch * seq, d)
    for l in range(DEPTH):
        h = _layer(h, batch, w_in[l], b_gate[l], rel_bias[l], w_br_sb[l], w_br_ca[l], w_out[l],
                   ln1_g[l], ln1_b[l], w_group[l], b_group[l], w_erouter[l], b_erouter[l],
                   w_gate[l], w_up[l], w_down[l], ln2_g[l], ln2_b[l])
    return h.reshape(batch, seq, d)
```

```python
import functools

import jax
import jax.numpy as jnp
import numpy as np
from jax import lax
from jax.experimental import pallas as pl
from jax.experimental.pallas import tpu as pltpu

D_MODEL = 1024
DEPTH = 2
CHUNK = 64
HEAD_DIM = 64
H_SB = 8
H_CA = 8
W_SB = H_SB * HEAD_DIM
W_CA = H_CA * HEAD_DIM
N_PAST_CHUNKS = 8
REL_CLIP = 128
N_GROUPS = 4
EXPERTS_PER_GROUP = 8
N_EXPERTS = N_GROUPS * EXPERTS_PER_GROUP
D_EXPERT = 256
ALPHA = (2.0 * DEPTH) ** 0.25
LN_EPS = 1e-5
D_QKV = 3 * W_SB + 3 * W_CA
D_IN = D_QKV + 2 * D_MODEL
NEG_INF = -1e30

LANES = 128
HEADS_PER_TILE = LANES // HEAD_DIM
ROUTER_LANES = LANES
VMEM_LIMIT = 56 * 1024 * 1024

BF16 = jnp.bfloat16
F32 = jnp.float32

_NT = (((1,), (1,)), ((), ()))


def _dot(a, b):
    return jnp.dot(a, b, preferred_element_type=F32)


def _layer_norm(h, g, b):
    mu = jnp.mean(h, axis=-1, keepdims=True)
    hc = h - mu
    var = jnp.mean(hc * hc, axis=-1, keepdims=True)
    return hc * lax.rsqrt(var + LN_EPS) * g + b


def _split_bf16(a):
    hi = a.astype(BF16)
    lo = (a - hi.astype(F32)).astype(BF16)
    return hi, lo


def _in_proj_kernel(x_ref, w_ref, scale_ref, bg_ref, qkv_ref, gate_ref):
    xb = x_ref[...].astype(BF16)
    for c in range(D_QKV // D_MODEL):
        cols = slice(c * D_MODEL, (c + 1) * D_MODEL)
        acc = _dot(xb, w_ref[:, cols])
        qkv_ref[:, cols] = (acc * scale_ref[:, cols]).astype(BF16)
    for c in range(2):
        cols = slice(c * D_MODEL, (c + 1) * D_MODEL)
        wcols = slice(D_QKV + c * D_MODEL, D_QKV + (c + 1) * D_MODEL)
        logit = _dot(xb, w_ref[:, wcols]) + bg_ref[:, cols]
        gate_ref[:, cols] = 1.0 / (1.0 + jnp.exp(-logit))


def _in_proj(x2d, w_in_bf16, qscale, b_gate_row, tm=512):
    n = x2d.shape[0]
    return pl.pallas_call(
        _in_proj_kernel,
        grid=(n // tm,),
        in_specs=[
            pl.BlockSpec((tm, D_MODEL), lambda i: (i, 0)),
            pl.BlockSpec((D_MODEL, D_IN), lambda i: (0, 0)),
            pl.BlockSpec((1, D_QKV), lambda i: (0, 0)),
            pl.BlockSpec((1, 2 * D_MODEL), lambda i: (0, 0)),
        ],
        out_specs=[
            pl.BlockSpec((tm, D_QKV), lambda i: (i, 0)),
            pl.BlockSpec((tm, 2 * D_MODEL), lambda i: (i, 0)),
        ],
        out_shape=[
            jax.ShapeDtypeStruct((n, D_QKV), BF16),
            jax.ShapeDtypeStruct((n, 2 * D_MODEL), F32),
        ],
        compiler_params=pltpu.CompilerParams(
            dimension_semantics=("arbitrary",), vmem_limit_bytes=VMEM_LIMIT),
        name="in_proj",
    )(x2d, w_in_bf16, qscale, b_gate_row)


SB_DEAD_BITS = 160.0


def _sb_kernel(q_ref, k_ref, v_ref, u_ref, o_ref, z_ref, arg_ref, rs_ref, acc_ref, car_ref, *, tq):
    i = pl.program_id(2)
    q = q_ref[...]
    lane_q = lax.broadcasted_iota(jnp.int32, (tq, LANES), 1)
    zero_q = jnp.zeros_like(q)
    q_heads = (jnp.where(lane_q < HEAD_DIM, q, zero_q), jnp.where(lane_q < HEAD_DIM, zero_q, q))

    def block_rows(n):
        return pl.ds(pl.multiple_of(jnp.maximum(i - n, 0) * tq, tq), tq)

    def block_pair(n, diagonal):
        for b in range(2):
            k = k_ref[block_rows(n + b), :]
            for h in range(HEADS_PER_TILE):
                z = lax.dot_general(q_heads[h], k, _NT, preferred_element_type=F32)
                if diagonal and b == 0:
                    row = lax.broadcasted_iota(jnp.int32, (tq, tq), 0)
                    col = lax.broadcasted_iota(jnp.int32, (tq, tq), 1)
                    z = jnp.where(col < row, z, NEG_INF)
                z_ref[b, h] = z
        for b in range(2):
            for h in range(HEADS_PER_TILE):
                z = z_ref[b, h]
                sp = jnp.maximum(z, 0.0) + jnp.log2(1.0 + jnp.exp2(-jnp.abs(z)))
                later = _dot(sp.astype(BF16), u_ref[...])
                arg_ref[b, h] = (z - sp) - later
                rs_ref[b, h] = jnp.sum(sp, axis=1, keepdims=True)
        for b in range(2):
            v = v_ref[block_rows(n + b), :]
            zero_v = jnp.zeros_like(v)
            if b == 1:
                v = jnp.where(n + b <= i, v, zero_v)
            v_heads = jnp.concatenate(
                [jnp.where(lane_q < HEAD_DIM, v, zero_v), jnp.where(lane_q < HEAD_DIM, zero_v, v)], axis=0)
            ws = []
            for h in range(HEADS_PER_TILE):
                carry = car_ref[h]
                ws.append(jnp.exp2(arg_ref[b, h] - carry).astype(BF16))
                car_ref[h] = carry + rs_ref[b, h]
            acc_ref[...] += _dot(jnp.concatenate(ws, axis=1), v_heads)
        return jnp.min(car_ref[...])

    acc_ref[...] = jnp.zeros_like(acc_ref)
    car_ref[...] = jnp.zeros_like(car_ref)
    least = block_pair(0, True)

    def more(state):
        n, least = state
        return (n <= i) & (least < SB_DEAD_BITS)

    def body(state):
        n, _ = state
        return n + 2, block_pair(n, False)

    lax.while_loop(more, body, (jnp.int32(2), least))
    o_ref[...] = acc_ref[...].astype(o_ref.dtype)


def _sb_attn(qkv, tq=256):
    b, s, _ = qkv.shape
    n_tiles = W_SB // LANES
    u = jnp.asarray(np.arange(tq)[:, None] > np.arange(tq)[None, :], dtype=BF16)
    return pl.pallas_call(
        functools.partial(_sb_kernel, tq=tq),
        grid=(b, n_tiles, s // tq),
        in_specs=[
            pl.BlockSpec((None, tq, LANES), lambda bi, hp, i: (bi, i, hp)),
            pl.BlockSpec((None, s, LANES), lambda bi, hp, i: (bi, 0, n_tiles + hp)),
            pl.BlockSpec((None, s, LANES), lambda bi, hp, i: (bi, 0, 2 * n_tiles + hp)),
            pl.BlockSpec((tq, tq), lambda bi, hp, i: (0, 0)),
        ],
        out_specs=pl.BlockSpec((None, tq, LANES), lambda bi, hp, i: (bi, i, hp)),
        out_shape=jax.ShapeDtypeStruct((b, s, W_SB), BF16),
        scratch_shapes=[
            pltpu.VMEM((2, HEADS_PER_TILE, tq, tq), F32),
            pltpu.VMEM((2, HEADS_PER_TILE, tq, tq), F32),
            pltpu.VMEM((2, HEADS_PER_TILE, tq, 1), F32),
            pltpu.VMEM((tq, LANES), F32),
            pltpu.VMEM((HEADS_PER_TILE, tq, 1), F32),
        ],
        compiler_params=pltpu.CompilerParams(
            dimension_semantics=("arbitrary", "arbitrary", "arbitrary"), vmem_limit_bytes=VMEM_LIMIT),
        name="sb_attn",
    )(qkv, qkv, qkv, u)


CA_GROUP = 4
CA_TQ = CA_GROUP * CHUNK
CA_BAND = (CA_GROUP + N_PAST_CHUNKS) * CHUNK
CA_PAD = N_PAST_CHUNKS * CHUNK


def _ca_kernel(q_ref, k_ref, v_ref, bias_ref, o_ref, kp_ref, vp_ref):
    c = pl.program_id(2)
    s = k_ref.shape[0]

    @pl.when(c == 0)
    def _():
        kp_ref[0:CA_PAD, :] = jnp.zeros((CA_PAD, LANES), BF16)
        vp_ref[0:CA_PAD, :] = jnp.zeros((CA_PAD, LANES), BF16)
        kp_ref[CA_PAD:CA_PAD + s, :] = k_ref[...]
        vp_ref[CA_PAD:CA_PAD + s, :] = v_ref[...]

    start = pl.multiple_of(c * CA_TQ, CA_TQ)
    kb = kp_ref[pl.ds(start, CA_BAND), :]
    vb = vp_ref[pl.ds(start, CA_BAND), :]
    q = q_ref[...]
    lane_q = lax.broadcasted_iota(jnp.int32, (CA_TQ, LANES), 1)
    lane_v = lax.broadcasted_iota(jnp.int32, (CA_BAND, LANES), 1)
    zero_q = jnp.zeros_like(q)
    zero_v = jnp.zeros_like(vb)
    q_heads = (jnp.where(lane_q < HEAD_DIM, q, zero_q), jnp.where(lane_q < HEAD_DIM, zero_q, q))
    v_heads = jnp.concatenate(
        [jnp.where(lane_v < HEAD_DIM, vb, zero_v), jnp.where(lane_v < HEAD_DIM, zero_v, vb)], axis=0)
    pos = lax.broadcasted_iota(jnp.int32, (CA_TQ, CA_BAND), 1)
    exists = pos >= CA_PAD - c * CA_TQ
    es = []
    inv = []
    for h in range(HEADS_PER_TILE):
        sc = lax.dot_general(q_heads[h], kb, _NT, preferred_element_type=F32) + bias_ref[h]
        sc = jnp.where(exists, sc, NEG_INF)
        m = jnp.max(sc, axis=1, keepdims=True)
        e = jnp.exp(sc - m)
        inv.append(1.0 / jnp.sum(e, axis=1, keepdims=True))
        es.append(e.astype(BF16))
    out = _dot(jnp.concatenate(es, axis=1), v_heads)
    o_ref[...] = (out * jnp.where(lane_q < HEAD_DIM, inv[0], inv[1])).astype(o_ref.dtype)


def _ca_bias_table(rel_bias):
    h = rel_bias.shape[0]
    r = np.arange(CA_TQ)[:, None]
    p = np.arange(CA_BAND)[None, :]
    qc = r // CHUNK
    kc = p // CHUNK
    in_band = (kc >= qc) & (kc <= qc + N_PAST_CHUNKS)
    n_far = CA_PAD + CA_TQ - 1 - REL_CLIP
    n_neg = CA_BAND - 1 - CA_PAD - REL_CLIP
    rb = rel_bias.astype(F32)
    line = jnp.concatenate([jnp.broadcast_to(rb[:, 2 * REL_CLIP:], (h, n_far)), rb[:, ::-1],
                            jnp.broadcast_to(rb[:, :1], (h, n_neg))], axis=1)
    length = CA_TQ + CA_BAND - 1
    assert line.shape[1] == length
    line = jnp.pad(line, ((0, 0), (0, 1)))
    skew = jnp.tile(line, (1, CA_TQ))[:, :CA_TQ * length].reshape(h, CA_TQ, length)
    table = skew[:, :, CA_TQ - 1:]
    return jnp.where(jnp.asarray(in_band)[None], table, NEG_INF)


def _ca_attn(qkv, bias_table):
    b, s, _ = qkv.shape
    n_tiles = W_CA // LANES
    base = 3 * W_SB // LANES
    return pl.pallas_call(
        _ca_kernel,
        grid=(b, n_tiles, s // CA_TQ),
        in_specs=[
            pl.BlockSpec((None, CA_TQ, LANES), lambda bi, hp, c: (bi, c, base + hp)),
            pl.BlockSpec((None, s, LANES), lambda bi, hp, c: (bi, 0, base + n_tiles + hp)),
            pl.BlockSpec((None, s, LANES), lambda bi, hp, c: (bi, 0, base + 2 * n_tiles + hp)),
            pl.BlockSpec((HEADS_PER_TILE, CA_TQ, CA_BAND), lambda bi, hp, c: (hp, 0, 0)),
        ],
        out_specs=pl.BlockSpec((None, CA_TQ, LANES), lambda bi, hp, c: (bi, c, hp)),
        out_shape=jax.ShapeDtypeStruct((b, s, W_CA), BF16),
        scratch_shapes=[
            pltpu.VMEM((CA_PAD + s, LANES), BF16),
            pltpu.VMEM((CA_PAD + s, LANES), BF16),
        ],
        compiler_params=pltpu.CompilerParams(
            dimension_semantics=("arbitrary", "arbitrary", "arbitrary"), vmem_limit_bytes=VMEM_LIMIT),
        name="ca_attn",
    )(qkv, qkv, qkv, bias_table)


def _route(lg):
    lane = lax.broadcasted_iota(jnp.int32, lg.shape, 1)
    big = jnp.int32(ROUTER_LANES)
    is_group = lane < N_GROUPS
    g_max = jnp.max(jnp.where(is_group, lg, -jnp.inf), axis=1, keepdims=True)
    g_idx = jnp.min(jnp.where(is_group & (lg == g_max), lane, big), axis=1, keepdims=True)
    g_den = jnp.sum(jnp.where(is_group, jnp.exp(lg - g_max), 0.0), axis=1, keepdims=True)
    g_val = 1.0 / g_den
    lo = N_GROUPS + EXPERTS_PER_GROUP * g_idx
    in_group = (lane >= lo) & (lane < lo + EXPERTS_PER_GROUP)
    v1 = jnp.max(jnp.where(in_group, lg, -jnp.inf), axis=1, keepdims=True)
    i1 = jnp.min(jnp.where(in_group & (lg == v1), lane, big), axis=1, keepdims=True)
    rest = in_group & (lane != i1)
    v2 = jnp.max(jnp.where(rest, lg, -jnp.inf), axis=1, keepdims=True)
    i2 = jnp.min(jnp.where(rest & (lg == v2), lane, big), axis=1, keepdims=True)
    e2 = jnp.exp(v2 - v1)
    w1 = g_val / (1.0 + e2)
    w2 = g_val * e2 / (1.0 + e2)
    return jnp.where(lane == i1, w1, 0.0) + jnp.where(lane == i2, w2, 0.0)


def _post_kernel(ysb_ref, yca_ref, gate_ref, x_ref, wsb_ref, wca_ref, wout_ref, wr_ref, br_ref,
                 g_ref, b_ref, x1_ref, cw_ref):
    a = _dot(ysb_ref[...], wsb_ref[...])
    c = _dot(yca_ref[...], wca_ref[...])
    mix = gate_ref[:, 0:D_MODEL] * a + gate_ref[:, D_MODEL:2 * D_MODEL] * c
    mixed = _dot(mix.astype(BF16), wout_ref[...])
    x1 = _layer_norm(ALPHA * x_ref[...] + mixed, g_ref[...], b_ref[...])
    x1_ref[...] = x1
    x_hi, x_lo = _split_bf16(x1)
    w_hi, w_lo = _split_bf16(wr_ref[...])
    lg = _dot(jnp.concatenate([x_hi, x_lo, x_hi], axis=1),
              jnp.concatenate([w_hi, w_hi, w_lo], axis=0)) + br_ref[...]
    cw_ref[...] = _route(lg)


def _post_attn(y_sb, y_ca, gates, x2d, w_br_sb, w_br_ca, w_out, w_router, b_router, ln_g, ln_b, tm=256):
    n = x2d.shape[0]
    row = lambda i: (i, 0)
    fixed = lambda i: (0, 0)
    return pl.pallas_call(
        _post_kernel,
        grid=(n // tm,),
        in_specs=[
            pl.BlockSpec((tm, W_SB), row),
            pl.BlockSpec((tm, W_CA), row),
            pl.BlockSpec((tm, 2 * D_MODEL), row),
            pl.BlockSpec((tm, D_MODEL), row),
            pl.BlockSpec((W_SB, D_MODEL), fixed),
            pl.BlockSpec((W_CA, D_MODEL), fixed),
            pl.BlockSpec((D_MODEL, D_MODEL), fixed),
            pl.BlockSpec((D_MODEL, ROUTER_LANES), fixed),
            pl.BlockSpec((1, ROUTER_LANES), fixed),
            pl.BlockSpec((1, D_MODEL), fixed),
            pl.BlockSpec((1, D_MODEL), fixed),
        ],
        out_specs=[
            pl.BlockSpec((tm, D_MODEL), row),
            pl.BlockSpec((tm, ROUTER_LANES), row),
        ],
        out_shape=[
            jax.ShapeDtypeStruct((n, D_MODEL), F32),
            jax.ShapeDtypeStruct((n, ROUTER_LANES), F32),
        ],
        compiler_params=pltpu.CompilerParams(
            dimension_semantics=("arbitrary",), vmem_limit_bytes=VMEM_LIMIT),
        name="post_attn",
    )(y_sb, y_ca, gates, x2d, w_br_sb, w_br_ca, w_out, w_router, b_router, ln_g, ln_b)


def _moe_kernel(x1_ref, cw_ref, wgu_ref, wd_ref, g_ref, b_ref, out_ref, xb_ref, acc_ref):
    e = pl.program_id(1)

    @pl.when(e == 0)
    def _():
        xb_ref[...] = x1_ref[...].astype(BF16)
        acc_ref[...] = jnp.zeros_like(acc_ref)

    gu = _dot(xb_ref[...], wgu_ref[...])
    gate = gu[:, 0:D_EXPERT]
    up = gu[:, D_EXPERT:2 * D_EXPERT]
    cw = cw_ref[...]
    lane = lax.broadcasted_iota(jnp.int32, cw.shape, 1)
    cwe = jnp.sum(jnp.where(lane == N_GROUPS + e, cw, 0.0), axis=1, keepdims=True)
    hid = (gate * (1.0 / (1.0 + jnp.exp(-gate)))) * up * cwe
    acc_ref[...] += _dot(hid.astype(BF16), wd_ref[...])

    @pl.when(e == N_EXPERTS - 1)
    def _():
        out_ref[...] = _layer_norm(ALPHA * x1_ref[...] + acc_ref[...], g_ref[...], b_ref[...])


def _moe(x1, cw, w_gu, w_down, ln_g, ln_b, tm=1024):
    n = x1.shape[0]
    return pl.pallas_call(
        _moe_kernel,
        grid=(n // tm, N_EXPERTS),
        in_specs=[
            pl.BlockSpec((tm, D_MODEL), lambda i, e: (i, 0)),
            pl.BlockSpec((tm, ROUTER_LANES), lambda i, e: (i, 0)),
            pl.BlockSpec((None, D_MODEL, 2 * D_EXPERT), lambda i, e: (e, 0, 0)),
            pl.BlockSpec((None, D_EXPERT, D_MODEL), lambda i, e: (e, 0, 0)),
            pl.BlockSpec((1, D_MODEL), lambda i, e: (0, 0)),
            pl.BlockSpec((1, D_MODEL), lambda i, e: (0, 0)),
        ],
        out_specs=pl.BlockSpec((tm, D_MODEL), lambda i, e: (i, 0)),
        out_shape=jax.ShapeDtypeStruct((n, D_MODEL), F32),
        scratch_shapes=[
            pltpu.VMEM((tm, D_MODEL), BF16),
            pltpu.VMEM((tm, D_MODEL), F32),
        ],
        compiler_params=pltpu.CompilerParams(
            dimension_semantics=("arbitrary", "arbitrary"), vmem_limit_bytes=VMEM_LIMIT),
        name="moe",
    )(x1, cw, w_gu, w_down, ln_g, ln_b)


def _layer(x2d, batch, w_in, b_gate, rel_bias, w_br_sb, w_br_ca, w_out, ln1_g, ln1_b,
           w_group, b_group, w_erouter, b_erouter, w_gate, w_up, w_down, ln2_g, ln2_b):
    n = x2d.shape[0]
    seq = n // batch
    scale = HEAD_DIM ** -0.5
    qscale = np.ones((1, D_QKV), np.float32)
    qscale[:, 0:W_SB] = scale * np.log2(np.e)
    qscale[:, 3 * W_SB:3 * W_SB + W_CA] = scale
    qkv, gates = _in_proj(x2d, w_in.astype(BF16), jnp.asarray(qscale), b_gate.reshape(1, 2 * D_MODEL))
    qkv = qkv.reshape(batch, seq, D_QKV)
    y_sb = _sb_attn(qkv).reshape(n, W_SB)
    y_ca = _ca_attn(qkv, _ca_bias_table(rel_bias)).reshape(n, W_CA)

    w_router = jnp.concatenate(
        [w_group, w_erouter.transpose(1, 0, 2).reshape(D_MODEL, N_EXPERTS)], axis=1)
    w_router = jnp.pad(w_router, ((0, 0), (0, ROUTER_LANES - N_GROUPS - N_EXPERTS)))
    b_router = jnp.pad(jnp.concatenate([b_group, b_erouter.reshape(N_EXPERTS)]),
                       (0, ROUTER_LANES - N_GROUPS - N_EXPERTS)).reshape(1, ROUTER_LANES)
    x1, cw = _post_attn(y_sb, y_ca, gates, x2d, w_br_sb.astype(BF16), w_br_ca.astype(BF16),
                        w_out.astype(BF16), w_router, b_router,
                        ln1_g.reshape(1, D_MODEL), ln1_b.reshape(1, D_MODEL))

    w_gu = jnp.concatenate([w_gate, w_up], axis=-1).reshape(N_EXPERTS, D_MODEL, 2 * D_EXPERT).astype(BF16)
    w_dn = w_down.reshape(N_EXPERTS, D_EXPERT, D_MODEL).astype(BF16)
    return _moe(x1, cw, w_gu, w_dn, ln2_g.reshape(1, D_MODEL), ln2_b.reshape(1, D_MODEL))


def kernel(x, w_in, b_gate, rel_bias, w_br_sb, w_br_ca, w_out, ln1_g, ln1_b, w_group, b_group,
           w_erouter, b_erouter, w_gate, w_up, w_down, ln2_g, ln2_b):
    batch, seq, d = x.shape
    h = x.reshape(batch * seq, d)
    for l in range(DEPTH):
        h = _layer(h, batch, w_in[l], b_gate[l], rel_bias[l], w_br_sb[l], w_br_ca[l], w_out[l],
                   ln1_g[l], ln1_b[l], w_group[l], b_group[l], w_erouter[l], b_erouter[l],
                   w_gate[l], w_up[l], w_down[l], ln2_g[l], ln2_b[l])
    return h.reshape(batch, seq, d)
```

```python
import functools

import jax
import jax.numpy as jnp
import numpy as np
from jax import lax
from jax.experimental import pallas as pl
from jax.experimental.pallas import tpu as pltpu

D_MODEL = 1024
DEPTH = 2
CHUNK = 64
HEAD_DIM = 64
H_SB = 8
H_CA = 8
W_SB = H_SB * HEAD_DIM
W_CA = H_CA * HEAD_DIM
N_PAST_CHUNKS = 8
REL_CLIP = 128
N_GROUPS = 4
EXPERTS_PER_GROUP = 8
N_EXPERTS = N_GROUPS * EXPERTS_PER_GROUP
D_EXPERT = 256
ALPHA = (2.0 * DEPTH) ** 0.25
LN_EPS = 1e-5
D_QKV = 3 * W_SB + 3 * W_CA
D_IN = D_QKV + 2 * D_MODEL
NEG_INF = -1e30

LANES = 128
HEADS_PER_TILE = LANES // HEAD_DIM
ROUTER_LANES = LANES
VMEM_LIMIT = 56 * 1024 * 1024

BF16 = jnp.bfloat16
F32 = jnp.float32

_NT = (((1,), (1,)), ((), ()))


def _dot(a, b):
    return jnp.dot(a, b, preferred_element_type=F32)


def _layer_norm(h, g, b):
    mu = jnp.mean(h, axis=-1, keepdims=True)
    hc = h - mu
    var = jnp.mean(hc * hc, axis=-1, keepdims=True)
    return hc * lax.rsqrt(var + LN_EPS) * g + b


def _split_bf16(a):
    hi = a.astype(BF16)
    lo = (a - hi.astype(F32)).astype(BF16)
    return hi, lo


def _in_proj_kernel(x_ref, w_ref, scale_ref, bg_ref, qkv_ref, gate_ref):
    xb = x_ref[...].astype(BF16)
    for c in range(D_QKV // D_MODEL):
        cols = slice(c * D_MODEL, (c + 1) * D_MODEL)
        acc = _dot(xb, w_ref[:, cols])
        qkv_ref[:, cols] = (acc * scale_ref[:, cols]).astype(BF16)
    for c in range(2):
        cols = slice(c * D_MODEL, (c + 1) * D_MODEL)
        wcols = slice(D_QKV + c * D_MODEL, D_QKV + (c + 1) * D_MODEL)
        logit = _dot(xb, w_ref[:, wcols]) + bg_ref[:, cols]
        gate_ref[:, cols] = 1.0 / (1.0 + jnp.exp(-logit))


def _in_proj(x2d, w_in_bf16, qscale, b_gate_row, tm=512):
    n = x2d.shape[0]
    return pl.pallas_call(
        _in_proj_kernel,
        grid=(n // tm,),
        in_specs=[
            pl.BlockSpec((tm, D_MODEL), lambda i: (i, 0)),
            pl.BlockSpec((D_MODEL, D_IN), lambda i: (0, 0)),
            pl.BlockSpec((1, D_QKV), lambda i: (0, 0)),
            pl.BlockSpec((1, 2 * D_MODEL), lambda i: (0, 0)),
        ],
        out_specs=[
            pl.BlockSpec((tm, D_QKV), lambda i: (i, 0)),
            pl.BlockSpec((tm, 2 * D_MODEL), lambda i: (i, 0)),
        ],
        out_shape=[
            jax.ShapeDtypeStruct((n, D_QKV), BF16),
            jax.ShapeDtypeStruct((n, 2 * D_MODEL), F32),
        ],
        compiler_params=pltpu.CompilerParams(
            dimension_semantics=("arbitrary",), vmem_limit_bytes=VMEM_LIMIT),
        name="in_proj",
    )(x2d, w_in_bf16, qscale, b_gate_row)


SB_DEAD_BITS = 160.0


def _sb_kernel(q_ref, k_ref, v_ref, u_ref, o_ref, z_ref, arg_ref, rs_ref, acc_ref, car_ref, *, tq):
    i = pl.program_id(2)
    q = q_ref[...]
    lane_q = lax.broadcasted_iota(jnp.int32, (tq, LANES), 1)
    zero_q = jnp.zeros_like(q)
    q_heads = (jnp.where(lane_q < HEAD_DIM, q, zero_q), jnp.where(lane_q < HEAD_DIM, zero_q, q))

    def block_rows(n):
        return pl.ds(pl.multiple_of(jnp.maximum(i - n, 0) * tq, tq), tq)

    def block_pair(n, diagonal):
        for b in range(2):
            k = k_ref[block_rows(n + b), :]
            for h in range(HEADS_PER_TILE):
                z = lax.dot_general(q_heads[h], k, _NT, preferred_element_type=F32)
                if diagonal and b == 0:
                    row = lax.broadcasted_iota(jnp.int32, (tq, tq), 0)
                    col = lax.broadcasted_iota(jnp.int32, (tq, tq), 1)
                    z = jnp.where(col < row, z, NEG_INF)
                z_ref[b, h] = z
        for b in range(2):
            for h in range(HEADS_PER_TILE):
                z = z_ref[b, h]
                sp = jnp.maximum(z, 0.0) + jnp.log2(1.0 + jnp.exp2(-jnp.abs(z)))
                later = _dot(sp.astype(BF16), u_ref[...])
                arg_ref[b, h] = (z - sp) - later
                rs_ref[b, h] = jnp.sum(sp, axis=1, keepdims=True)
        for b in range(2):
            v = v_ref[block_rows(n + b), :]
            zero_v = jnp.zeros_like(v)
            if b == 1:
                v = jnp.where(n + b <= i, v, zero_v)
            v_heads = jnp.concatenate(
                [jnp.where(lane_q < HEAD_DIM, v, zero_v), jnp.where(lane_q < HEAD_DIM, zero_v, v)], axis=0)
            ws = []
            for h in range(HEADS_PER_TILE):
                carry = car_ref[h]
                ws.append(jnp.exp2(arg_ref[b, h] - carry).astype(BF16))
                car_ref[h] = carry + rs_ref[b, h]
            acc_ref[...] += _dot(jnp.concatenate(ws, axis=1), v_heads)
        return jnp.min(car_ref[...])

    acc_ref[...] = jnp.zeros_like(acc_ref)
    car_ref[...] = jnp.zeros_like(car_ref)
    least = block_pair(0, True)

    def more(state):
        n, least = state
        return (n <= i) & (least < SB_DEAD_BITS)

    def body(state):
        n, _ = state
        return n + 2, block_pair(n, False)

    lax.while_loop(more, body, (jnp.int32(2), least))
    o_ref[...] = acc_ref[...].astype(o_ref.dtype)


def _sb_attn(qkv, tq=256):
    b, s, _ = qkv.shape
    n_tiles = W_SB // LANES
    u = jnp.asarray(np.arange(tq)[:, None] > np.arange(tq)[None, :], dtype=BF16)
    return pl.pallas_call(
        functools.partial(_sb_kernel, tq=tq),
        grid=(b, n_tiles, s // tq),
        in_specs=[
            pl.BlockSpec((None, tq, LANES), lambda bi, hp, i: (bi, i, hp)),
            pl.BlockSpec((None, s, LANES), lambda bi, hp, i: (bi, 0, n_tiles + hp)),
            pl.BlockSpec((None, s, LANES), lambda bi, hp, i: (bi, 0, 2 * n_tiles + hp)),
            pl.BlockSpec((tq, tq), lambda bi, hp, i: (0, 0)),
        ],
        out_specs=pl.BlockSpec((None, tq, LANES), lambda bi, hp, i: (bi, i, hp)),
        out_shape=jax.ShapeDtypeStruct((b, s, W_SB), BF16),
        scratch_shapes=[
            pltpu.VMEM((2, HEADS_PER_TILE, tq, tq), F32),
            pltpu.VMEM((2, HEADS_PER_TILE, tq, tq), F32),
            pltpu.VMEM((2, HEADS_PER_TILE, tq, 1), F32),
            pltpu.VMEM((tq, LANES), F32),
            pltpu.VMEM((HEADS_PER_TILE, tq, 1), F32),
        ],
        compiler_params=pltpu.CompilerParams(
            dimension_semantics=("arbitrary", "arbitrary", "arbitrary"), vmem_limit_bytes=VMEM_LIMIT),
        name="sb_attn",
    )(qkv, qkv, qkv, u)


CA_GROUP = 4
CA_TQ = CA_GROUP * CHUNK
CA_BAND = (CA_GROUP + N_PAST_CHUNKS) * CHUNK
CA_PAD = N_PAST_CHUNKS * CHUNK


def _ca_kernel(q_ref, k_ref, v_ref, bias_ref, o_ref, kp_ref, vp_ref):
    c = pl.program_id(2)
    s = k_ref.shape[0]

    @pl.when(c == 0)
    def _():
        kp_ref[0:CA_PAD, :] = jnp.zeros((CA_PAD, LANES), BF16)
        vp_ref[0:CA_PAD, :] = jnp.zeros((CA_PAD, LANES), BF16)
        kp_ref[CA_PAD:CA_PAD + s, :] = k_ref[...]
        vp_ref[CA_PAD:CA_PAD + s, :] = v_ref[...]

    start = pl.multiple_of(c * CA_TQ, CA_TQ)
    kb = kp_ref[pl.ds(start, CA_BAND), :]
    vb = vp_ref[pl.ds(start, CA_BAND), :]
    q = q_ref[...]
    lane_q = lax.broadcasted_iota(jnp.int32, (CA_TQ, LANES), 1)
    lane_v = lax.broadcasted_iota(jnp.int32, (CA_BAND, LANES), 1)
    zero_q = jnp.zeros_like(q)
    zero_v = jnp.zeros_like(vb)
    q_heads = (jnp.where(lane_q < HEAD_DIM, q, zero_q), jnp.where(lane_q < HEAD_DIM, zero_q, q))
    v_heads = jnp.concatenate(
        [jnp.where(lane_v < HEAD_DIM, vb, zero_v), jnp.where(lane_v < HEAD_DIM, zero_v, vb)], axis=0)
    pos = lax.broadcasted_iota(jnp.int32, (CA_TQ, CA_BAND), 1)
    exists = pos >= CA_PAD - c * CA_TQ
    es = []
    inv = []
    for h in range(HEADS_PER_TILE):
        sc = lax.dot_general(q_heads[h], kb, _NT, preferred_element_type=F32) + bias_ref[h]
        sc = jnp.where(exists, sc, NEG_INF)
        m = jnp.max(sc, axis=1, keepdims=True)
        e = jnp.exp(sc - m)
        inv.append(1.0 / jnp.sum(e, axis=1, keepdims=True))
        es.append(e.astype(BF16))
    out = _dot(jnp.concatenate(es, axis=1), v_heads)
    o_ref[...] = (out * jnp.where(lane_q < HEAD_DIM, inv[0], inv[1])).astype(o_ref.dtype)


def _ca_bias_table(rel_bias):
    h = rel_bias.shape[0]
    r = np.arange(CA_TQ)[:, None]
    p = np.arange(CA_BAND)[None, :]
    qc = r // CHUNK
    kc = p // CHUNK
    in_band = (kc >= qc) & (kc <= qc + N_PAST_CHUNKS)
    n_far = CA_PAD + CA_TQ - 1 - REL_CLIP
    n_neg = CA_BAND - 1 - CA_PAD - REL_CLIP
    rb = rel_bias.astype(F32)
    line = jnp.concatenate([jnp.broadcast_to(rb[:, 2 * REL_CLIP:], (h, n_far)), rb[:, ::-1],
                            jnp.broadcast_to(rb[:, :1], (h, n_neg))], axis=1)
    length = CA_TQ + CA_BAND - 1
    assert line.shape[1] == length
    line = jnp.pad(line, ((0, 0), (0, 1)))
    skew = jnp.tile(line, (1, CA_TQ))[:, :CA_TQ * length].reshape(h, CA_TQ, length)
    table = skew[:, :, CA_TQ - 1:]
    return jnp.where(jnp.asarray(in_band)[None], table, NEG_INF)


def _ca_attn(qkv, bias_table):
    b, s, _ = qkv.shape
    n_tiles = W_CA // LANES
    base = 3 * W_SB // LANES
    return pl.pallas_call(
        _ca_kernel,
        grid=(b, n_tiles, s // CA_TQ),
        in_specs=[
            pl.BlockSpec((None, CA_TQ, LANES), lambda bi, hp, c: (bi, c, base + hp)),
            pl.BlockSpec((None, s, LANES), lambda bi, hp, c: (bi, 0, base + n_tiles + hp)),
            pl.BlockSpec((None, s, LANES), lambda bi, hp, c: (bi, 0, base + 2 * n_tiles + hp)),
            pl.BlockSpec((HEADS_PER_TILE, CA_TQ, CA_BAND), lambda bi, hp, c: (hp, 0, 0)),
        ],
        out_specs=pl.BlockSpec((None, CA_TQ, LANES), lambda bi, hp, c: (bi, c, hp)),
        out_shape=jax.ShapeDtypeStruct((b, s, W_CA), BF16),
        scratch_shapes=[
            pltpu.VMEM((CA_PAD + s, LANES), BF16),
            pltpu.VMEM((CA_PAD + s, LANES), BF16),
        ],
        compiler_params=pltpu.CompilerParams(
            dimension_semantics=("arbitrary", "arbitrary", "arbitrary"), vmem_limit_bytes=VMEM_LIMIT),
        name="ca_attn",
    )(qkv, qkv, qkv, bias_table)


ROUTE_E1, ROUTE_E2, ROUTE_W1, ROUTE_W2 = 0, 1, 2, 3


def _route(lg):
    lane = lax.broadcasted_iota(jnp.int32, lg.shape, 1)
    big = jnp.int32(ROUTER_LANES)
    is_group = lane < N_GROUPS
    g_max = jnp.max(jnp.where(is_group, lg, -jnp.inf), axis=1, keepdims=True)
    g_idx = jnp.min(jnp.where(is_group & (lg == g_max), lane, big), axis=1, keepdims=True)
    g_den = jnp.sum(jnp.where(is_group, jnp.exp(lg - g_max), 0.0), axis=1, keepdims=True)
    g_val = 1.0 / g_den
    lo = N_GROUPS + EXPERTS_PER_GROUP * g_idx
    in_group = (lane >= lo) & (lane < lo + EXPERTS_PER_GROUP)
    v1 = jnp.max(jnp.where(in_group, lg, -jnp.inf), axis=1, keepdims=True)
    i1 = jnp.min(jnp.where(in_group & (lg == v1), lane, big), axis=1, keepdims=True)
    rest = in_group & (lane != i1)
    v2 = jnp.max(jnp.where(rest, lg, -jnp.inf), axis=1, keepdims=True)
    i2 = jnp.min(jnp.where(rest & (lg == v2), lane, big), axis=1, keepdims=True)
    e2 = jnp.exp(v2 - v1)
    w1 = g_val / (1.0 + e2)
    w2 = g_val * e2 / (1.0 + e2)
    picked = ((lane == i1) | (lane == i2)).astype(BF16)
    info = (jnp.where(lane == ROUTE_E1, (i1 - N_GROUPS).astype(F32), 0.0)
            + jnp.where(lane == ROUTE_E2, (i2 - N_GROUPS).astype(F32), 0.0)
            + jnp.where(lane == ROUTE_W1, w1, 0.0) + jnp.where(lane == ROUTE_W2, w2, 0.0))
    return info, picked


def _post_kernel(ysb_ref, yca_ref, gate_ref, x_ref, wsb_ref, wca_ref, wout_ref, wr_ref, br_ref,
                 g_ref, b_ref, x1_ref, info_ref, pick_ref):
    a = _dot(ysb_ref[...], wsb_ref[...])
    c = _dot(yca_ref[...], wca_ref[...])
    mix = gate_ref[:, 0:D_MODEL] * a + gate_ref[:, D_MODEL:2 * D_MODEL] * c
    mixed = _dot(mix.astype(BF16), wout_ref[...])
    x1 = _layer_norm(ALPHA * x_ref[...] + mixed, g_ref[...], b_ref[...])
    x1_ref[...] = x1
    x_hi, x_lo = _split_bf16(x1)
    w_hi, w_lo = _split_bf16(wr_ref[...])
    lg = _dot(jnp.concatenate([x_hi, x_lo, x_hi], axis=1),
              jnp.concatenate([w_hi, w_hi, w_lo], axis=0)) + br_ref[...]
    info_ref[...], pick_ref[...] = _route(lg)


def _post_attn(y_sb, y_ca, gates, x2d, w_br_sb, w_br_ca, w_out, w_router, b_router, ln_g, ln_b, tm=256):
    n = x2d.shape[0]
    row = lambda i: (i, 0)
    fixed = lambda i: (0, 0)
    return pl.pallas_call(
        _post_kernel,
        grid=(n // tm,),
        in_specs=[
            pl.BlockSpec((tm, W_SB), row),
            pl.BlockSpec((tm, W_CA), row),
            pl.BlockSpec((tm, 2 * D_MODEL), row),
            pl.BlockSpec((tm, D_MODEL), row),
            pl.BlockSpec((W_SB, D_MODEL), fixed),
            pl.BlockSpec((W_CA, D_MODEL), fixed),
            pl.BlockSpec((D_MODEL, D_MODEL), fixed),
            pl.BlockSpec((D_MODEL, ROUTER_LANES), fixed),
            pl.BlockSpec((1, ROUTER_LANES), fixed),
            pl.BlockSpec((1, D_MODEL), fixed),
            pl.BlockSpec((1, D_MODEL), fixed),
        ],
        out_specs=[
            pl.BlockSpec((tm, D_MODEL), row),
            pl.BlockSpec((tm, ROUTER_LANES), row),
            pl.BlockSpec((tm, ROUTER_LANES), row),
        ],
        out_shape=[
            jax.ShapeDtypeStruct((n, D_MODEL), F32),
            jax.ShapeDtypeStruct((n, ROUTER_LANES), F32),
            jax.ShapeDtypeStruct((n, ROUTER_LANES), BF16),
        ],
        compiler_params=pltpu.CompilerParams(
            dimension_semantics=("arbitrary",), vmem_limit_bytes=VMEM_LIMIT),
        name="post_attn",
    )(y_sb, y_ca, gates, x2d, w_br_sb, w_br_ca, w_out, w_router, b_router, ln_g, ln_b)


MOE_TILE = 256
TOP_K = 2


def _moe_rows(n):
    return n * TOP_K + N_EXPERTS * MOE_TILE


def _rank_kernel(pick_ref, info_ref, tri_ref, dest_ref, cnt_ref, run_ref, off_ref, *, tb):
    p = pl.program_id(0)
    i = pl.program_id(1)
    pick = pick_ref[...]

    @pl.when((p == 0) & (i == 0))
    def _():
        run_ref[...] = jnp.zeros_like(run_ref)

    @pl.when(p == 0)
    def _():
        run_ref[...] += jnp.sum(pick.astype(F32), axis=0, keepdims=True)

    @pl.when((p == 1) & (i == 0))
    def _():
        cnt = run_ref[...]
        cnt_ref[...] = cnt
        padded = jnp.ceil(cnt * (1.0 / MOE_TILE)) * MOE_TILE
        lane = lax.broadcasted_iota(jnp.int32, padded.shape, 1)
        scan = padded
        step = 1
        while step < ROUTER_LANES:
            scan = scan + jnp.where(lane >= step, pltpu.roll(scan, step, axis=1), 0.0)
            step *= 2
        off_ref[...] = scan - padded
        run_ref[...] = jnp.zeros_like(run_ref)

    @pl.when(p == 1)
    def _():
        seen = run_ref[0:1, :]
        earlier = _dot(tri_ref[...], pick)
        row_of = earlier + seen + off_ref[0:1, :]
        info = info_ref[...]
        lane = lax.broadcasted_iota(jnp.int32, info.shape, 1)
        lane_f = lane.astype(F32)
        e1 = jnp.sum(jnp.where(lane == ROUTE_E1, info, 0.0), axis=1, keepdims=True)
        e2 = jnp.sum(jnp.where(lane == ROUTE_E2, info, 0.0), axis=1, keepdims=True)
        d1 = jnp.sum(jnp.where(lane_f == e1 + N_GROUPS, row_of, 0.0), axis=1, keepdims=True)
        d2 = jnp.sum(jnp.where(lane_f == e2 + N_GROUPS, row_of, 0.0), axis=1, keepdims=True)
        dest_ref[...] = (jnp.where(lane == 0, d1, 0.0) + jnp.where(lane == 1, d2, 0.0)).astype(jnp.int32)
        run_ref[...] += jnp.sum(pick.astype(F32), axis=0, keepdims=True)


def _moe_rank(pick, info, tb=256):
    n = pick.shape[0]
    tri = jnp.asarray(np.arange(tb)[None, :] < np.arange(tb)[:, None], dtype=BF16)
    return pl.pallas_call(
        functools.partial(_rank_kernel, tb=tb),
        grid=(2, n // tb),
        in_specs=[
            pl.BlockSpec((tb, ROUTER_LANES), lambda p, i: (i, 0)),
            pl.BlockSpec((tb, ROUTER_LANES), lambda p, i: (i, 0)),
            pl.BlockSpec((tb, tb), lambda p, i: (0, 0)),
        ],
        out_specs=[
            pl.BlockSpec((tb, ROUTER_LANES), lambda p, i: (i * p, 0)),
            pl.BlockSpec((8, ROUTER_LANES), lambda p, i: (0, 0)),
        ],
        out_shape=[
            jax.ShapeDtypeStruct((n, ROUTER_LANES), jnp.int32),
            jax.ShapeDtypeStruct((8, ROUTER_LANES), F32),
        ],
        scratch_shapes=[
            pltpu.VMEM((8, ROUTER_LANES), F32),
            pltpu.VMEM((8, ROUTER_LANES), F32),
        ],
        compiler_params=pltpu.CompilerParams(
            dimension_semantics=("arbitrary", "arbitrary"), vmem_limit_bytes=VMEM_LIMIT),
        name="moe_rank",
    )(pick, info, tri)


def _row_copy(src_ref, src_row, dst_ref, dst_row, sem):
    return pltpu.make_async_copy(src_ref.at[pl.ds(src_row, 1), :], dst_ref.at[pl.ds(dst_row, 1), :], sem)


def _dispatch_kernel(d1_ref, d2_ref, x_ref, zero_hbm, xs_hbm, sem, *, tb):
    del zero_hbm
    base = pl.program_id(0) * tb

    def issue(r, c):
        _row_copy(x_ref, r, xs_hbm, d1_ref[base + r], sem).start()
        _row_copy(x_ref, r, xs_hbm, d2_ref[base + r], sem).start()
        return c

    lax.fori_loop(0, tb, issue, 0, unroll=8)
    for _ in range(TOP_K):
        pltpu.make_async_copy(x_ref, xs_hbm.at[pl.ds(0, tb), :], sem).wait()


def _moe_dispatch(x1, dest1, dest2, tb=256):
    n = x1.shape[0]
    rows = _moe_rows(n)
    grid_spec = pltpu.PrefetchScalarGridSpec(
        num_scalar_prefetch=2,
        grid=(n // tb,),
        in_specs=[
            pl.BlockSpec((tb, D_MODEL), lambda i, d1, d2: (i, 0)),
            pl.BlockSpec(memory_space=pl.ANY),
        ],
        out_specs=pl.BlockSpec(memory_space=pl.ANY),
        scratch_shapes=[pltpu.SemaphoreType.DMA(())],
    )
    return pl.pallas_call(
        functools.partial(_dispatch_kernel, tb=tb),
        grid_spec=grid_spec,
        out_shape=jax.ShapeDtypeStruct((rows, D_MODEL), F32),
        input_output_aliases={3: 0},
        compiler_params=pltpu.CompilerParams(
            dimension_semantics=("arbitrary",), vmem_limit_bytes=VMEM_LIMIT),
        name="moe_dispatch",
    )(dest1, dest2, x1, jnp.zeros((rows, D_MODEL), F32))


def _experts_kernel(te_ref, nt_ref, xs_ref, wg_ref, wu_ref, wd_ref, ys_ref, wgu_s, wd_s):
    t = pl.program_id(0)
    changed = (t == 0) | (te_ref[t] != te_ref[jnp.maximum(t - 1, 0)])

    @pl.when(changed)
    def _():
        wgu_s[:, 0:D_EXPERT] = wg_ref[...].astype(BF16)
        wgu_s[:, D_EXPERT:2 * D_EXPERT] = wu_ref[...].astype(BF16)
        wd_s[...] = wd_ref[...].astype(BF16)

    @pl.when(t < nt_ref[0])
    def _():
        gu = _dot(xs_ref[...].astype(BF16), wgu_s[...])
        gate = gu[:, 0:D_EXPERT]
        up = gu[:, D_EXPERT:2 * D_EXPERT]
        hid = (gate * (1.0 / (1.0 + jnp.exp(-gate)))) * up
        ys_ref[...] = _dot(hid.astype(BF16), wd_s[...])

    @pl.when(t >= nt_ref[0])
    def _():
        ys_ref[...] = jnp.zeros_like(ys_ref)


def _moe_experts(xs, tile_expert, n_tiles_used, w_gate, w_up, w_down):
    rows = xs.shape[0]
    grid_spec = pltpu.PrefetchScalarGridSpec(
        num_scalar_prefetch=2,
        grid=(rows // MOE_TILE,),
        in_specs=[
            pl.BlockSpec((MOE_TILE, D_MODEL), lambda t, te, nt: (t, 0)),
            pl.BlockSpec((None, D_MODEL, D_EXPERT), lambda t, te, nt: (te[t], 0, 0)),
            pl.BlockSpec((None, D_MODEL, D_EXPERT), lambda t, te, nt: (te[t], 0, 0)),
            pl.BlockSpec((None, D_EXPERT, D_MODEL), lambda t, te, nt: (te[t], 0, 0)),
        ],
        out_specs=pl.BlockSpec((MOE_TILE, D_MODEL), lambda t, te, nt: (t, 0)),
        scratch_shapes=[
            pltpu.VMEM((D_MODEL, 2 * D_EXPERT), BF16),
            pltpu.VMEM((D_EXPERT, D_MODEL), BF16),
        ],
    )
    return pl.pallas_call(
        _experts_kernel,
        grid_spec=grid_spec,
        out_shape=jax.ShapeDtypeStruct((rows, D_MODEL), F32),
        compiler_params=pltpu.CompilerParams(
            dimension_semantics=("arbitrary",), vmem_limit_bytes=VMEM_LIMIT),
        name="moe_experts",
    )(tile_expert, n_tiles_used, xs, w_gate, w_up, w_down)


def _combine_kernel(d1_ref, d2_ref, x1_ref, info_ref, ys_hbm, g_ref, b_ref, out_ref, y1_ref, y2_ref, sem, *, tb):
    base = pl.program_id(0) * tb

    def issue(r, c):
        _row_copy(ys_hbm, d1_ref[base + r], y1_ref, r, sem).start()
        _row_copy(ys_hbm, d2_ref[base + r], y2_ref, r, sem).start()
        return c

    lax.fori_loop(0, tb, issue, 0, unroll=8)
    for y_ref in (y1_ref, y2_ref):
        pltpu.make_async_copy(ys_hbm.at[pl.ds(0, tb), :], y_ref, sem).wait()
    info = info_ref[...]
    lane = lax.broadcasted_iota(jnp.int32, info.shape, 1)
    w1 = jnp.sum(jnp.where(lane == ROUTE_W1, info, 0.0), axis=1, keepdims=True)
    w2 = jnp.sum(jnp.where(lane == ROUTE_W2, info, 0.0), axis=1, keepdims=True)
    ffn = w1 * y1_ref[...] + w2 * y2_ref[...]
    out_ref[...] = _layer_norm(ALPHA * x1_ref[...] + ffn, g_ref[...], b_ref[...])


def _moe_combine(x1, info, ys, dest1, dest2, ln_g, ln_b, tb=256):
    n = x1.shape[0]
    grid_spec = pltpu.PrefetchScalarGridSpec(
        num_scalar_prefetch=2,
        grid=(n // tb,),
        in_specs=[
            pl.BlockSpec((tb, D_MODEL), lambda i, d1, d2: (i, 0)),
            pl.BlockSpec((tb, ROUTER_LANES), lambda i, d1, d2: (i, 0)),
            pl.BlockSpec(memory_space=pl.ANY),
            pl.BlockSpec((1, D_MODEL), lambda i, d1, d2: (0, 0)),
            pl.BlockSpec((1, D_MODEL), lambda i, d1, d2: (0, 0)),
        ],
        out_specs=pl.BlockSpec((tb, D_MODEL), lambda i, d1, d2: (i, 0)),
        scratch_shapes=[
            pltpu.VMEM((tb, D_MODEL), F32),
            pltpu.VMEM((tb, D_MODEL), F32),
            pltpu.SemaphoreType.DMA(()),
        ],
    )
    return pl.pallas_call(
        functools.partial(_combine_kernel, tb=tb),
        grid_spec=grid_spec,
        out_shape=jax.ShapeDtypeStruct((n, D_MODEL), F32),
        compiler_params=pltpu.CompilerParams(
            dimension_semantics=("arbitrary",), vmem_limit_bytes=VMEM_LIMIT),
        name="moe_combine",
    )(dest1, dest2, x1, info, ys, ln_g, ln_b)


def _moe(x1, info, pick, w_gate, w_up, w_down, ln_g, ln_b):
    n = x1.shape[0]
    dest, counts = _moe_rank(pick, info)
    dest1 = dest[:, 0]
    dest2 = dest[:, 1]
    tiles = jnp.ceil(counts[0, N_GROUPS:N_GROUPS + N_EXPERTS] * (1.0 / MOE_TILE)).astype(jnp.int32)
    last_tile = jnp.cumsum(tiles)
    tile_ids = jnp.arange(_moe_rows(n) // MOE_TILE, dtype=jnp.int32)
    tile_expert = jnp.minimum(jnp.sum(tile_ids[:, None] >= last_tile[None, :], axis=1), N_EXPERTS - 1)
    xs = _moe_dispatch(x1, dest1, dest2)
    ys = _moe_experts(xs, tile_expert.astype(jnp.int32), last_tile[N_EXPERTS - 1:], w_gate, w_up, w_down)
    return _moe_combine(x1, info, ys, dest1, dest2, ln_g, ln_b)


def _layer(x2d, batch, w_in, b_gate, rel_bias, w_br_sb, w_br_ca, w_out, ln1_g, ln1_b,
           w_group, b_group, w_erouter, b_erouter, w_gate, w_up, w_down, ln2_g, ln2_b):
    n = x2d.shape[0]
    seq = n // batch
    scale = HEAD_DIM ** -0.5
    qscale = np.ones((1, D_QKV), np.float32)
    qscale[:, 0:W_SB] = scale * np.log2(np.e)
    qscale[:, 3 * W_SB:3 * W_SB + W_CA] = scale
    qkv, gates = _in_proj(x2d, w_in.astype(BF16), jnp.asarray(qscale), b_gate.reshape(1, 2 * D_MODEL))
    qkv = qkv.reshape(batch, seq, D_QKV)
    y_sb = _sb_attn(qkv).reshape(n, W_SB)
    y_ca = _ca_attn(qkv, _ca_bias_table(rel_bias)).reshape(n, W_CA)

    w_router = jnp.concatenate(
        [w_group, w_erouter.transpose(1, 0, 2).reshape(D_MODEL, N_EXPERTS)], axis=1)
    w_router = jnp.pad(w_router, ((0, 0), (0, ROUTER_LANES - N_GROUPS - N_EXPERTS)))
    b_router = jnp.pad(jnp.concatenate([b_group, b_erouter.reshape(N_EXPERTS)]),
                       (0, ROUTER_LANES - N_GROUPS - N_EXPERTS)).reshape(1, ROUTER_LANES)
    x1, info, pick = _post_attn(y_sb, y_ca, gates, x2d, w_br_sb.astype(BF16), w_br_ca.astype(BF16),
                        w_out.astype(BF16), w_router, b_router,
                        ln1_g.reshape(1, D_MODEL), ln1_b.reshape(1, D_MODEL))

    return _moe(x1, info, pick, w_gate.reshape(N_EXPERTS, D_MODEL, D_EXPERT),
                w_up.reshape(N_EXPERTS, D_MODEL, D_EXPERT), w_down.reshape(N_EXPERTS, D_EXPERT, D_MODEL),
                ln2_g.reshape(1, D_MODEL), ln2_b.reshape(1, D_MODEL))


def kernel(x, w_in, b_gate, rel_bias, w_br_sb, w_br_ca, w_out, ln1_g, ln1_b, w_group, b_group,
           w_erouter, b_erouter, w_gate, w_up, w_down, ln2_g, ln2_b):
    batch, seq, d = x.shape
    h = x.reshape(batch * seq, d)
    for l in range(DEPTH):
        h = _layer(h, batch, w_in[l], b_gate[l], rel_bias[l], w_br_sb[l], w_br_ca[l], w_out[l],
                   ln1_g[l], ln1_b[l], w_group[l], b_group[l], w_erouter[l], b_erouter[l],
                   w_gate[l], w_up[l], w_down[l], ln2_g[l], ln2_b[l])
    return h.reshape(batch, seq, d)
```

```python
import functools

import jax
import jax.numpy as jnp
import numpy as np
from jax import lax
from jax.experimental import pallas as pl
from jax.experimental.pallas import tpu as pltpu

D_MODEL = 1024
DEPTH = 2
CHUNK = 64
HEAD_DIM = 64
H_SB = 8
H_CA = 8
W_SB = H_SB * HEAD_DIM
W_CA = H_CA * HEAD_DIM
N_PAST_CHUNKS = 8
REL_CLIP = 128
N_GROUPS = 4
EXPERTS_PER_GROUP = 8
N_EXPERTS = N_GROUPS * EXPERTS_PER_GROUP
D_EXPERT = 256
ALPHA = (2.0 * DEPTH) ** 0.25
LN_EPS = 1e-5
D_QKV = 3 * W_SB + 3 * W_CA
D_IN = D_QKV + 2 * D_MODEL
NEG_INF = -1e30

LANES = 128
HEADS_PER_TILE = LANES // HEAD_DIM
ROUTER_LANES = LANES
VMEM_LIMIT = 56 * 1024 * 1024

BF16 = jnp.bfloat16
F32 = jnp.float32

_NT = (((1,), (1,)), ((), ()))


def _dot(a, b):
    return jnp.dot(a, b, preferred_element_type=F32)


def _layer_norm(h, g, b):
    mu = jnp.mean(h, axis=-1, keepdims=True)
    hc = h - mu
    var = jnp.mean(hc * hc, axis=-1, keepdims=True)
    return hc * lax.rsqrt(var + LN_EPS) * g + b


def _split_bf16(a):
    hi = a.astype(BF16)
    lo = (a - hi.astype(F32)).astype(BF16)
    return hi, lo


def _in_proj_kernel(x_ref, w_ref, scale_ref, bg_ref, qkv_ref, gate_ref):
    xb = x_ref[...].astype(BF16)
    for c in range(D_QKV // D_MODEL):
        cols = slice(c * D_MODEL, (c + 1) * D_MODEL)
        acc = _dot(xb, w_ref[:, cols])
        qkv_ref[:, cols] = (acc * scale_ref[:, cols]).astype(BF16)
    for c in range(2):
        cols = slice(c * D_MODEL, (c + 1) * D_MODEL)
        wcols = slice(D_QKV + c * D_MODEL, D_QKV + (c + 1) * D_MODEL)
        logit = _dot(xb, w_ref[:, wcols]) + bg_ref[:, cols]
        gate_ref[:, cols] = 1.0 / (1.0 + jnp.exp(-logit))


def _in_proj(x2d, w_in_bf16, qscale, b_gate_row, tm=512):
    n = x2d.shape[0]
    return pl.pallas_call(
        _in_proj_kernel,
        grid=(n // tm,),
        in_specs=[
            pl.BlockSpec((tm, D_MODEL), lambda i: (i, 0)),
            pl.BlockSpec((D_MODEL, D_IN), lambda i: (0, 0)),
            pl.BlockSpec((1, D_QKV), lambda i: (0, 0)),
            pl.BlockSpec((1, 2 * D_MODEL), lambda i: (0, 0)),
        ],
        out_specs=[
            pl.BlockSpec((tm, D_QKV), lambda i: (i, 0)),
            pl.BlockSpec((tm, 2 * D_MODEL), lambda i: (i, 0)),
        ],
        out_shape=[
            jax.ShapeDtypeStruct((n, D_QKV), BF16),
            jax.ShapeDtypeStruct((n, 2 * D_MODEL), F32),
        ],
        compiler_params=pltpu.CompilerParams(
            dimension_semantics=("arbitrary",), vmem_limit_bytes=VMEM_LIMIT),
        name="in_proj",
    )(x2d, w_in_bf16, qscale, b_gate_row)


SB_DEAD_BITS = 160.0


def _sb_kernel(q_ref, k_ref, v_ref, u_ref, o_ref, z_ref, arg_ref, rs_ref, acc_ref, car_ref, *, tq):
    i = pl.program_id(2)
    q = q_ref[...]
    lane_q = lax.broadcasted_iota(jnp.int32, (tq, LANES), 1)
    zero_q = jnp.zeros_like(q)
    q_heads = (jnp.where(lane_q < HEAD_DIM, q, zero_q), jnp.where(lane_q < HEAD_DIM, zero_q, q))

    def block_rows(n):
        return pl.ds(pl.multiple_of(jnp.maximum(i - n, 0) * tq, tq), tq)

    def block_pair(n, diagonal):
        for b in range(2):
            k = k_ref[block_rows(n + b), :]
            for h in range(HEADS_PER_TILE):
                z = lax.dot_general(q_heads[h], k, _NT, preferred_element_type=F32)
                if diagonal and b == 0:
                    row = lax.broadcasted_iota(jnp.int32, (tq, tq), 0)
                    col = lax.broadcasted_iota(jnp.int32, (tq, tq), 1)
                    z = jnp.where(col < row, z, NEG_INF)
                z_ref[b, h] = z
        for b in range(2):
            for h in range(HEADS_PER_TILE):
                z = z_ref[b, h]
                sp = jnp.maximum(z, 0.0) + jnp.log2(1.0 + jnp.exp2(-jnp.abs(z)))
                later = _dot(sp.astype(BF16), u_ref[...])
                arg_ref[b, h] = (z - sp) - later
                rs_ref[b, h] = jnp.sum(sp, axis=1, keepdims=True)
        for b in range(2):
            v = v_ref[block_rows(n + b), :]
            zero_v = jnp.zeros_like(v)
            if b == 1:
                v = jnp.where(n + b <= i, v, zero_v)
            v_heads = jnp.concatenate(
                [jnp.where(lane_q < HEAD_DIM, v, zero_v), jnp.where(lane_q < HEAD_DIM, zero_v, v)], axis=0)
            ws = []
            for h in range(HEADS_PER_TILE):
                carry = car_ref[h]
                ws.append(jnp.exp2(arg_ref[b, h] - carry).astype(BF16))
                car_ref[h] = carry + rs_ref[b, h]
            acc_ref[...] += _dot(jnp.concatenate(ws, axis=1), v_heads)
        return jnp.min(car_ref[...])

    acc_ref[...] = jnp.zeros_like(acc_ref)
    car_ref[...] = jnp.zeros_like(car_ref)
    least = block_pair(0, True)

    def more(state):
        n, least = state
        return (n <= i) & (least < SB_DEAD_BITS)

    def body(state):
        n, _ = state
        return n + 2, block_pair(n, False)

    lax.while_loop(more, body, (jnp.int32(2), least))
    o_ref[...] = acc_ref[...].astype(o_ref.dtype)


def _sb_attn(qkv, tq=256):
    b, s, _ = qkv.shape
    n_tiles = W_SB // LANES
    u = jnp.asarray(np.arange(tq)[:, None] > np.arange(tq)[None, :], dtype=BF16)
    return pl.pallas_call(
        functools.partial(_sb_kernel, tq=tq),
        grid=(b, n_tiles, s // tq),
        in_specs=[
            pl.BlockSpec((None, tq, LANES), lambda bi, hp, i: (bi, i, hp)),
            pl.BlockSpec((None, s, LANES), lambda bi, hp, i: (bi, 0, n_tiles + hp)),
            pl.BlockSpec((None, s, LANES), lambda bi, hp, i: (bi, 0, 2 * n_tiles + hp)),
            pl.BlockSpec((tq, tq), lambda bi, hp, i: (0, 0)),
        ],
        out_specs=pl.BlockSpec((None, tq, LANES), lambda bi, hp, i: (bi, i, hp)),
        out_shape=jax.ShapeDtypeStruct((b, s, W_SB), BF16),
        scratch_shapes=[
            pltpu.VMEM((2, HEADS_PER_TILE, tq, tq), F32),
            pltpu.VMEM((2, HEADS_PER_TILE, tq, tq), F32),
            pltpu.VMEM((2, HEADS_PER_TILE, tq, 1), F32),
            pltpu.VMEM((tq, LANES), F32),
            pltpu.VMEM((HEADS_PER_TILE, tq, 1), F32),
        ],
        compiler_params=pltpu.CompilerParams(
            dimension_semantics=("arbitrary", "arbitrary", "arbitrary"), vmem_limit_bytes=VMEM_LIMIT),
        name="sb_attn",
    )(qkv, qkv, qkv, u)


CA_GROUP = 4
CA_TQ = CA_GROUP * CHUNK
CA_BAND = (CA_GROUP + N_PAST_CHUNKS) * CHUNK
CA_PAD = N_PAST_CHUNKS * CHUNK


def _ca_kernel(q_ref, k_ref, v_ref, bias_ref, o_ref, kp_ref, vp_ref):
    c = pl.program_id(2)
    s = k_ref.shape[0]

    @pl.when(c == 0)
    def _():
        kp_ref[0:CA_PAD, :] = jnp.zeros((CA_PAD, LANES), BF16)
        vp_ref[0:CA_PAD, :] = jnp.zeros((CA_PAD, LANES), BF16)
        kp_ref[CA_PAD:CA_PAD + s, :] = k_ref[...]
        vp_ref[CA_PAD:CA_PAD + s, :] = v_ref[...]

    start = pl.multiple_of(c * CA_TQ, CA_TQ)
    kb = kp_ref[pl.ds(start, CA_BAND), :]
    vb = vp_ref[pl.ds(start, CA_BAND), :]
    q = q_ref[...]
    lane_q = lax.broadcasted_iota(jnp.int32, (CA_TQ, LANES), 1)
    lane_v = lax.broadcasted_iota(jnp.int32, (CA_BAND, LANES), 1)
    zero_q = jnp.zeros_like(q)
    zero_v = jnp.zeros_like(vb)
    q_heads = (jnp.where(lane_q < HEAD_DIM, q, zero_q), jnp.where(lane_q < HEAD_DIM, zero_q, q))
    v_heads = jnp.concatenate(
        [jnp.where(lane_v < HEAD_DIM, vb, zero_v), jnp.where(lane_v < HEAD_DIM, zero_v, vb)], axis=0)
    pos = lax.broadcasted_iota(jnp.int32, (CA_TQ, CA_BAND), 1)
    exists = pos >= CA_PAD - c * CA_TQ
    es = []
    inv = []
    for h in range(HEADS_PER_TILE):
        sc = lax.dot_general(q_heads[h], kb, _NT, preferred_element_type=F32) + bias_ref[h]
        sc = jnp.where(exists, sc, NEG_INF)
        m = jnp.max(sc, axis=1, keepdims=True)
        e = jnp.exp(sc - m)
        inv.append(1.0 / jnp.sum(e, axis=1, keepdims=True))
        es.append(e.astype(BF16))
    out = _dot(jnp.concatenate(es, axis=1), v_heads)
    o_ref[...] = (out * jnp.where(lane_q < HEAD_DIM, inv[0], inv[1])).astype(o_ref.dtype)


def _ca_bias_table(rel_bias):
    h = rel_bias.shape[0]
    r = np.arange(CA_TQ)[:, None]
    p = np.arange(CA_BAND)[None, :]
    qc = r // CHUNK
    kc = p // CHUNK
    in_band = (kc >= qc) & (kc <= qc + N_PAST_CHUNKS)
    n_far = CA_PAD + CA_TQ - 1 - REL_CLIP
    n_neg = CA_BAND - 1 - CA_PAD - REL_CLIP
    rb = rel_bias.astype(F32)
    line = jnp.concatenate([jnp.broadcast_to(rb[:, 2 * REL_CLIP:], (h, n_far)), rb[:, ::-1],
                            jnp.broadcast_to(rb[:, :1], (h, n_neg))], axis=1)
    length = CA_TQ + CA_BAND - 1
    assert line.shape[1] == length
    line = jnp.pad(line, ((0, 0), (0, 1)))
    skew = jnp.tile(line, (1, CA_TQ))[:, :CA_TQ * length].reshape(h, CA_TQ, length)
    table = skew[:, :, CA_TQ - 1:]
    return jnp.where(jnp.asarray(in_band)[None], table, NEG_INF)


def _ca_attn(qkv, bias_table):
    b, s, _ = qkv.shape
    n_tiles = W_CA // LANES
    base = 3 * W_SB // LANES
    return pl.pallas_call(
        _ca_kernel,
        grid=(b, n_tiles, s // CA_TQ),
        in_specs=[
            pl.BlockSpec((None, CA_TQ, LANES), lambda bi, hp, c: (bi, c, base + hp)),
            pl.BlockSpec((None, s, LANES), lambda bi, hp, c: (bi, 0, base + n_tiles + hp)),
            pl.BlockSpec((None, s, LANES), lambda bi, hp, c: (bi, 0, base + 2 * n_tiles + hp)),
            pl.BlockSpec((HEADS_PER_TILE, CA_TQ, CA_BAND), lambda bi, hp, c: (hp, 0, 0)),
        ],
        out_specs=pl.BlockSpec((None, CA_TQ, LANES), lambda bi, hp, c: (bi, c, hp)),
        out_shape=jax.ShapeDtypeStruct((b, s, W_CA), BF16),
        scratch_shapes=[
            pltpu.VMEM((CA_PAD + s, LANES), BF16),
            pltpu.VMEM((CA_PAD + s, LANES), BF16),
        ],
        compiler_params=pltpu.CompilerParams(
            dimension_semantics=("arbitrary", "arbitrary", "arbitrary"), vmem_limit_bytes=VMEM_LIMIT),
        name="ca_attn",
    )(qkv, qkv, qkv, bias_table)


ROUTE_E1, ROUTE_E2, ROUTE_W1, ROUTE_W2 = 0, 1, 2, 3


def _route(lg):
    lane = lax.broadcasted_iota(jnp.int32, lg.shape, 1)
    big = jnp.int32(ROUTER_LANES)
    is_group = lane < N_GROUPS
    g_max = jnp.max(jnp.where(is_group, lg, -jnp.inf), axis=1, keepdims=True)
    g_idx = jnp.min(jnp.where(is_group & (lg == g_max), lane, big), axis=1, keepdims=True)
    g_den = jnp.sum(jnp.where(is_group, jnp.exp(lg - g_max), 0.0), axis=1, keepdims=True)
    g_val = 1.0 / g_den
    lo = N_GROUPS + EXPERTS_PER_GROUP * g_idx
    in_group = (lane >= lo) & (lane < lo + EXPERTS_PER_GROUP)
    v1 = jnp.max(jnp.where(in_group, lg, -jnp.inf), axis=1, keepdims=True)
    i1 = jnp.min(jnp.where(in_group & (lg == v1), lane, big), axis=1, keepdims=True)
    rest = in_group & (lane != i1)
    v2 = jnp.max(jnp.where(rest, lg, -jnp.inf), axis=1, keepdims=True)
    i2 = jnp.min(jnp.where(rest & (lg == v2), lane, big), axis=1, keepdims=True)
    e2 = jnp.exp(v2 - v1)
    w1 = g_val / (1.0 + e2)
    w2 = g_val * e2 / (1.0 + e2)
    picked = ((lane == i1) | (lane == i2)).astype(BF16)
    info = (jnp.where(lane == ROUTE_E1, (i1 - N_GROUPS).astype(F32), 0.0)
            + jnp.where(lane == ROUTE_E2, (i2 - N_GROUPS).astype(F32), 0.0)
            + jnp.where(lane == ROUTE_W1, w1, 0.0) + jnp.where(lane == ROUTE_W2, w2, 0.0))
    return info, picked


def _post_kernel(ysb_ref, yca_ref, gate_ref, x_ref, wsb_ref, wca_ref, wout_ref, wr_ref, br_ref,
                 g_ref, b_ref, x1_ref, info_ref, pick_ref):
    a = _dot(ysb_ref[...], wsb_ref[...])
    c = _dot(yca_ref[...], wca_ref[...])
    mix = gate_ref[:, 0:D_MODEL] * a + gate_ref[:, D_MODEL:2 * D_MODEL] * c
    mixed = _dot(mix.astype(BF16), wout_ref[...])
    x1 = _layer_norm(ALPHA * x_ref[...] + mixed, g_ref[...], b_ref[...])
    x1_ref[...] = x1
    x_hi, x_lo = _split_bf16(x1)
    w_hi, w_lo = _split_bf16(wr_ref[...])
    lg = _dot(jnp.concatenate([x_hi, x_lo, x_hi], axis=1),
              jnp.concatenate([w_hi, w_hi, w_lo], axis=0)) + br_ref[...]
    info_ref[...], pick_ref[...] = _route(lg)


def _post_attn(y_sb, y_ca, gates, x2d, w_br_sb, w_br_ca, w_out, w_router, b_router, ln_g, ln_b, tm=256):
    n = x2d.shape[0]
    row = lambda i: (i, 0)
    fixed = lambda i: (0, 0)
    return pl.pallas_call(
        _post_kernel,
        grid=(n // tm,),
        in_specs=[
            pl.BlockSpec((tm, W_SB), row),
            pl.BlockSpec((tm, W_CA), row),
            pl.BlockSpec((tm, 2 * D_MODEL), row),
            pl.BlockSpec((tm, D_MODEL), row),
            pl.BlockSpec((W_SB, D_MODEL), fixed),
            pl.BlockSpec((W_CA, D_MODEL), fixed),
            pl.BlockSpec((D_MODEL, D_MODEL), fixed),
            pl.BlockSpec((D_MODEL, ROUTER_LANES), fixed),
            pl.BlockSpec((1, ROUTER_LANES), fixed),
            pl.BlockSpec((1, D_MODEL), fixed),
            pl.BlockSpec((1, D_MODEL), fixed),
        ],
        out_specs=[
            pl.BlockSpec((tm, D_MODEL), row),
            pl.BlockSpec((tm, ROUTER_LANES), row),
            pl.BlockSpec((tm, ROUTER_LANES), row),
        ],
        out_shape=[
            jax.ShapeDtypeStruct((n, D_MODEL), F32),
            jax.ShapeDtypeStruct((n, ROUTER_LANES), F32),
            jax.ShapeDtypeStruct((n, ROUTER_LANES), BF16),
        ],
        compiler_params=pltpu.CompilerParams(
            dimension_semantics=("arbitrary",), vmem_limit_bytes=VMEM_LIMIT),
        name="post_attn",
    )(y_sb, y_ca, gates, x2d, w_br_sb, w_br_ca, w_out, w_router, b_router, ln_g, ln_b)


MOE_TILE = 256
TOP_K = 2


def _moe_rows(n):
    return n * TOP_K + N_EXPERTS * MOE_TILE


def _rank_kernel(pick_ref, info_ref, tri_ref, dest_ref, cnt_ref, run_ref, off_ref, *, tb):
    p = pl.program_id(0)
    i = pl.program_id(1)
    pick = pick_ref[...]

    @pl.when((p == 0) & (i == 0))
    def _():
        run_ref[...] = jnp.zeros_like(run_ref)

    @pl.when(p == 0)
    def _():
        run_ref[...] += jnp.sum(pick.astype(F32), axis=0, keepdims=True)

    @pl.when((p == 1) & (i == 0))
    def _():
        cnt = run_ref[...]
        cnt_ref[...] = cnt
        padded = jnp.ceil(cnt * (1.0 / MOE_TILE)) * MOE_TILE
        lane = lax.broadcasted_iota(jnp.int32, padded.shape, 1)
        scan = padded
        step = 1
        while step < ROUTER_LANES:
            scan = scan + jnp.where(lane >= step, pltpu.roll(scan, step, axis=1), 0.0)
            step *= 2
        off_ref[...] = scan - padded
        run_ref[...] = jnp.zeros_like(run_ref)

    @pl.when(p == 1)
    def _():
        seen = run_ref[0:1, :]
        earlier = _dot(tri_ref[...], pick)
        row_of = earlier + seen + off_ref[0:1, :]
        info = info_ref[...]
        lane = lax.broadcasted_iota(jnp.int32, info.shape, 1)
        lane_f = lane.astype(F32)
        e1 = jnp.sum(jnp.where(lane == ROUTE_E1, info, 0.0), axis=1, keepdims=True)
        e2 = jnp.sum(jnp.where(lane == ROUTE_E2, info, 0.0), axis=1, keepdims=True)
        d1 = jnp.sum(jnp.where(lane_f == e1 + N_GROUPS, row_of, 0.0), axis=1, keepdims=True)
        d2 = jnp.sum(jnp.where(lane_f == e2 + N_GROUPS, row_of, 0.0), axis=1, keepdims=True)
        dest_ref[...] = (jnp.where(lane == 0, d1, 0.0) + jnp.where(lane == 1, d2, 0.0)).astype(jnp.int32)
        run_ref[...] += jnp.sum(pick.astype(F32), axis=0, keepdims=True)


def _moe_rank(pick, info, tb=1024):
    n = pick.shape[0]
    tri = jnp.asarray(np.arange(tb)[None, :] < np.arange(tb)[:, None], dtype=BF16)
    return pl.pallas_call(
        functools.partial(_rank_kernel, tb=tb),
        grid=(2, n // tb),
        in_specs=[
            pl.BlockSpec((tb, ROUTER_LANES), lambda p, i: (i, 0)),
            pl.BlockSpec((tb, ROUTER_LANES), lambda p, i: (i, 0)),
            pl.BlockSpec((tb, tb), lambda p, i: (0, 0)),
        ],
        out_specs=[
            pl.BlockSpec((tb, ROUTER_LANES), lambda p, i: (i * p, 0)),
            pl.BlockSpec((8, ROUTER_LANES), lambda p, i: (0, 0)),
        ],
        out_shape=[
            jax.ShapeDtypeStruct((n, ROUTER_LANES), jnp.int32),
            jax.ShapeDtypeStruct((8, ROUTER_LANES), F32),
        ],
        scratch_shapes=[
            pltpu.VMEM((8, ROUTER_LANES), F32),
            pltpu.VMEM((8, ROUTER_LANES), F32),
        ],
        compiler_params=pltpu.CompilerParams(
            dimension_semantics=("arbitrary", "arbitrary"), vmem_limit_bytes=VMEM_LIMIT),
        name="moe_rank",
    )(pick, info, tri)


def _row_copy(src_ref, src_row, dst_ref, dst_row, sem):
    return pltpu.make_async_copy(src_ref.at[pl.ds(src_row, 1), :], dst_ref.at[pl.ds(dst_row, 1), :], sem)


def _dispatch_kernel(d1_ref, d2_ref, last_ref, x_ref, xs_hbm, zero_ref, sem, zsem, *, tb):
    base = pl.program_id(0) * tb

    @pl.when(pl.program_id(0) == 0)
    def _():
        zero_ref[...] = jnp.zeros_like(zero_ref)

        def last_tile_copy(e):
            first_row = pl.multiple_of((last_ref[e] - 1) * MOE_TILE, MOE_TILE)
            return pltpu.make_async_copy(zero_ref, xs_hbm.at[pl.ds(first_row, MOE_TILE), :], zsem)

        def unused_tile_copy(t):
            return pltpu.make_async_copy(zero_ref, xs_hbm.at[pl.ds(t * MOE_TILE, MOE_TILE), :], zsem)

        n_tiles = xs_hbm.shape[0] // MOE_TILE
        min_used = n_tiles - N_EXPERTS
        for wait in (False, True):
            for e in range(N_EXPERTS):
                owns_tiles = last_ref[e] > (last_ref[e - 1] if e else 0)
                pl.when(owns_tiles)(
                    lambda e=e, wait=wait: last_tile_copy(e).wait() if wait else last_tile_copy(e).start())
            for t in range(min_used, n_tiles):
                pl.when(t >= last_ref[N_EXPERTS - 1])(
                    lambda t=t, wait=wait: unused_tile_copy(t).wait() if wait else unused_tile_copy(t).start())

    def issue(r, c):
        _row_copy(x_ref, r, xs_hbm, d1_ref[base + r], sem).start(priority=0)
        _row_copy(x_ref, r, xs_hbm, d2_ref[base + r], sem).start(priority=1)
        return c

    lax.fori_loop(0, tb, issue, 0, unroll=8)
    for _ in range(TOP_K):
        pltpu.make_async_copy(x_ref, xs_hbm.at[pl.ds(0, tb), :], sem).wait()


def _moe_dispatch(x1, dest1, dest2, last_tile, tb=256):
    n = x1.shape[0]
    rows = _moe_rows(n)
    grid_spec = pltpu.PrefetchScalarGridSpec(
        num_scalar_prefetch=3,
        grid=(n // tb,),
        in_specs=[pl.BlockSpec((tb, D_MODEL), lambda i, d1, d2, last: (i, 0))],
        out_specs=pl.BlockSpec(memory_space=pl.ANY),
        scratch_shapes=[
            pltpu.VMEM((MOE_TILE, D_MODEL), F32),
            pltpu.SemaphoreType.DMA(()),
            pltpu.SemaphoreType.DMA(()),
        ],
    )
    return pl.pallas_call(
        functools.partial(_dispatch_kernel, tb=tb),
        grid_spec=grid_spec,
        out_shape=jax.ShapeDtypeStruct((rows, D_MODEL), F32),
        compiler_params=pltpu.CompilerParams(
            dimension_semantics=("arbitrary",), vmem_limit_bytes=VMEM_LIMIT),
        name="moe_dispatch",
    )(dest1, dest2, last_tile, x1)


def _experts_kernel(te_ref, nt_ref, xs_ref, wg_ref, wu_ref, wd_ref, ys_ref, wgu_s, wd_s):
    t = pl.program_id(0)
    changed = (t == 0) | (te_ref[t] != te_ref[jnp.maximum(t - 1, 0)])

    @pl.when(changed)
    def _():
        wgu_s[:, 0:D_EXPERT] = wg_ref[...].astype(BF16)
        wgu_s[:, D_EXPERT:2 * D_EXPERT] = wu_ref[...].astype(BF16)
        wd_s[...] = wd_ref[...].astype(BF16)

    @pl.when(t < nt_ref[0])
    def _():
        gu = _dot(xs_ref[...].astype(BF16), wgu_s[...])
        gate = gu[:, 0:D_EXPERT]
        up = gu[:, D_EXPERT:2 * D_EXPERT]
        hid = (gate * (1.0 / (1.0 + jnp.exp(-gate)))) * up
        ys_ref[...] = _dot(hid.astype(BF16), wd_s[...])

    @pl.when(t >= nt_ref[0])
    def _():
        ys_ref[...] = jnp.zeros_like(ys_ref)


def _moe_experts(xs, tile_expert, n_tiles_used, w_gate, w_up, w_down, layer):
    rows = xs.shape[0]
    first = layer * N_EXPERTS
    grid_spec = pltpu.PrefetchScalarGridSpec(
        num_scalar_prefetch=2,
        grid=(rows // MOE_TILE,),
        in_specs=[
            pl.BlockSpec((MOE_TILE, D_MODEL), lambda t, te, nt: (jnp.minimum(t, nt[0] - 1), 0)),
            pl.BlockSpec((None, D_MODEL, D_EXPERT), lambda t, te, nt: (first + te[t], 0, 0)),
            pl.BlockSpec((None, D_MODEL, D_EXPERT), lambda t, te, nt: (first + te[t], 0, 0)),
            pl.BlockSpec((None, D_EXPERT, D_MODEL), lambda t, te, nt: (first + te[t], 0, 0)),
        ],
        out_specs=pl.BlockSpec((MOE_TILE, D_MODEL), lambda t, te, nt: (t, 0)),
        scratch_shapes=[
            pltpu.VMEM((D_MODEL, 2 * D_EXPERT), BF16),
            pltpu.VMEM((D_EXPERT, D_MODEL), BF16),
        ],
    )
    return pl.pallas_call(
        _experts_kernel,
        grid_spec=grid_spec,
        out_shape=jax.ShapeDtypeStruct((rows, D_MODEL), F32),
        compiler_params=pltpu.CompilerParams(
            dimension_semantics=("arbitrary",), vmem_limit_bytes=VMEM_LIMIT),
        name="moe_experts",
    )(tile_expert, n_tiles_used, xs, w_gate, w_up, w_down)


def _combine_kernel(d1_ref, d2_ref, x1_ref, info_ref, ys_hbm, g_ref, b_ref, out_ref, y1_ref, y2_ref, sem, *, tb):
    base = pl.program_id(0) * tb

    def issue(r, c):
        _row_copy(ys_hbm, d1_ref[base + r], y1_ref, r, sem).start(priority=0)
        _row_copy(ys_hbm, d2_ref[base + r], y2_ref, r, sem).start(priority=1)
        return c

    lax.fori_loop(0, tb, issue, 0, unroll=8)
    for y_ref in (y1_ref, y2_ref):
        pltpu.make_async_copy(ys_hbm.at[pl.ds(0, tb), :], y_ref, sem).wait()
    info = info_ref[...]
    lane = lax.broadcasted_iota(jnp.int32, info.shape, 1)
    w1 = jnp.sum(jnp.where(lane == ROUTE_W1, info, 0.0), axis=1, keepdims=True)
    w2 = jnp.sum(jnp.where(lane == ROUTE_W2, info, 0.0), axis=1, keepdims=True)
    ffn = w1 * y1_ref[...] + w2 * y2_ref[...]
    out_ref[...] = _layer_norm(ALPHA * x1_ref[...] + ffn, g_ref[...], b_ref[...])


def _moe_combine(x1, info, ys, dest1, dest2, ln_g, ln_b, tb=256):
    n = x1.shape[0]
    grid_spec = pltpu.PrefetchScalarGridSpec(
        num_scalar_prefetch=2,
        grid=(n // tb,),
        in_specs=[
            pl.BlockSpec((tb, D_MODEL), lambda i, d1, d2: (i, 0)),
            pl.BlockSpec((tb, ROUTER_LANES), lambda i, d1, d2: (i, 0)),
            pl.BlockSpec(memory_space=pl.ANY),
            pl.BlockSpec((1, D_MODEL), lambda i, d1, d2: (0, 0)),
            pl.BlockSpec((1, D_MODEL), lambda i, d1, d2: (0, 0)),
        ],
        out_specs=pl.BlockSpec((tb, D_MODEL), lambda i, d1, d2: (i, 0)),
        scratch_shapes=[
            pltpu.VMEM((tb, D_MODEL), F32),
            pltpu.VMEM((tb, D_MODEL), F32),
            pltpu.SemaphoreType.DMA(()),
        ],
    )
    return pl.pallas_call(
        functools.partial(_combine_kernel, tb=tb),
        grid_spec=grid_spec,
        out_shape=jax.ShapeDtypeStruct((n, D_MODEL), F32),
        compiler_params=pltpu.CompilerParams(
            dimension_semantics=("arbitrary",), vmem_limit_bytes=VMEM_LIMIT),
        name="moe_combine",
    )(dest1, dest2, x1, info, ys, ln_g, ln_b)


def _moe(x1, info, pick, w_gate, w_up, w_down, layer, ln_g, ln_b):
    n = x1.shape[0]
    dest, counts = _moe_rank(pick, info)
    dest1 = dest[:, 0]
    dest2 = dest[:, 1]
    tiles = jnp.ceil(counts[0, N_GROUPS:N_GROUPS + N_EXPERTS] * (1.0 / MOE_TILE)).astype(jnp.int32)
    last_tile = jnp.cumsum(tiles)
    tile_ids = jnp.arange(_moe_rows(n) // MOE_TILE, dtype=jnp.int32)
    tile_expert = jnp.minimum(jnp.sum(tile_ids[:, None] >= last_tile[None, :], axis=1), N_EXPERTS - 1)
    xs = _moe_dispatch(x1, dest1, dest2, last_tile)
    ys = _moe_experts(xs, tile_expert.astype(jnp.int32), last_tile[N_EXPERTS - 1:], w_gate, w_up, w_down, layer)
    return _moe_combine(x1, info, ys, dest1, dest2, ln_g, ln_b)


def _layer(x2d, batch, w_in, b_gate, rel_bias, w_br_sb, w_br_ca, w_out, ln1_g, ln1_b,
           w_group, b_group, w_erouter, b_erouter, w_gate, w_up, w_down, layer, ln2_g, ln2_b):
    n = x2d.shape[0]
    seq = n // batch
    scale = HEAD_DIM ** -0.5
    qscale = np.ones((1, D_QKV), np.float32)
    qscale[:, 0:W_SB] = scale * np.log2(np.e)
    qscale[:, 3 * W_SB:3 * W_SB + W_CA] = scale
    qkv, gates = _in_proj(x2d, w_in.astype(BF16), jnp.asarray(qscale), b_gate.reshape(1, 2 * D_MODEL))
    qkv = qkv.reshape(batch, seq, D_QKV)
    y_sb = _sb_attn(qkv).reshape(n, W_SB)
    y_ca = _ca_attn(qkv, _ca_bias_table(rel_bias)).reshape(n, W_CA)

    w_router = jnp.concatenate(
        [w_group, w_erouter.transpose(1, 0, 2).reshape(D_MODEL, N_EXPERTS)], axis=1)
    w_router = jnp.pad(w_router, ((0, 0), (0, ROUTER_LANES - N_GROUPS - N_EXPERTS)))
    b_router = jnp.pad(jnp.concatenate([b_group, b_erouter.reshape(N_EXPERTS)]),
                       (0, ROUTER_LANES - N_GROUPS - N_EXPERTS)).reshape(1, ROUTER_LANES)
    x1, info, pick = _post_attn(y_sb, y_ca, gates, x2d, w_br_sb.astype(BF16), w_br_ca.astype(BF16),
                        w_out.astype(BF16), w_router, b_router,
                        ln1_g.reshape(1, D_MODEL), ln1_b.reshape(1, D_MODEL))

    return _moe(x1, info, pick, w_gate, w_up, w_down, layer,
                ln2_g.reshape(1, D_MODEL), ln2_b.reshape(1, D_MODEL))


def kernel(x, w_in, b_gate, rel_bias, w_br_sb, w_br_ca, w_out, ln1_g, ln1_b, w_group, b_group,
           w_erouter, b_erouter, w_gate, w_up, w_down, ln2_g, ln2_b):
    batch, seq, d = x.shape
    h = x.reshape(batch * seq, d)
    w_gate = w_gate.reshape(DEPTH * N_EXPERTS, D_MODEL, D_EXPERT)
    w_up = w_up.reshape(DEPTH * N_EXPERTS, D_MODEL, D_EXPERT)
    w_down = w_down.reshape(DEPTH * N_EXPERTS, D_EXPERT, D_MODEL)
    for l in range(DEPTH):
        h = _layer(h, batch, w_in[l], b_gate[l], rel_bias[l], w_br_sb[l], w_br_ca[l], w_out[l],
                   ln1_g[l], ln1_b[l], w_group[l], b_group[l], w_erouter[l], b_erouter[l],
                   w_gate, w_up, w_down, l, ln2_g[l], ln2_b[l])
    return h.reshape(batch, seq, d)
```

```python
import functools

import jax
import jax.numpy as jnp
import numpy as np
from jax import lax
from jax.experimental import pallas as pl
from jax.experimental.pallas import tpu as pltpu

D_MODEL = 1024
DEPTH = 2
CHUNK = 64
HEAD_DIM = 64
H_SB = 8
H_CA = 8
W_SB = H_SB * HEAD_DIM
W_CA = H_CA * HEAD_DIM
N_PAST_CHUNKS = 8
REL_CLIP = 128
N_GROUPS = 4
EXPERTS_PER_GROUP = 8
N_EXPERTS = N_GROUPS * EXPERTS_PER_GROUP
D_EXPERT = 256
ALPHA = (2.0 * DEPTH) ** 0.25
LN_EPS = 1e-5
D_QKV = 3 * W_SB + 3 * W_CA
D_IN = D_QKV + 2 * D_MODEL
NEG_INF = -1e30

LANES = 128
HEADS_PER_TILE = LANES // HEAD_DIM
ROUTER_LANES = LANES
VMEM_LIMIT = 56 * 1024 * 1024

BF16 = jnp.bfloat16
F32 = jnp.float32

_NT = (((1,), (1,)), ((), ()))


def _dot(a, b):
    return jnp.dot(a, b, preferred_element_type=F32)


def _layer_norm(h, g, b):
    mu = jnp.mean(h, axis=-1, keepdims=True)
    hc = h - mu
    var = jnp.mean(hc * hc, axis=-1, keepdims=True)
    return hc * lax.rsqrt(var + LN_EPS) * g + b


def _split_bf16(a):
    hi = a.astype(BF16)
    lo = (a - hi.astype(F32)).astype(BF16)
    return hi, lo


def _in_proj_kernel(x_ref, w_ref, scale_ref, bg_ref, qkv_ref, gate_ref):
    xb = x_ref[...].astype(BF16)
    for c in range(D_QKV // D_MODEL):
        cols = slice(c * D_MODEL, (c + 1) * D_MODEL)
        acc = _dot(xb, w_ref[:, cols])
        qkv_ref[:, cols] = (acc * scale_ref[:, cols]).astype(BF16)
    for c in range(2):
        cols = slice(c * D_MODEL, (c + 1) * D_MODEL)
        wcols = slice(D_QKV + c * D_MODEL, D_QKV + (c + 1) * D_MODEL)
        logit = _dot(xb, w_ref[:, wcols]) + bg_ref[:, cols]
        gate_ref[:, cols] = 1.0 / (1.0 + jnp.exp(-logit))


def _in_proj(x2d, w_in_bf16, qscale, b_gate_row, tm=512):
    n = x2d.shape[0]
    return pl.pallas_call(
        _in_proj_kernel,
        grid=(n // tm,),
        in_specs=[
            pl.BlockSpec((tm, D_MODEL), lambda i: (i, 0)),
            pl.BlockSpec((D_MODEL, D_IN), lambda i: (0, 0)),
            pl.BlockSpec((1, D_QKV), lambda i: (0, 0)),
            pl.BlockSpec((1, 2 * D_MODEL), lambda i: (0, 0)),
        ],
        out_specs=[
            pl.BlockSpec((tm, D_QKV), lambda i: (i, 0)),
            pl.BlockSpec((tm, 2 * D_MODEL), lambda i: (i, 0)),
        ],
        out_shape=[
            jax.ShapeDtypeStruct((n, D_QKV), BF16),
            jax.ShapeDtypeStruct((n, 2 * D_MODEL), F32),
        ],
        compiler_params=pltpu.CompilerParams(
            dimension_semantics=("arbitrary",), vmem_limit_bytes=VMEM_LIMIT),
        name="in_proj",
    )(x2d, w_in_bf16, qscale, b_gate_row)


SB_DEAD_BITS = 160.0


def _sb_kernel(q_ref, k_ref, v_ref, u_ref, o_ref, z_ref, arg_ref, rs_ref, acc_ref, car_ref, *, tq):
    i = pl.program_id(2)
    q = q_ref[...]
    lane_q = lax.broadcasted_iota(jnp.int32, (tq, LANES), 1)
    zero_q = jnp.zeros_like(q)
    q_heads = (jnp.where(lane_q < HEAD_DIM, q, zero_q), jnp.where(lane_q < HEAD_DIM, zero_q, q))

    def block_rows(n):
        return pl.ds(pl.multiple_of(jnp.maximum(i - n, 0) * tq, tq), tq)

    def block_pair(n, diagonal):
        halves = [pl.ds(r * (tq // 2), tq // 2) for r in range(2)]
        for b in range(2):
            k = k_ref[block_rows(n + b), :]
            for h in range(HEADS_PER_TILE):
                for r, rows in enumerate(halves):
                    z = lax.dot_general(q_heads[h][r * (tq // 2):(r + 1) * (tq // 2)], k, _NT,
                                        preferred_element_type=F32)
                    if diagonal and b == 0:
                        row = lax.broadcasted_iota(jnp.int32, (tq // 2, tq), 0) + r * (tq // 2)
                        col = lax.broadcasted_iota(jnp.int32, (tq // 2, tq), 1)
                        z = jnp.where(col < row, z, NEG_INF)
                    z_ref[b, h, rows, :] = z
        for b in range(2):
            for h in range(HEADS_PER_TILE):
                for rows in halves:
                    z = z_ref[b, h, rows, :]
                    sp = jnp.maximum(z, 0.0) + jnp.log2(1.0 + jnp.exp2(-jnp.abs(z)))
                    sums = _dot(sp.astype(BF16), u_ref[...])
                    arg_ref[b, h, rows, :] = (z - sp) - sums[:, 0:tq]
                    rs_ref[b, h, rows, :] = sums[:, tq:tq + LANES]
        carries = [car_ref[...]]
        for b in range(2):
            carries.append(carries[b] + rs_ref[b])
        car_ref[...] = carries[2]
        least = jnp.min(carries[2])
        v_parts = []
        for b in range(2):
            v = v_ref[block_rows(n + b), :]
            zero_v = jnp.zeros_like(v)
            if b == 1:
                v = jnp.where(n + b <= i, v, zero_v)
            v_parts += [jnp.where(lane_q < HEAD_DIM, v, zero_v), jnp.where(lane_q < HEAD_DIM, zero_v, v)]
        v_all = jnp.concatenate(v_parts, axis=0)
        for r, rows in enumerate(halves):
            sl = slice(r * (tq // 2), (r + 1) * (tq // 2))
            ws = []
            for b in range(2):
                for h in range(HEADS_PER_TILE):
                    carry = carries[b][h, sl, :]
                    carry = jnp.concatenate([carry] * (tq // LANES), axis=1)
                    ws.append(jnp.exp2(arg_ref[b, h, rows, :] - carry).astype(BF16))
            acc_ref[rows, :] += _dot(jnp.concatenate(ws, axis=1), v_all)
        return least

    acc_ref[...] = jnp.zeros_like(acc_ref)
    car_ref[...] = jnp.zeros_like(car_ref)
    least = block_pair(0, True)

    def more(state):
        n, least = state
        return (n <= i) & (least < SB_DEAD_BITS)

    def body(state):
        n, _ = state
        return n + 2, block_pair(n, False)

    lax.while_loop(more, body, (jnp.int32(2), least))
    o_ref[...] = acc_ref[...].astype(o_ref.dtype)


def _sb_attn(qkv, tq=256):
    b, s, _ = qkv.shape
    n_tiles = W_SB // LANES
    u = jnp.asarray(np.concatenate([np.arange(tq)[:, None] > np.arange(tq)[None, :],
                                    np.ones((tq, LANES), bool)], axis=1), dtype=BF16)
    return pl.pallas_call(
        functools.partial(_sb_kernel, tq=tq),
        grid=(b, n_tiles, s // tq),
        in_specs=[
            pl.BlockSpec((None, tq, LANES), lambda bi, hp, i: (bi, i, hp)),
            pl.BlockSpec((None, s, LANES), lambda bi, hp, i: (bi, 0, n_tiles + hp)),
            pl.BlockSpec((None, s, LANES), lambda bi, hp, i: (bi, 0, 2 * n_tiles + hp)),
            pl.BlockSpec((tq, tq + LANES), lambda bi, hp, i: (0, 0)),
        ],
        out_specs=pl.BlockSpec((None, tq, LANES), lambda bi, hp, i: (bi, i, hp)),
        out_shape=jax.ShapeDtypeStruct((b, s, W_SB), BF16),
        scratch_shapes=[
            pltpu.VMEM((2, HEADS_PER_TILE, tq, tq), F32),
            pltpu.VMEM((2, HEADS_PER_TILE, tq, tq), F32),
            pltpu.VMEM((2, HEADS_PER_TILE, tq, LANES), F32),
            pltpu.VMEM((tq, LANES), F32),
            pltpu.VMEM((HEADS_PER_TILE, tq, LANES), F32),
        ],
        compiler_params=pltpu.CompilerParams(
            dimension_semantics=("arbitrary", "arbitrary", "arbitrary"), vmem_limit_bytes=VMEM_LIMIT),
        name="sb_attn",
    )(qkv, qkv, qkv, u)


CA_GROUP = 4
CA_TQ = CA_GROUP * CHUNK
CA_BAND = (CA_GROUP + N_PAST_CHUNKS) * CHUNK
CA_PAD = N_PAST_CHUNKS * CHUNK


def _ca_kernel(q_ref, k_ref, v_ref, bias_ref, o_ref, kp_ref, vp_ref):
    c = pl.program_id(2)
    s = k_ref.shape[0]

    @pl.when(c == 0)
    def _():
        kp_ref[0:CA_PAD, :] = jnp.zeros((CA_PAD, LANES), BF16)
        vp_ref[0:CA_PAD, :] = jnp.zeros((CA_PAD, LANES), BF16)
        kp_ref[CA_PAD:CA_PAD + s, :] = k_ref[...]
        vp_ref[CA_PAD:CA_PAD + s, :] = v_ref[...]

    start = pl.multiple_of(c * CA_TQ, CA_TQ)
    kb = kp_ref[pl.ds(start, CA_BAND), :]
    vb = vp_ref[pl.ds(start, CA_BAND), :]
    q = q_ref[...]
    lane_q = lax.broadcasted_iota(jnp.int32, (CA_TQ, LANES), 1)
    lane_v = lax.broadcasted_iota(jnp.int32, (CA_BAND, LANES), 1)
    zero_q = jnp.zeros_like(q)
    zero_v = jnp.zeros_like(vb)
    q_heads = (jnp.where(lane_q < HEAD_DIM, q, zero_q), jnp.where(lane_q < HEAD_DIM, zero_q, q))
    v_heads = jnp.concatenate(
        [jnp.where(lane_v < HEAD_DIM, vb, zero_v), jnp.where(lane_v < HEAD_DIM, zero_v, vb)], axis=0)
    pos = lax.broadcasted_iota(jnp.int32, (CA_TQ, CA_BAND), 1)
    exists = pos >= CA_PAD - c * CA_TQ
    es = []
    inv = []
    for h in range(HEADS_PER_TILE):
        sc = lax.dot_general(q_heads[h], kb, _NT, preferred_element_type=F32) + bias_ref[h]
        sc = jnp.where(exists, sc, NEG_INF)
        m = jnp.max(sc, axis=1, keepdims=True)
        e = jnp.exp(sc - m)
        inv.append(1.0 / jnp.sum(e, axis=1, keepdims=True))
        es.append(e.astype(BF16))
    out = _dot(jnp.concatenate(es, axis=1), v_heads)
    o_ref[...] = (out * jnp.where(lane_q < HEAD_DIM, inv[0], inv[1])).astype(o_ref.dtype)


def _ca_bias_table(rel_bias):
    h = rel_bias.shape[0]
    r = np.arange(CA_TQ)[:, None]
    p = np.arange(CA_BAND)[None, :]
    qc = r // CHUNK
    kc = p // CHUNK
    in_band = (kc >= qc) & (kc <= qc + N_PAST_CHUNKS)
    n_far = CA_PAD + CA_TQ - 1 - REL_CLIP
    n_neg = CA_BAND - 1 - CA_PAD - REL_CLIP
    rb = rel_bias.astype(F32)
    line = jnp.concatenate([jnp.broadcast_to(rb[:, 2 * REL_CLIP:], (h, n_far)), rb[:, ::-1],
                            jnp.broadcast_to(rb[:, :1], (h, n_neg))], axis=1)
    length = CA_TQ + CA_BAND - 1
    assert line.shape[1] == length
    line = jnp.pad(line, ((0, 0), (0, 1)))
    skew = jnp.tile(line, (1, CA_TQ))[:, :CA_TQ * length].reshape(h, CA_TQ, length)
    table = skew[:, :, CA_TQ - 1:]
    return jnp.where(jnp.asarray(in_band)[None], table, NEG_INF)


def _ca_attn(qkv, bias_table):
    b, s, _ = qkv.shape
    n_tiles = W_CA // LANES
    base = 3 * W_SB // LANES
    return pl.pallas_call(
        _ca_kernel,
        grid=(b, n_tiles, s // CA_TQ),
        in_specs=[
            pl.BlockSpec((None, CA_TQ, LANES), lambda bi, hp, c: (bi, c, base + hp)),
            pl.BlockSpec((None, s, LANES), lambda bi, hp, c: (bi, 0, base + n_tiles + hp)),
            pl.BlockSpec((None, s, LANES), lambda bi, hp, c: (bi, 0, base + 2 * n_tiles + hp)),
            pl.BlockSpec((HEADS_PER_TILE, CA_TQ, CA_BAND), lambda bi, hp, c: (hp, 0, 0)),
        ],
        out_specs=pl.BlockSpec((None, CA_TQ, LANES), lambda bi, hp, c: (bi, c, hp)),
        out_shape=jax.ShapeDtypeStruct((b, s, W_CA), BF16),
        scratch_shapes=[
            pltpu.VMEM((CA_PAD + s, LANES), BF16),
            pltpu.VMEM((CA_PAD + s, LANES), BF16),
        ],
        compiler_params=pltpu.CompilerParams(
            dimension_semantics=("arbitrary", "arbitrary", "arbitrary"), vmem_limit_bytes=VMEM_LIMIT),
        name="ca_attn",
    )(qkv, qkv, qkv, bias_table)


ROUTE_E1, ROUTE_E2, ROUTE_W1, ROUTE_W2 = 0, 1, 2, 3


def _route(lg):
    lane = lax.broadcasted_iota(jnp.int32, lg.shape, 1)
    big = jnp.int32(ROUTER_LANES)
    is_group = lane < N_GROUPS
    g_max = jnp.max(jnp.where(is_group, lg, -jnp.inf), axis=1, keepdims=True)
    g_idx = jnp.min(jnp.where(is_group & (lg == g_max), lane, big), axis=1, keepdims=True)
    g_den = jnp.sum(jnp.where(is_group, jnp.exp(lg - g_max), 0.0), axis=1, keepdims=True)
    g_val = 1.0 / g_den
    lo = N_GROUPS + EXPERTS_PER_GROUP * g_idx
    in_group = (lane >= lo) & (lane < lo + EXPERTS_PER_GROUP)
    v1 = jnp.max(jnp.where(in_group, lg, -jnp.inf), axis=1, keepdims=True)
    i1 = jnp.min(jnp.where(in_group & (lg == v1), lane, big), axis=1, keepdims=True)
    rest = in_group & (lane != i1)
    v2 = jnp.max(jnp.where(rest, lg, -jnp.inf), axis=1, keepdims=True)
    i2 = jnp.min(jnp.where(rest & (lg == v2), lane, big), axis=1, keepdims=True)
    e2 = jnp.exp(v2 - v1)
    w1 = g_val / (1.0 + e2)
    w2 = g_val * e2 / (1.0 + e2)
    picked = ((lane == i1) | (lane == i2)).astype(BF16)
    info = (jnp.where(lane == ROUTE_E1, (i1 - N_GROUPS).astype(F32), 0.0)
            + jnp.where(lane == ROUTE_E2, (i2 - N_GROUPS).astype(F32), 0.0)
            + jnp.where(lane == ROUTE_W1, w1, 0.0) + jnp.where(lane == ROUTE_W2, w2, 0.0))
    return info, picked


def _post_kernel(ysb_ref, yca_ref, gate_ref, x_ref, wsb_ref, wca_ref, wout_ref, wr_ref, br_ref,
                 g_ref, b_ref, x1_ref, info_ref, pick_ref):
    a = _dot(ysb_ref[...], wsb_ref[...])
    c = _dot(yca_ref[...], wca_ref[...])
    mix = gate_ref[:, 0:D_MODEL] * a + gate_ref[:, D_MODEL:2 * D_MODEL] * c
    mixed = _dot(mix.astype(BF16), wout_ref[...])
    x1 = _layer_norm(ALPHA * x_ref[...] + mixed, g_ref[...], b_ref[...])
    x1_ref[...] = x1
    x_hi, x_lo = _split_bf16(x1)
    w_hi, w_lo = _split_bf16(wr_ref[...])
    lg = _dot(jnp.concatenate([x_hi, x_lo, x_hi], axis=1),
              jnp.concatenate([w_hi, w_hi, w_lo], axis=0)) + br_ref[...]
    info_ref[...], pick_ref[...] = _route(lg)


def _post_attn(y_sb, y_ca, gates, x2d, w_br_sb, w_br_ca, w_out, w_router, b_router, ln_g, ln_b, tm=256):
    n = x2d.shape[0]
    row = lambda i: (i, 0)
    fixed = lambda i: (0, 0)
    return pl.pallas_call(
        _post_kernel,
        grid=(n // tm,),
        in_specs=[
            pl.BlockSpec((tm, W_SB), row),
            pl.BlockSpec((tm, W_CA), row),
            pl.BlockSpec((tm, 2 * D_MODEL), row),
            pl.BlockSpec((tm, D_MODEL), row),
            pl.BlockSpec((W_SB, D_MODEL), fixed),
            pl.BlockSpec((W_CA, D_MODEL), fixed),
            pl.BlockSpec((D_MODEL, D_MODEL), fixed),
            pl.BlockSpec((D_MODEL, ROUTER_LANES), fixed),
            pl.BlockSpec((1, ROUTER_LANES), fixed),
            pl.BlockSpec((1, D_MODEL), fixed),
            pl.BlockSpec((1, D_MODEL), fixed),
        ],
        out_specs=[
            pl.BlockSpec((tm, D_MODEL), row),
            pl.BlockSpec((tm, ROUTER_LANES), row),
            pl.BlockSpec((tm, ROUTER_LANES), row),
        ],
        out_shape=[
            jax.ShapeDtypeStruct((n, D_MODEL), F32),
            jax.ShapeDtypeStruct((n, ROUTER_LANES), F32),
            jax.ShapeDtypeStruct((n, ROUTER_LANES), BF16),
        ],
        compiler_params=pltpu.CompilerParams(
            dimension_semantics=("arbitrary",), vmem_limit_bytes=VMEM_LIMIT),
        name="post_attn",
    )(y_sb, y_ca, gates, x2d, w_br_sb, w_br_ca, w_out, w_router, b_router, ln_g, ln_b)


MOE_TILE = 256
TOP_K = 2


def _moe_rows(n):
    return n * TOP_K + N_EXPERTS * MOE_TILE


def _rank_kernel(pick_ref, info_ref, tri_ref, dest_ref, cnt_ref, run_ref, off_ref, *, tb):
    p = pl.program_id(0)
    i = pl.program_id(1)
    pick = pick_ref[...]

    @pl.when((p == 0) & (i == 0))
    def _():
        run_ref[...] = jnp.zeros_like(run_ref)

    @pl.when(p == 0)
    def _():
        run_ref[...] += jnp.sum(pick.astype(F32), axis=0, keepdims=True)

    @pl.when((p == 1) & (i == 0))
    def _():
        cnt = run_ref[...]
        cnt_ref[...] = cnt
        padded = jnp.ceil(cnt * (1.0 / MOE_TILE)) * MOE_TILE
        lane = lax.broadcasted_iota(jnp.int32, padded.shape, 1)
        scan = padded
        step = 1
        while step < ROUTER_LANES:
            scan = scan + jnp.where(lane >= step, pltpu.roll(scan, step, axis=1), 0.0)
            step *= 2
        off_ref[...] = scan - padded
        run_ref[...] = jnp.zeros_like(run_ref)

    @pl.when(p == 1)
    def _():
        seen = run_ref[0:1, :]
        earlier = _dot(tri_ref[...], pick)
        row_of = earlier + seen + off_ref[0:1, :]
        info = info_ref[...]
        lane = lax.broadcasted_iota(jnp.int32, info.shape, 1)
        lane_f = lane.astype(F32)
        e1 = jnp.sum(jnp.where(lane == ROUTE_E1, info, 0.0), axis=1, keepdims=True)
        e2 = jnp.sum(jnp.where(lane == ROUTE_E2, info, 0.0), axis=1, keepdims=True)
        d1 = jnp.sum(jnp.where(lane_f == e1 + N_GROUPS, row_of, 0.0), axis=1, keepdims=True)
        d2 = jnp.sum(jnp.where(lane_f == e2 + N_GROUPS, row_of, 0.0), axis=1, keepdims=True)
        dest_ref[...] = (jnp.where(lane == 0, d1, 0.0) + jnp.where(lane == 1, d2, 0.0)).astype(jnp.int32)
        run_ref[...] += jnp.sum(pick.astype(F32), axis=0, keepdims=True)


def _moe_rank(pick, info, tb=1024):
    n = pick.shape[0]
    tri = jnp.asarray(np.arange(tb)[None, :] < np.arange(tb)[:, None], dtype=BF16)
    return pl.pallas_call(
        functools.partial(_rank_kernel, tb=tb),
        grid=(2, n // tb),
        in_specs=[
            pl.BlockSpec((tb, ROUTER_LANES), lambda p, i: (i, 0)),
            pl.BlockSpec((tb, ROUTER_LANES), lambda p, i: (i, 0)),
            pl.BlockSpec((tb, tb), lambda p, i: (0, 0)),
        ],
        out_specs=[
            pl.BlockSpec((tb, ROUTER_LANES), lambda p, i: (i * p, 0)),
            pl.BlockSpec((8, ROUTER_LANES), lambda p, i: (0, 0)),
        ],
        out_shape=[
            jax.ShapeDtypeStruct((n, ROUTER_LANES), jnp.int32),
            jax.ShapeDtypeStruct((8, ROUTER_LANES), F32),
        ],
        scratch_shapes=[
            pltpu.VMEM((8, ROUTER_LANES), F32),
            pltpu.VMEM((8, ROUTER_LANES), F32),
        ],
        compiler_params=pltpu.CompilerParams(
            dimension_semantics=("arbitrary", "arbitrary"), vmem_limit_bytes=VMEM_LIMIT),
        name="moe_rank",
    )(pick, info, tri)


def _row_copy(src_ref, src_row, dst_ref, dst_row, sem):
    return pltpu.make_async_copy(src_ref.at[pl.ds(src_row, 1), :], dst_ref.at[pl.ds(dst_row, 1), :], sem)


def _dispatch_kernel(d1_ref, d2_ref, last_ref, x_ref, xs_hbm, zero_ref, sem, zsem, *, tb):
    base = pl.program_id(0) * tb

    @pl.when(pl.program_id(0) == 0)
    def _():
        zero_ref[...] = jnp.zeros_like(zero_ref)

        def last_tile_copy(e):
            first_row = pl.multiple_of((last_ref[e] - 1) * MOE_TILE, MOE_TILE)
            return pltpu.make_async_copy(zero_ref, xs_hbm.at[pl.ds(first_row, MOE_TILE), :], zsem)

        def unused_tile_copy(t):
            return pltpu.make_async_copy(zero_ref, xs_hbm.at[pl.ds(t * MOE_TILE, MOE_TILE), :], zsem)

        n_tiles = xs_hbm.shape[0] // MOE_TILE
        min_used = n_tiles - N_EXPERTS
        for wait in (False, True):
            for e in range(N_EXPERTS):
                owns_tiles = last_ref[e] > (last_ref[e - 1] if e else 0)
                pl.when(owns_tiles)(
                    lambda e=e, wait=wait: last_tile_copy(e).wait() if wait else last_tile_copy(e).start())
            for t in range(min_used, n_tiles):
                pl.when(t >= last_ref[N_EXPERTS - 1])(
                    lambda t=t, wait=wait: unused_tile_copy(t).wait() if wait else unused_tile_copy(t).start())

    def issue(r, c):
        _row_copy(x_ref, r, xs_hbm, d1_ref[base + r], sem).start(priority=0)
        _row_copy(x_ref, r, xs_hbm, d2_ref[base + r], sem).start(priority=1)
        return c

    lax.fori_loop(0, tb, issue, 0, unroll=8)
    for _ in range(TOP_K):
        pltpu.make_async_copy(x_ref, xs_hbm.at[pl.ds(0, tb), :], sem).wait()


def _moe_dispatch(x1, dest1, dest2, last_tile, tb=256):
    n = x1.shape[0]
    rows = _moe_rows(n)
    grid_spec = pltpu.PrefetchScalarGridSpec(
        num_scalar_prefetch=3,
        grid=(n // tb,),
        in_specs=[pl.BlockSpec((tb, D_MODEL), lambda i, d1, d2, last: (i, 0))],
        out_specs=pl.BlockSpec(memory_space=pl.ANY),
        scratch_shapes=[
            pltpu.VMEM((MOE_TILE, D_MODEL), F32),
            pltpu.SemaphoreType.DMA(()),
            pltpu.SemaphoreType.DMA(()),
        ],
    )
    return pl.pallas_call(
        functools.partial(_dispatch_kernel, tb=tb),
        grid_spec=grid_spec,
        out_shape=jax.ShapeDtypeStruct((rows, D_MODEL), F32),
        compiler_params=pltpu.CompilerParams(
            dimension_semantics=("arbitrary",), vmem_limit_bytes=VMEM_LIMIT),
        name="moe_dispatch",
    )(dest1, dest2, last_tile, x1)


def _experts_kernel(te_ref, nt_ref, xs_ref, wg_ref, wu_ref, wd_ref, ys_ref, wgu_s, wd_s):
    t = pl.program_id(0)
    changed = (t == 0) | (te_ref[t] != te_ref[jnp.maximum(t - 1, 0)])

    @pl.when(changed)
    def _():
        wgu_s[:, 0:D_EXPERT] = wg_ref[...].astype(BF16)
        wgu_s[:, D_EXPERT:2 * D_EXPERT] = wu_ref[...].astype(BF16)
        wd_s[...] = wd_ref[...].astype(BF16)

    @pl.when(t < nt_ref[0])
    def _():
        gu = _dot(xs_ref[...].astype(BF16), wgu_s[...])
        gate = gu[:, 0:D_EXPERT]
        up = gu[:, D_EXPERT:2 * D_EXPERT]
        hid = (gate * (1.0 / (1.0 + jnp.exp(-gate)))) * up
        ys_ref[...] = _dot(hid.astype(BF16), wd_s[...])

    @pl.when(t >= nt_ref[0])
    def _():
        ys_ref[...] = jnp.zeros_like(ys_ref)


def _moe_experts(xs, tile_expert, n_tiles_used, w_gate, w_up, w_down, layer):
    rows = xs.shape[0]
    first = layer * N_EXPERTS
    grid_spec = pltpu.PrefetchScalarGridSpec(
        num_scalar_prefetch=2,
        grid=(rows // MOE_TILE,),
        in_specs=[
            pl.BlockSpec((MOE_TILE, D_MODEL), lambda t, te, nt: (jnp.minimum(t, nt[0] - 1), 0)),
            pl.BlockSpec((None, D_MODEL, D_EXPERT), lambda t, te, nt: (first + te[t], 0, 0)),
            pl.BlockSpec((None, D_MODEL, D_EXPERT), lambda t, te, nt: (first + te[t], 0, 0)),
            pl.BlockSpec((None, D_EXPERT, D_MODEL), lambda t, te, nt: (first + te[t], 0, 0)),
        ],
        out_specs=pl.BlockSpec((MOE_TILE, D_MODEL), lambda t, te, nt: (t, 0)),
        scratch_shapes=[
            pltpu.VMEM((D_MODEL, 2 * D_EXPERT), BF16),
            pltpu.VMEM((D_EXPERT, D_MODEL), BF16),
        ],
    )
    return pl.pallas_call(
        _experts_kernel,
        grid_spec=grid_spec,
        out_shape=jax.ShapeDtypeStruct((rows, D_MODEL), F32),
        compiler_params=pltpu.CompilerParams(
            dimension_semantics=("arbitrary",), vmem_limit_bytes=VMEM_LIMIT),
        name="moe_experts",
    )(tile_expert, n_tiles_used, xs, w_gate, w_up, w_down)


def _combine_kernel(d1_ref, d2_ref, x1_ref, info_ref, ys_hbm, g_ref, b_ref, out_ref, y1_ref, y2_ref, sem, *, tb):
    base = pl.program_id(0) * tb

    def issue(r, c):
        _row_copy(ys_hbm, d1_ref[base + r], y1_ref, r, sem).start(priority=0)
        _row_copy(ys_hbm, d2_ref[base + r], y2_ref, r, sem).start(priority=1)
        return c

    lax.fori_loop(0, tb, issue, 0, unroll=8)
    for y_ref in (y1_ref, y2_ref):
        pltpu.make_async_copy(ys_hbm.at[pl.ds(0, tb), :], y_ref, sem).wait()
    info = info_ref[...]
    lane = lax.broadcasted_iota(jnp.int32, info.shape, 1)
    w1 = jnp.sum(jnp.where(lane == ROUTE_W1, info, 0.0), axis=1, keepdims=True)
    w2 = jnp.sum(jnp.where(lane == ROUTE_W2, info, 0.0), axis=1, keepdims=True)
    ffn = w1 * y1_ref[...] + w2 * y2_ref[...]
    out_ref[...] = _layer_norm(ALPHA * x1_ref[...] + ffn, g_ref[...], b_ref[...])


def _moe_combine(x1, info, ys, dest1, dest2, ln_g, ln_b, tb=256):
    n = x1.shape[0]
    grid_spec = pltpu.PrefetchScalarGridSpec(
        num_scalar_prefetch=2,
        grid=(n // tb,),
        in_specs=[
            pl.BlockSpec((tb, D_MODEL), lambda i, d1, d2: (i, 0)),
            pl.BlockSpec((tb, ROUTER_LANES), lambda i, d1, d2: (i, 0)),
            pl.BlockSpec(memory_space=pl.ANY),
            pl.BlockSpec((1, D_MODEL), lambda i, d1, d2: (0, 0)),
            pl.BlockSpec((1, D_MODEL), lambda i, d1, d2: (0, 0)),
        ],
        out_specs=pl.BlockSpec((tb, D_MODEL), lambda i, d1, d2: (i, 0)),
        scratch_shapes=[
            pltpu.VMEM((tb, D_MODEL), F32),
            pltpu.VMEM((tb, D_MODEL), F32),
            pltpu.SemaphoreType.DMA(()),
        ],
    )
    return pl.pallas_call(
        functools.partial(_combine_kernel, tb=tb),
        grid_spec=grid_spec,
        out_shape=jax.ShapeDtypeStruct((n, D_MODEL), F32),
        compiler_params=pltpu.CompilerParams(
            dimension_semantics=("arbitrary",), vmem_limit_bytes=VMEM_LIMIT),
        name="moe_combine",
    )(dest1, dest2, x1, info, ys, ln_g, ln_b)


def _moe(x1, info, pick, w_gate, w_up, w_down, layer, ln_g, ln_b):
    n = x1.shape[0]
    dest, counts = _moe_rank(pick, info)
    dest1 = dest[:, 0]
    dest2 = dest[:, 1]
    tiles = jnp.ceil(counts[0, N_GROUPS:N_GROUPS + N_EXPERTS] * (1.0 / MOE_TILE)).astype(jnp.int32)
    last_tile = jnp.cumsum(tiles)
    tile_ids = jnp.arange(_moe_rows(n) // MOE_TILE, dtype=jnp.int32)
    tile_expert = jnp.minimum(jnp.sum(tile_ids[:, None] >= last_tile[None, :], axis=1), N_EXPERTS - 1)
    xs = _moe_dispatch(x1, dest1, dest2, last_tile)
    ys = _moe_experts(xs, tile_expert.astype(jnp.int32), last_tile[N_EXPERTS - 1:], w_gate, w_up, w_down, layer)
    return _moe_combine(x1, info, ys, dest1, dest2, ln_g, ln_b)


def _layer(x2d, batch, w_in, b_gate, rel_bias, w_br_sb, w_br_ca, w_out, ln1_g, ln1_b,
           w_group, b_group, w_erouter, b_erouter, w_gate, w_up, w_down, layer, ln2_g, ln2_b):
    n = x2d.shape[0]
    seq = n // batch
    scale = HEAD_DIM ** -0.5
    qscale = np.ones((1, D_QKV), np.float32)
    qscale[:, 0:W_SB] = scale * np.log2(np.e)
    qscale[:, 3 * W_SB:3 * W_SB + W_CA] = scale
    qkv, gates = _in_proj(x2d, w_in.astype(BF16), jnp.asarray(qscale), b_gate.reshape(1, 2 * D_MODEL))
    qkv = qkv.reshape(batch, seq, D_QKV)
    y_sb = _sb_attn(qkv).reshape(n, W_SB)
    y_ca = _ca_attn(qkv, _ca_bias_table(rel_bias)).reshape(n, W_CA)

    w_router = jnp.concatenate(
        [w_group, w_erouter.transpose(1, 0, 2).reshape(D_MODEL, N_EXPERTS)], axis=1)
    w_router = jnp.pad(w_router, ((0, 0), (0, ROUTER_LANES - N_GROUPS - N_EXPERTS)))
    b_router = jnp.pad(jnp.concatenate([b_group, b_erouter.reshape(N_EXPERTS)]),
                       (0, ROUTER_LANES - N_GROUPS - N_EXPERTS)).reshape(1, ROUTER_LANES)
    x1, info, pick = _post_attn(y_sb, y_ca, gates, x2d, w_br_sb.astype(BF16), w_br_ca.astype(BF16),
                        w_out.astype(BF16), w_router, b_router,
                        ln1_g.reshape(1, D_MODEL), ln1_b.reshape(1, D_MODEL))

    return _moe(x1, info, pick, w_gate, w_up, w_down, layer,
                ln2_g.reshape(1, D_MODEL), ln2_b.reshape(1, D_MODEL))


def kernel(x, w_in, b_gate, rel_bias, w_br_sb, w_br_ca, w_out, ln1_g, ln1_b, w_group, b_group,
           w_erouter, b_erouter, w_gate, w_up, w_down, ln2_g, ln2_b):
    batch, seq, d = x.shape
    h = x.reshape(batch * seq, d)
    w_gate = w_gate.reshape(DEPTH * N_EXPERTS, D_MODEL, D_EXPERT)
    w_up = w_up.reshape(DEPTH * N_EXPERTS, D_MODEL, D_EXPERT)
    w_down = w_down.reshape(DEPTH * N_EXPERTS, D_EXPERT, D_MODEL)
    for l in range(DEPTH):
        h = _layer(h, batch, w_in[l], b_gate[l], rel_bias[l], w_br_sb[l], w_br_ca[l], w_out[l],
                   ln1_g[l], ln1_b[l], w_group[l], b_group[l], w_erouter[l], b_erouter[l],
                   w_gate, w_up, w_down, l, ln2_g[l], ln2_b[l])
    return h.reshape(batch, seq, d)
```

```python
import functools

import jax
import jax.numpy as jnp
import numpy as np
from jax import lax
from jax.experimental import pallas as pl
from jax.experimental.pallas import tpu as pltpu

D_MODEL = 1024
DEPTH = 2
CHUNK = 64
HEAD_DIM = 64
H_SB = 8
H_CA = 8
W_SB = H_SB * HEAD_DIM
W_CA = H_CA * HEAD_DIM
N_PAST_CHUNKS = 8
REL_CLIP = 128
N_GROUPS = 4
EXPERTS_PER_GROUP = 8
N_EXPERTS = N_GROUPS * EXPERTS_PER_GROUP
D_EXPERT = 256
ALPHA = (2.0 * DEPTH) ** 0.25
LN_EPS = 1e-5
D_QKV = 3 * W_SB + 3 * W_CA
D_IN = D_QKV + 2 * D_MODEL
NEG_INF = -1e30

LANES = 128
HEADS_PER_TILE = LANES // HEAD_DIM
ROUTER_LANES = LANES
VMEM_LIMIT = 56 * 1024 * 1024

BF16 = jnp.bfloat16
F32 = jnp.float32

_NT = (((1,), (1,)), ((), ()))


def _dot(a, b):
    return jnp.dot(a, b, preferred_element_type=F32)


def _layer_norm(h, g, b):
    mu = jnp.mean(h, axis=-1, keepdims=True)
    hc = h - mu
    var = jnp.mean(hc * hc, axis=-1, keepdims=True)
    return hc * lax.rsqrt(var + LN_EPS) * g + b


def _split_bf16(a):
    hi = a.astype(BF16)
    lo = (a - hi.astype(F32)).astype(BF16)
    return hi, lo


def _in_proj_kernel(x_ref, w_ref, scale_ref, bg_ref, qkv_ref, gate_ref):
    xb = x_ref[...].astype(BF16)
    for c in range(D_QKV // D_MODEL):
        cols = slice(c * D_MODEL, (c + 1) * D_MODEL)
        acc = _dot(xb, w_ref[:, cols])
        qkv_ref[:, cols] = (acc * scale_ref[:, cols]).astype(BF16)
    for c in range(2):
        cols = slice(c * D_MODEL, (c + 1) * D_MODEL)
        wcols = slice(D_QKV + c * D_MODEL, D_QKV + (c + 1) * D_MODEL)
        logit = _dot(xb, w_ref[:, wcols]) + bg_ref[:, cols]
        gate_ref[:, cols] = 1.0 / (1.0 + jnp.exp(-logit))


def _in_proj(x2d, w_in_bf16, qscale, b_gate_row, tm=512):
    n = x2d.shape[0]
    return pl.pallas_call(
        _in_proj_kernel,
        grid=(n // tm,),
        in_specs=[
            pl.BlockSpec((tm, D_MODEL), lambda i: (i, 0)),
            pl.BlockSpec((D_MODEL, D_IN), lambda i: (0, 0)),
            pl.BlockSpec((1, D_QKV), lambda i: (0, 0)),
            pl.BlockSpec((1, 2 * D_MODEL), lambda i: (0, 0)),
        ],
        out_specs=[
            pl.BlockSpec((tm, D_QKV), lambda i: (i, 0)),
            pl.BlockSpec((tm, 2 * D_MODEL), lambda i: (i, 0)),
        ],
        out_shape=[
            jax.ShapeDtypeStruct((n, D_QKV), BF16),
            jax.ShapeDtypeStruct((n, 2 * D_MODEL), F32),
        ],
        compiler_params=pltpu.CompilerParams(
            dimension_semantics=("arbitrary",), vmem_limit_bytes=VMEM_LIMIT),
        name="in_proj",
    )(x2d, w_in_bf16, qscale, b_gate_row)


SB_DEAD_BITS = 160.0
SB_ROW_PARTS = 1


def _sb_kernel(q_ref, k_ref, v_ref, u_ref, o_ref, z_ref, arg_ref, rs_ref, acc_ref, car_ref, *, tq):
    i = pl.program_id(2)
    q = q_ref[...]
    lane_q = lax.broadcasted_iota(jnp.int32, (tq, LANES), 1)
    zero_q = jnp.zeros_like(q)
    q_heads = (jnp.where(lane_q < HEAD_DIM, q, zero_q), jnp.where(lane_q < HEAD_DIM, zero_q, q))

    def block_rows(n):
        return pl.ds(pl.multiple_of(jnp.maximum(i - n, 0) * tq, tq), tq)

    def block_pair(n, diagonal):
        part = tq // SB_ROW_PARTS
        halves = [pl.ds(r * part, part) for r in range(SB_ROW_PARTS)]
        for b in range(2):
            k = k_ref[block_rows(n + b), :]
            for h in range(HEADS_PER_TILE):
                for r, rows in enumerate(halves):
                    z = lax.dot_general(q_heads[h][r * part:(r + 1) * part], k, _NT,
                                        preferred_element_type=F32)
                    if diagonal and b == 0:
                        row = lax.broadcasted_iota(jnp.int32, (part, tq), 0) + r * part
                        col = lax.broadcasted_iota(jnp.int32, (part, tq), 1)
                        z = jnp.where(col < row, z, NEG_INF)
                    z_ref[b, h, rows, :] = z
        for b in range(2):
            for h in range(HEADS_PER_TILE):
                for rows in halves:
                    z = z_ref[b, h, rows, :]
                    sp = jnp.maximum(z, 0.0) + jnp.log2(1.0 + jnp.exp2(-jnp.abs(z)))
                    sums = _dot(sp.astype(BF16), u_ref[...])
                    arg_ref[b, h, rows, :] = (z - sp) - sums[:, 0:tq]
                    rs_ref[b, h, rows, :] = sums[:, tq:tq + LANES]
        carries = [car_ref[...]]
        for b in range(2):
            carries.append(carries[b] + rs_ref[b])
        car_ref[...] = carries[2]
        least = jnp.min(carries[2])
        v_parts = []
        for b in range(2):
            v = v_ref[block_rows(n + b), :]
            zero_v = jnp.zeros_like(v)
            if b == 1:
                v = jnp.where(n + b <= i, v, zero_v)
            v_parts += [jnp.where(lane_q < HEAD_DIM, v, zero_v), jnp.where(lane_q < HEAD_DIM, zero_v, v)]
        v_all = jnp.concatenate(v_parts, axis=0)
        for r, rows in enumerate(halves):
            sl = slice(r * part, (r + 1) * part)
            ws = []
            for b in range(2):
                for h in range(HEADS_PER_TILE):
                    carry = carries[b][h, sl, :]
                    carry = jnp.concatenate([carry] * (tq // LANES), axis=1)
                    ws.append(jnp.exp2(arg_ref[b, h, rows, :] - carry).astype(BF16))
            acc_ref[rows, :] += _dot(jnp.concatenate(ws, axis=1), v_all)
        return least

    acc_ref[...] = jnp.zeros_like(acc_ref)
    car_ref[...] = jnp.zeros_like(car_ref)
    least = block_pair(0, True)

    def more(state):
        n, least = state
        return (n <= i) & (least < SB_DEAD_BITS)

    def body(state):
        n, _ = state
        return n + 2, block_pair(n, False)

    lax.while_loop(more, body, (jnp.int32(2), least))
    o_ref[...] = acc_ref[...].astype(o_ref.dtype)


def _sb_attn(qkv, tq=256):
    b, s, _ = qkv.shape
    n_tiles = W_SB // LANES
    u = jnp.asarray(np.concatenate([np.arange(tq)[:, None] > np.arange(tq)[None, :],
                                    np.ones((tq, LANES), bool)], axis=1), dtype=BF16)
    return pl.pallas_call(
        functools.partial(_sb_kernel, tq=tq),
        grid=(b, n_tiles, s // tq),
        in_specs=[
            pl.BlockSpec((None, tq, LANES), lambda bi, hp, i: (bi, i, hp)),
            pl.BlockSpec((None, s, LANES), lambda bi, hp, i: (bi, 0, n_tiles + hp)),
            pl.BlockSpec((None, s, LANES), lambda bi, hp, i: (bi, 0, 2 * n_tiles + hp)),
            pl.BlockSpec((tq, tq + LANES), lambda bi, hp, i: (0, 0)),
        ],
        out_specs=pl.BlockSpec((None, tq, LANES), lambda bi, hp, i: (bi, i, hp)),
        out_shape=jax.ShapeDtypeStruct((b, s, W_SB), BF16),
        scratch_shapes=[
            pltpu.VMEM((2, HEADS_PER_TILE, tq, tq), F32),
            pltpu.VMEM((2, HEADS_PER_TILE, tq, tq), F32),
            pltpu.VMEM((2, HEADS_PER_TILE, tq, LANES), F32),
            pltpu.VMEM((tq, LANES), F32),
            pltpu.VMEM((HEADS_PER_TILE, tq, LANES), F32),
        ],
        compiler_params=pltpu.CompilerParams(
            dimension_semantics=("arbitrary", "arbitrary", "arbitrary"), vmem_limit_bytes=VMEM_LIMIT),
        name="sb_attn",
    )(qkv, qkv, qkv, u)


CA_GROUP = 4
CA_STEP_GROUPS = 4
CA_TQ = CA_GROUP * CHUNK
CA_BAND = (CA_GROUP + N_PAST_CHUNKS) * CHUNK
CA_PAD = N_PAST_CHUNKS * CHUNK


def _ca_kernel(q_ref, k_ref, v_ref, bias_ref, ones_ref, o_ref, kp_ref, vp_ref):
    c = pl.program_id(2)
    s = k_ref.shape[0]

    @pl.when(c == 0)
    def _():
        kp_ref[0:CA_PAD, :] = jnp.zeros((CA_PAD, LANES), BF16)
        vp_ref[0:CA_PAD, :] = jnp.zeros((CA_PAD, LANES), BF16)
        kp_ref[CA_PAD:CA_PAD + s, :] = k_ref[...]
        vp_ref[CA_PAD:CA_PAD + s, :] = v_ref[...]

    lane_q = lax.broadcasted_iota(jnp.int32, (CA_TQ, LANES), 1)
    lane_v = lax.broadcasted_iota(jnp.int32, (CA_BAND, LANES), 1)

    def attend(g, masked):
        group = c * CA_STEP_GROUPS + g
        start = pl.multiple_of(group * CA_TQ, CA_TQ)
        rows = pl.ds(g * CA_TQ, CA_TQ)
        kb = kp_ref[pl.ds(start, CA_BAND), :]
        vb = vp_ref[pl.ds(start, CA_BAND), :]
        q = q_ref[rows, :]
        zero_q = jnp.zeros_like(q)
        zero_v = jnp.zeros_like(vb)
        q_heads = (jnp.where(lane_q < HEAD_DIM, q, zero_q), jnp.where(lane_q < HEAD_DIM, zero_q, q))
        v_heads = jnp.concatenate(
            [jnp.where(lane_v < HEAD_DIM, vb, zero_v), jnp.where(lane_v < HEAD_DIM, zero_v, vb)], axis=0)
        v_and_ones = jnp.concatenate([v_heads, ones_ref[...]], axis=1)
        es = []
        for h in range(HEADS_PER_TILE):
            sc = lax.dot_general(q_heads[h], kb, _NT, preferred_element_type=F32) + bias_ref[h]
            if masked:
                pos = lax.broadcasted_iota(jnp.int32, (CA_TQ, CA_BAND), 1)
                sc = jnp.where(pos >= CA_PAD - group * CA_TQ, sc, NEG_INF)
            m = jnp.max(sc, axis=1, keepdims=True)
            es.append(jnp.exp2(sc - m).astype(BF16))
        both = _dot(jnp.concatenate(es, axis=1), v_and_ones)
        o_ref[rows, :] = (both[:, 0:LANES] / both[:, LANES:2 * LANES]).astype(o_ref.dtype)

    def step(masked):
        for g in range(CA_STEP_GROUPS):
            attend(g, masked)

    assert CA_PAD // CA_TQ <= CA_STEP_GROUPS
    pl.when(c == 0)(lambda: step(True))
    pl.when(c > 0)(lambda: step(False))


def _ca_bias_table(rel_bias):
    h = rel_bias.shape[0]
    r = np.arange(CA_TQ)[:, None]
    p = np.arange(CA_BAND)[None, :]
    qc = r // CHUNK
    kc = p // CHUNK
    in_band = (kc >= qc) & (kc <= qc + N_PAST_CHUNKS)
    n_far = CA_PAD + CA_TQ - 1 - REL_CLIP
    n_neg = CA_BAND - 1 - CA_PAD - REL_CLIP
    rb = rel_bias.astype(F32)
    line = jnp.concatenate([jnp.broadcast_to(rb[:, 2 * REL_CLIP:], (h, n_far)), rb[:, ::-1],
                            jnp.broadcast_to(rb[:, :1], (h, n_neg))], axis=1)
    length = CA_TQ + CA_BAND - 1
    assert line.shape[1] == length
    line = jnp.pad(line, ((0, 0), (0, 1)))
    skew = jnp.tile(line, (1, CA_TQ))[:, :CA_TQ * length].reshape(h, CA_TQ, length)
    table = skew[:, :, CA_TQ - 1:]
    return jnp.where(jnp.asarray(in_band)[None], table * np.float32(np.log2(np.e)), NEG_INF)


def _ca_attn(qkv, bias_table):
    b, s, _ = qkv.shape
    n_tiles = W_CA // LANES
    base = 3 * W_SB // LANES
    step_rows = CA_STEP_GROUPS * CA_TQ
    head_of_row = np.arange(HEADS_PER_TILE * CA_BAND)[:, None] // CA_BAND
    head_of_lane = np.arange(LANES)[None, :] // HEAD_DIM
    ones = jnp.asarray(head_of_row == head_of_lane, dtype=BF16)
    return pl.pallas_call(
        _ca_kernel,
        grid=(b, n_tiles, s // step_rows),
        in_specs=[
            pl.BlockSpec((None, step_rows, LANES), lambda bi, hp, c: (bi, c, base + hp)),
            pl.BlockSpec((None, s, LANES), lambda bi, hp, c: (bi, 0, base + n_tiles + hp)),
            pl.BlockSpec((None, s, LANES), lambda bi, hp, c: (bi, 0, base + 2 * n_tiles + hp)),
            pl.BlockSpec((HEADS_PER_TILE, CA_TQ, CA_BAND), lambda bi, hp, c: (hp, 0, 0)),
            pl.BlockSpec((HEADS_PER_TILE * CA_BAND, LANES), lambda bi, hp, c: (0, 0)),
        ],
        out_specs=pl.BlockSpec((None, step_rows, LANES), lambda bi, hp, c: (bi, c, hp)),
        out_shape=jax.ShapeDtypeStruct((b, s, W_CA), BF16),
        scratch_shapes=[
            pltpu.VMEM((CA_PAD + s, LANES), BF16),
            pltpu.VMEM((CA_PAD + s, LANES), BF16),
        ],
        compiler_params=pltpu.CompilerParams(
            dimension_semantics=("arbitrary", "arbitrary", "arbitrary"), vmem_limit_bytes=VMEM_LIMIT),
        name="ca_attn",
    )(qkv, qkv, qkv, bias_table, ones)


ROUTE_E1, ROUTE_E2, ROUTE_W1, ROUTE_W2 = 0, 1, 2, 3


def _route(lg):
    lane = lax.broadcasted_iota(jnp.int32, lg.shape, 1)
    big = jnp.int32(ROUTER_LANES)
    is_group = lane < N_GROUPS
    g_max = jnp.max(jnp.where(is_group, lg, -jnp.inf), axis=1, keepdims=True)
    g_idx = jnp.min(jnp.where(is_group & (lg == g_max), lane, big), axis=1, keepdims=True)
    g_den = jnp.sum(jnp.where(is_group, jnp.exp(lg - g_max), 0.0), axis=1, keepdims=True)
    g_val = 1.0 / g_den
    lo = N_GROUPS + EXPERTS_PER_GROUP * g_idx
    in_group = (lane >= lo) & (lane < lo + EXPERTS_PER_GROUP)
    v1 = jnp.max(jnp.where(in_group, lg, -jnp.inf), axis=1, keepdims=True)
    i1 = jnp.min(jnp.where(in_group & (lg == v1), lane, big), axis=1, keepdims=True)
    rest = in_group & (lane != i1)
    v2 = jnp.max(jnp.where(rest, lg, -jnp.inf), axis=1, keepdims=True)
    i2 = jnp.min(jnp.where(rest & (lg == v2), lane, big), axis=1, keepdims=True)
    e2 = jnp.exp(v2 - v1)
    w1 = g_val / (1.0 + e2)
    w2 = g_val * e2 / (1.0 + e2)
    picked = ((lane == i1) | (lane == i2)).astype(BF16)
    info = (jnp.where(lane == ROUTE_E1, (i1 - N_GROUPS).astype(F32), 0.0)
            + jnp.where(lane == ROUTE_E2, (i2 - N_GROUPS).astype(F32), 0.0)
            + jnp.where(lane == ROUTE_W1, w1, 0.0) + jnp.where(lane == ROUTE_W2, w2, 0.0))
    return info, picked


def _post_kernel(ysb_ref, yca_ref, gate_ref, x_ref, wsb_ref, wca_ref, wout_ref, wr_ref, br_ref,
                 g_ref, b_ref, x1_ref, info_ref, pick_ref):
    w_hi, w_lo = _split_bf16(wr_ref[...])
    w_split = jnp.concatenate([w_hi, w_lo], axis=1)
    tm = x_ref.shape[0]
    for rows in [pl.ds(s * POST_SUB, POST_SUB) for s in range(tm // POST_SUB)]:
        a = _dot(ysb_ref[rows, :], wsb_ref[...])
        c = _dot(yca_ref[rows, :], wca_ref[...])
        mix = gate_ref[rows, 0:D_MODEL] * a + gate_ref[rows, D_MODEL:2 * D_MODEL] * c
        mixed = _dot(mix.astype(BF16), wout_ref[...])
        x1 = _layer_norm(ALPHA * x_ref[rows, :] + mixed, g_ref[...], b_ref[...])
        x1_ref[rows, :] = x1
        x_hi, x_lo = _split_bf16(x1)
        parts = _dot(jnp.concatenate([x_hi, x_lo], axis=0), w_split)
        lg = (parts[0:POST_SUB, 0:ROUTER_LANES] + parts[0:POST_SUB, ROUTER_LANES:]
              + parts[POST_SUB:, 0:ROUTER_LANES] + parts[POST_SUB:, ROUTER_LANES:]) + br_ref[...]
        info_ref[rows, :], pick_ref[rows, :] = _route(lg)


POST_SUB = 256


def _post_attn(y_sb, y_ca, gates, x2d, w_br_sb, w_br_ca, w_out, w_router, b_router, ln_g, ln_b, tm=512):
    n = x2d.shape[0]
    row = lambda i: (i, 0)
    fixed = lambda i: (0, 0)
    return pl.pallas_call(
        _post_kernel,
        grid=(n // tm,),
        in_specs=[
            pl.BlockSpec((tm, W_SB), row),
            pl.BlockSpec((tm, W_CA), row),
            pl.BlockSpec((tm, 2 * D_MODEL), row),
            pl.BlockSpec((tm, D_MODEL), row),
            pl.BlockSpec((W_SB, D_MODEL), fixed),
            pl.BlockSpec((W_CA, D_MODEL), fixed),
            pl.BlockSpec((D_MODEL, D_MODEL), fixed),
            pl.BlockSpec((D_MODEL, ROUTER_LANES), fixed),
            pl.BlockSpec((1, ROUTER_LANES), fixed),
            pl.BlockSpec((1, D_MODEL), fixed),
            pl.BlockSpec((1, D_MODEL), fixed),
        ],
        out_specs=[
            pl.BlockSpec((tm, D_MODEL), row),
            pl.BlockSpec((tm, ROUTER_LANES), row),
            pl.BlockSpec((tm, ROUTER_LANES), row),
        ],
        out_shape=[
            jax.ShapeDtypeStruct((n, D_MODEL), F32),
            jax.ShapeDtypeStruct((n, ROUTER_LANES), F32),
            jax.ShapeDtypeStruct((n, ROUTER_LANES), BF16),
        ],
        compiler_params=pltpu.CompilerParams(
            dimension_semantics=("arbitrary",), vmem_limit_bytes=VMEM_LIMIT),
        name="post_attn",
    )(y_sb, y_ca, gates, x2d, w_br_sb, w_br_ca, w_out, w_router, b_router, ln_g, ln_b)


MOE_TILE = 256
TOP_K = 2


def _moe_rows(n):
    return n * TOP_K + N_EXPERTS * MOE_TILE


def _rank_kernel(pick_ref, info_ref, tri_ref, dest_ref, cnt_ref, run_ref, off_ref, *, tb):
    p = pl.program_id(0)
    i = pl.program_id(1)
    pick = pick_ref[...]

    @pl.when((p == 0) & (i == 0))
    def _():
        run_ref[...] = jnp.zeros_like(run_ref)

    @pl.when(p == 0)
    def _():
        run_ref[...] += jnp.sum(pick.astype(F32), axis=0, keepdims=True)

    @pl.when((p == 1) & (i == 0))
    def _():
        cnt = run_ref[...]
        cnt_ref[...] = cnt
        padded = jnp.ceil(cnt * (1.0 / MOE_TILE)) * MOE_TILE
        lane = lax.broadcasted_iota(jnp.int32, padded.shape, 1)
        scan = padded
        step = 1
        while step < ROUTER_LANES:
            scan = scan + jnp.where(lane >= step, pltpu.roll(scan, step, axis=1), 0.0)
            step *= 2
        off_ref[...] = scan - padded
        run_ref[...] = jnp.zeros_like(run_ref)

    @pl.when(p == 1)
    def _():
        seen = run_ref[0:1, :]
        earlier = _dot(tri_ref[...], pick)
        row_of = earlier + seen + off_ref[0:1, :]
        info = info_ref[...]
        lane = lax.broadcasted_iota(jnp.int32, info.shape, 1)
        lane_f = lane.astype(F32)
        e1 = jnp.sum(jnp.where(lane == ROUTE_E1, info, 0.0), axis=1, keepdims=True)
        e2 = jnp.sum(jnp.where(lane == ROUTE_E2, info, 0.0), axis=1, keepdims=True)
        d1 = jnp.sum(jnp.where(lane_f == e1 + N_GROUPS, row_of, 0.0), axis=1, keepdims=True)
        d2 = jnp.sum(jnp.where(lane_f == e2 + N_GROUPS, row_of, 0.0), axis=1, keepdims=True)
        dest_ref[...] = (jnp.where(lane == 0, d1, 0.0) + jnp.where(lane == 1, d2, 0.0)).astype(jnp.int32)
        run_ref[...] += jnp.sum(pick.astype(F32), axis=0, keepdims=True)


def _moe_rank(pick, info, tb=1024):
    n = pick.shape[0]
    tri = jnp.asarray(np.arange(tb)[None, :] < np.arange(tb)[:, None], dtype=BF16)
    return pl.pallas_call(
        functools.partial(_rank_kernel, tb=tb),
        grid=(2, n // tb),
        in_specs=[
            pl.BlockSpec((tb, ROUTER_LANES), lambda p, i: (i, 0)),
            pl.BlockSpec((tb, ROUTER_LANES), lambda p, i: (i, 0)),
            pl.BlockSpec((tb, tb), lambda p, i: (0, 0)),
        ],
        out_specs=[
            pl.BlockSpec((tb, ROUTER_LANES), lambda p, i: (i * p, 0)),
            pl.BlockSpec((8, ROUTER_LANES), lambda p, i: (0, 0)),
        ],
        out_shape=[
            jax.ShapeDtypeStruct((n, ROUTER_LANES), jnp.int32),
            jax.ShapeDtypeStruct((8, ROUTER_LANES), F32),
        ],
        scratch_shapes=[
            pltpu.VMEM((8, ROUTER_LANES), F32),
            pltpu.VMEM((8, ROUTER_LANES), F32),
        ],
        compiler_params=pltpu.CompilerParams(
            dimension_semantics=("arbitrary", "arbitrary"), vmem_limit_bytes=VMEM_LIMIT),
        name="moe_rank",
    )(pick, info, tri)


def _row_copy(src_ref, src_row, dst_ref, dst_row, sem):
    return pltpu.make_async_copy(src_ref.at[pl.ds(src_row, 1), :], dst_ref.at[pl.ds(dst_row, 1), :], sem)


def _dispatch_kernel(d1_ref, d2_ref, last_ref, x_ref, xs_hbm, zero_ref, sem, zsem, *, tb):
    base = pl.program_id(0) * tb

    @pl.when(pl.program_id(0) == 0)
    def _():
        zero_ref[...] = jnp.zeros_like(zero_ref)

        def last_tile_copy(e):
            first_row = pl.multiple_of((last_ref[e] - 1) * MOE_TILE, MOE_TILE)
            return pltpu.make_async_copy(zero_ref, xs_hbm.at[pl.ds(first_row, MOE_TILE), :], zsem)

        def unused_tile_copy(t):
            return pltpu.make_async_copy(zero_ref, xs_hbm.at[pl.ds(t * MOE_TILE, MOE_TILE), :], zsem)

        n_tiles = xs_hbm.shape[0] // MOE_TILE
        min_used = n_tiles - N_EXPERTS
        for wait in (False, True):
            for e in range(N_EXPERTS):
                owns_tiles = last_ref[e] > (last_ref[e - 1] if e else 0)
                pl.when(owns_tiles)(
                    lambda e=e, wait=wait: last_tile_copy(e).wait() if wait else last_tile_copy(e).start())
            for t in range(min_used, n_tiles):
                pl.when(t >= last_ref[N_EXPERTS - 1])(
                    lambda t=t, wait=wait: unused_tile_copy(t).wait() if wait else unused_tile_copy(t).start())

    def issue(r, c):
        _row_copy(x_ref, r, xs_hbm, d1_ref[base + r], sem).start(priority=0)
        _row_copy(x_ref, r, xs_hbm, d2_ref[base + r], sem).start(priority=1)
        return c

    lax.fori_loop(0, tb, issue, 0, unroll=8)
    for _ in range(TOP_K):
        pltpu.make_async_copy(x_ref, xs_hbm.at[pl.ds(0, tb), :], sem).wait()


def _moe_dispatch(x1, dest1, dest2, last_tile, tb=256):
    n = x1.shape[0]
    rows = _moe_rows(n)
    grid_spec = pltpu.PrefetchScalarGridSpec(
        num_scalar_prefetch=3,
        grid=(n // tb,),
        in_specs=[pl.BlockSpec((tb, D_MODEL), lambda i, d1, d2, last: (i, 0))],
        out_specs=pl.BlockSpec(memory_space=pl.ANY),
        scratch_shapes=[
            pltpu.VMEM((MOE_TILE, D_MODEL), F32),
            pltpu.SemaphoreType.DMA(()),
            pltpu.SemaphoreType.DMA(()),
        ],
    )
    return pl.pallas_call(
        functools.partial(_dispatch_kernel, tb=tb),
        grid_spec=grid_spec,
        out_shape=jax.ShapeDtypeStruct((rows, D_MODEL), F32),
        compiler_params=pltpu.CompilerParams(
            dimension_semantics=("arbitrary",), vmem_limit_bytes=VMEM_LIMIT),
        name="moe_dispatch",
    )(dest1, dest2, last_tile, x1)


def _experts_kernel(te_ref, nt_ref, xs_ref, wg_ref, wu_ref, wd_ref, ys_ref, wgu_s, wd_s):
    t = pl.program_id(0)
    changed = (t == 0) | (te_ref[t] != te_ref[jnp.maximum(t - 1, 0)])

    @pl.when(changed)
    def _():
        wgu_s[:, 0:D_EXPERT] = wg_ref[...].astype(BF16)
        wgu_s[:, D_EXPERT:2 * D_EXPERT] = wu_ref[...].astype(BF16)
        wd_s[...] = wd_ref[...].astype(BF16)

    @pl.when(t < nt_ref[0])
    def _():
        gu = _dot(xs_ref[...].astype(BF16), wgu_s[...])
        gate = gu[:, 0:D_EXPERT]
        up = gu[:, D_EXPERT:2 * D_EXPERT]
        hid = (gate * (1.0 / (1.0 + jnp.exp(-gate)))) * up
        ys_ref[...] = _dot(hid.astype(BF16), wd_s[...])

    @pl.when(t >= nt_ref[0])
    def _():
        ys_ref[...] = jnp.zeros_like(ys_ref)


def _moe_experts(xs, tile_expert, n_tiles_used, w_gate, w_up, w_down, layer):
    rows = xs.shape[0]
    first = layer * N_EXPERTS
    grid_spec = pltpu.PrefetchScalarGridSpec(
        num_scalar_prefetch=2,
        grid=(rows // MOE_TILE,),
        in_specs=[
            pl.BlockSpec((MOE_TILE, D_MODEL), lambda t, te, nt: (jnp.minimum(t, nt[0] - 1), 0)),
            pl.BlockSpec((None, D_MODEL, D_EXPERT), lambda t, te, nt: (first + te[t], 0, 0)),
            pl.BlockSpec((None, D_MODEL, D_EXPERT), lambda t, te, nt: (first + te[t], 0, 0)),
            pl.BlockSpec((None, D_EXPERT, D_MODEL), lambda t, te, nt: (first + te[t], 0, 0)),
        ],
        out_specs=pl.BlockSpec((MOE_TILE, D_MODEL), lambda t, te, nt: (t, 0)),
        scratch_shapes=[
            pltpu.VMEM((D_MODEL, 2 * D_EXPERT), BF16),
            pltpu.VMEM((D_EXPERT, D_MODEL), BF16),
        ],
    )
    return pl.pallas_call(
        _experts_kernel,
        grid_spec=grid_spec,
        out_shape=jax.ShapeDtypeStruct((rows, D_MODEL), F32),
        compiler_params=pltpu.CompilerParams(
            dimension_semantics=("arbitrary",), vmem_limit_bytes=VMEM_LIMIT),
        name="moe_experts",
    )(tile_expert, n_tiles_used, xs, w_gate, w_up, w_down)


def _combine_kernel(d1_ref, d2_ref, x1_ref, info_ref, ys_hbm, g_ref, b_ref, out_ref, y1_ref, y2_ref, sem, *, tb):
    base = pl.program_id(0) * tb

    def issue(r, c):
        _row_copy(ys_hbm, d1_ref[base + r], y1_ref, r, sem).start(priority=0)
        _row_copy(ys_hbm, d2_ref[base + r], y2_ref, r, sem).start(priority=1)
        return c

    lax.fori_loop(0, tb, issue, 0, unroll=8)
    for y_ref in (y1_ref, y2_ref):
        pltpu.make_async_copy(ys_hbm.at[pl.ds(0, tb), :], y_ref, sem).wait()
    info = info_ref[...]
    lane = lax.broadcasted_iota(jnp.int32, info.shape, 1)
    w1 = jnp.sum(jnp.where(lane == ROUTE_W1, info, 0.0), axis=1, keepdims=True)
    w2 = jnp.sum(jnp.where(lane == ROUTE_W2, info, 0.0), axis=1, keepdims=True)
    ffn = w1 * y1_ref[...] + w2 * y2_ref[...]
    out_ref[...] = _layer_norm(ALPHA * x1_ref[...] + ffn, g_ref[...], b_ref[...])


def _moe_combine(x1, info, ys, dest1, dest2, ln_g, ln_b, tb=256):
    n = x1.shape[0]
    grid_spec = pltpu.PrefetchScalarGridSpec(
        num_scalar_prefetch=2,
        grid=(n // tb,),
        in_specs=[
            pl.BlockSpec((tb, D_MODEL), lambda i, d1, d2: (i, 0)),
            pl.BlockSpec((tb, ROUTER_LANES), lambda i, d1, d2: (i, 0)),
            pl.BlockSpec(memory_space=pl.ANY),
            pl.BlockSpec((1, D_MODEL), lambda i, d1, d2: (0, 0)),
            pl.BlockSpec((1, D_MODEL), lambda i, d1, d2: (0, 0)),
        ],
        out_specs=pl.BlockSpec((tb, D_MODEL), lambda i, d1, d2: (i, 0)),
        scratch_shapes=[
            pltpu.VMEM((tb, D_MODEL), F32),
            pltpu.VMEM((tb, D_MODEL), F32),
            pltpu.SemaphoreType.DMA(()),
        ],
    )
    return pl.pallas_call(
        functools.partial(_combine_kernel, tb=tb),
        grid_spec=grid_spec,
        out_shape=jax.ShapeDtypeStruct((n, D_MODEL), F32),
        compiler_params=pltpu.CompilerParams(
            dimension_semantics=("arbitrary",), vmem_limit_bytes=VMEM_LIMIT),
        name="moe_combine",
    )(dest1, dest2, x1, info, ys, ln_g, ln_b)


def _moe(x1, info, pick, w_gate, w_up, w_down, layer, ln_g, ln_b):
    n = x1.shape[0]
    dest, counts = _moe_rank(pick, info)
    dest1 = dest[:, 0]
    dest2 = dest[:, 1]
    tiles = jnp.ceil(counts[0, N_GROUPS:N_GROUPS + N_EXPERTS] * (1.0 / MOE_TILE)).astype(jnp.int32)
    last_tile = jnp.cumsum(tiles)
    tile_ids = jnp.arange(_moe_rows(n) // MOE_TILE, dtype=jnp.int32)
    tile_expert = jnp.minimum(jnp.sum(tile_ids[:, None] >= last_tile[None, :], axis=1), N_EXPERTS - 1)
    xs = _moe_dispatch(x1, dest1, dest2, last_tile)
    ys = _moe_experts(xs, tile_expert.astype(jnp.int32), last_tile[N_EXPERTS - 1:], w_gate, w_up, w_down, layer)
    return _moe_combine(x1, info, ys, dest1, dest2, ln_g, ln_b)


def _layer(x2d, batch, w_in, b_gate, rel_bias, w_br_sb, w_br_ca, w_out, ln1_g, ln1_b,
           w_group, b_group, w_erouter, b_erouter, w_gate, w_up, w_down, layer, ln2_g, ln2_b):
    n = x2d.shape[0]
    seq = n // batch
    scale = HEAD_DIM ** -0.5
    qscale = np.ones((1, D_QKV), np.float32)
    qscale[:, 0:W_SB] = scale * np.log2(np.e)
    qscale[:, 3 * W_SB:3 * W_SB + W_CA] = scale * np.log2(np.e)
    qkv, gates = _in_proj(x2d, w_in.astype(BF16), jnp.asarray(qscale), b_gate.reshape(1, 2 * D_MODEL))
    qkv = qkv.reshape(batch, seq, D_QKV)
    y_sb = _sb_attn(qkv).reshape(n, W_SB)
    y_ca = _ca_attn(qkv, _ca_bias_table(rel_bias)).reshape(n, W_CA)

    w_router = jnp.concatenate(
        [w_group, w_erouter.transpose(1, 0, 2).reshape(D_MODEL, N_EXPERTS)], axis=1)
    w_router = jnp.pad(w_router, ((0, 0), (0, ROUTER_LANES - N_GROUPS - N_EXPERTS)))
    b_router = jnp.pad(jnp.concatenate([b_group, b_erouter.reshape(N_EXPERTS)]),
                       (0, ROUTER_LANES - N_GROUPS - N_EXPERTS)).reshape(1, ROUTER_LANES)
    x1, info, pick = _post_attn(y_sb, y_ca, gates, x2d, w_br_sb.astype(BF16), w_br_ca.astype(BF16),
                        w_out.astype(BF16), w_router, b_router,
                        ln1_g.reshape(1, D_MODEL), ln1_b.reshape(1, D_MODEL))

    return _moe(x1, info, pick, w_gate, w_up, w_down, layer,
                ln2_g.reshape(1, D_MODEL), ln2_b.reshape(1, D_MODEL))


def kernel(x, w_in, b_gate, rel_bias, w_br_sb, w_br_ca, w_out, ln1_g, ln1_b, w_group, b_group,
           w_erouter, b_erouter, w_gate, w_up, w_down, ln2_g, ln2_b):
    batch, seq, d = x.shape
    h = x.reshape(batch * seq, d)
    w_gate = w_gate.reshape(DEPTH * N_EXPERTS, D_MODEL, D_EXPERT)
    w_up = w_up.reshape(DEPTH * N_EXPERTS, D_MODEL, D_EXPERT)
    w_down = w_down.reshape(DEPTH * N_EXPERTS, D_EXPERT, D_MODEL)
    for l in range(DEPTH):
        h = _layer(h, batch, w_in[l], b_gate[l], rel_bias[l], w_br_sb[l], w_br_ca[l], w_out[l],
                   ln1_g[l], ln1_b[l], w_group[l], b_group[l], w_erouter[l], b_erouter[l],
                   w_gate, w_up, w_down, l, ln2_g[l], ln2_b[l])
    return h.reshape(batch, seq, d)
```

```python
import functools

import jax
import jax.numpy as jnp
import numpy as np
from jax import lax
from jax.experimental import pallas as pl
from jax.experimental.pallas import tpu as pltpu

D_MODEL = 1024
DEPTH = 2
CHUNK = 64
HEAD_DIM = 64
H_SB = 8
H_CA = 8
W_SB = H_SB * HEAD_DIM
W_CA = H_CA * HEAD_DIM
N_PAST_CHUNKS = 8
REL_CLIP = 128
N_GROUPS = 4
EXPERTS_PER_GROUP = 8
N_EXPERTS = N_GROUPS * EXPERTS_PER_GROUP
D_EXPERT = 256
ALPHA = (2.0 * DEPTH) ** 0.25
LN_EPS = 1e-5
D_QKV = 3 * W_SB + 3 * W_CA
D_IN = D_QKV + 2 * D_MODEL
NEG_INF = -1e30

LANES = 128
HEADS_PER_TILE = LANES // HEAD_DIM
ROUTER_LANES = LANES
VMEM_LIMIT = 56 * 1024 * 1024

BF16 = jnp.bfloat16
F32 = jnp.float32

_NT = (((1,), (1,)), ((), ()))


def _dot(a, b):
    return jnp.dot(a, b, preferred_element_type=F32)


def _layer_norm(h, g, b):
    mu = jnp.mean(h, axis=-1, keepdims=True)
    hc = h - mu
    var = jnp.mean(hc * hc, axis=-1, keepdims=True)
    return hc * lax.rsqrt(var + LN_EPS) * g + b


def _split_bf16(a):
    hi = a.astype(BF16)
    lo = (a - hi.astype(F32)).astype(BF16)
    return hi, lo


def _in_proj_kernel(x_ref, w_ref, scale_ref, bg_ref, qkv_ref, gate_ref):
    xb = x_ref[...].astype(BF16)
    for c in range(D_QKV // D_MODEL):
        cols = slice(c * D_MODEL, (c + 1) * D_MODEL)
        acc = _dot(xb, w_ref[:, cols])
        qkv_ref[:, cols] = (acc * scale_ref[:, cols]).astype(BF16)
    for c in range(2):
        cols = slice(c * D_MODEL, (c + 1) * D_MODEL)
        wcols = slice(D_QKV + c * D_MODEL, D_QKV + (c + 1) * D_MODEL)
        logit = _dot(xb, w_ref[:, wcols]) + bg_ref[:, cols]
        gate_ref[:, cols] = 1.0 / (1.0 + jnp.exp(-logit))


def _in_proj(x2d, w_in_bf16, qscale, b_gate_row, tm=512):
    n = x2d.shape[0]
    return pl.pallas_call(
        _in_proj_kernel,
        grid=(n // tm,),
        in_specs=[
            pl.BlockSpec((tm, D_MODEL), lambda i: (i, 0)),
            pl.BlockSpec((D_MODEL, D_IN), lambda i: (0, 0)),
            pl.BlockSpec((1, D_QKV), lambda i: (0, 0)),
            pl.BlockSpec((1, 2 * D_MODEL), lambda i: (0, 0)),
        ],
        out_specs=[
            pl.BlockSpec((tm, D_QKV), lambda i: (i, 0)),
            pl.BlockSpec((tm, 2 * D_MODEL), lambda i: (i, 0)),
        ],
        out_shape=[
            jax.ShapeDtypeStruct((n, D_QKV), BF16),
            jax.ShapeDtypeStruct((n, 2 * D_MODEL), F32),
        ],
        compiler_params=pltpu.CompilerParams(
            dimension_semantics=("arbitrary",), vmem_limit_bytes=VMEM_LIMIT),
        name="in_proj",
    )(x2d, w_in_bf16, qscale, b_gate_row)


SB_DEAD_BITS = 160.0
SB_ROW_PARTS = 1


def _sb_kernel(q_ref, k_ref, v_ref, u_ref, o_ref, z_ref, arg_ref, rs_ref, acc_ref, car_ref, *, tq):
    i = pl.program_id(2)
    q = q_ref[...]
    lane_q = lax.broadcasted_iota(jnp.int32, (tq, LANES), 1)
    zero_q = jnp.zeros_like(q)
    q_heads = (jnp.where(lane_q < HEAD_DIM, q, zero_q), jnp.where(lane_q < HEAD_DIM, zero_q, q))

    def block_rows(n):
        return pl.ds(pl.multiple_of(jnp.maximum(i - n, 0) * tq, tq), tq)

    def block_pair(n, diagonal):
        part = tq // SB_ROW_PARTS
        halves = [pl.ds(r * part, part) for r in range(SB_ROW_PARTS)]
        for b in range(2):
            k = k_ref[block_rows(n + b), :]
            for h in range(HEADS_PER_TILE):
                for r, rows in enumerate(halves):
                    z = lax.dot_general(q_heads[h][r * part:(r + 1) * part], k, _NT,
                                        preferred_element_type=F32)
                    if diagonal and b == 0:
                        row = lax.broadcasted_iota(jnp.int32, (part, tq), 0) + r * part
                        col = lax.broadcasted_iota(jnp.int32, (part, tq), 1)
                        z = jnp.where(col < row, z, NEG_INF)
                    z_ref[b, h, rows, :] = z
        for b in range(2):
            for h in range(HEADS_PER_TILE):
                for rows in halves:
                    z = z_ref[b, h, rows, :]
                    sp = jnp.maximum(z, 0.0) + jnp.log2(1.0 + jnp.exp2(-jnp.abs(z)))
                    sums = _dot(sp.astype(BF16), u_ref[...])
                    arg_ref[b, h, rows, :] = (z - sp) - sums[:, 0:tq]
                    rs_ref[b, h, rows, :] = sums[:, tq:tq + LANES]
        carries = [car_ref[...]]
        for b in range(2):
            carries.append(carries[b] + rs_ref[b])
        car_ref[...] = carries[2]
        least = jnp.min(carries[2])
        v_parts = []
        for b in range(2):
            v = v_ref[block_rows(n + b), :]
            zero_v = jnp.zeros_like(v)
            if b == 1:
                v = jnp.where(n + b <= i, v, zero_v)
            v_parts += [jnp.where(lane_q < HEAD_DIM, v, zero_v), jnp.where(lane_q < HEAD_DIM, zero_v, v)]
        v_all = jnp.concatenate(v_parts, axis=0)
        for r, rows in enumerate(halves):
            sl = slice(r * part, (r + 1) * part)
            ws = []
            for b in range(2):
                for h in range(HEADS_PER_TILE):
                    carry = carries[b][h, sl, :]
                    carry = jnp.concatenate([carry] * (tq // LANES), axis=1)
                    ws.append(jnp.exp2(arg_ref[b, h, rows, :] - carry).astype(BF16))
            acc_ref[rows, :] += _dot(jnp.concatenate(ws, axis=1), v_all)
        return least

    acc_ref[...] = jnp.zeros_like(acc_ref)
    car_ref[...] = jnp.zeros_like(car_ref)
    least = block_pair(0, True)

    def more(state):
        n, least = state
        return (n <= i) & (least < SB_DEAD_BITS)

    def body(state):
        n, _ = state
        return n + 2, block_pair(n, False)

    lax.while_loop(more, body, (jnp.int32(2), least))
    o_ref[...] = acc_ref[...].astype(o_ref.dtype)


def _sb_attn(qkv, tq=256):
    b, s, _ = qkv.shape
    n_tiles = W_SB // LANES
    u = jnp.asarray(np.concatenate([np.arange(tq)[:, None] > np.arange(tq)[None, :],
                                    np.ones((tq, LANES), bool)], axis=1), dtype=BF16)
    return pl.pallas_call(
        functools.partial(_sb_kernel, tq=tq),
        grid=(b, n_tiles, s // tq),
        in_specs=[
            pl.BlockSpec((None, tq, LANES), lambda bi, hp, i: (bi, i, hp)),
            pl.BlockSpec((None, s, LANES), lambda bi, hp, i: (bi, 0, n_tiles + hp)),
            pl.BlockSpec((None, s, LANES), lambda bi, hp, i: (bi, 0, 2 * n_tiles + hp)),
            pl.BlockSpec((tq, tq + LANES), lambda bi, hp, i: (0, 0)),
        ],
        out_specs=pl.BlockSpec((None, tq, LANES), lambda bi, hp, i: (bi, i, hp)),
        out_shape=jax.ShapeDtypeStruct((b, s, W_SB), BF16),
        scratch_shapes=[
            pltpu.VMEM((2, HEADS_PER_TILE, tq, tq), F32),
            pltpu.VMEM((2, HEADS_PER_TILE, tq, tq), F32),
            pltpu.VMEM((2, HEADS_PER_TILE, tq, LANES), F32),
            pltpu.VMEM((tq, LANES), F32),
            pltpu.VMEM((HEADS_PER_TILE, tq, LANES), F32),
        ],
        compiler_params=pltpu.CompilerParams(
            dimension_semantics=("arbitrary", "arbitrary", "arbitrary"), vmem_limit_bytes=VMEM_LIMIT),
        name="sb_attn",
    )(qkv, qkv, qkv, u)


CA_GROUP = 4
CA_STEP_GROUPS = 4
CA_TQ = CA_GROUP * CHUNK
CA_BAND = (CA_GROUP + N_PAST_CHUNKS) * CHUNK
CA_PAD = N_PAST_CHUNKS * CHUNK


def _ca_kernel(q_ref, k_ref, v_ref, bias_ref, ones_ref, o_ref, kp_ref, vp_ref):
    c = pl.program_id(2)
    s = k_ref.shape[0]

    @pl.when(c == 0)
    def _():
        kp_ref[0:CA_PAD, :] = jnp.zeros((CA_PAD, LANES), BF16)
        vp_ref[0:CA_PAD, :] = jnp.zeros((CA_PAD, LANES), BF16)
        kp_ref[CA_PAD:CA_PAD + s, :] = k_ref[...]
        vp_ref[CA_PAD:CA_PAD + s, :] = v_ref[...]

    lane_q = lax.broadcasted_iota(jnp.int32, (CA_TQ, LANES), 1)
    lane_v = lax.broadcasted_iota(jnp.int32, (CA_BAND, LANES), 1)

    def attend(g, masked):
        group = c * CA_STEP_GROUPS + g
        start = pl.multiple_of(group * CA_TQ, CA_TQ)
        rows = pl.ds(g * CA_TQ, CA_TQ)
        kb = kp_ref[pl.ds(start, CA_BAND), :]
        vb = vp_ref[pl.ds(start, CA_BAND), :]
        q = q_ref[rows, :]
        zero_q = jnp.zeros_like(q)
        zero_v = jnp.zeros_like(vb)
        q_heads = (jnp.where(lane_q < HEAD_DIM, q, zero_q), jnp.where(lane_q < HEAD_DIM, zero_q, q))
        v_heads = jnp.concatenate(
            [jnp.where(lane_v < HEAD_DIM, vb, zero_v), jnp.where(lane_v < HEAD_DIM, zero_v, vb)], axis=0)
        v_and_ones = jnp.concatenate([v_heads, ones_ref[...]], axis=1)
        es = []
        for h in range(HEADS_PER_TILE):
            sc = lax.dot_general(q_heads[h], kb, _NT, preferred_element_type=F32) + bias_ref[h]
            if masked:
                pos = lax.broadcasted_iota(jnp.int32, (CA_TQ, CA_BAND), 1)
                sc = jnp.where(pos >= CA_PAD - group * CA_TQ, sc, NEG_INF)
            m = jnp.max(sc, axis=1, keepdims=True)
            es.append(jnp.exp2(sc - m).astype(BF16))
        both = _dot(jnp.concatenate(es, axis=1), v_and_ones)
        o_ref[rows, :] = (both[:, 0:LANES] / both[:, LANES:2 * LANES]).astype(o_ref.dtype)

    def step(masked):
        for g in range(CA_STEP_GROUPS):
            attend(g, masked)

    assert CA_PAD // CA_TQ <= CA_STEP_GROUPS
    pl.when(c == 0)(lambda: step(True))
    pl.when(c > 0)(lambda: step(False))


def _ca_bias_table(rel_bias):
    h = rel_bias.shape[0]
    r = np.arange(CA_TQ)[:, None]
    p = np.arange(CA_BAND)[None, :]
    qc = r // CHUNK
    kc = p // CHUNK
    in_band = (kc >= qc) & (kc <= qc + N_PAST_CHUNKS)
    n_far = CA_PAD + CA_TQ - 1 - REL_CLIP
    n_neg = CA_BAND - 1 - CA_PAD - REL_CLIP
    rb = rel_bias.astype(F32)
    line = jnp.concatenate([jnp.broadcast_to(rb[:, 2 * REL_CLIP:], (h, n_far)), rb[:, ::-1],
                            jnp.broadcast_to(rb[:, :1], (h, n_neg))], axis=1)
    length = CA_TQ + CA_BAND - 1
    assert line.shape[1] == length
    line = jnp.pad(line, ((0, 0), (0, 1)))
    skew = jnp.tile(line, (1, CA_TQ))[:, :CA_TQ * length].reshape(h, CA_TQ, length)
    table = skew[:, :, CA_TQ - 1:]
    return jnp.where(jnp.asarray(in_band)[None], table * np.float32(np.log2(np.e)), NEG_INF)


def _ca_attn(qkv, bias_table):
    b, s, _ = qkv.shape
    n_tiles = W_CA // LANES
    base = 3 * W_SB // LANES
    step_rows = CA_STEP_GROUPS * CA_TQ
    head_of_row = np.arange(HEADS_PER_TILE * CA_BAND)[:, None] // CA_BAND
    head_of_lane = np.arange(LANES)[None, :] // HEAD_DIM
    ones = jnp.asarray(head_of_row == head_of_lane, dtype=BF16)
    return pl.pallas_call(
        _ca_kernel,
        grid=(b, n_tiles, s // step_rows),
        in_specs=[
            pl.BlockSpec((None, step_rows, LANES), lambda bi, hp, c: (bi, c, base + hp)),
            pl.BlockSpec((None, s, LANES), lambda bi, hp, c: (bi, 0, base + n_tiles + hp)),
            pl.BlockSpec((None, s, LANES), lambda bi, hp, c: (bi, 0, base + 2 * n_tiles + hp)),
            pl.BlockSpec((HEADS_PER_TILE, CA_TQ, CA_BAND), lambda bi, hp, c: (hp, 0, 0)),
            pl.BlockSpec((HEADS_PER_TILE * CA_BAND, LANES), lambda bi, hp, c: (0, 0)),
        ],
        out_specs=pl.BlockSpec((None, step_rows, LANES), lambda bi, hp, c: (bi, c, hp)),
        out_shape=jax.ShapeDtypeStruct((b, s, W_CA), BF16),
        scratch_shapes=[
            pltpu.VMEM((CA_PAD + s, LANES), BF16),
            pltpu.VMEM((CA_PAD + s, LANES), BF16),
        ],
        compiler_params=pltpu.CompilerParams(
            dimension_semantics=("arbitrary", "arbitrary", "arbitrary"), vmem_limit_bytes=VMEM_LIMIT),
        name="ca_attn",
    )(qkv, qkv, qkv, bias_table, ones)


ROUTE_E1, ROUTE_E2, ROUTE_W1, ROUTE_W2 = 0, 1, 2, 3


def _route(lg):
    lane = lax.broadcasted_iota(jnp.int32, lg.shape, 1)
    big = jnp.int32(ROUTER_LANES)
    is_group = lane < N_GROUPS
    g_max = jnp.max(jnp.where(is_group, lg, -jnp.inf), axis=1, keepdims=True)
    g_idx = jnp.min(jnp.where(is_group & (lg == g_max), lane, big), axis=1, keepdims=True)
    g_den = jnp.sum(jnp.where(is_group, jnp.exp(lg - g_max), 0.0), axis=1, keepdims=True)
    g_val = 1.0 / g_den
    lo = N_GROUPS + EXPERTS_PER_GROUP * g_idx
    in_group = (lane >= lo) & (lane < lo + EXPERTS_PER_GROUP)
    v1 = jnp.max(jnp.where(in_group, lg, -jnp.inf), axis=1, keepdims=True)
    i1 = jnp.min(jnp.where(in_group & (lg == v1), lane, big), axis=1, keepdims=True)
    rest = in_group & (lane != i1)
    v2 = jnp.max(jnp.where(rest, lg, -jnp.inf), axis=1, keepdims=True)
    i2 = jnp.min(jnp.where(rest & (lg == v2), lane, big), axis=1, keepdims=True)
    e2 = jnp.exp(v2 - v1)
    w1 = g_val / (1.0 + e2)
    w2 = g_val * e2 / (1.0 + e2)
    picked = ((lane == i1) | (lane == i2)).astype(BF16)
    info = (jnp.where(lane == ROUTE_E1, (i1 - N_GROUPS).astype(F32), 0.0)
            + jnp.where(lane == ROUTE_E2, (i2 - N_GROUPS).astype(F32), 0.0)
            + jnp.where(lane == ROUTE_W1, w1, 0.0) + jnp.where(lane == ROUTE_W2, w2, 0.0))
    return info, picked


def _post_kernel(ysb_ref, yca_ref, gate_ref, x_ref, wsb_ref, wca_ref, wout_ref, wr_ref, br_ref,
                 g_ref, b_ref, x1_ref, info_ref, pick_ref):
    w_hi, w_lo = _split_bf16(wr_ref[...])
    w_split = jnp.concatenate([w_hi, w_lo], axis=1)
    tm = x_ref.shape[0]
    for rows in [pl.ds(s * POST_SUB, POST_SUB) for s in range(tm // POST_SUB)]:
        a = _dot(ysb_ref[rows, :], wsb_ref[...])
        c = _dot(yca_ref[rows, :], wca_ref[...])
        mix = gate_ref[rows, 0:D_MODEL] * a + gate_ref[rows, D_MODEL:2 * D_MODEL] * c
        mixed = _dot(mix.astype(BF16), wout_ref[...])
        x1 = _layer_norm(ALPHA * x_ref[rows, :] + mixed, g_ref[...], b_ref[...])
        x1_ref[rows, :] = x1
        x_hi, x_lo = _split_bf16(x1)
        parts = _dot(jnp.concatenate([x_hi, x_lo], axis=0), w_split)
        lg = (parts[0:POST_SUB, 0:ROUTER_LANES] + parts[0:POST_SUB, ROUTER_LANES:]
              + parts[POST_SUB:, 0:ROUTER_LANES] + parts[POST_SUB:, ROUTER_LANES:]) + br_ref[...]
        info_ref[rows, :], pick_ref[rows, :] = _route(lg)


POST_SUB = 256


def _post_attn(y_sb, y_ca, gates, x2d, w_br_sb, w_br_ca, w_out, w_router, b_router, ln_g, ln_b, tm=512):
    n = x2d.shape[0]
    row = lambda i: (i, 0)
    fixed = lambda i: (0, 0)
    return pl.pallas_call(
        _post_kernel,
        grid=(n // tm,),
        in_specs=[
            pl.BlockSpec((tm, W_SB), row),
            pl.BlockSpec((tm, W_CA), row),
            pl.BlockSpec((tm, 2 * D_MODEL), row),
            pl.BlockSpec((tm, D_MODEL), row),
            pl.BlockSpec((W_SB, D_MODEL), fixed),
            pl.BlockSpec((W_CA, D_MODEL), fixed),
            pl.BlockSpec((D_MODEL, D_MODEL), fixed),
            pl.BlockSpec((D_MODEL, ROUTER_LANES), fixed),
            pl.BlockSpec((1, ROUTER_LANES), fixed),
            pl.BlockSpec((1, D_MODEL), fixed),
            pl.BlockSpec((1, D_MODEL), fixed),
        ],
        out_specs=[
            pl.BlockSpec((tm, D_MODEL), row),
            pl.BlockSpec((tm, ROUTER_LANES), row),
            pl.BlockSpec((tm, ROUTER_LANES), row),
        ],
        out_shape=[
            jax.ShapeDtypeStruct((n, D_MODEL), F32),
            jax.ShapeDtypeStruct((n, ROUTER_LANES), F32),
            jax.ShapeDtypeStruct((n, ROUTER_LANES), BF16),
        ],
        compiler_params=pltpu.CompilerParams(
            dimension_semantics=("arbitrary",), vmem_limit_bytes=VMEM_LIMIT),
        name="post_attn",
    )(y_sb, y_ca, gates, x2d, w_br_sb, w_br_ca, w_out, w_router, b_router, ln_g, ln_b)


MOE_TILE = 512
TOP_K = 2


def _moe_rows(n):
    return n * TOP_K + N_EXPERTS * MOE_TILE


def _rank_kernel(pick_ref, info_ref, tri_ref, dest_ref, cnt_ref, run_ref, off_ref, *, tb):
    p = pl.program_id(0)
    i = pl.program_id(1)
    pick = pick_ref[...]

    @pl.when((p == 0) & (i == 0))
    def _():
        run_ref[...] = jnp.zeros_like(run_ref)

    @pl.when(p == 0)
    def _():
        run_ref[...] += jnp.sum(pick.astype(F32), axis=0, keepdims=True)

    @pl.when((p == 1) & (i == 0))
    def _():
        cnt = run_ref[...]
        cnt_ref[...] = cnt
        padded = jnp.ceil(cnt * (1.0 / MOE_TILE)) * MOE_TILE
        lane = lax.broadcasted_iota(jnp.int32, padded.shape, 1)
        scan = padded
        step = 1
        while step < ROUTER_LANES:
            scan = scan + jnp.where(lane >= step, pltpu.roll(scan, step, axis=1), 0.0)
            step *= 2
        off_ref[...] = scan - padded
        run_ref[...] = jnp.zeros_like(run_ref)

    @pl.when(p == 1)
    def _():
        seen = run_ref[0:1, :]
        earlier = _dot(tri_ref[...], pick)
        row_of = earlier + seen + off_ref[0:1, :]
        info = info_ref[...]
        lane = lax.broadcasted_iota(jnp.int32, info.shape, 1)
        lane_f = lane.astype(F32)
        e1 = jnp.sum(jnp.where(lane == ROUTE_E1, info, 0.0), axis=1, keepdims=True)
        e2 = jnp.sum(jnp.where(lane == ROUTE_E2, info, 0.0), axis=1, keepdims=True)
        d1 = jnp.sum(jnp.where(lane_f == e1 + N_GROUPS, row_of, 0.0), axis=1, keepdims=True)
        d2 = jnp.sum(jnp.where(lane_f == e2 + N_GROUPS, row_of, 0.0), axis=1, keepdims=True)
        dest_ref[...] = (jnp.where(lane == 0, d1, 0.0) + jnp.where(lane == 1, d2, 0.0)).astype(jnp.int32)
        run_ref[...] += jnp.sum(pick.astype(F32), axis=0, keepdims=True)


def _moe_rank(pick, info, tb=1024):
    n = pick.shape[0]
    tri = jnp.asarray(np.arange(tb)[None, :] < np.arange(tb)[:, None], dtype=BF16)
    return pl.pallas_call(
        functools.partial(_rank_kernel, tb=tb),
        grid=(2, n // tb),
        in_specs=[
            pl.BlockSpec((tb, ROUTER_LANES), lambda p, i: (i, 0)),
            pl.BlockSpec((tb, ROUTER_LANES), lambda p, i: (i, 0)),
            pl.BlockSpec((tb, tb), lambda p, i: (0, 0)),
        ],
        out_specs=[
            pl.BlockSpec((tb, ROUTER_LANES), lambda p, i: (i * p, 0)),
            pl.BlockSpec((8, ROUTER_LANES), lambda p, i: (0, 0)),
        ],
        out_shape=[
            jax.ShapeDtypeStruct((n, ROUTER_LANES), jnp.int32),
            jax.ShapeDtypeStruct((8, ROUTER_LANES), F32),
        ],
        scratch_shapes=[
            pltpu.VMEM((8, ROUTER_LANES), F32),
            pltpu.VMEM((8, ROUTER_LANES), F32),
        ],
        compiler_params=pltpu.CompilerParams(
            dimension_semantics=("arbitrary", "arbitrary"), vmem_limit_bytes=VMEM_LIMIT),
        name="moe_rank",
    )(pick, info, tri)


def _row_copy(src_ref, src_row, dst_ref, dst_row, sem):
    return pltpu.make_async_copy(src_ref.at[pl.ds(src_row, 1), :], dst_ref.at[pl.ds(dst_row, 1), :], sem)


def _dispatch_kernel(d1_ref, d2_ref, last_ref, x_ref, xs_hbm, zero_ref, sem, zsem, *, tb):
    base = pl.program_id(0) * tb

    @pl.when(pl.program_id(0) == 0)
    def _():
        zero_ref[...] = jnp.zeros_like(zero_ref)

        def last_tile_copy(e):
            first_row = pl.multiple_of((last_ref[e] - 1) * MOE_TILE, MOE_TILE)
            return pltpu.make_async_copy(zero_ref, xs_hbm.at[pl.ds(first_row, MOE_TILE), :], zsem)

        def unused_tile_copy(t):
            return pltpu.make_async_copy(zero_ref, xs_hbm.at[pl.ds(t * MOE_TILE, MOE_TILE), :], zsem)

        n_tiles = xs_hbm.shape[0] // MOE_TILE
        min_used = n_tiles - N_EXPERTS
        for wait in (False, True):
            for e in range(N_EXPERTS):
                owns_tiles = last_ref[e] > (last_ref[e - 1] if e else 0)
                pl.when(owns_tiles)(
                    lambda e=e, wait=wait: last_tile_copy(e).wait() if wait else last_tile_copy(e).start())
            for t in range(min_used, n_tiles):
                pl.when(t >= last_ref[N_EXPERTS - 1])(
                    lambda t=t, wait=wait: unused_tile_copy(t).wait() if wait else unused_tile_copy(t).start())

    def issue(r, c):
        _row_copy(x_ref, r, xs_hbm, d1_ref[base + r], sem).start(priority=0)
        _row_copy(x_ref, r, xs_hbm, d2_ref[base + r], sem).start(priority=1)
        return c

    lax.fori_loop(0, tb, issue, 0, unroll=8)
    for _ in range(TOP_K):
        pltpu.make_async_copy(x_ref, xs_hbm.at[pl.ds(0, tb), :], sem).wait()


def _moe_dispatch(x1, dest1, dest2, last_tile, tb=512):
    n = x1.shape[0]
    rows = _moe_rows(n)
    grid_spec = pltpu.PrefetchScalarGridSpec(
        num_scalar_prefetch=3,
        grid=(n // tb,),
        in_specs=[pl.BlockSpec((tb, D_MODEL), lambda i, d1, d2, last: (i, 0))],
        out_specs=pl.BlockSpec(memory_space=pl.ANY),
        scratch_shapes=[
            pltpu.VMEM((MOE_TILE, D_MODEL), F32),
            pltpu.SemaphoreType.DMA(()),
            pltpu.SemaphoreType.DMA(()),
        ],
    )
    return pl.pallas_call(
        functools.partial(_dispatch_kernel, tb=tb),
        grid_spec=grid_spec,
        out_shape=jax.ShapeDtypeStruct((rows, D_MODEL), F32),
        compiler_params=pltpu.CompilerParams(
            dimension_semantics=("arbitrary",), vmem_limit_bytes=VMEM_LIMIT),
        name="moe_dispatch",
    )(dest1, dest2, last_tile, x1)


def _experts_kernel(te_ref, nt_ref, xs_ref, wg_ref, wu_ref, wd_ref, ys_ref, wgu_s, wd_s):
    t = pl.program_id(0)
    changed = (t == 0) | (te_ref[t] != te_ref[jnp.maximum(t - 1, 0)])

    @pl.when(changed)
    def _():
        wgu_s[:, 0:D_EXPERT] = wg_ref[...].astype(BF16)
        wgu_s[:, D_EXPERT:2 * D_EXPERT] = wu_ref[...].astype(BF16)
        wd_s[...] = wd_ref[...].astype(BF16)

    @pl.when(t < nt_ref[0])
    def _():
        gu = _dot(xs_ref[...].astype(BF16), wgu_s[...])
        gate = gu[:, 0:D_EXPERT]
        up = gu[:, D_EXPERT:2 * D_EXPERT]
        hid = (gate * (1.0 / (1.0 + jnp.exp(-gate)))) * up
        ys_ref[...] = _dot(hid.astype(BF16), wd_s[...])

    @pl.when(t >= nt_ref[0])
    def _():
        ys_ref[...] = jnp.zeros_like(ys_ref)


def _moe_experts(xs, tile_expert, n_tiles_used, w_gate, w_up, w_down, layer):
    rows = xs.shape[0]
    first = layer * N_EXPERTS
    grid_spec = pltpu.PrefetchScalarGridSpec(
        num_scalar_prefetch=2,
        grid=(rows // MOE_TILE,),
        in_specs=[
            pl.BlockSpec((MOE_TILE, D_MODEL), lambda t, te, nt: (jnp.minimum(t, nt[0] - 1), 0)),
            pl.BlockSpec((None, D_MODEL, D_EXPERT), lambda t, te, nt: (first + te[t], 0, 0)),
            pl.BlockSpec((None, D_MODEL, D_EXPERT), lambda t, te, nt: (first + te[t], 0, 0)),
            pl.BlockSpec((None, D_EXPERT, D_MODEL), lambda t, te, nt: (first + te[t], 0, 0)),
        ],
        out_specs=pl.BlockSpec((MOE_TILE, D_MODEL), lambda t, te, nt: (t, 0)),
        scratch_shapes=[
            pltpu.VMEM((D_MODEL, 2 * D_EXPERT), BF16),
            pltpu.VMEM((D_EXPERT, D_MODEL), BF16),
        ],
    )
    return pl.pallas_call(
        _experts_kernel,
        grid_spec=grid_spec,
        out_shape=jax.ShapeDtypeStruct((rows, D_MODEL), F32),
        compiler_params=pltpu.CompilerParams(
            dimension_semantics=("arbitrary",), vmem_limit_bytes=VMEM_LIMIT),
        name="moe_experts",
    )(tile_expert, n_tiles_used, xs, w_gate, w_up, w_down)


def _combine_kernel(d1_ref, d2_ref, x1_ref, info_ref, ys_hbm, g_ref, b_ref, out_ref, y1_ref, y2_ref, sem, *, tb):
    base = pl.program_id(0) * tb

    def issue(r, c):
        _row_copy(ys_hbm, d1_ref[base + r], y1_ref, r, sem).start(priority=0)
        _row_copy(ys_hbm, d2_ref[base + r], y2_ref, r, sem).start(priority=1)
        return c

    lax.fori_loop(0, tb, issue, 0, unroll=8)
    for y_ref in (y1_ref, y2_ref):
        pltpu.make_async_copy(ys_hbm.at[pl.ds(0, tb), :], y_ref, sem).wait()
    info = info_ref[...]
    lane = lax.broadcasted_iota(jnp.int32, info.shape, 1)
    w1 = jnp.sum(jnp.where(lane == ROUTE_W1, info, 0.0), axis=1, keepdims=True)
    w2 = jnp.sum(jnp.where(lane == ROUTE_W2, info, 0.0), axis=1, keepdims=True)
    ffn = w1 * y1_ref[...] + w2 * y2_ref[...]
    out_ref[...] = _layer_norm(ALPHA * x1_ref[...] + ffn, g_ref[...], b_ref[...])


def _moe_combine(x1, info, ys, dest1, dest2, ln_g, ln_b, tb=512):
    n = x1.shape[0]
    grid_spec = pltpu.PrefetchScalarGridSpec(
        num_scalar_prefetch=2,
        grid=(n // tb,),
        in_specs=[
            pl.BlockSpec((tb, D_MODEL), lambda i, d1, d2: (i, 0)),
            pl.BlockSpec((tb, ROUTER_LANES), lambda i, d1, d2: (i, 0)),
            pl.BlockSpec(memory_space=pl.ANY),
            pl.BlockSpec((1, D_MODEL), lambda i, d1, d2: (0, 0)),
            pl.BlockSpec((1, D_MODEL), lambda i, d1, d2: (0, 0)),
        ],
        out_specs=pl.BlockSpec((tb, D_MODEL), lambda i, d1, d2: (i, 0)),
        scratch_shapes=[
            pltpu.VMEM((tb, D_MODEL), F32),
            pltpu.VMEM((tb, D_MODEL), F32),
            pltpu.SemaphoreType.DMA(()),
        ],
    )
    return pl.pallas_call(
        functools.partial(_combine_kernel, tb=tb),
        grid_spec=grid_spec,
        out_shape=jax.ShapeDtypeStruct((n, D_MODEL), F32),
        compiler_params=pltpu.CompilerParams(
            dimension_semantics=("arbitrary",), vmem_limit_bytes=VMEM_LIMIT),
        name="moe_combine",
    )(dest1, dest2, x1, info, ys, ln_g, ln_b)


def _moe(x1, info, pick, w_gate, w_up, w_down, layer, ln_g, ln_b):
    n = x1.shape[0]
    dest, counts = _moe_rank(pick, info)
    dest1 = dest[:, 0]
    dest2 = dest[:, 1]
    tiles = jnp.ceil(counts[0, N_GROUPS:N_GROUPS + N_EXPERTS] * (1.0 / MOE_TILE)).astype(jnp.int32)
    last_tile = jnp.cumsum(tiles)
    tile_ids = jnp.arange(_moe_rows(n) // MOE_TILE, dtype=jnp.int32)
    tile_expert = jnp.minimum(jnp.sum(tile_ids[:, None] >= last_tile[None, :], axis=1), N_EXPERTS - 1)
    xs = _moe_dispatch(x1, dest1, dest2, last_tile)
    ys = _moe_experts(xs, tile_expert.astype(jnp.int32), last_tile[N_EXPERTS - 1:], w_gate, w_up, w_down, layer)
    return _moe_combine(x1, info, ys, dest1, dest2, ln_g, ln_b)


def _layer(x2d, batch, w_in, b_gate, rel_bias, w_br_sb, w_br_ca, w_out, ln1_g, ln1_b,
           w_group, b_group, w_erouter, b_erouter, w_gate, w_up, w_down, layer, ln2_g, ln2_b):
    n = x2d.shape[0]
    seq = n // batch
    scale = HEAD_DIM ** -0.5
    qscale = np.ones((1, D_QKV), np.float32)
    qscale[:, 0:W_SB] = scale * np.log2(np.e)
    qscale[:, 3 * W_SB:3 * W_SB + W_CA] = scale * np.log2(np.e)
    qkv, gates = _in_proj(x2d, w_in.astype(BF16), jnp.asarray(qscale), b_gate.reshape(1, 2 * D_MODEL))
    qkv = qkv.reshape(batch, seq, D_QKV)
    y_sb = _sb_attn(qkv).reshape(n, W_SB)
    y_ca = _ca_attn(qkv, _ca_bias_table(rel_bias)).reshape(n, W_CA)

    w_router = jnp.concatenate(
        [w_group, w_erouter.transpose(1, 0, 2).reshape(D_MODEL, N_EXPERTS)], axis=1)
    w_router = jnp.pad(w_router, ((0, 0), (0, ROUTER_LANES - N_GROUPS - N_EXPERTS)))
    b_router = jnp.pad(jnp.concatenate([b_group, b_erouter.reshape(N_EXPERTS)]),
                       (0, ROUTER_LANES - N_GROUPS - N_EXPERTS)).reshape(1, ROUTER_LANES)
    x1, info, pick = _post_attn(y_sb, y_ca, gates, x2d, w_br_sb.astype(BF16), w_br_ca.astype(BF16),
                        w_out.astype(BF16), w_router, b_router,
                        ln1_g.reshape(1, D_MODEL), ln1_b.reshape(1, D_MODEL))

    return _moe(x1, info, pick, w_gate, w_up, w_down, layer,
                ln2_g.reshape(1, D_MODEL), ln2_b.reshape(1, D_MODEL))


def kernel(x, w_in, b_gate, rel_bias, w_br_sb, w_br_ca, w_out, ln1_g, ln1_b, w_group, b_group,
           w_erouter, b_erouter, w_gate, w_up, w_down, ln2_g, ln2_b):
    batch, seq, d = x.shape
    h = x.reshape(batch * seq, d)
    w_gate = w_gate.reshape(DEPTH * N_EXPERTS, D_MODEL, D_EXPERT)
    w_up = w_up.reshape(DEPTH * N_EXPERTS, D_MODEL, D_EXPERT)
    w_down = w_down.reshape(DEPTH * N_EXPERTS, D_EXPERT, D_MODEL)
    for l in range(DEPTH):
        h = _layer(h, batch, w_in[l], b_gate[l], rel_bias[l], w_br_sb[l], w_br_ca[l], w_out[l],
                   ln1_g[l], ln1_b[l], w_group[l], b_group[l], w_erouter[l], b_erouter[l],
                   w_gate, w_up, w_down, l, ln2_g[l], ln2_b[l])
    return h.reshape(batch, seq, d)
```

```python
import functools

import jax
import jax.numpy as jnp
import numpy as np
from jax import lax
from jax.experimental import pallas as pl
from jax.experimental.pallas import tpu as pltpu

D_MODEL = 1024
DEPTH = 2
CHUNK = 64
HEAD_DIM = 64
H_SB = 8
H_CA = 8
W_SB = H_SB * HEAD_DIM
W_CA = H_CA * HEAD_DIM
N_PAST_CHUNKS = 8
REL_CLIP = 128
N_GROUPS = 4
EXPERTS_PER_GROUP = 8
N_EXPERTS = N_GROUPS * EXPERTS_PER_GROUP
D_EXPERT = 256
ALPHA = (2.0 * DEPTH) ** 0.25
LN_EPS = 1e-5
D_QKV = 3 * W_SB + 3 * W_CA
D_IN = D_QKV + 2 * D_MODEL
NEG_INF = -1e30

LANES = 128
HEADS_PER_TILE = LANES // HEAD_DIM
ROUTER_LANES = LANES
VMEM_LIMIT = 56 * 1024 * 1024

BF16 = jnp.bfloat16
F32 = jnp.float32

_NT = (((1,), (1,)), ((), ()))


def _dot(a, b):
    return jnp.dot(a, b, preferred_element_type=F32)


def _layer_norm(h, g, b):
    mu = jnp.mean(h, axis=-1, keepdims=True)
    hc = h - mu
    var = jnp.mean(hc * hc, axis=-1, keepdims=True)
    return hc * lax.rsqrt(var + LN_EPS) * g + b


def _split_bf16(a):
    hi = a.astype(BF16)
    lo = (a - hi.astype(F32)).astype(BF16)
    return hi, lo


def _in_proj_kernel(x_ref, w_ref, scale_ref, bg_ref, qkv_ref, gate_ref):
    xb = x_ref[...].astype(BF16)
    for c in range(D_QKV // D_MODEL):
        cols = slice(c * D_MODEL, (c + 1) * D_MODEL)
        acc = _dot(xb, w_ref[:, cols])
        qkv_ref[:, cols] = (acc * scale_ref[:, cols]).astype(BF16)
    for c in range(2):
        cols = slice(c * D_MODEL, (c + 1) * D_MODEL)
        wcols = slice(D_QKV + c * D_MODEL, D_QKV + (c + 1) * D_MODEL)
        logit = _dot(xb, w_ref[:, wcols]) + bg_ref[:, cols]
        gate_ref[:, cols] = 1.0 / (1.0 + jnp.exp(-logit))


def _in_proj(x2d, w_in_bf16, qscale, b_gate_row, tm=512):
    n = x2d.shape[0]
    return pl.pallas_call(
        _in_proj_kernel,
        grid=(n // tm,),
        in_specs=[
            pl.BlockSpec((tm, D_MODEL), lambda i: (i, 0)),
            pl.BlockSpec((D_MODEL, D_IN), lambda i: (0, 0)),
            pl.BlockSpec((1, D_QKV), lambda i: (0, 0)),
            pl.BlockSpec((1, 2 * D_MODEL), lambda i: (0, 0)),
        ],
        out_specs=[
            pl.BlockSpec((tm, D_QKV), lambda i: (i, 0)),
            pl.BlockSpec((tm, 2 * D_MODEL), lambda i: (i, 0)),
        ],
        out_shape=[
            jax.ShapeDtypeStruct((n, D_QKV), BF16),
            jax.ShapeDtypeStruct((n, 2 * D_MODEL), F32),
        ],
        compiler_params=pltpu.CompilerParams(
            dimension_semantics=("arbitrary",), vmem_limit_bytes=VMEM_LIMIT),
        name="in_proj",
    )(x2d, w_in_bf16, qscale, b_gate_row)


SB_DEAD_BITS = 160.0
SB_STEP_BLOCKS = 4


def _sb_kernel(q_ref, k_ref, v_ref, u_ref, o_ref, z_ref, arg_ref, rs_ref, acc_ref, car_ref, *, tq):
    step = pl.program_id(2)
    lane_q = lax.broadcasted_iota(jnp.int32, (tq, LANES), 1)

    def split_heads(x):
        zero = jnp.zeros_like(x)
        return jnp.where(lane_q < HEAD_DIM, x, zero), jnp.where(lane_q < HEAD_DIM, zero, x)

    blocks = [step * SB_STEP_BLOCKS + j for j in range(SB_STEP_BLOCKS)]
    q_heads = [split_heads(q_ref[pl.ds(j * tq, tq), :]) for j in range(SB_STEP_BLOCKS)]

    def key_rows(i, n):
        return pl.ds(pl.multiple_of(jnp.maximum(i - n, 0) * tq, tq), tq)

    def block_pairs(subs, n, diagonal):
        for j in subs:
            for b in range(2):
                k = k_ref[key_rows(blocks[j], n + b), :]
                for h in range(HEADS_PER_TILE):
                    z = lax.dot_general(q_heads[j][h], k, _NT, preferred_element_type=F32)
                    if diagonal and b == 0:
                        row = lax.broadcasted_iota(jnp.int32, (tq, tq), 0)
                        col = lax.broadcasted_iota(jnp.int32, (tq, tq), 1)
                        z = jnp.where(col < row, z, NEG_INF)
                    z_ref[j, b, h] = z
        for j in subs:
            for b in range(2):
                for h in range(HEADS_PER_TILE):
                    z = z_ref[j, b, h]
                    sp = jnp.maximum(z, 0.0) + jnp.log2(1.0 + jnp.exp2(-jnp.abs(z)))
                    sums = _dot(sp.astype(BF16), u_ref[...])
                    arg_ref[j, b, h] = (z - sp) - sums[:, 0:tq]
                    rs_ref[j, b, h] = sums[:, tq:tq + LANES]
        least = []
        for j in subs:
            carries = [car_ref[j]]
            for b in range(2):
                carries.append(carries[b] + rs_ref[j, b])
            car_ref[j] = carries[2]
            least.append(jnp.min(carries[2]))
            v_parts = []
            for b in range(2):
                v = v_ref[key_rows(blocks[j], n + b), :]
                if b == 1:
                    v = jnp.where(n + b <= blocks[j], v, jnp.zeros_like(v))
                v_parts += split_heads(v)
            ws = []
            for b in range(2):
                for h in range(HEADS_PER_TILE):
                    carry = jnp.concatenate([carries[b][h]] * (tq // LANES), axis=1)
                    ws.append(jnp.exp2(arg_ref[j, b, h] - carry).astype(BF16))
            acc_ref[j] += _dot(jnp.concatenate(ws, axis=1), jnp.concatenate(v_parts, axis=0))
        return least

    acc_ref[...] = jnp.zeros_like(acc_ref)
    car_ref[...] = jnp.zeros_like(car_ref)
    first = block_pairs(range(SB_STEP_BLOCKS), 0, True)

    for j in range(SB_STEP_BLOCKS):
        def more(state, j=j):
            n, least = state
            return (n <= blocks[j]) & (least < SB_DEAD_BITS)

        def body(state, j=j):
            n, _ = state
            return n + 2, block_pairs([j], n, False)[0]

        lax.while_loop(more, body, (jnp.int32(2), first[j]))
        o_ref[pl.ds(j * tq, tq), :] = acc_ref[j].astype(o_ref.dtype)


def _sb_attn(qkv, tq=256):
    b, s, _ = qkv.shape
    n_tiles = W_SB // LANES
    step_rows = SB_STEP_BLOCKS * tq
    u = jnp.asarray(np.concatenate([np.arange(tq)[:, None] > np.arange(tq)[None, :],
                                    np.ones((tq, LANES), bool)], axis=1), dtype=BF16)
    return pl.pallas_call(
        functools.partial(_sb_kernel, tq=tq),
        grid=(b, n_tiles, s // step_rows),
        in_specs=[
            pl.BlockSpec((None, step_rows, LANES), lambda bi, hp, i: (bi, i, hp)),
            pl.BlockSpec((None, s, LANES), lambda bi, hp, i: (bi, 0, n_tiles + hp)),
            pl.BlockSpec((None, s, LANES), lambda bi, hp, i: (bi, 0, 2 * n_tiles + hp)),
            pl.BlockSpec((tq, tq + LANES), lambda bi, hp, i: (0, 0)),
        ],
        out_specs=pl.BlockSpec((None, step_rows, LANES), lambda bi, hp, i: (bi, i, hp)),
        out_shape=jax.ShapeDtypeStruct((b, s, W_SB), BF16),
        scratch_shapes=[
            pltpu.VMEM((SB_STEP_BLOCKS, 2, HEADS_PER_TILE, tq, tq), F32),
            pltpu.VMEM((SB_STEP_BLOCKS, 2, HEADS_PER_TILE, tq, tq), F32),
            pltpu.VMEM((SB_STEP_BLOCKS, 2, HEADS_PER_TILE, tq, LANES), F32),
            pltpu.VMEM((SB_STEP_BLOCKS, tq, LANES), F32),
            pltpu.VMEM((SB_STEP_BLOCKS, HEADS_PER_TILE, tq, LANES), F32),
        ],
        compiler_params=pltpu.CompilerParams(
            dimension_semantics=("arbitrary", "arbitrary", "arbitrary"), vmem_limit_bytes=VMEM_LIMIT),
        name="sb_attn",
    )(qkv, qkv, qkv, u)


CA_GROUP = 4
CA_STEP_GROUPS = 8
CA_TQ = CA_GROUP * CHUNK
CA_BAND = (CA_GROUP + N_PAST_CHUNKS) * CHUNK
CA_PAD = N_PAST_CHUNKS * CHUNK


def _ca_kernel(q_ref, k_ref, v_ref, bias_ref, ones_ref, o_ref, kp_ref, vp_ref):
    c = pl.program_id(2)
    s = k_ref.shape[0]

    @pl.when(c == 0)
    def _():
        kp_ref[0:CA_PAD, :] = jnp.zeros((CA_PAD, LANES), BF16)
        vp_ref[0:CA_PAD, :] = jnp.zeros((CA_PAD, LANES), BF16)
        kp_ref[CA_PAD:CA_PAD + s, :] = k_ref[...]
        vp_ref[CA_PAD:CA_PAD + s, :] = v_ref[...]

    lane_q = lax.broadcasted_iota(jnp.int32, (CA_TQ, LANES), 1)
    lane_v = lax.broadcasted_iota(jnp.int32, (CA_BAND, LANES), 1)

    def attend(g, masked):
        group = c * CA_STEP_GROUPS + g
        start = pl.multiple_of(group * CA_TQ, CA_TQ)
        rows = pl.ds(g * CA_TQ, CA_TQ)
        kb = kp_ref[pl.ds(start, CA_BAND), :]
        vb = vp_ref[pl.ds(start, CA_BAND), :]
        q = q_ref[rows, :]
        zero_q = jnp.zeros_like(q)
        zero_v = jnp.zeros_like(vb)
        q_heads = (jnp.where(lane_q < HEAD_DIM, q, zero_q), jnp.where(lane_q < HEAD_DIM, zero_q, q))
        v_heads = jnp.concatenate(
            [jnp.where(lane_v < HEAD_DIM, vb, zero_v), jnp.where(lane_v < HEAD_DIM, zero_v, vb)], axis=0)
        v_and_ones = jnp.concatenate([v_heads, ones_ref[...]], axis=1)
        es = []
        for h in range(HEADS_PER_TILE):
            sc = lax.dot_general(q_heads[h], kb, _NT, preferred_element_type=F32) + bias_ref[h]
            if masked:
                pos = lax.broadcasted_iota(jnp.int32, (CA_TQ, CA_BAND), 1)
                sc = jnp.where(pos >= CA_PAD - group * CA_TQ, sc, NEG_INF)
            m = jnp.max(sc, axis=1, keepdims=True)
            es.append(jnp.exp2(sc - m).astype(BF16))
        both = _dot(jnp.concatenate(es, axis=1), v_and_ones)
        o_ref[rows, :] = (both[:, 0:LANES] / both[:, LANES:2 * LANES]).astype(o_ref.dtype)

    def step(masked):
        for g in range(CA_STEP_GROUPS):
            attend(g, masked)

    assert CA_PAD // CA_TQ <= CA_STEP_GROUPS
    pl.when(c == 0)(lambda: step(True))
    pl.when(c > 0)(lambda: step(False))


def _ca_bias_table(rel_bias):
    h = rel_bias.shape[0]
    r = np.arange(CA_TQ)[:, None]
    p = np.arange(CA_BAND)[None, :]
    qc = r // CHUNK
    kc = p // CHUNK
    in_band = (kc >= qc) & (kc <= qc + N_PAST_CHUNKS)
    n_far = CA_PAD + CA_TQ - 1 - REL_CLIP
    n_neg = CA_BAND - 1 - CA_PAD - REL_CLIP
    rb = rel_bias.astype(F32)
    line = jnp.concatenate([jnp.broadcast_to(rb[:, 2 * REL_CLIP:], (h, n_far)), rb[:, ::-1],
                            jnp.broadcast_to(rb[:, :1], (h, n_neg))], axis=1)
    length = CA_TQ + CA_BAND - 1
    assert line.shape[1] == length
    line = jnp.pad(line, ((0, 0), (0, 1)))
    skew = jnp.tile(line, (1, CA_TQ))[:, :CA_TQ * length].reshape(h, CA_TQ, length)
    table = skew[:, :, CA_TQ - 1:]
    return jnp.where(jnp.asarray(in_band)[None], table * np.float32(np.log2(np.e)), NEG_INF)


def _ca_attn(qkv, bias_table):
    b, s, _ = qkv.shape
    n_tiles = W_CA // LANES
    base = 3 * W_SB // LANES
    step_rows = CA_STEP_GROUPS * CA_TQ
    head_of_row = np.arange(HEADS_PER_TILE * CA_BAND)[:, None] // CA_BAND
    head_of_lane = np.arange(LANES)[None, :] // HEAD_DIM
    ones = jnp.asarray(head_of_row == head_of_lane, dtype=BF16)
    return pl.pallas_call(
        _ca_kernel,
        grid=(b, n_tiles, s // step_rows),
        in_specs=[
            pl.BlockSpec((None, step_rows, LANES), lambda bi, hp, c: (bi, c, base + hp)),
            pl.BlockSpec((None, s, LANES), lambda bi, hp, c: (bi, 0, base + n_tiles + hp)),
            pl.BlockSpec((None, s, LANES), lambda bi, hp, c: (bi, 0, base + 2 * n_tiles + hp)),
            pl.BlockSpec((HEADS_PER_TILE, CA_TQ, CA_BAND), lambda bi, hp, c: (hp, 0, 0)),
            pl.BlockSpec((HEADS_PER_TILE * CA_BAND, LANES), lambda bi, hp, c: (0, 0)),
        ],
        out_specs=pl.BlockSpec((None, step_rows, LANES), lambda bi, hp, c: (bi, c, hp)),
        out_shape=jax.ShapeDtypeStruct((b, s, W_CA), BF16),
        scratch_shapes=[
            pltpu.VMEM((CA_PAD + s, LANES), BF16),
            pltpu.VMEM((CA_PAD + s, LANES), BF16),
        ],
        compiler_params=pltpu.CompilerParams(
            dimension_semantics=("arbitrary", "arbitrary", "arbitrary"), vmem_limit_bytes=VMEM_LIMIT),
        name="ca_attn",
    )(qkv, qkv, qkv, bias_table, ones)


ROUTE_E1, ROUTE_E2, ROUTE_W1, ROUTE_W2 = 0, 1, 2, 3


def _route(lg):
    lane = lax.broadcasted_iota(jnp.int32, lg.shape, 1)
    big = jnp.int32(ROUTER_LANES)
    is_group = lane < N_GROUPS
    g_max = jnp.max(jnp.where(is_group, lg, -jnp.inf), axis=1, keepdims=True)
    g_idx = jnp.min(jnp.where(is_group & (lg == g_max), lane, big), axis=1, keepdims=True)
    g_den = jnp.sum(jnp.where(is_group, jnp.exp(lg - g_max), 0.0), axis=1, keepdims=True)
    g_val = 1.0 / g_den
    lo = N_GROUPS + EXPERTS_PER_GROUP * g_idx
    in_group = (lane >= lo) & (lane < lo + EXPERTS_PER_GROUP)
    v1 = jnp.max(jnp.where(in_group, lg, -jnp.inf), axis=1, keepdims=True)
    i1 = jnp.min(jnp.where(in_group & (lg == v1), lane, big), axis=1, keepdims=True)
    rest = in_group & (lane != i1)
    v2 = jnp.max(jnp.where(rest, lg, -jnp.inf), axis=1, keepdims=True)
    i2 = jnp.min(jnp.where(rest & (lg == v2), lane, big), axis=1, keepdims=True)
    e2 = jnp.exp(v2 - v1)
    w1 = g_val / (1.0 + e2)
    w2 = g_val * e2 / (1.0 + e2)
    picked = ((lane == i1) | (lane == i2)).astype(BF16)
    info = (jnp.where(lane == ROUTE_E1, (i1 - N_GROUPS).astype(F32), 0.0)
            + jnp.where(lane == ROUTE_E2, (i2 - N_GROUPS).astype(F32), 0.0)
            + jnp.where(lane == ROUTE_W1, w1, 0.0) + jnp.where(lane == ROUTE_W2, w2, 0.0))
    return info, picked


def _post_kernel(ysb_ref, yca_ref, gate_ref, x_ref, wsb_ref, wca_ref, wout_ref, wr_ref, br_ref,
                 g_ref, b_ref, x1_ref, info_ref, pick_ref):
    w_hi, w_lo = _split_bf16(wr_ref[...])
    w_split = jnp.concatenate([w_hi, w_lo], axis=1)
    tm = x_ref.shape[0]
    for rows in [pl.ds(s * POST_SUB, POST_SUB) for s in range(tm // POST_SUB)]:
        a = _dot(ysb_ref[rows, :], wsb_ref[...])
        c = _dot(yca_ref[rows, :], wca_ref[...])
        mix = gate_ref[rows, 0:D_MODEL] * a + gate_ref[rows, D_MODEL:2 * D_MODEL] * c
        mixed = _dot(mix.astype(BF16), wout_ref[...])
        x1 = _layer_norm(ALPHA * x_ref[rows, :] + mixed, g_ref[...], b_ref[...])
        x1_ref[rows, :] = x1
        x_hi, x_lo = _split_bf16(x1)
        parts = _dot(jnp.concatenate([x_hi, x_lo], axis=0), w_split)
        lg = (parts[0:POST_SUB, 0:ROUTER_LANES] + parts[0:POST_SUB, ROUTER_LANES:]
              + parts[POST_SUB:, 0:ROUTER_LANES] + parts[POST_SUB:, ROUTER_LANES:]) + br_ref[...]
        info_ref[rows, :], pick_ref[rows, :] = _route(lg)


POST_SUB = 256


def _post_attn(y_sb, y_ca, gates, x2d, w_br_sb, w_br_ca, w_out, w_router, b_router, ln_g, ln_b, tm=512):
    n = x2d.shape[0]
    row = lambda i: (i, 0)
    fixed = lambda i: (0, 0)
    return pl.pallas_call(
        _post_kernel,
        grid=(n // tm,),
        in_specs=[
            pl.BlockSpec((tm, W_SB), row),
            pl.BlockSpec((tm, W_CA), row),
            pl.BlockSpec((tm, 2 * D_MODEL), row),
            pl.BlockSpec((tm, D_MODEL), row),
            pl.BlockSpec((W_SB, D_MODEL), fixed),
            pl.BlockSpec((W_CA, D_MODEL), fixed),
            pl.BlockSpec((D_MODEL, D_MODEL), fixed),
            pl.BlockSpec((D_MODEL, ROUTER_LANES), fixed),
            pl.BlockSpec((1, ROUTER_LANES), fixed),
            pl.BlockSpec((1, D_MODEL), fixed),
            pl.BlockSpec((1, D_MODEL), fixed),
        ],
        out_specs=[
            pl.BlockSpec((tm, D_MODEL), row),
            pl.BlockSpec((tm, ROUTER_LANES), row),
            pl.BlockSpec((tm, ROUTER_LANES), row),
        ],
        out_shape=[
            jax.ShapeDtypeStruct((n, D_MODEL), F32),
            jax.ShapeDtypeStruct((n, ROUTER_LANES), F32),
            jax.ShapeDtypeStruct((n, ROUTER_LANES), BF16),
        ],
        compiler_params=pltpu.CompilerParams(
            dimension_semantics=("arbitrary",), vmem_limit_bytes=VMEM_LIMIT),
        name="post_attn",
    )(y_sb, y_ca, gates, x2d, w_br_sb, w_br_ca, w_out, w_router, b_router, ln_g, ln_b)


MOE_TILE = 512
DEST_LANES = 8
TOP_K = 2


def _moe_rows(n):
    return n * TOP_K + N_EXPERTS * MOE_TILE


def _rank_kernel(pick_ref, info_ref, tri_ref, dest_ref, cnt_ref, run_ref, off_ref, *, tb):
    p = pl.program_id(0)
    i = pl.program_id(1)
    pick = pick_ref[...]

    @pl.when((p == 0) & (i == 0))
    def _():
        run_ref[...] = jnp.zeros_like(run_ref)

    @pl.when(p == 0)
    def _():
        run_ref[...] += jnp.sum(pick.astype(F32), axis=0, keepdims=True)

    @pl.when((p == 1) & (i == 0))
    def _():
        cnt = run_ref[...]
        cnt_ref[...] = cnt
        padded = jnp.ceil(cnt * (1.0 / MOE_TILE)) * MOE_TILE
        lane = lax.broadcasted_iota(jnp.int32, padded.shape, 1)
        scan = padded
        step = 1
        while step < ROUTER_LANES:
            scan = scan + jnp.where(lane >= step, pltpu.roll(scan, step, axis=1), 0.0)
            step *= 2
        off_ref[...] = scan - padded
        run_ref[...] = jnp.zeros_like(run_ref)

    @pl.when(p == 1)
    def _():
        seen = run_ref[0:1, :]
        earlier = _dot(tri_ref[...], pick)
        row_of = earlier + seen + off_ref[0:1, :]
        info = info_ref[...]
        lane = lax.broadcasted_iota(jnp.int32, info.shape, 1)
        lane_f = lane.astype(F32)
        e1 = jnp.sum(jnp.where(lane == ROUTE_E1, info, 0.0), axis=1, keepdims=True)
        e2 = jnp.sum(jnp.where(lane == ROUTE_E2, info, 0.0), axis=1, keepdims=True)
        d1 = jnp.sum(jnp.where(lane_f == e1 + N_GROUPS, row_of, 0.0), axis=1, keepdims=True)
        d2 = jnp.sum(jnp.where(lane_f == e2 + N_GROUPS, row_of, 0.0), axis=1, keepdims=True)
        dest = (jnp.where(lane == 0, d1, 0.0) + jnp.where(lane == 1, d2, 0.0)).astype(jnp.int32)
        dest_ref[...] = dest[:, 0:DEST_LANES]
        run_ref[...] += jnp.sum(pick.astype(F32), axis=0, keepdims=True)


def _moe_rank(pick, info, tb=1024):
    n = pick.shape[0]
    tri = jnp.asarray(np.arange(tb)[None, :] < np.arange(tb)[:, None], dtype=BF16)
    return pl.pallas_call(
        functools.partial(_rank_kernel, tb=tb),
        grid=(2, n // tb),
        in_specs=[
            pl.BlockSpec((tb, ROUTER_LANES), lambda p, i: (i, 0)),
            pl.BlockSpec((tb, ROUTER_LANES), lambda p, i: (i, 0)),
            pl.BlockSpec((tb, tb), lambda p, i: (0, 0)),
        ],
        out_specs=[
            pl.BlockSpec((tb, DEST_LANES), lambda p, i: (i * p, 0)),
            pl.BlockSpec((8, ROUTER_LANES), lambda p, i: (0, 0)),
        ],
        out_shape=[
            jax.ShapeDtypeStruct((n, DEST_LANES), jnp.int32),
            jax.ShapeDtypeStruct((8, ROUTER_LANES), F32),
        ],
        scratch_shapes=[
            pltpu.VMEM((8, ROUTER_LANES), F32),
            pltpu.VMEM((8, ROUTER_LANES), F32),
        ],
        compiler_params=pltpu.CompilerParams(
            dimension_semantics=("arbitrary", "arbitrary"), vmem_limit_bytes=VMEM_LIMIT),
        name="moe_rank",
    )(pick, info, tri)


def _row_copy(src_ref, src_row, dst_ref, dst_row, sem):
    return pltpu.make_async_copy(src_ref.at[pl.ds(src_row, 1), :], dst_ref.at[pl.ds(dst_row, 1), :], sem)


def _dispatch_kernel(d1_ref, d2_ref, last_ref, x_ref, xs_hbm, zero_ref, sem, zsem, *, tb):
    base = pl.program_id(0) * tb

    @pl.when(pl.program_id(0) == 0)
    def _():
        zero_ref[...] = jnp.zeros_like(zero_ref)

        def last_tile_copy(e):
            first_row = pl.multiple_of((last_ref[e] - 1) * MOE_TILE, MOE_TILE)
            return pltpu.make_async_copy(zero_ref, xs_hbm.at[pl.ds(first_row, MOE_TILE), :], zsem)

        def unused_tile_copy(t):
            return pltpu.make_async_copy(zero_ref, xs_hbm.at[pl.ds(t * MOE_TILE, MOE_TILE), :], zsem)

        n_tiles = xs_hbm.shape[0] // MOE_TILE
        min_used = n_tiles - N_EXPERTS
        for wait in (False, True):
            for e in range(N_EXPERTS):
                owns_tiles = last_ref[e] > (last_ref[e - 1] if e else 0)
                pl.when(owns_tiles)(
                    lambda e=e, wait=wait: last_tile_copy(e).wait() if wait else last_tile_copy(e).start())
            for t in range(min_used, n_tiles):
                pl.when(t >= last_ref[N_EXPERTS - 1])(
                    lambda t=t, wait=wait: unused_tile_copy(t).wait() if wait else unused_tile_copy(t).start())

    def issue(r, c):
        _row_copy(x_ref, r, xs_hbm, d1_ref[base + r], sem).start(priority=0)
        _row_copy(x_ref, r, xs_hbm, d2_ref[base + r], sem).start(priority=1)
        return c

    lax.fori_loop(0, tb, issue, 0, unroll=8)
    for _ in range(TOP_K):
        pltpu.make_async_copy(x_ref, xs_hbm.at[pl.ds(0, tb), :], sem).wait()


def _moe_dispatch(x1, dest1, dest2, last_tile, tb=512):
    n = x1.shape[0]
    rows = _moe_rows(n)
    grid_spec = pltpu.PrefetchScalarGridSpec(
        num_scalar_prefetch=3,
        grid=(n // tb,),
        in_specs=[pl.BlockSpec((tb, D_MODEL), lambda i, d1, d2, last: (i, 0))],
        out_specs=pl.BlockSpec(memory_space=pl.ANY),
        scratch_shapes=[
            pltpu.VMEM((MOE_TILE, D_MODEL), F32),
            pltpu.SemaphoreType.DMA(()),
            pltpu.SemaphoreType.DMA(()),
        ],
    )
    return pl.pallas_call(
        functools.partial(_dispatch_kernel, tb=tb),
        grid_spec=grid_spec,
        out_shape=jax.ShapeDtypeStruct((rows, D_MODEL), F32),
        compiler_params=pltpu.CompilerParams(
            dimension_semantics=("arbitrary",), vmem_limit_bytes=VMEM_LIMIT),
        name="moe_dispatch",
    )(dest1, dest2, last_tile, x1)


def _experts_kernel(te_ref, nt_ref, xs_ref, wg_ref, wu_ref, wd_ref, ys_ref, wgu_s, wd_s):
    t = pl.program_id(0)
    changed = (t == 0) | (te_ref[t] != te_ref[jnp.maximum(t - 1, 0)])

    @pl.when(changed)
    def _():
        wgu_s[:, 0:D_EXPERT] = wg_ref[...].astype(BF16)
        wgu_s[:, D_EXPERT:2 * D_EXPERT] = wu_ref[...].astype(BF16)
        wd_s[...] = wd_ref[...].astype(BF16)

    @pl.when(t < nt_ref[0])
    def _():
        gu = _dot(xs_ref[...].astype(BF16), wgu_s[...])
        gate = gu[:, 0:D_EXPERT]
        up = gu[:, D_EXPERT:2 * D_EXPERT]
        hid = (gate * (1.0 / (1.0 + jnp.exp(-gate)))) * up
        ys_ref[...] = _dot(hid.astype(BF16), wd_s[...])

    @pl.when(t >= nt_ref[0])
    def _():
        ys_ref[...] = jnp.zeros_like(ys_ref)


def _moe_experts(xs, tile_expert, n_tiles_used, w_gate, w_up, w_down, layer):
    rows = xs.shape[0]
    first = layer * N_EXPERTS
    grid_spec = pltpu.PrefetchScalarGridSpec(
        num_scalar_prefetch=2,
        grid=(rows // MOE_TILE,),
        in_specs=[
            pl.BlockSpec((MOE_TILE, D_MODEL), lambda t, te, nt: (jnp.minimum(t, nt[0] - 1), 0)),
            pl.BlockSpec((None, D_MODEL, D_EXPERT), lambda t, te, nt: (first + te[t], 0, 0)),
            pl.BlockSpec((None, D_MODEL, D_EXPERT), lambda t, te, nt: (first + te[t], 0, 0)),
            pl.BlockSpec((None, D_EXPERT, D_MODEL), lambda t, te, nt: (first + te[t], 0, 0)),
        ],
        out_specs=pl.BlockSpec((MOE_TILE, D_MODEL), lambda t, te, nt: (t, 0)),
        scratch_shapes=[
            pltpu.VMEM((D_MODEL, 2 * D_EXPERT), BF16),
            pltpu.VMEM((D_EXPERT, D_MODEL), BF16),
        ],
    )
    return pl.pallas_call(
        _experts_kernel,
        grid_spec=grid_spec,
        out_shape=jax.ShapeDtypeStruct((rows, D_MODEL), F32),
        compiler_params=pltpu.CompilerParams(
            dimension_semantics=("arbitrary",), vmem_limit_bytes=VMEM_LIMIT),
        name="moe_experts",
    )(tile_expert, n_tiles_used, xs, w_gate, w_up, w_down)


def _combine_kernel(d1_ref, d2_ref, x1_ref, info_ref, ys_hbm, g_ref, b_ref, out_ref, y1_ref, y2_ref, sem, *, tb):
    base = pl.program_id(0) * tb

    def issue(r, c):
        _row_copy(ys_hbm, d1_ref[base + r], y1_ref, r, sem).start(priority=0)
        _row_copy(ys_hbm, d2_ref[base + r], y2_ref, r, sem).start(priority=1)
        return c

    lax.fori_loop(0, tb, issue, 0, unroll=8)
    for y_ref in (y1_ref, y2_ref):
        pltpu.make_async_copy(ys_hbm.at[pl.ds(0, tb), :], y_ref, sem).wait()
    info = info_ref[...]
    lane = lax.broadcasted_iota(jnp.int32, info.shape, 1)
    w1 = jnp.sum(jnp.where(lane == ROUTE_W1, info, 0.0), axis=1, keepdims=True)
    w2 = jnp.sum(jnp.where(lane == ROUTE_W2, info, 0.0), axis=1, keepdims=True)
    ffn = w1 * y1_ref[...] + w2 * y2_ref[...]
    out_ref[...] = _layer_norm(ALPHA * x1_ref[...] + ffn, g_ref[...], b_ref[...])


def _moe_combine(x1, info, ys, dest1, dest2, ln_g, ln_b, tb=512):
    n = x1.shape[0]
    grid_spec = pltpu.PrefetchScalarGridSpec(
        num_scalar_prefetch=2,
        grid=(n // tb,),
        in_specs=[
            pl.BlockSpec((tb, D_MODEL), lambda i, d1, d2: (i, 0)),
            pl.BlockSpec((tb, ROUTER_LANES), lambda i, d1, d2: (i, 0)),
            pl.BlockSpec(memory_space=pl.ANY),
            pl.BlockSpec((1, D_MODEL), lambda i, d1, d2: (0, 0)),
            pl.BlockSpec((1, D_MODEL), lambda i, d1, d2: (0, 0)),
        ],
        out_specs=pl.BlockSpec((tb, D_MODEL), lambda i, d1, d2: (i, 0)),
        scratch_shapes=[
            pltpu.VMEM((tb, D_MODEL), F32),
            pltpu.VMEM((tb, D_MODEL), F32),
            pltpu.SemaphoreType.DMA(()),
        ],
    )
    return pl.pallas_call(
        functools.partial(_combine_kernel, tb=tb),
        grid_spec=grid_spec,
        out_shape=jax.ShapeDtypeStruct((n, D_MODEL), F32),
        compiler_params=pltpu.CompilerParams(
            dimension_semantics=("arbitrary",), vmem_limit_bytes=VMEM_LIMIT),
        name="moe_combine",
    )(dest1, dest2, x1, info, ys, ln_g, ln_b)


def _moe(x1, info, pick, w_gate, w_up, w_down, layer, ln_g, ln_b):
    n = x1.shape[0]
    dest, counts = _moe_rank(pick, info)
    dest1 = dest[:, 0]
    dest2 = dest[:, 1]
    tiles = jnp.ceil(counts[0, N_GROUPS:N_GROUPS + N_EXPERTS] * (1.0 / MOE_TILE)).astype(jnp.int32)
    last_tile = jnp.cumsum(tiles)
    tile_ids = jnp.arange(_moe_rows(n) // MOE_TILE, dtype=jnp.int32)
    tile_expert = jnp.minimum(jnp.sum(tile_ids[:, None] >= last_tile[None, :], axis=1), N_EXPERTS - 1)
    xs = _moe_dispatch(x1, dest1, dest2, last_tile)
    ys = _moe_experts(xs, tile_expert.astype(jnp.int32), last_tile[N_EXPERTS - 1:], w_gate, w_up, w_down, layer)
    return _moe_combine(x1, info, ys, dest1, dest2, ln_g, ln_b)


def _layer(x2d, batch, w_in, b_gate, rel_bias, w_br_sb, w_br_ca, w_out, ln1_g, ln1_b,
           w_group, b_group, w_erouter, b_erouter, w_gate, w_up, w_down, layer, ln2_g, ln2_b):
    n = x2d.shape[0]
    seq = n // batch
    scale = HEAD_DIM ** -0.5
    qscale = np.ones((1, D_QKV), np.float32)
    qscale[:, 0:W_SB] = scale * np.log2(np.e)
    qscale[:, 3 * W_SB:3 * W_SB + W_CA] = scale * np.log2(np.e)
    qkv, gates = _in_proj(x2d, w_in.astype(BF16), jnp.asarray(qscale), b_gate.reshape(1, 2 * D_MODEL))
    qkv = qkv.reshape(batch, seq, D_QKV)
    y_sb = _sb_attn(qkv).reshape(n, W_SB)
    y_ca = _ca_attn(qkv, _ca_bias_table(rel_bias)).reshape(n, W_CA)

    w_router = jnp.concatenate(
        [w_group, w_erouter.transpose(1, 0, 2).reshape(D_MODEL, N_EXPERTS)], axis=1)
    w_router = jnp.pad(w_router, ((0, 0), (0, ROUTER_LANES - N_GROUPS - N_EXPERTS)))
    b_router = jnp.pad(jnp.concatenate([b_group, b_erouter.reshape(N_EXPERTS)]),
                       (0, ROUTER_LANES - N_GROUPS - N_EXPERTS)).reshape(1, ROUTER_LANES)
    x1, info, pick = _post_attn(y_sb, y_ca, gates, x2d, w_br_sb.astype(BF16), w_br_ca.astype(BF16),
                        w_out.astype(BF16), w_router, b_router,
                        ln1_g.reshape(1, D_MODEL), ln1_b.reshape(1, D_MODEL))

    return _moe(x1, info, pick, w_gate, w_up, w_down, layer,
                ln2_g.reshape(1, D_MODEL), ln2_b.reshape(1, D_MODEL))


def kernel(x, w_in, b_gate, rel_bias, w_br_sb, w_br_ca, w_out, ln1_g, ln1_b, w_group, b_group,
           w_erouter, b_erouter, w_gate, w_up, w_down, ln2_g, ln2_b):
    batch, seq, d = x.shape
    h = x.reshape(batch * seq, d)
    w_gate = w_gate.reshape(DEPTH * N_EXPERTS, D_MODEL, D_EXPERT)
    w_up = w_up.reshape(DEPTH * N_EXPERTS, D_MODEL, D_EXPERT)
    w_down = w_down.reshape(DEPTH * N_EXPERTS, D_EXPERT, D_MODEL)
    for l in range(DEPTH):
        h = _layer(h, batch, w_in[l], b_gate[l], rel_bias[l], w_br_sb[l], w_br_ca[l], w_out[l],
                   ln1_g[l], ln1_b[l], w_group[l], b_group[l], w_erouter[l], b_erouter[l],
                   w_gate, w_up, w_down, l, ln2_g[l], ln2_b[l])
    return h.reshape(batch, seq, d)
```

```python
import functools

import jax
import jax.numpy as jnp
import numpy as np
from jax import lax
from jax.experimental import pallas as pl
from jax.experimental.pallas import tpu as pltpu

D_MODEL = 1024
DEPTH = 2
CHUNK = 64
HEAD_DIM = 64
H_SB = 8
H_CA = 8
W_SB = H_SB * HEAD_DIM
W_CA = H_CA * HEAD_DIM
N_PAST_CHUNKS = 8
REL_CLIP = 128
N_GROUPS = 4
EXPERTS_PER_GROUP = 8
N_EXPERTS = N_GROUPS * EXPERTS_PER_GROUP
D_EXPERT = 256
ALPHA = (2.0 * DEPTH) ** 0.25
LN_EPS = 1e-5
D_QKV = 3 * W_SB + 3 * W_CA
D_IN = D_QKV + 2 * D_MODEL
NEG_INF = -1e30

LANES = 128
HEADS_PER_TILE = LANES // HEAD_DIM
ROUTER_LANES = LANES
VMEM_LIMIT = 56 * 1024 * 1024

BF16 = jnp.bfloat16
F32 = jnp.float32

_NT = (((1,), (1,)), ((), ()))


def _dot(a, b):
    return jnp.dot(a, b, preferred_element_type=F32)


def _layer_norm(h, g, b):
    mu = jnp.mean(h, axis=-1, keepdims=True)
    hc = h - mu
    var = jnp.mean(hc * hc, axis=-1, keepdims=True)
    return hc * lax.rsqrt(var + LN_EPS) * g + b


def _split_bf16(a):
    hi = a.astype(BF16)
    lo = (a - hi.astype(F32)).astype(BF16)
    return hi, lo


def _in_proj_kernel(x_ref, w_ref, scale_ref, bg_ref, qkv_ref, gate_ref):
    xb = x_ref[...].astype(BF16)
    for c in range(D_QKV // D_MODEL):
        cols = slice(c * D_MODEL, (c + 1) * D_MODEL)
        acc = _dot(xb, w_ref[:, cols])
        qkv_ref[:, cols] = (acc * scale_ref[:, cols]).astype(BF16)
    for c in range(2):
        cols = slice(c * D_MODEL, (c + 1) * D_MODEL)
        wcols = slice(D_QKV + c * D_MODEL, D_QKV + (c + 1) * D_MODEL)
        logit = _dot(xb, w_ref[:, wcols]) + bg_ref[:, cols]
        gate_ref[:, cols] = 1.0 / (1.0 + jnp.exp(-logit))


def _in_proj(x2d, w_in_bf16, qscale, b_gate_row, tm=512):
    n = x2d.shape[0]
    return pl.pallas_call(
        _in_proj_kernel,
        grid=(n // tm,),
        in_specs=[
            pl.BlockSpec((tm, D_MODEL), lambda i: (i, 0)),
            pl.BlockSpec((D_MODEL, D_IN), lambda i: (0, 0)),
            pl.BlockSpec((1, D_QKV), lambda i: (0, 0)),
            pl.BlockSpec((1, 2 * D_MODEL), lambda i: (0, 0)),
        ],
        out_specs=[
            pl.BlockSpec((tm, D_QKV), lambda i: (i, 0)),
            pl.BlockSpec((tm, 2 * D_MODEL), lambda i: (i, 0)),
        ],
        out_shape=[
            jax.ShapeDtypeStruct((n, D_QKV), BF16),
            jax.ShapeDtypeStruct((n, 2 * D_MODEL), F32),
        ],
        compiler_params=pltpu.CompilerParams(
            dimension_semantics=("arbitrary",), vmem_limit_bytes=VMEM_LIMIT),
        name="in_proj",
    )(x2d, w_in_bf16, qscale, b_gate_row)


SB_DEAD_BITS = 160.0
SB_STEP_BLOCKS = 4


def _sb_kernel(q_ref, k_ref, v_ref, u_ref, o_ref, z_ref, arg_ref, rs_ref, acc_ref, car_ref, *, tq):
    step = pl.program_id(2)
    lane_q = lax.broadcasted_iota(jnp.int32, (tq, LANES), 1)

    def split_heads(x):
        zero = jnp.zeros_like(x)
        return jnp.where(lane_q < HEAD_DIM, x, zero), jnp.where(lane_q < HEAD_DIM, zero, x)

    blocks = [step * SB_STEP_BLOCKS + j for j in range(SB_STEP_BLOCKS)]
    q_heads = [split_heads(q_ref[pl.ds(j * tq, tq), :]) for j in range(SB_STEP_BLOCKS)]

    def key_rows(i, n):
        return pl.ds(pl.multiple_of(jnp.maximum(i - n, 0) * tq, tq), tq)

    def block_pairs(subs, n, diagonal):
        for j in subs:
            for b in range(2):
                k = k_ref[key_rows(blocks[j], n + b), :]
                for h in range(HEADS_PER_TILE):
                    z = lax.dot_general(q_heads[j][h], k, _NT, preferred_element_type=F32)
                    if diagonal and b == 0:
                        row = lax.broadcasted_iota(jnp.int32, (tq, tq), 0)
                        col = lax.broadcasted_iota(jnp.int32, (tq, tq), 1)
                        z = jnp.where(col < row, z, NEG_INF)
                    z_ref[j, b, h] = z
        for j in subs:
            for b in range(2):
                for h in range(HEADS_PER_TILE):
                    z = z_ref[j, b, h]
                    sp = jnp.maximum(z, 0.0) + jnp.log2(1.0 + jnp.exp2(-jnp.abs(z)))
                    sums = _dot(sp.astype(BF16), u_ref[...])
                    arg_ref[j, b, h] = (z - sp) - sums[:, 0:tq]
                    rs_ref[j, b, h] = sums[:, tq:tq + LANES]
        least = []
        for j in subs:
            carries = [car_ref[j]]
            for b in range(2):
                carries.append(carries[b] + rs_ref[j, b])
            car_ref[j] = carries[2]
            least.append(jnp.min(carries[2]))
            v_parts = []
            for b in range(2):
                v = v_ref[key_rows(blocks[j], n + b), :]
                if b == 1:
                    v = jnp.where(n + b <= blocks[j], v, jnp.zeros_like(v))
                v_parts += split_heads(v)
            ws = []
            for b in range(2):
                for h in range(HEADS_PER_TILE):
                    carry = jnp.concatenate([carries[b][h]] * (tq // LANES), axis=1)
                    ws.append(jnp.exp2(arg_ref[j, b, h] - carry).astype(BF16))
            acc_ref[j] += _dot(jnp.concatenate(ws, axis=1), jnp.concatenate(v_parts, axis=0))
        return least

    acc_ref[...] = jnp.zeros_like(acc_ref)
    car_ref[...] = jnp.zeros_like(car_ref)
    first = block_pairs(range(SB_STEP_BLOCKS), 0, True)

    for j in range(SB_STEP_BLOCKS):
        def more(state, j=j):
            n, least = state
            return (n <= blocks[j]) & (least < SB_DEAD_BITS)

        def body(state, j=j):
            n, _ = state
            return n + 2, block_pairs([j], n, False)[0]

        lax.while_loop(more, body, (jnp.int32(2), first[j]))
        o_ref[pl.ds(j * tq, tq), :] = acc_ref[j].astype(o_ref.dtype)


def _sb_attn(qkv, b, tq=256):
    s = qkv.shape[0] // b
    n_tiles = W_SB // LANES
    step_rows = SB_STEP_BLOCKS * tq
    steps = s // step_rows
    u = jnp.asarray(np.concatenate([np.arange(tq)[:, None] > np.arange(tq)[None, :],
                                    np.ones((tq, LANES), bool)], axis=1), dtype=BF16)
    return pl.pallas_call(
        functools.partial(_sb_kernel, tq=tq),
        grid=(b, n_tiles, steps),
        in_specs=[
            pl.BlockSpec((step_rows, LANES), lambda bi, hp, i: (bi * steps + i, hp)),
            pl.BlockSpec((s, LANES), lambda bi, hp, i: (bi, n_tiles + hp)),
            pl.BlockSpec((s, LANES), lambda bi, hp, i: (bi, 2 * n_tiles + hp)),
            pl.BlockSpec((tq, tq + LANES), lambda bi, hp, i: (0, 0)),
        ],
        out_specs=pl.BlockSpec((step_rows, LANES), lambda bi, hp, i: (bi * steps + i, hp)),
        out_shape=jax.ShapeDtypeStruct((b * s, W_SB), BF16),
        scratch_shapes=[
            pltpu.VMEM((SB_STEP_BLOCKS, 2, HEADS_PER_TILE, tq, tq), F32),
            pltpu.VMEM((SB_STEP_BLOCKS, 2, HEADS_PER_TILE, tq, tq), F32),
            pltpu.VMEM((SB_STEP_BLOCKS, 2, HEADS_PER_TILE, tq, LANES), F32),
            pltpu.VMEM((SB_STEP_BLOCKS, tq, LANES), F32),
            pltpu.VMEM((SB_STEP_BLOCKS, HEADS_PER_TILE, tq, LANES), F32),
        ],
        compiler_params=pltpu.CompilerParams(
            dimension_semantics=("arbitrary", "arbitrary", "arbitrary"), vmem_limit_bytes=VMEM_LIMIT),
        name="sb_attn",
    )(qkv, qkv, qkv, u)


CA_GROUP = 4
CA_STEP_GROUPS = 8
CA_TQ = CA_GROUP * CHUNK
CA_BAND = (CA_GROUP + N_PAST_CHUNKS) * CHUNK
CA_PAD = N_PAST_CHUNKS * CHUNK


def _ca_kernel(q_ref, k_ref, v_ref, bias_ref, ones_ref, o_ref, kp_ref, vp_ref):
    c = pl.program_id(2)
    s = k_ref.shape[0]

    @pl.when(c == 0)
    def _():
        kp_ref[0:CA_PAD, :] = jnp.zeros((CA_PAD, LANES), BF16)
        vp_ref[0:CA_PAD, :] = jnp.zeros((CA_PAD, LANES), BF16)
        kp_ref[CA_PAD:CA_PAD + s, :] = k_ref[...]
        vp_ref[CA_PAD:CA_PAD + s, :] = v_ref[...]

    lane_q = lax.broadcasted_iota(jnp.int32, (CA_TQ, LANES), 1)
    lane_v = lax.broadcasted_iota(jnp.int32, (CA_BAND, LANES), 1)

    def attend(g, masked):
        group = c * CA_STEP_GROUPS + g
        start = pl.multiple_of(group * CA_TQ, CA_TQ)
        rows = pl.ds(g * CA_TQ, CA_TQ)
        kb = kp_ref[pl.ds(start, CA_BAND), :]
        vb = vp_ref[pl.ds(start, CA_BAND), :]
        q = q_ref[rows, :]
        zero_q = jnp.zeros_like(q)
        zero_v = jnp.zeros_like(vb)
        q_heads = (jnp.where(lane_q < HEAD_DIM, q, zero_q), jnp.where(lane_q < HEAD_DIM, zero_q, q))
        v_heads = jnp.concatenate(
            [jnp.where(lane_v < HEAD_DIM, vb, zero_v), jnp.where(lane_v < HEAD_DIM, zero_v, vb)], axis=0)
        v_and_ones = jnp.concatenate([v_heads, ones_ref[...]], axis=1)
        es = []
        for h in range(HEADS_PER_TILE):
            sc = lax.dot_general(q_heads[h], kb, _NT, preferred_element_type=F32) + bias_ref[h]
            if masked:
                pos = lax.broadcasted_iota(jnp.int32, (CA_TQ, CA_BAND), 1)
                sc = jnp.where(pos >= CA_PAD - group * CA_TQ, sc, NEG_INF)
            m = jnp.max(sc, axis=1, keepdims=True)
            es.append(jnp.exp2(sc - m).astype(BF16))
        both = _dot(jnp.concatenate(es, axis=1), v_and_ones)
        o_ref[rows, :] = (both[:, 0:LANES] / both[:, LANES:2 * LANES]).astype(o_ref.dtype)

    def step(masked):
        for g in range(CA_STEP_GROUPS):
            attend(g, masked)

    assert CA_PAD // CA_TQ <= CA_STEP_GROUPS
    pl.when(c == 0)(lambda: step(True))
    pl.when(c > 0)(lambda: step(False))


def _ca_bias_table(rel_bias):
    h = rel_bias.shape[0]
    r = np.arange(CA_TQ)[:, None]
    p = np.arange(CA_BAND)[None, :]
    qc = r // CHUNK
    kc = p // CHUNK
    in_band = (kc >= qc) & (kc <= qc + N_PAST_CHUNKS)
    n_far = CA_PAD + CA_TQ - 1 - REL_CLIP
    n_neg = CA_BAND - 1 - CA_PAD - REL_CLIP
    rb = rel_bias.astype(F32)
    line = jnp.concatenate([jnp.broadcast_to(rb[:, 2 * REL_CLIP:], (h, n_far)), rb[:, ::-1],
                            jnp.broadcast_to(rb[:, :1], (h, n_neg))], axis=1)
    length = CA_TQ + CA_BAND - 1
    assert line.shape[1] == length
    line = jnp.pad(line, ((0, 0), (0, 1)))
    skew = jnp.tile(line, (1, CA_TQ))[:, :CA_TQ * length].reshape(h, CA_TQ, length)
    table = skew[:, :, CA_TQ - 1:]
    return jnp.where(jnp.asarray(in_band)[None], table * np.float32(np.log2(np.e)), NEG_INF)


def _ca_attn(qkv, b, bias_table):
    s = qkv.shape[0] // b
    n_tiles = W_CA // LANES
    base = 3 * W_SB // LANES
    step_rows = CA_STEP_GROUPS * CA_TQ
    steps = s // step_rows
    head_of_row = np.arange(HEADS_PER_TILE * CA_BAND)[:, None] // CA_BAND
    head_of_lane = np.arange(LANES)[None, :] // HEAD_DIM
    ones = jnp.asarray(head_of_row == head_of_lane, dtype=BF16)
    return pl.pallas_call(
        _ca_kernel,
        grid=(b, n_tiles, steps),
        in_specs=[
            pl.BlockSpec((step_rows, LANES), lambda bi, hp, c: (bi * steps + c, base + hp)),
            pl.BlockSpec((s, LANES), lambda bi, hp, c: (bi, base + n_tiles + hp)),
            pl.BlockSpec((s, LANES), lambda bi, hp, c: (bi, base + 2 * n_tiles + hp)),
            pl.BlockSpec((HEADS_PER_TILE, CA_TQ, CA_BAND), lambda bi, hp, c: (hp, 0, 0)),
            pl.BlockSpec((HEADS_PER_TILE * CA_BAND, LANES), lambda bi, hp, c: (0, 0)),
        ],
        out_specs=pl.BlockSpec((step_rows, LANES), lambda bi, hp, c: (bi * steps + c, hp)),
        out_shape=jax.ShapeDtypeStruct((b * s, W_CA), BF16),
        scratch_shapes=[
            pltpu.VMEM((CA_PAD + s, LANES), BF16),
            pltpu.VMEM((CA_PAD + s, LANES), BF16),
        ],
        compiler_params=pltpu.CompilerParams(
            dimension_semantics=("arbitrary", "arbitrary", "arbitrary"), vmem_limit_bytes=VMEM_LIMIT),
        name="ca_attn",
    )(qkv, qkv, qkv, bias_table, ones)


ROUTE_E1, ROUTE_E2, ROUTE_W1, ROUTE_W2 = 0, 1, 2, 3


def _route(lg):
    lane = lax.broadcasted_iota(jnp.int32, lg.shape, 1)
    big = jnp.int32(ROUTER_LANES)
    is_group = lane < N_GROUPS
    g_max = jnp.max(jnp.where(is_group, lg, -jnp.inf), axis=1, keepdims=True)
    g_idx = jnp.min(jnp.where(is_group & (lg == g_max), lane, big), axis=1, keepdims=True)
    g_den = jnp.sum(jnp.where(is_group, jnp.exp(lg - g_max), 0.0), axis=1, keepdims=True)
    g_val = 1.0 / g_den
    lo = N_GROUPS + EXPERTS_PER_GROUP * g_idx
    in_group = (lane >= lo) & (lane < lo + EXPERTS_PER_GROUP)
    v1 = jnp.max(jnp.where(in_group, lg, -jnp.inf), axis=1, keepdims=True)
    i1 = jnp.min(jnp.where(in_group & (lg == v1), lane, big), axis=1, keepdims=True)
    rest = in_group & (lane != i1)
    v2 = jnp.max(jnp.where(rest, lg, -jnp.inf), axis=1, keepdims=True)
    i2 = jnp.min(jnp.where(rest & (lg == v2), lane, big), axis=1, keepdims=True)
    e2 = jnp.exp(v2 - v1)
    w1 = g_val / (1.0 + e2)
    w2 = g_val * e2 / (1.0 + e2)
    picked = ((lane == i1) | (lane == i2)).astype(BF16)
    info = (jnp.where(lane == ROUTE_E1, (i1 - N_GROUPS).astype(F32), 0.0)
            + jnp.where(lane == ROUTE_E2, (i2 - N_GROUPS).astype(F32), 0.0)
            + jnp.where(lane == ROUTE_W1, w1, 0.0) + jnp.where(lane == ROUTE_W2, w2, 0.0))
    return info, picked


def _post_kernel(ysb_ref, yca_ref, gate_ref, x_ref, wsb_ref, wca_ref, wout_ref, wr_ref, br_ref,
                 g_ref, b_ref, x1_ref, info_ref, pick_ref):
    w_hi, w_lo = _split_bf16(wr_ref[...])
    w_split = jnp.concatenate([w_hi, w_lo], axis=1)
    tm = x_ref.shape[0]
    for rows in [pl.ds(s * POST_SUB, POST_SUB) for s in range(tm // POST_SUB)]:
        a = _dot(ysb_ref[rows, :], wsb_ref[...])
        c = _dot(yca_ref[rows, :], wca_ref[...])
        mix = gate_ref[rows, 0:D_MODEL] * a + gate_ref[rows, D_MODEL:2 * D_MODEL] * c
        mixed = _dot(mix.astype(BF16), wout_ref[...])
        x1 = _layer_norm(ALPHA * x_ref[rows, :] + mixed, g_ref[...], b_ref[...])
        x1_ref[rows, :] = x1
        x_hi, x_lo = _split_bf16(x1)
        parts = _dot(jnp.concatenate([x_hi, x_lo], axis=0), w_split)
        lg = (parts[0:POST_SUB, 0:ROUTER_LANES] + parts[0:POST_SUB, ROUTER_LANES:]
              + parts[POST_SUB:, 0:ROUTER_LANES] + parts[POST_SUB:, ROUTER_LANES:]) + br_ref[...]
        info_ref[rows, :], pick_ref[rows, :] = _route(lg)


POST_SUB = 256


def _post_attn(y_sb, y_ca, gates, x2d, w_br_sb, w_br_ca, w_out, w_router, b_router, ln_g, ln_b, tm=512):
    n = x2d.shape[0]
    row = lambda i: (i, 0)
    fixed = lambda i: (0, 0)
    return pl.pallas_call(
        _post_kernel,
        grid=(n // tm,),
        in_specs=[
            pl.BlockSpec((tm, W_SB), row),
            pl.BlockSpec((tm, W_CA), row),
            pl.BlockSpec((tm, 2 * D_MODEL), row),
            pl.BlockSpec((tm, D_MODEL), row),
            pl.BlockSpec((W_SB, D_MODEL), fixed),
            pl.BlockSpec((W_CA, D_MODEL), fixed),
            pl.BlockSpec((D_MODEL, D_MODEL), fixed),
            pl.BlockSpec((D_MODEL, ROUTER_LANES), fixed),
            pl.BlockSpec((1, ROUTER_LANES), fixed),
            pl.BlockSpec((1, D_MODEL), fixed),
            pl.BlockSpec((1, D_MODEL), fixed),
        ],
        out_specs=[
            pl.BlockSpec((tm, D_MODEL), row),
            pl.BlockSpec((tm, ROUTER_LANES), row),
            pl.BlockSpec((tm, ROUTER_LANES), row),
        ],
        out_shape=[
            jax.ShapeDtypeStruct((n, D_MODEL), F32),
            jax.ShapeDtypeStruct((n, ROUTER_LANES), F32),
            jax.ShapeDtypeStruct((n, ROUTER_LANES), BF16),
        ],
        compiler_params=pltpu.CompilerParams(
            dimension_semantics=("arbitrary",), vmem_limit_bytes=VMEM_LIMIT),
        name="post_attn",
    )(y_sb, y_ca, gates, x2d, w_br_sb, w_br_ca, w_out, w_router, b_router, ln_g, ln_b)


MOE_TILE = 512
DEST_LANES = 8
TOP_K = 2


def _moe_rows(n):
    return n * TOP_K + N_EXPERTS * MOE_TILE


def _rank_kernel(pick_ref, info_ref, tri_ref, dest_ref, cnt_ref, run_ref, off_ref, *, tb):
    p = pl.program_id(0)
    i = pl.program_id(1)
    pick = pick_ref[...]

    @pl.when((p == 0) & (i == 0))
    def _():
        run_ref[...] = jnp.zeros_like(run_ref)

    @pl.when(p == 0)
    def _():
        run_ref[...] += jnp.sum(pick.astype(F32), axis=0, keepdims=True)

    @pl.when((p == 1) & (i == 0))
    def _():
        cnt = run_ref[...]
        cnt_ref[...] = cnt
        padded = jnp.ceil(cnt * (1.0 / MOE_TILE)) * MOE_TILE
        lane = lax.broadcasted_iota(jnp.int32, padded.shape, 1)
        scan = padded
        step = 1
        while step < ROUTER_LANES:
            scan = scan + jnp.where(lane >= step, pltpu.roll(scan, step, axis=1), 0.0)
            step *= 2
        off_ref[...] = scan - padded
        run_ref[...] = jnp.zeros_like(run_ref)

    @pl.when(p == 1)
    def _():
        seen = run_ref[0:1, :]
        earlier = _dot(tri_ref[...], pick)
        row_of = earlier + seen + off_ref[0:1, :]
        info = info_ref[...]
        lane = lax.broadcasted_iota(jnp.int32, info.shape, 1)
        lane_f = lane.astype(F32)
        e1 = jnp.sum(jnp.where(lane == ROUTE_E1, info, 0.0), axis=1, keepdims=True)
        e2 = jnp.sum(jnp.where(lane == ROUTE_E2, info, 0.0), axis=1, keepdims=True)
        d1 = jnp.sum(jnp.where(lane_f == e1 + N_GROUPS, row_of, 0.0), axis=1, keepdims=True)
        d2 = jnp.sum(jnp.where(lane_f == e2 + N_GROUPS, row_of, 0.0), axis=1, keepdims=True)
        dest = (jnp.where(lane == 0, d1, 0.0) + jnp.where(lane == 1, d2, 0.0)).astype(jnp.int32)
        dest_ref[...] = dest[:, 0:DEST_LANES]
        run_ref[...] += jnp.sum(pick.astype(F32), axis=0, keepdims=True)


def _moe_rank(pick, info, tb=1024):
    n = pick.shape[0]
    tri = jnp.asarray(np.arange(tb)[None, :] < np.arange(tb)[:, None], dtype=BF16)
    return pl.pallas_call(
        functools.partial(_rank_kernel, tb=tb),
        grid=(2, n // tb),
        in_specs=[
            pl.BlockSpec((tb, ROUTER_LANES), lambda p, i: (i, 0)),
            pl.BlockSpec((tb, ROUTER_LANES), lambda p, i: (i, 0)),
            pl.BlockSpec((tb, tb), lambda p, i: (0, 0)),
        ],
        out_specs=[
            pl.BlockSpec((tb, DEST_LANES), lambda p, i: (i * p, 0)),
            pl.BlockSpec((8, ROUTER_LANES), lambda p, i: (0, 0)),
        ],
        out_shape=[
            jax.ShapeDtypeStruct((n, DEST_LANES), jnp.int32),
            jax.ShapeDtypeStruct((8, ROUTER_LANES), F32),
        ],
        scratch_shapes=[
            pltpu.VMEM((8, ROUTER_LANES), F32),
            pltpu.VMEM((8, ROUTER_LANES), F32),
        ],
        compiler_params=pltpu.CompilerParams(
            dimension_semantics=("arbitrary", "arbitrary"), vmem_limit_bytes=VMEM_LIMIT),
        name="moe_rank",
    )(pick, info, tri)


def _row_copy(src_ref, src_row, dst_ref, dst_row, sem):
    return pltpu.make_async_copy(src_ref.at[pl.ds(src_row, 1), :], dst_ref.at[pl.ds(dst_row, 1), :], sem)


def _dispatch_kernel(d1_ref, d2_ref, last_ref, x_ref, xs_hbm, zero_ref, sem, zsem, *, tb):
    base = pl.program_id(0) * tb

    @pl.when(pl.program_id(0) == 0)
    def _():
        zero_ref[...] = jnp.zeros_like(zero_ref)

        def last_tile_copy(e):
            first_row = pl.multiple_of((last_ref[e] - 1) * MOE_TILE, MOE_TILE)
            return pltpu.make_async_copy(zero_ref, xs_hbm.at[pl.ds(first_row, MOE_TILE), :], zsem)

        def unused_tile_copy(t):
            return pltpu.make_async_copy(zero_ref, xs_hbm.at[pl.ds(t * MOE_TILE, MOE_TILE), :], zsem)

        n_tiles = xs_hbm.shape[0] // MOE_TILE
        min_used = n_tiles - N_EXPERTS
        for wait in (False, True):
            for e in range(N_EXPERTS):
                owns_tiles = last_ref[e] > (last_ref[e - 1] if e else 0)
                pl.when(owns_tiles)(
                    lambda e=e, wait=wait: last_tile_copy(e).wait() if wait else last_tile_copy(e).start())
            for t in range(min_used, n_tiles):
                pl.when(t >= last_ref[N_EXPERTS - 1])(
                    lambda t=t, wait=wait: unused_tile_copy(t).wait() if wait else unused_tile_copy(t).start())

    def issue(r, c):
        _row_copy(x_ref, r, xs_hbm, d1_ref[base + r], sem).start(priority=0)
        _row_copy(x_ref, r, xs_hbm, d2_ref[base + r], sem).start(priority=1)
        return c

    lax.fori_loop(0, tb, issue, 0, unroll=8)
    for _ in range(TOP_K):
        pltpu.make_async_copy(x_ref, xs_hbm.at[pl.ds(0, tb), :], sem).wait()


def _moe_dispatch(x1, dest1, dest2, last_tile, tb=512):
    n = x1.shape[0]
    rows = _moe_rows(n)
    grid_spec = pltpu.PrefetchScalarGridSpec(
        num_scalar_prefetch=3,
        grid=(n // tb,),
        in_specs=[pl.BlockSpec((tb, D_MODEL), lambda i, d1, d2, last: (i, 0))],
        out_specs=pl.BlockSpec(memory_space=pl.ANY),
        scratch_shapes=[
            pltpu.VMEM((MOE_TILE, D_MODEL), F32),
            pltpu.SemaphoreType.DMA(()),
            pltpu.SemaphoreType.DMA(()),
        ],
    )
    return pl.pallas_call(
        functools.partial(_dispatch_kernel, tb=tb),
        grid_spec=grid_spec,
        out_shape=jax.ShapeDtypeStruct((rows, D_MODEL), F32),
        compiler_params=pltpu.CompilerParams(
            dimension_semantics=("arbitrary",), vmem_limit_bytes=VMEM_LIMIT),
        name="moe_dispatch",
    )(dest1, dest2, last_tile, x1)


def _experts_kernel(te_ref, nt_ref, xs_ref, wg_ref, wu_ref, wd_ref, ys_ref, wgu_s, wd_s):
    t = pl.program_id(0)
    changed = (t == 0) | (te_ref[t] != te_ref[jnp.maximum(t - 1, 0)])

    @pl.when(changed)
    def _():
        wgu_s[:, 0:D_EXPERT] = wg_ref[...].astype(BF16)
        wgu_s[:, D_EXPERT:2 * D_EXPERT] = wu_ref[...].astype(BF16)
        wd_s[...] = wd_ref[...].astype(BF16)

    @pl.when(t < nt_ref[0])
    def _():
        gu = _dot(xs_ref[...].astype(BF16), wgu_s[...])
        gate = gu[:, 0:D_EXPERT]
        up = gu[:, D_EXPERT:2 * D_EXPERT]
        hid = (gate * (1.0 / (1.0 + jnp.exp(-gate)))) * up
        ys_ref[...] = _dot(hid.astype(BF16), wd_s[...])

    @pl.when(t >= nt_ref[0])
    def _():
        ys_ref[...] = jnp.zeros_like(ys_ref)


def _moe_experts(xs, tile_expert, n_tiles_used, w_gate, w_up, w_down, layer):
    rows = xs.shape[0]
    first = layer * N_EXPERTS
    grid_spec = pltpu.PrefetchScalarGridSpec(
        num_scalar_prefetch=2,
        grid=(rows // MOE_TILE,),
        in_specs=[
            pl.BlockSpec((MOE_TILE, D_MODEL), lambda t, te, nt: (jnp.minimum(t, nt[0] - 1), 0)),
            pl.BlockSpec((None, D_MODEL, D_EXPERT), lambda t, te, nt: (first + te[t], 0, 0)),
            pl.BlockSpec((None, D_MODEL, D_EXPERT), lambda t, te, nt: (first + te[t], 0, 0)),
            pl.BlockSpec((None, D_EXPERT, D_MODEL), lambda t, te, nt: (first + te[t], 0, 0)),
        ],
        out_specs=pl.BlockSpec((MOE_TILE, D_MODEL), lambda t, te, nt: (t, 0)),
        scratch_shapes=[
            pltpu.VMEM((D_MODEL, 2 * D_EXPERT), BF16),
            pltpu.VMEM((D_EXPERT, D_MODEL), BF16),
        ],
    )
    return pl.pallas_call(
        _experts_kernel,
        grid_spec=grid_spec,
        out_shape=jax.ShapeDtypeStruct((rows, D_MODEL), F32),
        compiler_params=pltpu.CompilerParams(
            dimension_semantics=("arbitrary",), vmem_limit_bytes=VMEM_LIMIT),
        name="moe_experts",
    )(tile_expert, n_tiles_used, xs, w_gate, w_up, w_down)


def _combine_kernel(d1_ref, d2_ref, x1_ref, info_ref, ys_hbm, g_ref, b_ref, out_ref, y1_ref, y2_ref, sem, *, tb):
    base = pl.program_id(0) * tb

    def issue(r, c):
        _row_copy(ys_hbm, d1_ref[base + r], y1_ref, r, sem).start(priority=0)
        _row_copy(ys_hbm, d2_ref[base + r], y2_ref, r, sem).start(priority=1)
        return c

    lax.fori_loop(0, tb, issue, 0, unroll=8)
    for y_ref in (y1_ref, y2_ref):
        pltpu.make_async_copy(ys_hbm.at[pl.ds(0, tb), :], y_ref, sem).wait()
    info = info_ref[...]
    lane = lax.broadcasted_iota(jnp.int32, info.shape, 1)
    w1 = jnp.sum(jnp.where(lane == ROUTE_W1, info, 0.0), axis=1, keepdims=True)
    w2 = jnp.sum(jnp.where(lane == ROUTE_W2, info, 0.0), axis=1, keepdims=True)
    ffn = w1 * y1_ref[...] + w2 * y2_ref[...]
    out_ref[...] = _layer_norm(ALPHA * x1_ref[...] + ffn, g_ref[...], b_ref[...])


def _moe_combine(x1, info, ys, dest1, dest2, ln_g, ln_b, tb=512):
    n = x1.shape[0]
    grid_spec = pltpu.PrefetchScalarGridSpec(
        num_scalar_prefetch=2,
        grid=(n // tb,),
        in_specs=[
            pl.BlockSpec((tb, D_MODEL), lambda i, d1, d2: (i, 0)),
            pl.BlockSpec((tb, ROUTER_LANES), lambda i, d1, d2: (i, 0)),
            pl.BlockSpec(memory_space=pl.ANY),
            pl.BlockSpec((1, D_MODEL), lambda i, d1, d2: (0, 0)),
            pl.BlockSpec((1, D_MODEL), lambda i, d1, d2: (0, 0)),
        ],
        out_specs=pl.BlockSpec((tb, D_MODEL), lambda i, d1, d2: (i, 0)),
        scratch_shapes=[
            pltpu.VMEM((tb, D_MODEL), F32),
            pltpu.VMEM((tb, D_MODEL), F32),
            pltpu.SemaphoreType.DMA(()),
        ],
    )
    return pl.pallas_call(
        functools.partial(_combine_kernel, tb=tb),
        grid_spec=grid_spec,
        out_shape=jax.ShapeDtypeStruct((n, D_MODEL), F32),
        compiler_params=pltpu.CompilerParams(
            dimension_semantics=("arbitrary",), vmem_limit_bytes=VMEM_LIMIT),
        name="moe_combine",
    )(dest1, dest2, x1, info, ys, ln_g, ln_b)


def _moe(x1, info, pick, w_gate, w_up, w_down, layer, ln_g, ln_b):
    n = x1.shape[0]
    dest, counts = _moe_rank(pick, info)
    dest1 = dest[:, 0]
    dest2 = dest[:, 1]
    tiles = jnp.ceil(counts[0, N_GROUPS:N_GROUPS + N_EXPERTS] * (1.0 / MOE_TILE)).astype(jnp.int32)
    upto = np.arange(N_EXPERTS)[None, :] <= np.arange(N_EXPERTS)[:, None]
    last_tile = jnp.sum(jnp.where(upto, tiles[None, :], 0), axis=1)
    tile_ids = jnp.arange(_moe_rows(n) // MOE_TILE, dtype=jnp.int32)
    tile_expert = jnp.minimum(jnp.sum(tile_ids[:, None] >= last_tile[None, :], axis=1), N_EXPERTS - 1)
    xs = _moe_dispatch(x1, dest1, dest2, last_tile)
    ys = _moe_experts(xs, tile_expert.astype(jnp.int32), last_tile[N_EXPERTS - 1:], w_gate, w_up, w_down, layer)
    return _moe_combine(x1, info, ys, dest1, dest2, ln_g, ln_b)


def _layer(x2d, batch, w_in, b_gate, rel_bias, w_br_sb, w_br_ca, w_out, ln1_g, ln1_b,
           w_group, b_group, w_erouter, b_erouter, w_gate, w_up, w_down, layer, ln2_g, ln2_b):
    n = x2d.shape[0]
    seq = n // batch
    scale = HEAD_DIM ** -0.5
    qscale = np.ones((1, D_QKV), np.float32)
    qscale[:, 0:W_SB] = scale * np.log2(np.e)
    qscale[:, 3 * W_SB:3 * W_SB + W_CA] = scale * np.log2(np.e)
    qkv, gates = _in_proj(x2d, w_in.astype(BF16), jnp.asarray(qscale), b_gate.reshape(1, 2 * D_MODEL))
    y_sb = _sb_attn(qkv, batch)
    y_ca = _ca_attn(qkv, batch, _ca_bias_table(rel_bias))

    w_router = jnp.concatenate(
        [w_group, w_erouter.transpose(1, 0, 2).reshape(D_MODEL, N_EXPERTS)], axis=1)
    w_router = jnp.pad(w_router, ((0, 0), (0, ROUTER_LANES - N_GROUPS - N_EXPERTS)))
    b_router = jnp.pad(jnp.concatenate([b_group, b_erouter.reshape(N_EXPERTS)]),
                       (0, ROUTER_LANES - N_GROUPS - N_EXPERTS)).reshape(1, ROUTER_LANES)
    x1, info, pick = _post_attn(y_sb, y_ca, gates, x2d, w_br_sb.astype(BF16), w_br_ca.astype(BF16),
                        w_out.astype(BF16), w_router, b_router,
                        ln1_g.reshape(1, D_MODEL), ln1_b.reshape(1, D_MODEL))

    return _moe(x1, info, pick, w_gate, w_up, w_down, layer,
                ln2_g.reshape(1, D_MODEL), ln2_b.reshape(1, D_MODEL))


def kernel(x, w_in, b_gate, rel_bias, w_br_sb, w_br_ca, w_out, ln1_g, ln1_b, w_group, b_group,
           w_erouter, b_erouter, w_gate, w_up, w_down, ln2_g, ln2_b):
    batch, seq, d = x.shape
    h = x.reshape(batch * seq, d)
    w_gate = w_gate.reshape(DEPTH * N_EXPERTS, D_MODEL, D_EXPERT)
    w_up = w_up.reshape(DEPTH * N_EXPERTS, D_MODEL, D_EXPERT)
    w_down = w_down.reshape(DEPTH * N_EXPERTS, D_EXPERT, D_MODEL)
    for l in range(DEPTH):
        h = _layer(h, batch, w_in[l], b_gate[l], rel_bias[l], w_br_sb[l], w_br_ca[l], w_out[l],
                   ln1_g[l], ln1_b[l], w_group[l], b_group[l], w_erouter[l], b_erouter[l],
                   w_gate, w_up, w_down, l, ln2_g[l], ln2_b[l])
    return h.reshape(batch, seq, d)
```

```python
import functools

import jax
import jax.numpy as jnp
import numpy as np
from jax import lax
from jax.experimental import pallas as pl
from jax.experimental.pallas import tpu as pltpu

D_MODEL = 1024
DEPTH = 2
CHUNK = 64
HEAD_DIM = 64
H_SB = 8
H_CA = 8
W_SB = H_SB * HEAD_DIM
W_CA = H_CA * HEAD_DIM
N_PAST_CHUNKS = 8
REL_CLIP = 128
N_GROUPS = 4
EXPERTS_PER_GROUP = 8
N_EXPERTS = N_GROUPS * EXPERTS_PER_GROUP
D_EXPERT = 256
ALPHA = (2.0 * DEPTH) ** 0.25
LN_EPS = 1e-5
D_QKV = 3 * W_SB + 3 * W_CA
D_IN = D_QKV + 2 * D_MODEL
NEG_INF = -1e30

LANES = 128
HEADS_PER_TILE = LANES // HEAD_DIM
ROUTER_LANES = LANES
VMEM_LIMIT = 56 * 1024 * 1024

BF16 = jnp.bfloat16
F32 = jnp.float32

_NT = (((1,), (1,)), ((), ()))


def _dot(a, b):
    return jnp.dot(a, b, preferred_element_type=F32)


def _layer_norm(h, g, b):
    mu = jnp.mean(h, axis=-1, keepdims=True)
    hc = h - mu
    var = jnp.mean(hc * hc, axis=-1, keepdims=True)
    return hc * lax.rsqrt(var + LN_EPS) * g + b


def _split_bf16(a):
    hi = a.astype(BF16)
    lo = (a - hi.astype(F32)).astype(BF16)
    return hi, lo


def _in_proj_kernel(x_ref, w_ref, scale_ref, bg_ref, qkv_ref, gate_ref):
    xb = x_ref[...].astype(BF16)
    for c in range(D_QKV // D_MODEL):
        cols = slice(c * D_MODEL, (c + 1) * D_MODEL)
        acc = _dot(xb, w_ref[:, cols])
        qkv_ref[:, cols] = (acc * scale_ref[:, cols]).astype(BF16)
    for c in range(2):
        cols = slice(c * D_MODEL, (c + 1) * D_MODEL)
        wcols = slice(D_QKV + c * D_MODEL, D_QKV + (c + 1) * D_MODEL)
        logit = _dot(xb, w_ref[:, wcols]) + bg_ref[:, cols]
        gate_ref[:, cols] = 1.0 / (1.0 + jnp.exp(-logit))


def _in_proj(x2d, w_in_bf16, qscale, b_gate_row, tm=512):
    n = x2d.shape[0]
    return pl.pallas_call(
        _in_proj_kernel,
        grid=(n // tm,),
        in_specs=[
            pl.BlockSpec((tm, D_MODEL), lambda i: (i, 0)),
            pl.BlockSpec((D_MODEL, D_IN), lambda i: (0, 0)),
            pl.BlockSpec((1, D_QKV), lambda i: (0, 0)),
            pl.BlockSpec((1, 2 * D_MODEL), lambda i: (0, 0)),
        ],
        out_specs=[
            pl.BlockSpec((tm, D_QKV), lambda i: (i, 0)),
            pl.BlockSpec((tm, 2 * D_MODEL), lambda i: (i, 0)),
        ],
        out_shape=[
            jax.ShapeDtypeStruct((n, D_QKV), BF16),
            jax.ShapeDtypeStruct((n, 2 * D_MODEL), F32),
        ],
        compiler_params=pltpu.CompilerParams(
            dimension_semantics=("arbitrary",), vmem_limit_bytes=VMEM_LIMIT),
        name="in_proj",
    )(x2d, w_in_bf16, qscale, b_gate_row)


SB_DEAD_BITS = 160.0
SB_STEP_BLOCKS = 4


def _sb_kernel(q_ref, k_ref, v_ref, u_ref, o_ref, z_ref, arg_ref, rs_ref, acc_ref, car_ref, *, tq):
    step = pl.program_id(2)
    lane_q = lax.broadcasted_iota(jnp.int32, (tq, LANES), 1)

    def split_heads(x):
        zero = jnp.zeros_like(x)
        return jnp.where(lane_q < HEAD_DIM, x, zero), jnp.where(lane_q < HEAD_DIM, zero, x)

    blocks = [step * SB_STEP_BLOCKS + j for j in range(SB_STEP_BLOCKS)]
    q_heads = [split_heads(q_ref[pl.ds(j * tq, tq), :]) for j in range(SB_STEP_BLOCKS)]

    def key_rows(i, n):
        return pl.ds(pl.multiple_of(jnp.maximum(i - n, 0) * tq, tq), tq)

    def block_pairs(subs, n, diagonal):
        for j in subs:
            for b in range(2):
                k = k_ref[key_rows(blocks[j], n + b), :]
                for h in range(HEADS_PER_TILE):
                    z = lax.dot_general(q_heads[j][h], k, _NT, preferred_element_type=F32)
                    if diagonal and b == 0:
                        row = lax.broadcasted_iota(jnp.int32, (tq, tq), 0)
                        col = lax.broadcasted_iota(jnp.int32, (tq, tq), 1)
                        z = jnp.where(col < row, z, NEG_INF)
                    z_ref[j, b, h] = z
        for j in subs:
            for b in range(2):
                for h in range(HEADS_PER_TILE):
                    z = z_ref[j, b, h]
                    sp = jnp.maximum(z, 0.0) + jnp.log2(1.0 + jnp.exp2(-jnp.abs(z)))
                    sums = _dot(sp.astype(BF16), u_ref[...])
                    arg_ref[j, b, h] = (z - sp) - sums[:, 0:tq]
                    rs_ref[j, b, h] = sums[:, tq:tq + LANES]
        least = []
        for j in subs:
            carries = [car_ref[j]]
            for b in range(2):
                carries.append(carries[b] + rs_ref[j, b])
            car_ref[j] = carries[2]
            least.append(jnp.min(carries[2]))
            v_parts = []
            for b in range(2):
                v = v_ref[key_rows(blocks[j], n + b), :]
                if b == 1:
                    v = jnp.where(n + b <= blocks[j], v, jnp.zeros_like(v))
                v_parts += split_heads(v)
            ws = []
            for b in range(2):
                for h in range(HEADS_PER_TILE):
                    carry = jnp.concatenate([carries[b][h]] * (tq // LANES), axis=1)
                    ws.append(jnp.exp2(arg_ref[j, b, h] - carry).astype(BF16))
            acc_ref[j] += _dot(jnp.concatenate(ws, axis=1), jnp.concatenate(v_parts, axis=0))
        return least

    acc_ref[...] = jnp.zeros_like(acc_ref)
    car_ref[...] = jnp.zeros_like(car_ref)
    first = block_pairs(range(SB_STEP_BLOCKS), 0, True)

    for j in range(SB_STEP_BLOCKS):
        def more(state, j=j):
            n, least = state
            return (n <= blocks[j]) & (least < SB_DEAD_BITS)

        def body(state, j=j):
            n, _ = state
            return n + 2, block_pairs([j], n, False)[0]

        lax.while_loop(more, body, (jnp.int32(2), first[j]))
        o_ref[pl.ds(j * tq, tq), :] = acc_ref[j].astype(o_ref.dtype)


def _sb_attn(qkv, b, tq=256):
    s = qkv.shape[0] // b
    n_tiles = W_SB // LANES
    step_rows = SB_STEP_BLOCKS * tq
    steps = s // step_rows
    u = jnp.asarray(np.concatenate([np.arange(tq)[:, None] > np.arange(tq)[None, :],
                                    np.ones((tq, LANES), bool)], axis=1), dtype=BF16)
    return pl.pallas_call(
        functools.partial(_sb_kernel, tq=tq),
        grid=(b, n_tiles, steps),
        in_specs=[
            pl.BlockSpec((step_rows, LANES), lambda bi, hp, i: (bi * steps + i, hp)),
            pl.BlockSpec((s, LANES), lambda bi, hp, i: (bi, n_tiles + hp)),
            pl.BlockSpec((s, LANES), lambda bi, hp, i: (bi, 2 * n_tiles + hp)),
            pl.BlockSpec((tq, tq + LANES), lambda bi, hp, i: (0, 0)),
        ],
        out_specs=pl.BlockSpec((step_rows, LANES), lambda bi, hp, i: (bi * steps + i, hp)),
        out_shape=jax.ShapeDtypeStruct((b * s, W_SB), BF16),
        scratch_shapes=[
            pltpu.VMEM((SB_STEP_BLOCKS, 2, HEADS_PER_TILE, tq, tq), F32),
            pltpu.VMEM((SB_STEP_BLOCKS, 2, HEADS_PER_TILE, tq, tq), F32),
            pltpu.VMEM((SB_STEP_BLOCKS, 2, HEADS_PER_TILE, tq, LANES), F32),
            pltpu.VMEM((SB_STEP_BLOCKS, tq, LANES), F32),
            pltpu.VMEM((SB_STEP_BLOCKS, HEADS_PER_TILE, tq, LANES), F32),
        ],
        compiler_params=pltpu.CompilerParams(
            dimension_semantics=("arbitrary", "arbitrary", "arbitrary"), vmem_limit_bytes=VMEM_LIMIT),
        name="sb_attn",
    )(qkv, qkv, qkv, u)


CA_GROUP = 4
CA_STEP_GROUPS = 8
CA_TQ = CA_GROUP * CHUNK
CA_BAND = (CA_GROUP + N_PAST_CHUNKS) * CHUNK
CA_PAD = N_PAST_CHUNKS * CHUNK


def _ca_kernel(q_ref, k_ref, v_ref, line_ref, ones_ref, o_ref, kp_ref, vp_ref, bias_ref):
    c = pl.program_id(2)
    s = k_ref.shape[0]

    @pl.when(c == 0)
    def _():
        kp_ref[0:CA_PAD, :] = jnp.zeros((CA_PAD, LANES), BF16)
        vp_ref[0:CA_PAD, :] = jnp.zeros((CA_PAD, LANES), BF16)
        kp_ref[CA_PAD:CA_PAD + s, :] = k_ref[...]
        vp_ref[CA_PAD:CA_PAD + s, :] = v_ref[...]
        r = lax.broadcasted_iota(jnp.int32, (CA_TQ, CA_BAND), 0)
        p = lax.broadcasted_iota(jnp.int32, (CA_TQ, CA_BAND), 1)
        shift = CHUNK.bit_length() - 1
        qc = lax.shift_right_logical(r, shift)
        kc = lax.shift_right_logical(p, shift)
        in_band = (kc >= qc) & (kc <= qc + N_PAST_CHUNKS)
        for h in range(HEADS_PER_TILE):
            rows = jnp.broadcast_to(line_ref[h], (CA_TQ, CA_LINE))
            skew = pltpu.roll(rows, 1, axis=1, stride=1, stride_axis=0)
            bias_ref[h] = jnp.where(in_band, skew[:, CA_TQ:CA_TQ + CA_BAND], NEG_INF)

    lane_q = lax.broadcasted_iota(jnp.int32, (CA_TQ, LANES), 1)
    lane_v = lax.broadcasted_iota(jnp.int32, (CA_BAND, LANES), 1)

    def attend(g, masked):
        group = c * CA_STEP_GROUPS + g
        start = pl.multiple_of(group * CA_TQ, CA_TQ)
        rows = pl.ds(g * CA_TQ, CA_TQ)
        kb = kp_ref[pl.ds(start, CA_BAND), :]
        vb = vp_ref[pl.ds(start, CA_BAND), :]
        q = q_ref[rows, :]
        zero_q = jnp.zeros_like(q)
        zero_v = jnp.zeros_like(vb)
        q_heads = (jnp.where(lane_q < HEAD_DIM, q, zero_q), jnp.where(lane_q < HEAD_DIM, zero_q, q))
        v_heads = jnp.concatenate(
            [jnp.where(lane_v < HEAD_DIM, vb, zero_v), jnp.where(lane_v < HEAD_DIM, zero_v, vb)], axis=0)
        v_and_ones = jnp.concatenate([v_heads, ones_ref[...]], axis=1)
        es = []
        for h in range(HEADS_PER_TILE):
            sc = lax.dot_general(q_heads[h], kb, _NT, preferred_element_type=F32) + bias_ref[h]
            if masked:
                pos = lax.broadcasted_iota(jnp.int32, (CA_TQ, CA_BAND), 1)
                sc = jnp.where(pos >= CA_PAD - group * CA_TQ, sc, NEG_INF)
            m = jnp.max(sc, axis=1, keepdims=True)
            es.append(jnp.exp2(sc - m).astype(BF16))
        both = _dot(jnp.concatenate(es, axis=1), v_and_ones)
        o_ref[rows, :] = (both[:, 0:LANES] / both[:, LANES:2 * LANES]).astype(o_ref.dtype)

    def step(masked):
        for g in range(CA_STEP_GROUPS):
            attend(g, masked)

    assert CA_PAD // CA_TQ <= CA_STEP_GROUPS
    pl.when(c == 0)(lambda: step(True))
    pl.when(c > 0)(lambda: step(False))


CA_LINE = CA_TQ + CA_BAND


def _ca_bias_line(rel_bias):
    h = rel_bias.shape[0]
    n_far = CA_PAD + CA_TQ - 1 - REL_CLIP
    n_neg = CA_BAND - 1 - CA_PAD - REL_CLIP
    rb = rel_bias.astype(F32) * np.float32(np.log2(np.e))
    line = jnp.concatenate([jnp.broadcast_to(rb[:, 2 * REL_CLIP:], (h, n_far)), rb[:, ::-1],
                            jnp.broadcast_to(rb[:, :1], (h, n_neg + 1))], axis=1)
    assert line.shape[1] == CA_LINE
    return line.reshape(h, 1, CA_LINE)


def _ca_attn(qkv, b, bias_line):
    s = qkv.shape[0] // b
    n_tiles = W_CA // LANES
    base = 3 * W_SB // LANES
    step_rows = CA_STEP_GROUPS * CA_TQ
    steps = s // step_rows
    head_of_row = np.arange(HEADS_PER_TILE * CA_BAND)[:, None] // CA_BAND
    head_of_lane = np.arange(LANES)[None, :] // HEAD_DIM
    ones = jnp.asarray(head_of_row == head_of_lane, dtype=BF16)
    return pl.pallas_call(
        _ca_kernel,
        grid=(b, n_tiles, steps),
        in_specs=[
            pl.BlockSpec((step_rows, LANES), lambda bi, hp, c: (bi * steps + c, base + hp)),
            pl.BlockSpec((s, LANES), lambda bi, hp, c: (bi, base + n_tiles + hp)),
            pl.BlockSpec((s, LANES), lambda bi, hp, c: (bi, base + 2 * n_tiles + hp)),
            pl.BlockSpec((HEADS_PER_TILE, 1, CA_LINE), lambda bi, hp, c: (hp, 0, 0)),
            pl.BlockSpec((HEADS_PER_TILE * CA_BAND, LANES), lambda bi, hp, c: (0, 0)),
        ],
        out_specs=pl.BlockSpec((step_rows, LANES), lambda bi, hp, c: (bi * steps + c, hp)),
        out_shape=jax.ShapeDtypeStruct((b * s, W_CA), BF16),
        scratch_shapes=[
            pltpu.VMEM((CA_PAD + s, LANES), BF16),
            pltpu.VMEM((CA_PAD + s, LANES), BF16),
            pltpu.VMEM((HEADS_PER_TILE, CA_TQ, CA_BAND), F32),
        ],
        compiler_params=pltpu.CompilerParams(
            dimension_semantics=("arbitrary", "arbitrary", "arbitrary"), vmem_limit_bytes=VMEM_LIMIT),
        name="ca_attn",
    )(qkv, qkv, qkv, bias_line, ones)


ROUTE_E1, ROUTE_E2, ROUTE_W1, ROUTE_W2 = 0, 1, 2, 3


def _route(lg):
    lane = lax.broadcasted_iota(jnp.int32, lg.shape, 1)
    big = jnp.int32(ROUTER_LANES)
    is_group = lane < N_GROUPS
    g_max = jnp.max(jnp.where(is_group, lg, -jnp.inf), axis=1, keepdims=True)
    g_idx = jnp.min(jnp.where(is_group & (lg == g_max), lane, big), axis=1, keepdims=True)
    g_den = jnp.sum(jnp.where(is_group, jnp.exp(lg - g_max), 0.0), axis=1, keepdims=True)
    g_val = 1.0 / g_den
    lo = N_GROUPS + EXPERTS_PER_GROUP * g_idx
    in_group = (lane >= lo) & (lane < lo + EXPERTS_PER_GROUP)
    v1 = jnp.max(jnp.where(in_group, lg, -jnp.inf), axis=1, keepdims=True)
    i1 = jnp.min(jnp.where(in_group & (lg == v1), lane, big), axis=1, keepdims=True)
    rest = in_group & (lane != i1)
    v2 = jnp.max(jnp.where(rest, lg, -jnp.inf), axis=1, keepdims=True)
    i2 = jnp.min(jnp.where(rest & (lg == v2), lane, big), axis=1, keepdims=True)
    e2 = jnp.exp(v2 - v1)
    w1 = g_val / (1.0 + e2)
    w2 = g_val * e2 / (1.0 + e2)
    picked = ((lane == i1) | (lane == i2)).astype(BF16)
    info = (jnp.where(lane == ROUTE_E1, (i1 - N_GROUPS).astype(F32), 0.0)
            + jnp.where(lane == ROUTE_E2, (i2 - N_GROUPS).astype(F32), 0.0)
            + jnp.where(lane == ROUTE_W1, w1, 0.0) + jnp.where(lane == ROUTE_W2, w2, 0.0))
    return info, picked


def _post_kernel(ysb_ref, yca_ref, gate_ref, x_ref, wsb_ref, wca_ref, wout_ref, wr_ref, br_ref,
                 g_ref, b_ref, x1_ref, info_ref, pick_ref):
    w_hi, w_lo = _split_bf16(wr_ref[...])
    w_split = jnp.concatenate([w_hi, w_lo], axis=1)
    tm = x_ref.shape[0]
    for rows in [pl.ds(s * POST_SUB, POST_SUB) for s in range(tm // POST_SUB)]:
        a = _dot(ysb_ref[rows, :], wsb_ref[...])
        c = _dot(yca_ref[rows, :], wca_ref[...])
        mix = gate_ref[rows, 0:D_MODEL] * a + gate_ref[rows, D_MODEL:2 * D_MODEL] * c
        mixed = _dot(mix.astype(BF16), wout_ref[...])
        x1 = _layer_norm(ALPHA * x_ref[rows, :] + mixed, g_ref[...], b_ref[...])
        x1_ref[rows, :] = x1
        x_hi, x_lo = _split_bf16(x1)
        parts = _dot(jnp.concatenate([x_hi, x_lo], axis=0), w_split)
        lg = (parts[0:POST_SUB, 0:ROUTER_LANES] + parts[0:POST_SUB, ROUTER_LANES:]
              + parts[POST_SUB:, 0:ROUTER_LANES] + parts[POST_SUB:, ROUTER_LANES:]) + br_ref[...]
        info_ref[rows, :], pick_ref[rows, :] = _route(lg)


POST_SUB = 256


def _post_attn(y_sb, y_ca, gates, x2d, w_br_sb, w_br_ca, w_out, w_router, b_router, ln_g, ln_b, tm=512):
    n = x2d.shape[0]
    row = lambda i: (i, 0)
    fixed = lambda i: (0, 0)
    return pl.pallas_call(
        _post_kernel,
        grid=(n // tm,),
        in_specs=[
            pl.BlockSpec((tm, W_SB), row),
            pl.BlockSpec((tm, W_CA), row),
            pl.BlockSpec((tm, 2 * D_MODEL), row),
            pl.BlockSpec((tm, D_MODEL), row),
            pl.BlockSpec((W_SB, D_MODEL), fixed),
            pl.BlockSpec((W_CA, D_MODEL), fixed),
            pl.BlockSpec((D_MODEL, D_MODEL), fixed),
            pl.BlockSpec((D_MODEL, ROUTER_LANES), fixed),
            pl.BlockSpec((1, ROUTER_LANES), fixed),
            pl.BlockSpec((1, D_MODEL), fixed),
            pl.BlockSpec((1, D_MODEL), fixed),
        ],
        out_specs=[
            pl.BlockSpec((tm, D_MODEL), row),
            pl.BlockSpec((tm, ROUTER_LANES), row),
            pl.BlockSpec((tm, ROUTER_LANES), row),
        ],
        out_shape=[
            jax.ShapeDtypeStruct((n, D_MODEL), F32),
            jax.ShapeDtypeStruct((n, ROUTER_LANES), F32),
            jax.ShapeDtypeStruct((n, ROUTER_LANES), BF16),
        ],
        compiler_params=pltpu.CompilerParams(
            dimension_semantics=("arbitrary",), vmem_limit_bytes=VMEM_LIMIT),
        name="post_attn",
    )(y_sb, y_ca, gates, x2d, w_br_sb, w_br_ca, w_out, w_router, b_router, ln_g, ln_b)


MOE_TILE = 512
DEST_LANES = 8
TOP_K = 2


def _moe_rows(n):
    return n * TOP_K + N_EXPERTS * MOE_TILE


def _rank_kernel(pick_ref, info_ref, tri_ref, dest_ref, cnt_ref, run_ref, off_ref, *, tb):
    p = pl.program_id(0)
    i = pl.program_id(1)
    pick = pick_ref[...]

    @pl.when((p == 0) & (i == 0))
    def _():
        run_ref[...] = jnp.zeros_like(run_ref)

    @pl.when(p == 0)
    def _():
        run_ref[...] += jnp.sum(pick.astype(F32), axis=0, keepdims=True)

    @pl.when((p == 1) & (i == 0))
    def _():
        cnt = run_ref[...]
        cnt_ref[...] = cnt
        padded = jnp.ceil(cnt * (1.0 / MOE_TILE)) * MOE_TILE
        lane = lax.broadcasted_iota(jnp.int32, padded.shape, 1)
        scan = padded
        step = 1
        while step < ROUTER_LANES:
            scan = scan + jnp.where(lane >= step, pltpu.roll(scan, step, axis=1), 0.0)
            step *= 2
        off_ref[...] = scan - padded
        run_ref[...] = jnp.zeros_like(run_ref)

    @pl.when(p == 1)
    def _():
        seen = run_ref[0:1, :]
        earlier = _dot(tri_ref[...], pick)
        row_of = earlier + seen + off_ref[0:1, :]
        info = info_ref[...]
        lane = lax.broadcasted_iota(jnp.int32, info.shape, 1)
        lane_f = lane.astype(F32)
        e1 = jnp.sum(jnp.where(lane == ROUTE_E1, info, 0.0), axis=1, keepdims=True)
        e2 = jnp.sum(jnp.where(lane == ROUTE_E2, info, 0.0), axis=1, keepdims=True)
        d1 = jnp.sum(jnp.where(lane_f == e1 + N_GROUPS, row_of, 0.0), axis=1, keepdims=True)
        d2 = jnp.sum(jnp.where(lane_f == e2 + N_GROUPS, row_of, 0.0), axis=1, keepdims=True)
        dest = (jnp.where(lane == 0, d1, 0.0) + jnp.where(lane == 1, d2, 0.0)).astype(jnp.int32)
        dest_ref[...] = dest[:, 0:DEST_LANES]
        run_ref[...] += jnp.sum(pick.astype(F32), axis=0, keepdims=True)


def _moe_rank(pick, info, tb=1024):
    n = pick.shape[0]
    tri = jnp.asarray(np.arange(tb)[None, :] < np.arange(tb)[:, None], dtype=BF16)
    return pl.pallas_call(
        functools.partial(_rank_kernel, tb=tb),
        grid=(2, n // tb),
        in_specs=[
            pl.BlockSpec((tb, ROUTER_LANES), lambda p, i: (i, 0)),
            pl.BlockSpec((tb, ROUTER_LANES), lambda p, i: (i, 0)),
            pl.BlockSpec((tb, tb), lambda p, i: (0, 0)),
        ],
        out_specs=[
            pl.BlockSpec((tb, DEST_LANES), lambda p, i: (i * p, 0)),
            pl.BlockSpec((8, ROUTER_LANES), lambda p, i: (0, 0)),
        ],
        out_shape=[
            jax.ShapeDtypeStruct((n, DEST_LANES), jnp.int32),
            jax.ShapeDtypeStruct((8, ROUTER_LANES), F32),
        ],
        scratch_shapes=[
            pltpu.VMEM((8, ROUTER_LANES), F32),
            pltpu.VMEM((8, ROUTER_LANES), F32),
        ],
        compiler_params=pltpu.CompilerParams(
            dimension_semantics=("arbitrary", "arbitrary"), vmem_limit_bytes=VMEM_LIMIT),
        name="moe_rank",
    )(pick, info, tri)


def _row_copy(src_ref, src_row, dst_ref, dst_row, sem):
    return pltpu.make_async_copy(src_ref.at[pl.ds(src_row, 1), :], dst_ref.at[pl.ds(dst_row, 1), :], sem)


def _dispatch_kernel(d1_ref, d2_ref, last_ref, x_ref, xs_hbm, zero_ref, sem, zsem, *, tb):
    base = pl.program_id(0) * tb

    @pl.when(pl.program_id(0) == 0)
    def _():
        zero_ref[...] = jnp.zeros_like(zero_ref)

        def last_tile_copy(e):
            first_row = pl.multiple_of((last_ref[e] - 1) * MOE_TILE, MOE_TILE)
            return pltpu.make_async_copy(zero_ref, xs_hbm.at[pl.ds(first_row, MOE_TILE), :], zsem)

        def unused_tile_copy(t):
            return pltpu.make_async_copy(zero_ref, xs_hbm.at[pl.ds(t * MOE_TILE, MOE_TILE), :], zsem)

        n_tiles = xs_hbm.shape[0] // MOE_TILE
        min_used = n_tiles - N_EXPERTS
        for wait in (False, True):
            for e in range(N_EXPERTS):
                owns_tiles = last_ref[e] > (last_ref[e - 1] if e else 0)
                pl.when(owns_tiles)(
                    lambda e=e, wait=wait: last_tile_copy(e).wait() if wait else last_tile_copy(e).start())
            for t in range(min_used, n_tiles):
                pl.when(t >= last_ref[N_EXPERTS - 1])(
                    lambda t=t, wait=wait: unused_tile_copy(t).wait() if wait else unused_tile_copy(t).start())

    def issue(r, c):
        _row_copy(x_ref, r, xs_hbm, d1_ref[base + r], sem).start(priority=0)
        _row_copy(x_ref, r, xs_hbm, d2_ref[base + r], sem).start(priority=1)
        return c

    lax.fori_loop(0, tb, issue, 0, unroll=8)
    for _ in range(TOP_K):
        pltpu.make_async_copy(x_ref, xs_hbm.at[pl.ds(0, tb), :], sem).wait()


def _moe_dispatch(x1, dest1, dest2, last_tile, tb=512):
    n = x1.shape[0]
    rows = _moe_rows(n)
    grid_spec = pltpu.PrefetchScalarGridSpec(
        num_scalar_prefetch=3,
        grid=(n // tb,),
        in_specs=[pl.BlockSpec((tb, D_MODEL), lambda i, d1, d2, last: (i, 0))],
        out_specs=pl.BlockSpec(memory_space=pl.ANY),
        scratch_shapes=[
            pltpu.VMEM((MOE_TILE, D_MODEL), F32),
            pltpu.SemaphoreType.DMA(()),
            pltpu.SemaphoreType.DMA(()),
        ],
    )
    return pl.pallas_call(
        functools.partial(_dispatch_kernel, tb=tb),
        grid_spec=grid_spec,
        out_shape=jax.ShapeDtypeStruct((rows, D_MODEL), F32),
        compiler_params=pltpu.CompilerParams(
            dimension_semantics=("arbitrary",), vmem_limit_bytes=VMEM_LIMIT),
        name="moe_dispatch",
    )(dest1, dest2, last_tile, x1)


def _experts_kernel(te_ref, nt_ref, xs_ref, wg_ref, wu_ref, wd_ref, ys_ref, wgu_s, wd_s):
    t = pl.program_id(0)
    changed = (t == 0) | (te_ref[t] != te_ref[jnp.maximum(t - 1, 0)])

    @pl.when(changed)
    def _():
        wgu_s[:, 0:D_EXPERT] = wg_ref[...].astype(BF16)
        wgu_s[:, D_EXPERT:2 * D_EXPERT] = wu_ref[...].astype(BF16)
        wd_s[...] = wd_ref[...].astype(BF16)

    @pl.when(t < nt_ref[0])
    def _():
        gu = _dot(xs_ref[...].astype(BF16), wgu_s[...])
        gate = gu[:, 0:D_EXPERT]
        up = gu[:, D_EXPERT:2 * D_EXPERT]
        hid = (gate * (1.0 / (1.0 + jnp.exp(-gate)))) * up
        ys_ref[...] = _dot(hid.astype(BF16), wd_s[...])

    @pl.when(t >= nt_ref[0])
    def _():
        ys_ref[...] = jnp.zeros_like(ys_ref)


def _moe_experts(xs, tile_expert, n_tiles_used, w_gate, w_up, w_down, layer):
    rows = xs.shape[0]
    first = layer * N_EXPERTS
    grid_spec = pltpu.PrefetchScalarGridSpec(
        num_scalar_prefetch=2,
        grid=(rows // MOE_TILE,),
        in_specs=[
            pl.BlockSpec((MOE_TILE, D_MODEL), lambda t, te, nt: (jnp.minimum(t, nt[0] - 1), 0)),
            pl.BlockSpec((None, D_MODEL, D_EXPERT), lambda t, te, nt: (first + te[t], 0, 0)),
            pl.BlockSpec((None, D_MODEL, D_EXPERT), lambda t, te, nt: (first + te[t], 0, 0)),
            pl.BlockSpec((None, D_EXPERT, D_MODEL), lambda t, te, nt: (first + te[t], 0, 0)),
        ],
        out_specs=pl.BlockSpec((MOE_TILE, D_MODEL), lambda t, te, nt: (t, 0)),
        scratch_shapes=[
            pltpu.VMEM((D_MODEL, 2 * D_EXPERT), BF16),
            pltpu.VMEM((D_EXPERT, D_MODEL), BF16),
        ],
    )
    return pl.pallas_call(
        _experts_kernel,
        grid_spec=grid_spec,
        out_shape=jax.ShapeDtypeStruct((rows, D_MODEL), F32),
        compiler_params=pltpu.CompilerParams(
            dimension_semantics=("arbitrary",), vmem_limit_bytes=VMEM_LIMIT),
        name="moe_experts",
    )(tile_expert, n_tiles_used, xs, w_gate, w_up, w_down)


def _combine_kernel(d1_ref, d2_ref, x1_ref, info_ref, ys_hbm, g_ref, b_ref, out_ref, y1_ref, y2_ref, sem, *, tb):
    base = pl.program_id(0) * tb

    def issue(r, c):
        _row_copy(ys_hbm, d1_ref[base + r], y1_ref, r, sem).start(priority=0)
        _row_copy(ys_hbm, d2_ref[base + r], y2_ref, r, sem).start(priority=1)
        return c

    lax.fori_loop(0, tb, issue, 0, unroll=8)
    for y_ref in (y1_ref, y2_ref):
        pltpu.make_async_copy(ys_hbm.at[pl.ds(0, tb), :], y_ref, sem).wait()
    info = info_ref[...]
    lane = lax.broadcasted_iota(jnp.int32, info.shape, 1)
    w1 = jnp.sum(jnp.where(lane == ROUTE_W1, info, 0.0), axis=1, keepdims=True)
    w2 = jnp.sum(jnp.where(lane == ROUTE_W2, info, 0.0), axis=1, keepdims=True)
    ffn = w1 * y1_ref[...] + w2 * y2_ref[...]
    out_ref[...] = _layer_norm(ALPHA * x1_ref[...] + ffn, g_ref[...], b_ref[...])


def _moe_combine(x1, info, ys, dest1, dest2, ln_g, ln_b, tb=512):
    n = x1.shape[0]
    grid_spec = pltpu.PrefetchScalarGridSpec(
        num_scalar_prefetch=2,
        grid=(n // tb,),
        in_specs=[
            pl.BlockSpec((tb, D_MODEL), lambda i, d1, d2: (i, 0)),
            pl.BlockSpec((tb, ROUTER_LANES), lambda i, d1, d2: (i, 0)),
            pl.BlockSpec(memory_space=pl.ANY),
            pl.BlockSpec((1, D_MODEL), lambda i, d1, d2: (0, 0)),
            pl.BlockSpec((1, D_MODEL), lambda i, d1, d2: (0, 0)),
        ],
        out_specs=pl.BlockSpec((tb, D_MODEL), lambda i, d1, d2: (i, 0)),
        scratch_shapes=[
            pltpu.VMEM((tb, D_MODEL), F32),
            pltpu.VMEM((tb, D_MODEL), F32),
            pltpu.SemaphoreType.DMA(()),
        ],
    )
    return pl.pallas_call(
        functools.partial(_combine_kernel, tb=tb),
        grid_spec=grid_spec,
        out_shape=jax.ShapeDtypeStruct((n, D_MODEL), F32),
        compiler_params=pltpu.CompilerParams(
            dimension_semantics=("arbitrary",), vmem_limit_bytes=VMEM_LIMIT),
        name="moe_combine",
    )(dest1, dest2, x1, info, ys, ln_g, ln_b)


def _moe(x1, info, pick, w_gate, w_up, w_down, layer, ln_g, ln_b):
    n = x1.shape[0]
    dest, counts = _moe_rank(pick, info)
    dest1 = dest[:, 0]
    dest2 = dest[:, 1]
    tiles = jnp.ceil(counts[0, N_GROUPS:N_GROUPS + N_EXPERTS] * (1.0 / MOE_TILE)).astype(jnp.int32)
    upto = np.arange(N_EXPERTS)[None, :] <= np.arange(N_EXPERTS)[:, None]
    last_tile = jnp.sum(jnp.where(upto, tiles[None, :], 0), axis=1)
    tile_ids = jnp.arange(_moe_rows(n) // MOE_TILE, dtype=jnp.int32)
    tile_expert = jnp.minimum(jnp.sum(tile_ids[:, None] >= last_tile[None, :], axis=1), N_EXPERTS - 1)
    xs = _moe_dispatch(x1, dest1, dest2, last_tile)
    ys = _moe_experts(xs, tile_expert.astype(jnp.int32), last_tile[N_EXPERTS - 1:], w_gate, w_up, w_down, layer)
    return _moe_combine(x1, info, ys, dest1, dest2, ln_g, ln_b)


def _layer(x2d, batch, w_in, b_gate, rel_bias, w_br_sb, w_br_ca, w_out, ln1_g, ln1_b,
           w_group, b_group, w_erouter, b_erouter, w_gate, w_up, w_down, layer, ln2_g, ln2_b):
    n = x2d.shape[0]
    seq = n // batch
    scale = HEAD_DIM ** -0.5
    qscale = np.ones((1, D_QKV), np.float32)
    qscale[:, 0:W_SB] = scale * np.log2(np.e)
    qscale[:, 3 * W_SB:3 * W_SB + W_CA] = scale * np.log2(np.e)
    qkv, gates = _in_proj(x2d, w_in.astype(BF16), jnp.asarray(qscale), b_gate.reshape(1, 2 * D_MODEL))
    y_sb = _sb_attn(qkv, batch)
    y_ca = _ca_attn(qkv, batch, _ca_bias_line(rel_bias))

    w_router = jnp.concatenate(
        [w_group, w_erouter.transpose(1, 0, 2).reshape(D_MODEL, N_EXPERTS)], axis=1)
    w_router = jnp.pad(w_router, ((0, 0), (0, ROUTER_LANES - N_GROUPS - N_EXPERTS)))
    b_router = jnp.pad(jnp.concatenate([b_group, b_erouter.reshape(N_EXPERTS)]),
                       (0, ROUTER_LANES - N_GROUPS - N_EXPERTS)).reshape(1, ROUTER_LANES)
    x1, info, pick = _post_attn(y_sb, y_ca, gates, x2d, w_br_sb.astype(BF16), w_br_ca.astype(BF16),
                        w_out.astype(BF16), w_router, b_router,
                        ln1_g.reshape(1, D_MODEL), ln1_b.reshape(1, D_MODEL))

    return _moe(x1, info, pick, w_gate, w_up, w_down, layer,
                ln2_g.reshape(1, D_MODEL), ln2_b.reshape(1, D_MODEL))


def kernel(x, w_in, b_gate, rel_bias, w_br_sb, w_br_ca, w_out, ln1_g, ln1_b, w_group, b_group,
           w_erouter, b_erouter, w_gate, w_up, w_down, ln2_g, ln2_b):
    batch, seq, d = x.shape
    h = x.reshape(batch * seq, d)
    w_gate = w_gate.reshape(DEPTH * N_EXPERTS, D_MODEL, D_EXPERT)
    w_up = w_up.reshape(DEPTH * N_EXPERTS, D_MODEL, D_EXPERT)
    w_down = w_down.reshape(DEPTH * N_EXPERTS, D_EXPERT, D_MODEL)
    for l in range(DEPTH):
        h = _layer(h, batch, w_in[l], b_gate[l], rel_bias[l], w_br_sb[l], w_br_ca[l], w_out[l],
                   ln1_g[l], ln1_b[l], w_group[l], b_group[l], w_erouter[l], b_erouter[l],
                   w_gate, w_up, w_down, l, ln2_g[l], ln2_b[l])
    return h.reshape(batch, seq, d)
```

```python
import functools

import jax
import jax.numpy as jnp
import numpy as np
from jax import lax
from jax.experimental import pallas as pl
from jax.experimental.pallas import tpu as pltpu

D_MODEL = 1024
DEPTH = 2
CHUNK = 64
HEAD_DIM = 64
H_SB = 8
H_CA = 8
W_SB = H_SB * HEAD_DIM
W_CA = H_CA * HEAD_DIM
N_PAST_CHUNKS = 8
REL_CLIP = 128
N_GROUPS = 4
EXPERTS_PER_GROUP = 8
N_EXPERTS = N_GROUPS * EXPERTS_PER_GROUP
D_EXPERT = 256
ALPHA = (2.0 * DEPTH) ** 0.25
LN_EPS = 1e-5
D_QKV = 3 * W_SB + 3 * W_CA
D_IN = D_QKV + 2 * D_MODEL
NEG_INF = -1e30

LANES = 128
HEADS_PER_TILE = LANES // HEAD_DIM
ROUTER_LANES = LANES
VMEM_LIMIT = 56 * 1024 * 1024

BF16 = jnp.bfloat16
F32 = jnp.float32

_NT = (((1,), (1,)), ((), ()))


def _dot(a, b):
    return jnp.dot(a, b, preferred_element_type=F32)


def _layer_norm(h, g, b):
    mu = jnp.mean(h, axis=-1, keepdims=True)
    hc = h - mu
    var = jnp.mean(hc * hc, axis=-1, keepdims=True)
    return hc * lax.rsqrt(var + LN_EPS) * g + b


def _split_bf16(a):
    hi = a.astype(BF16)
    lo = (a - hi.astype(F32)).astype(BF16)
    return hi, lo


def _in_proj_kernel(x_ref, w_ref, scale_ref, bg_ref, qkv_ref, gate_ref):
    xb = x_ref[...].astype(BF16)
    for c in range(D_QKV // D_MODEL):
        cols = slice(c * D_MODEL, (c + 1) * D_MODEL)
        acc = _dot(xb, w_ref[:, cols])
        qkv_ref[:, cols] = (acc * scale_ref[:, cols]).astype(BF16)
    for c in range(2):
        cols = slice(c * D_MODEL, (c + 1) * D_MODEL)
        wcols = slice(D_QKV + c * D_MODEL, D_QKV + (c + 1) * D_MODEL)
        logit = _dot(xb, w_ref[:, wcols]) + bg_ref[:, cols]
        gate_ref[:, cols] = 1.0 / (1.0 + jnp.exp(-logit))


def _in_proj(x2d, w_in_bf16, qscale, b_gate_row, tm=512):
    n = x2d.shape[0]
    return pl.pallas_call(
        _in_proj_kernel,
        grid=(n // tm,),
        in_specs=[
            pl.BlockSpec((tm, D_MODEL), lambda i: (i, 0)),
            pl.BlockSpec((D_MODEL, D_IN), lambda i: (0, 0)),
            pl.BlockSpec((1, D_QKV), lambda i: (0, 0)),
            pl.BlockSpec((1, 2 * D_MODEL), lambda i: (0, 0)),
        ],
        out_specs=[
            pl.BlockSpec((tm, D_QKV), lambda i: (i, 0)),
            pl.BlockSpec((tm, 2 * D_MODEL), lambda i: (i, 0)),
        ],
        out_shape=[
            jax.ShapeDtypeStruct((n, D_QKV), BF16),
            jax.ShapeDtypeStruct((n, 2 * D_MODEL), F32),
        ],
        compiler_params=pltpu.CompilerParams(
            dimension_semantics=("arbitrary",), vmem_limit_bytes=VMEM_LIMIT),
        name="in_proj",
    )(x2d, w_in_bf16, qscale, b_gate_row)


SB_DEAD_BITS = 160.0
SB_STEP_BLOCKS = 4


def _sb_kernel(q_ref, k_ref, v_ref, u_ref, o_ref, z_ref, arg_ref, rs_ref, acc_ref, car_ref, *, tq):
    step = pl.program_id(2)
    lane_q = lax.broadcasted_iota(jnp.int32, (tq, LANES), 1)

    def split_heads(x):
        zero = jnp.zeros_like(x)
        return jnp.where(lane_q < HEAD_DIM, x, zero), jnp.where(lane_q < HEAD_DIM, zero, x)

    blocks = [step * SB_STEP_BLOCKS + j for j in range(SB_STEP_BLOCKS)]
    q_heads = [split_heads(q_ref[pl.ds(j * tq, tq), :]) for j in range(SB_STEP_BLOCKS)]

    def key_rows(i, n):
        return pl.ds(pl.multiple_of(jnp.maximum(i - n, 0) * tq, tq), tq)

    def block_pairs(subs, n, diagonal):
        for j in subs:
            for b in range(2):
                k = k_ref[key_rows(blocks[j], n + b), :]
                for h in range(HEADS_PER_TILE):
                    z = lax.dot_general(q_heads[j][h], k, _NT, preferred_element_type=F32)
                    if diagonal and b == 0:
                        row = lax.broadcasted_iota(jnp.int32, (tq, tq), 0)
                        col = lax.broadcasted_iota(jnp.int32, (tq, tq), 1)
                        z = jnp.where(col < row, z, NEG_INF)
                    z_ref[j, b, h] = z
        def terms(j):
            for b in range(2):
                for h in range(HEADS_PER_TILE):
                    z = z_ref[j, b, h]
                    sp = jnp.maximum(z, 0.0) + jnp.log2(1.0 + jnp.exp2(-jnp.abs(z)))
                    sums = _dot(sp.astype(BF16), u_ref[...])
                    arg_ref[j, b, h] = (z - sp) - sums[:, 0:tq]
                    rs_ref[j, b, h] = sums[:, tq:tq + LANES]

        def apply(j):
            carries = [car_ref[j]]
            for b in range(2):
                carries.append(carries[b] + rs_ref[j, b])
            car_ref[j] = carries[2]
            v_parts = []
            for b in range(2):
                v = v_ref[key_rows(blocks[j], n + b), :]
                if b == 1:
                    v = jnp.where(n + b <= blocks[j], v, jnp.zeros_like(v))
                v_parts += split_heads(v)
            ws = []
            for b in range(2):
                for h in range(HEADS_PER_TILE):
                    carry = jnp.concatenate([carries[b][h]] * (tq // LANES), axis=1)
                    ws.append(jnp.exp2(arg_ref[j, b, h] - carry).astype(BF16))
            acc_ref[j] += _dot(jnp.concatenate(ws, axis=1), jnp.concatenate(v_parts, axis=0))
            return jnp.min(carries[2])

        subs = list(subs)
        least = []
        terms(subs[0])
        for prev, j in zip(subs, subs[1:]):
            terms(j)
            least.append(apply(prev))
        least.append(apply(subs[-1]))
        return least

    acc_ref[...] = jnp.zeros_like(acc_ref)
    car_ref[...] = jnp.zeros_like(car_ref)
    first = block_pairs(range(SB_STEP_BLOCKS), 0, True)

    for j in range(SB_STEP_BLOCKS):
        def more(state, j=j):
            n, least = state
            return (n <= blocks[j]) & (least < SB_DEAD_BITS)

        def body(state, j=j):
            n, _ = state
            return n + 2, block_pairs([j], n, False)[0]

        lax.while_loop(more, body, (jnp.int32(2), first[j]))
        o_ref[pl.ds(j * tq, tq), :] = acc_ref[j].astype(o_ref.dtype)


def _sb_attn(qkv, b, tq=256):
    s = qkv.shape[0] // b
    n_tiles = W_SB // LANES
    step_rows = SB_STEP_BLOCKS * tq
    steps = s // step_rows
    u = jnp.asarray(np.concatenate([np.arange(tq)[:, None] > np.arange(tq)[None, :],
                                    np.ones((tq, LANES), bool)], axis=1), dtype=BF16)
    return pl.pallas_call(
        functools.partial(_sb_kernel, tq=tq),
        grid=(b, n_tiles, steps),
        in_specs=[
            pl.BlockSpec((step_rows, LANES), lambda bi, hp, i: (bi * steps + i, hp)),
            pl.BlockSpec((s, LANES), lambda bi, hp, i: (bi, n_tiles + hp)),
            pl.BlockSpec((s, LANES), lambda bi, hp, i: (bi, 2 * n_tiles + hp)),
            pl.BlockSpec((tq, tq + LANES), lambda bi, hp, i: (0, 0)),
        ],
        out_specs=pl.BlockSpec((step_rows, LANES), lambda bi, hp, i: (bi * steps + i, hp)),
        out_shape=jax.ShapeDtypeStruct((b * s, W_SB), BF16),
        scratch_shapes=[
            pltpu.VMEM((SB_STEP_BLOCKS, 2, HEADS_PER_TILE, tq, tq), F32),
            pltpu.VMEM((SB_STEP_BLOCKS, 2, HEADS_PER_TILE, tq, tq), F32),
            pltpu.VMEM((SB_STEP_BLOCKS, 2, HEADS_PER_TILE, tq, LANES), F32),
            pltpu.VMEM((SB_STEP_BLOCKS, tq, LANES), F32),
            pltpu.VMEM((SB_STEP_BLOCKS, HEADS_PER_TILE, tq, LANES), F32),
        ],
        compiler_params=pltpu.CompilerParams(
            dimension_semantics=("arbitrary", "arbitrary", "arbitrary"), vmem_limit_bytes=VMEM_LIMIT),
        name="sb_attn",
    )(qkv, qkv, qkv, u)


CA_GROUP = 4
CA_STEP_GROUPS = 8
CA_TQ = CA_GROUP * CHUNK
CA_BAND = (CA_GROUP + N_PAST_CHUNKS) * CHUNK
CA_PAD = N_PAST_CHUNKS * CHUNK


def _ca_kernel(q_ref, k_ref, v_ref, line_ref, ones_ref, o_ref, kp_ref, vp_ref, bias_ref):
    c = pl.program_id(2)
    s = k_ref.shape[0]

    @pl.when(c == 0)
    def _():
        kp_ref[0:CA_PAD, :] = jnp.zeros((CA_PAD, LANES), BF16)
        vp_ref[0:CA_PAD, :] = jnp.zeros((CA_PAD, LANES), BF16)
        kp_ref[CA_PAD:CA_PAD + s, :] = k_ref[...]
        vp_ref[CA_PAD:CA_PAD + s, :] = v_ref[...]
        r = lax.broadcasted_iota(jnp.int32, (CA_TQ, CA_BAND), 0)
        p = lax.broadcasted_iota(jnp.int32, (CA_TQ, CA_BAND), 1)
        shift = CHUNK.bit_length() - 1
        qc = lax.shift_right_logical(r, shift)
        kc = lax.shift_right_logical(p, shift)
        in_band = (kc >= qc) & (kc <= qc + N_PAST_CHUNKS)
        for h in range(HEADS_PER_TILE):
            rows = jnp.broadcast_to(line_ref[h], (CA_TQ, CA_LINE))
            skew = pltpu.roll(rows, 1, axis=1, stride=1, stride_axis=0)
            bias_ref[h] = jnp.where(in_band, skew[:, CA_TQ:CA_TQ + CA_BAND], NEG_INF)

    lane_q = lax.broadcasted_iota(jnp.int32, (CA_TQ, LANES), 1)
    lane_v = lax.broadcasted_iota(jnp.int32, (CA_BAND, LANES), 1)

    def band_start(g):
        return pl.multiple_of((c * CA_STEP_GROUPS + g) * CA_TQ, CA_TQ)

    def scores(g):
        kb = kp_ref[pl.ds(band_start(g), CA_BAND), :]
        q = q_ref[pl.ds(g * CA_TQ, CA_TQ), :]
        zero_q = jnp.zeros_like(q)
        q_heads = (jnp.where(lane_q < HEAD_DIM, q, zero_q), jnp.where(lane_q < HEAD_DIM, zero_q, q))
        return [lax.dot_general(q_heads[h], kb, _NT, preferred_element_type=F32) for h in range(HEADS_PER_TILE)]

    def attend(g, qk, masked):
        group = c * CA_STEP_GROUPS + g
        vb = vp_ref[pl.ds(band_start(g), CA_BAND), :]
        zero_v = jnp.zeros_like(vb)
        v_heads = jnp.concatenate(
            [jnp.where(lane_v < HEAD_DIM, vb, zero_v), jnp.where(lane_v < HEAD_DIM, zero_v, vb)], axis=0)
        v_and_ones = jnp.concatenate([v_heads, ones_ref[...]], axis=1)
        es = []
        for h in range(HEADS_PER_TILE):
            sc = qk[h] + bias_ref[h]
            if masked:
                pos = lax.broadcasted_iota(jnp.int32, (CA_TQ, CA_BAND), 1)
                sc = jnp.where(pos >= CA_PAD - group * CA_TQ, sc, NEG_INF)
            m = jnp.max(sc, axis=1, keepdims=True)
            es.append(jnp.exp2(sc - m).astype(BF16))
        both = _dot(jnp.concatenate(es, axis=1), v_and_ones)
        o_ref[pl.ds(g * CA_TQ, CA_TQ), :] = (both[:, 0:LANES] / both[:, LANES:2 * LANES]).astype(o_ref.dtype)

    def step(masked):
        qk = scores(0)
        for g in range(CA_STEP_GROUPS):
            qk_next = scores(g + 1) if g + 1 < CA_STEP_GROUPS else None
            attend(g, qk, masked)
            qk = qk_next

    assert CA_PAD // CA_TQ <= CA_STEP_GROUPS
    pl.when(c == 0)(lambda: step(True))
    pl.when(c > 0)(lambda: step(False))


CA_LINE = CA_TQ + CA_BAND


def _ca_bias_line(rel_bias):
    h = rel_bias.shape[0]
    n_far = CA_PAD + CA_TQ - 1 - REL_CLIP
    n_neg = CA_BAND - 1 - CA_PAD - REL_CLIP
    rb = rel_bias.astype(F32) * np.float32(np.log2(np.e))
    line = jnp.concatenate([jnp.broadcast_to(rb[:, 2 * REL_CLIP:], (h, n_far)), rb[:, ::-1],
                            jnp.broadcast_to(rb[:, :1], (h, n_neg + 1))], axis=1)
    assert line.shape[1] == CA_LINE
    return line.reshape(h, 1, CA_LINE)


def _ca_attn(qkv, b, bias_line):
    s = qkv.shape[0] // b
    n_tiles = W_CA // LANES
    base = 3 * W_SB // LANES
    step_rows = CA_STEP_GROUPS * CA_TQ
    steps = s // step_rows
    head_of_row = np.arange(HEADS_PER_TILE * CA_BAND)[:, None] // CA_BAND
    head_of_lane = np.arange(LANES)[None, :] // HEAD_DIM
    ones = jnp.asarray(head_of_row == head_of_lane, dtype=BF16)
    return pl.pallas_call(
        _ca_kernel,
        grid=(b, n_tiles, steps),
        in_specs=[
            pl.BlockSpec((step_rows, LANES), lambda bi, hp, c: (bi * steps + c, base + hp)),
            pl.BlockSpec((s, LANES), lambda bi, hp, c: (bi, base + n_tiles + hp)),
            pl.BlockSpec((s, LANES), lambda bi, hp, c: (bi, base + 2 * n_tiles + hp)),
            pl.BlockSpec((HEADS_PER_TILE, 1, CA_LINE), lambda bi, hp, c: (hp, 0, 0)),
            pl.BlockSpec((HEADS_PER_TILE * CA_BAND, LANES), lambda bi, hp, c: (0, 0)),
        ],
        out_specs=pl.BlockSpec((step_rows, LANES), lambda bi, hp, c: (bi * steps + c, hp)),
        out_shape=jax.ShapeDtypeStruct((b * s, W_CA), BF16),
        scratch_shapes=[
            pltpu.VMEM((CA_PAD + s, LANES), BF16),
            pltpu.VMEM((CA_PAD + s, LANES), BF16),
            pltpu.VMEM((HEADS_PER_TILE, CA_TQ, CA_BAND), F32),
        ],
        compiler_params=pltpu.CompilerParams(
            dimension_semantics=("arbitrary", "arbitrary", "arbitrary"), vmem_limit_bytes=VMEM_LIMIT),
        name="ca_attn",
    )(qkv, qkv, qkv, bias_line, ones)


ROUTE_E1, ROUTE_E2, ROUTE_W1, ROUTE_W2 = 0, 1, 2, 3


def _route(lg):
    lane = lax.broadcasted_iota(jnp.int32, lg.shape, 1)
    big = jnp.int32(ROUTER_LANES)
    is_group = lane < N_GROUPS
    g_max = jnp.max(jnp.where(is_group, lg, -jnp.inf), axis=1, keepdims=True)
    g_idx = jnp.min(jnp.where(is_group & (lg == g_max), lane, big), axis=1, keepdims=True)
    g_den = jnp.sum(jnp.where(is_group, jnp.exp(lg - g_max), 0.0), axis=1, keepdims=True)
    g_val = 1.0 / g_den
    lo = N_GROUPS + EXPERTS_PER_GROUP * g_idx
    in_group = (lane >= lo) & (lane < lo + EXPERTS_PER_GROUP)
    v1 = jnp.max(jnp.where(in_group, lg, -jnp.inf), axis=1, keepdims=True)
    i1 = jnp.min(jnp.where(in_group & (lg == v1), lane, big), axis=1, keepdims=True)
    rest = in_group & (lane != i1)
    v2 = jnp.max(jnp.where(rest, lg, -jnp.inf), axis=1, keepdims=True)
    i2 = jnp.min(jnp.where(rest & (lg == v2), lane, big), axis=1, keepdims=True)
    e2 = jnp.exp(v2 - v1)
    w1 = g_val / (1.0 + e2)
    w2 = g_val * e2 / (1.0 + e2)
    picked = ((lane == i1) | (lane == i2)).astype(BF16)
    info = (jnp.where(lane == ROUTE_E1, (i1 - N_GROUPS).astype(F32), 0.0)
            + jnp.where(lane == ROUTE_E2, (i2 - N_GROUPS).astype(F32), 0.0)
            + jnp.where(lane == ROUTE_W1, w1, 0.0) + jnp.where(lane == ROUTE_W2, w2, 0.0))
    return info, picked


def _post_kernel(ysb_ref, yca_ref, gate_ref, x_ref, wsb_ref, wca_ref, wout_ref, wr_ref, br_ref,
                 g_ref, b_ref, x1_ref, info_ref, pick_ref):
    w_hi, w_lo = _split_bf16(wr_ref[...])
    w_split = jnp.concatenate([w_hi, w_lo], axis=1)
    tm = x_ref.shape[0]
    subs = [pl.ds(s * POST_SUB, POST_SUB) for s in range(tm // POST_SUB)]
    for rows in subs:
        a = _dot(ysb_ref[rows, :], wsb_ref[...])
        c = _dot(yca_ref[rows, :], wca_ref[...])
        mix = gate_ref[rows, 0:D_MODEL] * a + gate_ref[rows, D_MODEL:2 * D_MODEL] * c
        x1_ref[rows, :] = ALPHA * x_ref[rows, :] + _dot(mix.astype(BF16), wout_ref[...])
    for rows in subs:
        x1 = _layer_norm(x1_ref[rows, :], g_ref[...], b_ref[...])
        x1_ref[rows, :] = x1
        x_hi, x_lo = _split_bf16(x1)
        parts = _dot(jnp.concatenate([x_hi, x_lo], axis=0), w_split)
        lg = (parts[0:POST_SUB, 0:ROUTER_LANES] + parts[0:POST_SUB, ROUTER_LANES:]
              + parts[POST_SUB:, 0:ROUTER_LANES] + parts[POST_SUB:, ROUTER_LANES:]) + br_ref[...]
        info_ref[rows, :], pick_ref[rows, :] = _route(lg)


POST_SUB = 256


def _post_attn(y_sb, y_ca, gates, x2d, w_br_sb, w_br_ca, w_out, w_router, b_router, ln_g, ln_b, tm=1024):
    n = x2d.shape[0]
    row = lambda i: (i, 0)
    fixed = lambda i: (0, 0)
    return pl.pallas_call(
        _post_kernel,
        grid=(n // tm,),
        in_specs=[
            pl.BlockSpec((tm, W_SB), row),
            pl.BlockSpec((tm, W_CA), row),
            pl.BlockSpec((tm, 2 * D_MODEL), row),
            pl.BlockSpec((tm, D_MODEL), row),
            pl.BlockSpec((W_SB, D_MODEL), fixed),
            pl.BlockSpec((W_CA, D_MODEL), fixed),
            pl.BlockSpec((D_MODEL, D_MODEL), fixed),
            pl.BlockSpec((D_MODEL, ROUTER_LANES), fixed),
            pl.BlockSpec((1, ROUTER_LANES), fixed),
            pl.BlockSpec((1, D_MODEL), fixed),
            pl.BlockSpec((1, D_MODEL), fixed),
        ],
        out_specs=[
            pl.BlockSpec((tm, D_MODEL), row),
            pl.BlockSpec((tm, ROUTER_LANES), row),
            pl.BlockSpec((tm, ROUTER_LANES), row),
        ],
        out_shape=[
            jax.ShapeDtypeStruct((n, D_MODEL), F32),
            jax.ShapeDtypeStruct((n, ROUTER_LANES), F32),
            jax.ShapeDtypeStruct((n, ROUTER_LANES), BF16),
        ],
        compiler_params=pltpu.CompilerParams(
            dimension_semantics=("arbitrary",), vmem_limit_bytes=VMEM_LIMIT),
        name="post_attn",
    )(y_sb, y_ca, gates, x2d, w_br_sb, w_br_ca, w_out, w_router, b_router, ln_g, ln_b)


MOE_TILE = 512
DEST_LANES = 8
TOP_K = 2


def _moe_rows(n):
    return n * TOP_K + N_EXPERTS * MOE_TILE


def _rank_kernel(pick_ref, info_ref, tri_ref, dest_ref, cnt_ref, run_ref, off_ref, *, tb):
    p = pl.program_id(0)
    i = pl.program_id(1)
    pick = pick_ref[...]

    @pl.when((p == 0) & (i == 0))
    def _():
        run_ref[...] = jnp.zeros_like(run_ref)

    @pl.when(p == 0)
    def _():
        run_ref[...] += jnp.sum(pick.astype(F32), axis=0, keepdims=True)

    @pl.when((p == 1) & (i == 0))
    def _():
        cnt = run_ref[...]
        cnt_ref[...] = cnt
        padded = jnp.ceil(cnt * (1.0 / MOE_TILE)) * MOE_TILE
        lane = lax.broadcasted_iota(jnp.int32, padded.shape, 1)
        scan = padded
        step = 1
        while step < ROUTER_LANES:
            scan = scan + jnp.where(lane >= step, pltpu.roll(scan, step, axis=1), 0.0)
            step *= 2
        off_ref[...] = scan - padded
        run_ref[...] = jnp.zeros_like(run_ref)

    @pl.when(p == 1)
    def _():
        seen = run_ref[0:1, :]
        earlier = _dot(tri_ref[...], pick)
        row_of = earlier + seen + off_ref[0:1, :]
        info = info_ref[...]
        lane = lax.broadcasted_iota(jnp.int32, info.shape, 1)
        lane_f = lane.astype(F32)
        e1 = jnp.sum(jnp.where(lane == ROUTE_E1, info, 0.0), axis=1, keepdims=True)
        e2 = jnp.sum(jnp.where(lane == ROUTE_E2, info, 0.0), axis=1, keepdims=True)
        d1 = jnp.sum(jnp.where(lane_f == e1 + N_GROUPS, row_of, 0.0), axis=1, keepdims=True)
        d2 = jnp.sum(jnp.where(lane_f == e2 + N_GROUPS, row_of, 0.0), axis=1, keepdims=True)
        dest = (jnp.where(lane == 0, d1, 0.0) + jnp.where(lane == 1, d2, 0.0)).astype(jnp.int32)
        dest_ref[...] = dest[:, 0:DEST_LANES]
        run_ref[...] += jnp.sum(pick.astype(F32), axis=0, keepdims=True)


def _moe_rank(pick, info, tb=1024):
    n = pick.shape[0]
    tri = jnp.asarray(np.arange(tb)[None, :] < np.arange(tb)[:, None], dtype=BF16)
    return pl.pallas_call(
        functools.partial(_rank_kernel, tb=tb),
        grid=(2, n // tb),
        in_specs=[
            pl.BlockSpec((tb, ROUTER_LANES), lambda p, i: (i, 0)),
            pl.BlockSpec((tb, ROUTER_LANES), lambda p, i: (i, 0)),
            pl.BlockSpec((tb, tb), lambda p, i: (0, 0)),
        ],
        out_specs=[
            pl.BlockSpec((tb, DEST_LANES), lambda p, i: (i * p, 0)),
            pl.BlockSpec((8, ROUTER_LANES), lambda p, i: (0, 0)),
        ],
        out_shape=[
            jax.ShapeDtypeStruct((n, DEST_LANES), jnp.int32),
            jax.ShapeDtypeStruct((8, ROUTER_LANES), F32),
        ],
        scratch_shapes=[
            pltpu.VMEM((8, ROUTER_LANES), F32),
            pltpu.VMEM((8, ROUTER_LANES), F32),
        ],
        compiler_params=pltpu.CompilerParams(
            dimension_semantics=("arbitrary", "arbitrary"), vmem_limit_bytes=VMEM_LIMIT),
        name="moe_rank",
    )(pick, info, tri)


def _row_copy(src_ref, src_row, dst_ref, dst_row, sem):
    return pltpu.make_async_copy(src_ref.at[pl.ds(src_row, 1), :], dst_ref.at[pl.ds(dst_row, 1), :], sem)


def _dispatch_kernel(d1_ref, d2_ref, last_ref, x_ref, xs_hbm, zero_ref, sem, zsem, *, tb):
    base = pl.program_id(0) * tb

    @pl.when(pl.program_id(0) == 0)
    def _():
        zero_ref[...] = jnp.zeros_like(zero_ref)

        def last_tile_copy(e):
            first_row = pl.multiple_of((last_ref[e] - 1) * MOE_TILE, MOE_TILE)
            return pltpu.make_async_copy(zero_ref, xs_hbm.at[pl.ds(first_row, MOE_TILE), :], zsem)

        def unused_tile_copy(t):
            return pltpu.make_async_copy(zero_ref, xs_hbm.at[pl.ds(t * MOE_TILE, MOE_TILE), :], zsem)

        n_tiles = xs_hbm.shape[0] // MOE_TILE
        min_used = n_tiles - N_EXPERTS
        for wait in (False, True):
            for e in range(N_EXPERTS):
                owns_tiles = last_ref[e] > (last_ref[e - 1] if e else 0)
                pl.when(owns_tiles)(
                    lambda e=e, wait=wait: last_tile_copy(e).wait() if wait else last_tile_copy(e).start())
            for t in range(min_used, n_tiles):
                pl.when(t >= last_ref[N_EXPERTS - 1])(
                    lambda t=t, wait=wait: unused_tile_copy(t).wait() if wait else unused_tile_copy(t).start())

    def issue(r, c):
        _row_copy(x_ref, r, xs_hbm, d1_ref[base + r], sem).start(priority=0)
        _row_copy(x_ref, r, xs_hbm, d2_ref[base + r], sem).start(priority=1)
        return c

    lax.fori_loop(0, tb, issue, 0, unroll=8)
    for _ in range(TOP_K):
        pltpu.make_async_copy(x_ref, xs_hbm.at[pl.ds(0, tb), :], sem).wait()


def _moe_dispatch(x1, dest1, dest2, last_tile, tb=512):
    n = x1.shape[0]
    rows = _moe_rows(n)
    grid_spec = pltpu.PrefetchScalarGridSpec(
        num_scalar_prefetch=3,
        grid=(n // tb,),
        in_specs=[pl.BlockSpec((tb, D_MODEL), lambda i, d1, d2, last: (i, 0))],
        out_specs=pl.BlockSpec(memory_space=pl.ANY),
        scratch_shapes=[
            pltpu.VMEM((MOE_TILE, D_MODEL), F32),
            pltpu.SemaphoreType.DMA(()),
            pltpu.SemaphoreType.DMA(()),
        ],
    )
    return pl.pallas_call(
        functools.partial(_dispatch_kernel, tb=tb),
        grid_spec=grid_spec,
        out_shape=jax.ShapeDtypeStruct((rows, D_MODEL), F32),
        compiler_params=pltpu.CompilerParams(
            dimension_semantics=("arbitrary",), vmem_limit_bytes=VMEM_LIMIT),
        name="moe_dispatch",
    )(dest1, dest2, last_tile, x1)


def _experts_kernel(te_ref, nt_ref, xs_ref, wg_ref, wu_ref, wd_ref, ys_ref, wgu_s, wd_s):
    t = pl.program_id(0)
    changed = (t == 0) | (te_ref[t] != te_ref[jnp.maximum(t - 1, 0)])

    @pl.when(changed)
    def _():
        wgu_s[:, 0:D_EXPERT] = wg_ref[...].astype(BF16)
        wgu_s[:, D_EXPERT:2 * D_EXPERT] = wu_ref[...].astype(BF16)
        wd_s[...] = wd_ref[...].astype(BF16)

    @pl.when(t < nt_ref[0])
    def _():
        gu = _dot(xs_ref[...].astype(BF16), wgu_s[...])
        gate = gu[:, 0:D_EXPERT]
        up = gu[:, D_EXPERT:2 * D_EXPERT]
        hid = (gate * (1.0 / (1.0 + jnp.exp(-gate)))) * up
        ys_ref[...] = _dot(hid.astype(BF16), wd_s[...])

    @pl.when(t >= nt_ref[0])
    def _():
        ys_ref[...] = jnp.zeros_like(ys_ref)


def _moe_experts(xs, tile_expert, n_tiles_used, w_gate, w_up, w_down, layer):
    rows = xs.shape[0]
    first = layer * N_EXPERTS
    grid_spec = pltpu.PrefetchScalarGridSpec(
        num_scalar_prefetch=2,
        grid=(rows // MOE_TILE,),
        in_specs=[
            pl.BlockSpec((MOE_TILE, D_MODEL), lambda t, te, nt: (jnp.minimum(t, nt[0] - 1), 0)),
            pl.BlockSpec((None, D_MODEL, D_EXPERT), lambda t, te, nt: (first + te[t], 0, 0)),
            pl.BlockSpec((None, D_MODEL, D_EXPERT), lambda t, te, nt: (first + te[t], 0, 0)),
            pl.BlockSpec((None, D_EXPERT, D_MODEL), lambda t, te, nt: (first + te[t], 0, 0)),
        ],
        out_specs=pl.BlockSpec((MOE_TILE, D_MODEL), lambda t, te, nt: (t, 0)),
        scratch_shapes=[
            pltpu.VMEM((D_MODEL, 2 * D_EXPERT), BF16),
            pltpu.VMEM((D_EXPERT, D_MODEL), BF16),
        ],
    )
    return pl.pallas_call(
        _experts_kernel,
        grid_spec=grid_spec,
        out_shape=jax.ShapeDtypeStruct((rows, D_MODEL), F32),
        compiler_params=pltpu.CompilerParams(
            dimension_semantics=("arbitrary",), vmem_limit_bytes=VMEM_LIMIT),
        name="moe_experts",
    )(tile_expert, n_tiles_used, xs, w_gate, w_up, w_down)


def _combine_kernel(d1_ref, d2_ref, x1_ref, info_ref, ys_hbm, g_ref, b_ref, out_ref, y1_ref, y2_ref, sem, *, tb):
    base = pl.program_id(0) * tb

    def issue(r, c):
        _row_copy(ys_hbm, d1_ref[base + r], y1_ref, r, sem).start(priority=0)
        _row_copy(ys_hbm, d2_ref[base + r], y2_ref, r, sem).start(priority=1)
        return c

    lax.fori_loop(0, tb, issue, 0, unroll=8)
    for y_ref in (y1_ref, y2_ref):
        pltpu.make_async_copy(ys_hbm.at[pl.ds(0, tb), :], y_ref, sem).wait()
    info = info_ref[...]
    lane = lax.broadcasted_iota(jnp.int32, info.shape, 1)
    w1 = jnp.sum(jnp.where(lane == ROUTE_W1, info, 0.0), axis=1, keepdims=True)
    w2 = jnp.sum(jnp.where(lane == ROUTE_W2, info, 0.0), axis=1, keepdims=True)
    ffn = w1 * y1_ref[...] + w2 * y2_ref[...]
    out_ref[...] = _layer_norm(ALPHA * x1_ref[...] + ffn, g_ref[...], b_ref[...])


def _moe_combine(x1, info, ys, dest1, dest2, ln_g, ln_b, tb=512):
    n = x1.shape[0]
    grid_spec = pltpu.PrefetchScalarGridSpec(
        num_scalar_prefetch=2,
        grid=(n // tb,),
        in_specs=[
            pl.BlockSpec((tb, D_MODEL), lambda i, d1, d2: (i, 0)),
            pl.BlockSpec((tb, ROUTER_LANES), lambda i, d1, d2: (i, 0)),
            pl.BlockSpec(memory_space=pl.ANY),
            pl.BlockSpec((1, D_MODEL), lambda i, d1, d2: (0, 0)),
            pl.BlockSpec((1, D_MODEL), lambda i, d1, d2: (0, 0)),
        ],
        out_specs=pl.BlockSpec((tb, D_MODEL), lambda i, d1, d2: (i, 0)),
        scratch_shapes=[
            pltpu.VMEM((tb, D_MODEL), F32),
            pltpu.VMEM((tb, D_MODEL), F32),
            pltpu.SemaphoreType.DMA(()),
        ],
    )
    return pl.pallas_call(
        functools.partial(_combine_kernel, tb=tb),
        grid_spec=grid_spec,
        out_shape=jax.ShapeDtypeStruct((n, D_MODEL), F32),
        compiler_params=pltpu.CompilerParams(
            dimension_semantics=("arbitrary",), vmem_limit_bytes=VMEM_LIMIT),
        name="moe_combine",
    )(dest1, dest2, x1, info, ys, ln_g, ln_b)


def _moe(x1, info, pick, w_gate, w_up, w_down, layer, ln_g, ln_b):
    n = x1.shape[0]
    dest, counts = _moe_rank(pick, info)
    dest1 = dest[:, 0]
    dest2 = dest[:, 1]
    tiles = jnp.ceil(counts[0, N_GROUPS:N_GROUPS + N_EXPERTS] * (1.0 / MOE_TILE)).astype(jnp.int32)
    upto = np.arange(N_EXPERTS)[None, :] <= np.arange(N_EXPERTS)[:, None]
    last_tile = jnp.sum(jnp.where(upto, tiles[None, :], 0), axis=1)
    tile_ids = jnp.arange(_moe_rows(n) // MOE_TILE, dtype=jnp.int32)
    tile_expert = jnp.minimum(jnp.sum(tile_ids[:, None] >= last_tile[None, :], axis=1), N_EXPERTS - 1)
    xs = _moe_dispatch(x1, dest1, dest2, last_tile)
    ys = _moe_experts(xs, tile_expert.astype(jnp.int32), last_tile[N_EXPERTS - 1:], w_gate, w_up, w_down, layer)
    return _moe_combine(x1, info, ys, dest1, dest2, ln_g, ln_b)


def _layer(x2d, batch, w_in, b_gate, rel_bias, w_br_sb, w_br_ca, w_out, ln1_g, ln1_b,
           w_group, b_group, w_erouter, b_erouter, w_gate, w_up, w_down, layer, ln2_g, ln2_b):
    n = x2d.shape[0]
    seq = n // batch
    scale = HEAD_DIM ** -0.5
    qscale = np.ones((1, D_QKV), np.float32)
    qscale[:, 0:W_SB] = scale * np.log2(np.e)
    qscale[:, 3 * W_SB:3 * W_SB + W_CA] = scale * np.log2(np.e)
    qkv, gates = _in_proj(x2d, w_in.astype(BF16), jnp.asarray(qscale), b_gate.reshape(1, 2 * D_MODEL))
    y_sb = _sb_attn(qkv, batch)
    y_ca = _ca_attn(qkv, batch, _ca_bias_line(rel_bias))

    w_router = jnp.concatenate(
        [w_group, w_erouter.transpose(1, 0, 2).reshape(D_MODEL, N_EXPERTS)], axis=1)
    w_router = jnp.pad(w_router, ((0, 0), (0, ROUTER_LANES - N_GROUPS - N_EXPERTS)))
    b_router = jnp.pad(jnp.concatenate([b_group, b_erouter.reshape(N_EXPERTS)]),
                       (0, ROUTER_LANES - N_GROUPS - N_EXPERTS)).reshape(1, ROUTER_LANES)
    x1, info, pick = _post_attn(y_sb, y_ca, gates, x2d, w_br_sb.astype(BF16), w_br_ca.astype(BF16),
                        w_out.astype(BF16), w_router, b_router,
                        ln1_g.reshape(1, D_MODEL), ln1_b.reshape(1, D_MODEL))

    return _moe(x1, info, pick, w_gate, w_up, w_down, layer,
                ln2_g.reshape(1, D_MODEL), ln2_b.reshape(1, D_MODEL))


def kernel(x, w_in, b_gate, rel_bias, w_br_sb, w_br_ca, w_out, ln1_g, ln1_b, w_group, b_group,
           w_erouter, b_erouter, w_gate, w_up, w_down, ln2_g, ln2_b):
    batch, seq, d = x.shape
    h = x.reshape(batch * seq, d)
    w_gate = w_gate.reshape(DEPTH * N_EXPERTS, D_MODEL, D_EXPERT)
    w_up = w_up.reshape(DEPTH * N_EXPERTS, D_MODEL, D_EXPERT)
    w_down = w_down.reshape(DEPTH * N_EXPERTS, D_EXPERT, D_MODEL)
    for l in range(DEPTH):
        h = _layer(h, batch, w_in[l], b_gate[l], rel_bias[l], w_br_sb[l], w_br_ca[l], w_out[l],
                   ln1_g[l], ln1_b[l], w_group[l], b_group[l], w_erouter[l], b_erouter[l],
                   w_gate, w_up, w_down, l, ln2_g[l], ln2_b[l])
    return h.reshape(batch, seq, d)
```

```python
import functools

import jax
import jax.numpy as jnp
import numpy as np
from jax import lax
from jax.experimental import pallas as pl
from jax.experimental.pallas import tpu as pltpu

D_MODEL = 1024
DEPTH = 2
CHUNK = 64
HEAD_DIM = 64
H_SB = 8
H_CA = 8
W_SB = H_SB * HEAD_DIM
W_CA = H_CA * HEAD_DIM
N_PAST_CHUNKS = 8
REL_CLIP = 128
N_GROUPS = 4
EXPERTS_PER_GROUP = 8
N_EXPERTS = N_GROUPS * EXPERTS_PER_GROUP
D_EXPERT = 256
ALPHA = (2.0 * DEPTH) ** 0.25
LN_EPS = 1e-5
D_QKV = 3 * W_SB + 3 * W_CA
D_IN = D_QKV + 2 * D_MODEL
NEG_INF = -1e30

LANES = 128
HEADS_PER_TILE = LANES // HEAD_DIM
ROUTER_LANES = LANES
VMEM_LIMIT = 56 * 1024 * 1024

BF16 = jnp.bfloat16
F32 = jnp.float32

_NT = (((1,), (1,)), ((), ()))


def _dot(a, b):
    return jnp.dot(a, b, preferred_element_type=F32)


def _layer_norm(h, g, b):
    mu = jnp.mean(h, axis=-1, keepdims=True)
    hc = h - mu
    var = jnp.mean(hc * hc, axis=-1, keepdims=True)
    return hc * lax.rsqrt(var + LN_EPS) * g + b


def _split_bf16(a):
    hi = a.astype(BF16)
    lo = (a - hi.astype(F32)).astype(BF16)
    return hi, lo


def _in_proj_kernel(x_ref, w_ref, scale_ref, bg_ref, qkv_ref, gate_ref):
    xb = x_ref[...].astype(BF16)
    for c in range(D_QKV // D_MODEL):
        cols = slice(c * D_MODEL, (c + 1) * D_MODEL)
        acc = _dot(xb, w_ref[:, cols])
        qkv_ref[:, cols] = (acc * scale_ref[:, cols]).astype(BF16)
    for c in range(2):
        cols = slice(c * D_MODEL, (c + 1) * D_MODEL)
        wcols = slice(D_QKV + c * D_MODEL, D_QKV + (c + 1) * D_MODEL)
        logit = _dot(xb, w_ref[:, wcols]) + bg_ref[:, cols]
        gate_ref[:, cols] = 1.0 / (1.0 + jnp.exp(-logit))


def _in_proj(x2d, w_in_bf16, qscale, b_gate_row, tm=512):
    n = x2d.shape[0]
    return pl.pallas_call(
        _in_proj_kernel,
        grid=(n // tm,),
        in_specs=[
            pl.BlockSpec((tm, D_MODEL), lambda i: (i, 0)),
            pl.BlockSpec((D_MODEL, D_IN), lambda i: (0, 0)),
            pl.BlockSpec((1, D_QKV), lambda i: (0, 0)),
            pl.BlockSpec((1, 2 * D_MODEL), lambda i: (0, 0)),
        ],
        out_specs=[
            pl.BlockSpec((tm, D_QKV), lambda i: (i, 0)),
            pl.BlockSpec((tm, 2 * D_MODEL), lambda i: (i, 0)),
        ],
        out_shape=[
            jax.ShapeDtypeStruct((n, D_QKV), BF16),
            jax.ShapeDtypeStruct((n, 2 * D_MODEL), F32),
        ],
        compiler_params=pltpu.CompilerParams(
            dimension_semantics=("arbitrary",), vmem_limit_bytes=VMEM_LIMIT),
        name="in_proj",
    )(x2d, w_in_bf16, qscale, b_gate_row)


SB_DEAD_BITS = 160.0
SB_STEP_BLOCKS = 4


def _sb_kernel(q_ref, k_ref, v_ref, u_ref, o_ref, z_ref, arg_ref, rs_ref, acc_ref, car_ref, *, tq):
    step = pl.program_id(2)
    lane_q = lax.broadcasted_iota(jnp.int32, (tq, LANES), 1)

    def split_heads(x):
        zero = jnp.zeros_like(x)
        return jnp.where(lane_q < HEAD_DIM, x, zero), jnp.where(lane_q < HEAD_DIM, zero, x)

    blocks = [step * SB_STEP_BLOCKS + j for j in range(SB_STEP_BLOCKS)]
    q_heads = [split_heads(q_ref[pl.ds(j * tq, tq), :]) for j in range(SB_STEP_BLOCKS)]

    def key_rows(i, n):
        return pl.ds(pl.multiple_of(jnp.maximum(i - n, 0) * tq, tq), tq)

    def block_pairs(subs, n, diagonal):
        for j in subs:
            for b in range(2):
                k = k_ref[key_rows(blocks[j], n + b), :]
                for h in range(HEADS_PER_TILE):
                    z = lax.dot_general(q_heads[j][h], k, _NT, preferred_element_type=F32)
                    if diagonal and b == 0:
                        row = lax.broadcasted_iota(jnp.int32, (tq, tq), 0)
                        col = lax.broadcasted_iota(jnp.int32, (tq, tq), 1)
                        z = jnp.where(col < row, z, NEG_INF)
                    z_ref[j, b, h] = z
        def terms(j):
            for b in range(2):
                for h in range(HEADS_PER_TILE):
                    z = z_ref[j, b, h]
                    sp = jnp.maximum(z, 0.0) + jnp.log2(1.0 + jnp.exp2(-jnp.abs(z)))
                    sums = _dot(sp.astype(BF16), u_ref[...])
                    arg_ref[j, b, h] = (z - sp) - sums[:, 0:tq]
                    rs_ref[j, b, h] = sums[:, tq:tq + LANES]

        def apply(j):
            carries = [car_ref[j]]
            for b in range(2):
                carries.append(carries[b] + rs_ref[j, b])
            car_ref[j] = carries[2]
            v_parts = []
            for b in range(2):
                v = v_ref[key_rows(blocks[j], n + b), :]
                if b == 1:
                    v = jnp.where(n + b <= blocks[j], v, jnp.zeros_like(v))
                v_parts += split_heads(v)
            ws = []
            for b in range(2):
                for h in range(HEADS_PER_TILE):
                    carry = jnp.concatenate([carries[b][h]] * (tq // LANES), axis=1)
                    ws.append(jnp.exp2(arg_ref[j, b, h] - carry).astype(BF16))
            acc_ref[j] += _dot(jnp.concatenate(ws, axis=1), jnp.concatenate(v_parts, axis=0))
            return jnp.min(carries[2])

        subs = list(subs)
        least = []
        terms(subs[0])
        for prev, j in zip(subs, subs[1:]):
            terms(j)
            least.append(apply(prev))
        least.append(apply(subs[-1]))
        return least

    acc_ref[...] = jnp.zeros_like(acc_ref)
    car_ref[...] = jnp.zeros_like(car_ref)
    first = block_pairs(range(SB_STEP_BLOCKS), 0, True)

    for j in range(SB_STEP_BLOCKS):
        def more(state, j=j):
            n, least = state
            return (n <= blocks[j]) & (least < SB_DEAD_BITS)

        def body(state, j=j):
            n, _ = state
            return n + 2, block_pairs([j], n, False)[0]

        lax.while_loop(more, body, (jnp.int32(2), first[j]))
        o_ref[pl.ds(j * tq, tq), :] = acc_ref[j].astype(o_ref.dtype)


def _sb_attn(qkv, b, tq=256):
    s = qkv.shape[0] // b
    n_tiles = W_SB // LANES
    step_rows = SB_STEP_BLOCKS * tq
    steps = s // step_rows
    u = jnp.asarray(np.concatenate([np.arange(tq)[:, None] > np.arange(tq)[None, :],
                                    np.ones((tq, LANES), bool)], axis=1), dtype=BF16)
    return pl.pallas_call(
        functools.partial(_sb_kernel, tq=tq),
        grid=(b, n_tiles, steps),
        in_specs=[
            pl.BlockSpec((step_rows, LANES), lambda bi, hp, i: (bi * steps + i, hp)),
            pl.BlockSpec((s, LANES), lambda bi, hp, i: (bi, n_tiles + hp)),
            pl.BlockSpec((s, LANES), lambda bi, hp, i: (bi, 2 * n_tiles + hp)),
            pl.BlockSpec((tq, tq + LANES), lambda bi, hp, i: (0, 0)),
        ],
        out_specs=pl.BlockSpec((step_rows, LANES), lambda bi, hp, i: (bi * steps + i, hp)),
        out_shape=jax.ShapeDtypeStruct((b * s, W_SB), BF16),
        scratch_shapes=[
            pltpu.VMEM((SB_STEP_BLOCKS, 2, HEADS_PER_TILE, tq, tq), F32),
            pltpu.VMEM((SB_STEP_BLOCKS, 2, HEADS_PER_TILE, tq, tq), F32),
            pltpu.VMEM((SB_STEP_BLOCKS, 2, HEADS_PER_TILE, tq, LANES), F32),
            pltpu.VMEM((SB_STEP_BLOCKS, tq, LANES), F32),
            pltpu.VMEM((SB_STEP_BLOCKS, HEADS_PER_TILE, tq, LANES), F32),
        ],
        compiler_params=pltpu.CompilerParams(
            dimension_semantics=("arbitrary", "arbitrary", "arbitrary"), vmem_limit_bytes=VMEM_LIMIT),
        name="sb_attn",
    )(qkv, qkv, qkv, u)


CA_GROUP = 4
CA_STEP_GROUPS = 8
CA_TQ = CA_GROUP * CHUNK
CA_BAND = (CA_GROUP + N_PAST_CHUNKS) * CHUNK
CA_PAD = N_PAST_CHUNKS * CHUNK


def _ca_kernel(q_ref, k_ref, v_ref, line_ref, ones_ref, o_ref, kp_ref, vp_ref, bias_ref):
    c = pl.program_id(2)
    s = k_ref.shape[0]

    @pl.when(c == 0)
    def _():
        kp_ref[0:CA_PAD, :] = jnp.zeros((CA_PAD, LANES), BF16)
        vp_ref[0:CA_PAD, :] = jnp.zeros((CA_PAD, LANES), BF16)
        kp_ref[CA_PAD:CA_PAD + s, :] = k_ref[...]
        vp_ref[CA_PAD:CA_PAD + s, :] = v_ref[...]
        r = lax.broadcasted_iota(jnp.int32, (CA_TQ, CA_BAND), 0)
        p = lax.broadcasted_iota(jnp.int32, (CA_TQ, CA_BAND), 1)
        shift = CHUNK.bit_length() - 1
        qc = lax.shift_right_logical(r, shift)
        kc = lax.shift_right_logical(p, shift)
        in_band = (kc >= qc) & (kc <= qc + N_PAST_CHUNKS)
        for h in range(HEADS_PER_TILE):
            rows = jnp.broadcast_to(line_ref[h], (CA_TQ, CA_LINE))
            skew = pltpu.roll(rows, 1, axis=1, stride=1, stride_axis=0)
            bias_ref[h] = jnp.where(in_band, skew[:, CA_TQ:CA_TQ + CA_BAND], NEG_INF)

    lane_q = lax.broadcasted_iota(jnp.int32, (CA_TQ, LANES), 1)
    lane_v = lax.broadcasted_iota(jnp.int32, (CA_BAND, LANES), 1)

    def band_start(g):
        return pl.multiple_of((c * CA_STEP_GROUPS + g) * CA_TQ, CA_TQ)

    def scores(g):
        kb = kp_ref[pl.ds(band_start(g), CA_BAND), :]
        q = q_ref[pl.ds(g * CA_TQ, CA_TQ), :]
        zero_q = jnp.zeros_like(q)
        q_heads = (jnp.where(lane_q < HEAD_DIM, q, zero_q), jnp.where(lane_q < HEAD_DIM, zero_q, q))
        return [lax.dot_general(q_heads[h], kb, _NT, preferred_element_type=F32) for h in range(HEADS_PER_TILE)]

    def attend(g, qk, masked):
        group = c * CA_STEP_GROUPS + g
        vb = vp_ref[pl.ds(band_start(g), CA_BAND), :]
        zero_v = jnp.zeros_like(vb)
        v_heads = jnp.concatenate(
            [jnp.where(lane_v < HEAD_DIM, vb, zero_v), jnp.where(lane_v < HEAD_DIM, zero_v, vb)], axis=0)
        v_and_ones = jnp.concatenate([v_heads, ones_ref[...]], axis=1)
        es = []
        for h in range(HEADS_PER_TILE):
            sc = qk[h] + bias_ref[h]
            if masked:
                pos = lax.broadcasted_iota(jnp.int32, (CA_TQ, CA_BAND), 1)
                sc = jnp.where(pos >= CA_PAD - group * CA_TQ, sc, NEG_INF)
            m = jnp.max(sc, axis=1, keepdims=True)
            es.append(jnp.exp2(sc - m).astype(BF16))
        both = _dot(jnp.concatenate(es, axis=1), v_and_ones)
        o_ref[pl.ds(g * CA_TQ, CA_TQ), :] = (both[:, 0:LANES] / both[:, LANES:2 * LANES]).astype(o_ref.dtype)

    def step(masked):
        qk = scores(0)
        for g in range(CA_STEP_GROUPS):
            qk_next = scores(g + 1) if g + 1 < CA_STEP_GROUPS else None
            attend(g, qk, masked)
            qk = qk_next

    assert CA_PAD // CA_TQ <= CA_STEP_GROUPS
    pl.when(c == 0)(lambda: step(True))
    pl.when(c > 0)(lambda: step(False))


CA_LINE = CA_TQ + CA_BAND


def _ca_bias_line(rel_bias):
    h = rel_bias.shape[0]
    n_far = CA_PAD + CA_TQ - 1 - REL_CLIP
    n_neg = CA_BAND - 1 - CA_PAD - REL_CLIP
    rb = rel_bias.astype(F32) * np.float32(np.log2(np.e))
    line = jnp.concatenate([jnp.broadcast_to(rb[:, 2 * REL_CLIP:], (h, n_far)), rb[:, ::-1],
                            jnp.broadcast_to(rb[:, :1], (h, n_neg + 1))], axis=1)
    assert line.shape[1] == CA_LINE
    return line.reshape(h, 1, CA_LINE)


def _ca_attn(qkv, b, bias_line):
    s = qkv.shape[0] // b
    n_tiles = W_CA // LANES
    base = 3 * W_SB // LANES
    step_rows = CA_STEP_GROUPS * CA_TQ
    steps = s // step_rows
    head_of_row = np.arange(HEADS_PER_TILE * CA_BAND)[:, None] // CA_BAND
    head_of_lane = np.arange(LANES)[None, :] // HEAD_DIM
    ones = jnp.asarray(head_of_row == head_of_lane, dtype=BF16)
    return pl.pallas_call(
        _ca_kernel,
        grid=(b, n_tiles, steps),
        in_specs=[
            pl.BlockSpec((step_rows, LANES), lambda bi, hp, c: (bi * steps + c, base + hp)),
            pl.BlockSpec((s, LANES), lambda bi, hp, c: (bi, base + n_tiles + hp)),
            pl.BlockSpec((s, LANES), lambda bi, hp, c: (bi, base + 2 * n_tiles + hp)),
            pl.BlockSpec((HEADS_PER_TILE, 1, CA_LINE), lambda bi, hp, c: (hp, 0, 0)),
            pl.BlockSpec((HEADS_PER_TILE * CA_BAND, LANES), lambda bi, hp, c: (0, 0)),
        ],
        out_specs=pl.BlockSpec((step_rows, LANES), lambda bi, hp, c: (bi * steps + c, hp)),
        out_shape=jax.ShapeDtypeStruct((b * s, W_CA), BF16),
        scratch_shapes=[
            pltpu.VMEM((CA_PAD + s, LANES), BF16),
            pltpu.VMEM((CA_PAD + s, LANES), BF16),
            pltpu.VMEM((HEADS_PER_TILE, CA_TQ, CA_BAND), F32),
        ],
        compiler_params=pltpu.CompilerParams(
            dimension_semantics=("arbitrary", "arbitrary", "arbitrary"), vmem_limit_bytes=VMEM_LIMIT),
        name="ca_attn",
    )(qkv, qkv, qkv, bias_line, ones)


ROUTE_E1, ROUTE_E2, ROUTE_W1, ROUTE_W2 = 0, 1, 2, 3


def _route(lg):
    lane = lax.broadcasted_iota(jnp.int32, lg.shape, 1)
    big = jnp.int32(ROUTER_LANES)
    is_group = lane < N_GROUPS
    g_max = jnp.max(jnp.where(is_group, lg, -jnp.inf), axis=1, keepdims=True)
    g_idx = jnp.min(jnp.where(is_group & (lg == g_max), lane, big), axis=1, keepdims=True)
    g_den = jnp.sum(jnp.where(is_group, jnp.exp(lg - g_max), 0.0), axis=1, keepdims=True)
    g_val = 1.0 / g_den
    lo = N_GROUPS + EXPERTS_PER_GROUP * g_idx
    in_group = (lane >= lo) & (lane < lo + EXPERTS_PER_GROUP)
    v1 = jnp.max(jnp.where(in_group, lg, -jnp.inf), axis=1, keepdims=True)
    i1 = jnp.min(jnp.where(in_group & (lg == v1), lane, big), axis=1, keepdims=True)
    rest = in_group & (lane != i1)
    v2 = jnp.max(jnp.where(rest, lg, -jnp.inf), axis=1, keepdims=True)
    i2 = jnp.min(jnp.where(rest & (lg == v2), lane, big), axis=1, keepdims=True)
    e2 = jnp.exp(v2 - v1)
    w1 = g_val / (1.0 + e2)
    w2 = g_val * e2 / (1.0 + e2)
    picked = ((lane == i1) | (lane == i2)).astype(BF16)
    info = (jnp.where(lane == ROUTE_E1, (i1 - N_GROUPS).astype(F32), 0.0)
            + jnp.where(lane == ROUTE_E2, (i2 - N_GROUPS).astype(F32), 0.0)
            + jnp.where(lane == ROUTE_W1, w1, 0.0) + jnp.where(lane == ROUTE_W2, w2, 0.0))
    return info, picked


def _post_kernel(ysb_ref, yca_ref, gate_ref, x_ref, wsb_ref, wca_ref, wout_ref, wr_ref, br_ref,
                 g_ref, b_ref, x1_ref, info_ref, pick_ref):
    w_hi, w_lo = _split_bf16(wr_ref[...])
    w_split = jnp.concatenate([w_hi, w_lo], axis=1)
    tm = x_ref.shape[0]
    subs = [pl.ds(s * POST_SUB, POST_SUB) for s in range(tm // POST_SUB)]
    for rows in subs:
        a = _dot(ysb_ref[rows, :], wsb_ref[...])
        c = _dot(yca_ref[rows, :], wca_ref[...])
        mix = gate_ref[rows, 0:D_MODEL] * a + gate_ref[rows, D_MODEL:2 * D_MODEL] * c
        x1_ref[rows, :] = ALPHA * x_ref[rows, :] + _dot(mix.astype(BF16), wout_ref[...])
    for rows in subs:
        x1 = _layer_norm(x1_ref[rows, :], g_ref[...], b_ref[...])
        x1_ref[rows, :] = x1
        x_hi, x_lo = _split_bf16(x1)
        parts = _dot(jnp.concatenate([x_hi, x_lo], axis=0), w_split)
        lg = (parts[0:POST_SUB, 0:ROUTER_LANES] + parts[0:POST_SUB, ROUTER_LANES:]
              + parts[POST_SUB:, 0:ROUTER_LANES] + parts[POST_SUB:, ROUTER_LANES:]) + br_ref[...]
        info_ref[rows, :], pick_ref[rows, :] = _route(lg)


POST_SUB = 256


def _post_attn(y_sb, y_ca, gates, x2d, w_br_sb, w_br_ca, w_out, w_router, b_router, ln_g, ln_b, tm=1024):
    n = x2d.shape[0]
    row = lambda i: (i, 0)
    fixed = lambda i: (0, 0)
    return pl.pallas_call(
        _post_kernel,
        grid=(n // tm,),
        in_specs=[
            pl.BlockSpec((tm, W_SB), row),
            pl.BlockSpec((tm, W_CA), row),
            pl.BlockSpec((tm, 2 * D_MODEL), row),
            pl.BlockSpec((tm, D_MODEL), row),
            pl.BlockSpec((W_SB, D_MODEL), fixed),
            pl.BlockSpec((W_CA, D_MODEL), fixed),
            pl.BlockSpec((D_MODEL, D_MODEL), fixed),
            pl.BlockSpec((D_MODEL, ROUTER_LANES), fixed),
            pl.BlockSpec((1, ROUTER_LANES), fixed),
            pl.BlockSpec((1, D_MODEL), fixed),
            pl.BlockSpec((1, D_MODEL), fixed),
        ],
        out_specs=[
            pl.BlockSpec((tm, D_MODEL), row),
            pl.BlockSpec((tm, ROUTER_LANES), row),
            pl.BlockSpec((tm, ROUTER_LANES), row),
        ],
        out_shape=[
            jax.ShapeDtypeStruct((n, D_MODEL), F32),
            jax.ShapeDtypeStruct((n, ROUTER_LANES), F32),
            jax.ShapeDtypeStruct((n, ROUTER_LANES), BF16),
        ],
        compiler_params=pltpu.CompilerParams(
            dimension_semantics=("arbitrary",), vmem_limit_bytes=VMEM_LIMIT),
        name="post_attn",
    )(y_sb, y_ca, gates, x2d, w_br_sb, w_br_ca, w_out, w_router, b_router, ln_g, ln_b)


MOE_TILE = 512
DEST_LANES = 8
TOP_K = 2


def _moe_rows(n):
    return n * TOP_K + N_EXPERTS * MOE_TILE


def _rank_kernel(pick_ref, info_ref, tri_ref, dest_ref, cnt_ref, run_ref, off_ref, *, tb):
    p = pl.program_id(0)
    i = pl.program_id(1)
    pick = pick_ref[...]

    @pl.when((p == 0) & (i == 0))
    def _():
        run_ref[...] = jnp.zeros_like(run_ref)

    @pl.when(p == 0)
    def _():
        run_ref[...] += jnp.sum(pick.astype(F32), axis=0, keepdims=True)

    @pl.when((p == 1) & (i == 0))
    def _():
        cnt = run_ref[...]
        cnt_ref[...] = cnt
        padded = jnp.ceil(cnt * (1.0 / MOE_TILE)) * MOE_TILE
        lane = lax.broadcasted_iota(jnp.int32, padded.shape, 1)
        scan = padded
        step = 1
        while step < ROUTER_LANES:
            scan = scan + jnp.where(lane >= step, pltpu.roll(scan, step, axis=1), 0.0)
            step *= 2
        off_ref[...] = scan - padded
        run_ref[...] = jnp.zeros_like(run_ref)

    @pl.when(p == 1)
    def _():
        seen = run_ref[0:1, :]
        earlier = _dot(tri_ref[...], pick)
        row_of = earlier + seen + off_ref[0:1, :]
        info = info_ref[...]
        lane = lax.broadcasted_iota(jnp.int32, info.shape, 1)
        lane_f = lane.astype(F32)
        e1 = jnp.sum(jnp.where(lane == ROUTE_E1, info, 0.0), axis=1, keepdims=True)
        e2 = jnp.sum(jnp.where(lane == ROUTE_E2, info, 0.0), axis=1, keepdims=True)
        d1 = jnp.sum(jnp.where(lane_f == e1 + N_GROUPS, row_of, 0.0), axis=1, keepdims=True)
        d2 = jnp.sum(jnp.where(lane_f == e2 + N_GROUPS, row_of, 0.0), axis=1, keepdims=True)
        dest = (jnp.where(lane == 0, d1, 0.0) + jnp.where(lane == 1, d2, 0.0)).astype(jnp.int32)
        dest_ref[...] = dest[:, 0:DEST_LANES]
        run_ref[...] += jnp.sum(pick.astype(F32), axis=0, keepdims=True)


def _moe_rank(pick, info, tb=1024):
    n = pick.shape[0]
    tri = jnp.asarray(np.arange(tb)[None, :] < np.arange(tb)[:, None], dtype=BF16)
    return pl.pallas_call(
        functools.partial(_rank_kernel, tb=tb),
        grid=(2, n // tb),
        in_specs=[
            pl.BlockSpec((tb, ROUTER_LANES), lambda p, i: (i, 0)),
            pl.BlockSpec((tb, ROUTER_LANES), lambda p, i: (i, 0)),
            pl.BlockSpec((tb, tb), lambda p, i: (0, 0)),
        ],
        out_specs=[
            pl.BlockSpec((tb, DEST_LANES), lambda p, i: (i * p, 0)),
            pl.BlockSpec((8, ROUTER_LANES), lambda p, i: (0, 0)),
        ],
        out_shape=[
            jax.ShapeDtypeStruct((n, DEST_LANES), jnp.int32),
            jax.ShapeDtypeStruct((8, ROUTER_LANES), F32),
        ],
        scratch_shapes=[
            pltpu.VMEM((8, ROUTER_LANES), F32),
            pltpu.VMEM((8, ROUTER_LANES), F32),
        ],
        compiler_params=pltpu.CompilerParams(
            dimension_semantics=("arbitrary", "arbitrary"), vmem_limit_bytes=VMEM_LIMIT),
        name="moe_rank",
    )(pick, info, tri)


def _row_copy(src_ref, src_row, dst_ref, dst_row, sem):
    return pltpu.make_async_copy(src_ref.at[pl.ds(src_row, 1), :], dst_ref.at[pl.ds(dst_row, 1), :], sem)


def _dispatch_kernel(d1_ref, d2_ref, last_ref, x_ref, xs_hbm, zero_ref, sem, zsem, *, tb):
    base = pl.program_id(0) * tb

    @pl.when(pl.program_id(0) == 0)
    def _():
        zero_ref[...] = jnp.zeros_like(zero_ref)

        def last_tile_copy(e):
            first_row = pl.multiple_of((last_ref[e] - 1) * MOE_TILE, MOE_TILE)
            return pltpu.make_async_copy(zero_ref, xs_hbm.at[pl.ds(first_row, MOE_TILE), :], zsem)

        def unused_tile_copy(t):
            return pltpu.make_async_copy(zero_ref, xs_hbm.at[pl.ds(t * MOE_TILE, MOE_TILE), :], zsem)

        n_tiles = xs_hbm.shape[0] // MOE_TILE
        min_used = n_tiles - N_EXPERTS
        for wait in (False, True):
            for e in range(N_EXPERTS):
                owns_tiles = last_ref[e] > (last_ref[e - 1] if e else 0)
                pl.when(owns_tiles)(
                    lambda e=e, wait=wait: last_tile_copy(e).wait() if wait else last_tile_copy(e).start())
            for t in range(min_used, n_tiles):
                pl.when(t >= last_ref[N_EXPERTS - 1])(
                    lambda t=t, wait=wait: unused_tile_copy(t).wait() if wait else unused_tile_copy(t).start())

    def issue(r, c):
        _row_copy(x_ref, r, xs_hbm, d1_ref[base + r], sem).start(priority=0)
        _row_copy(x_ref, r, xs_hbm, d2_ref[base + r], sem).start(priority=1)
        return c

    lax.fori_loop(0, tb, issue, 0, unroll=8)
    for _ in range(TOP_K):
        pltpu.make_async_copy(x_ref, xs_hbm.at[pl.ds(0, tb), :], sem).wait()


def _moe_dispatch(x1, dest1, dest2, last_tile, tb=512):
    n = x1.shape[0]
    rows = _moe_rows(n)
    grid_spec = pltpu.PrefetchScalarGridSpec(
        num_scalar_prefetch=3,
        grid=(n // tb,),
        in_specs=[pl.BlockSpec((tb, D_MODEL), lambda i, d1, d2, last: (i, 0))],
        out_specs=pl.BlockSpec(memory_space=pl.ANY),
        scratch_shapes=[
            pltpu.VMEM((MOE_TILE, D_MODEL), F32),
            pltpu.SemaphoreType.DMA(()),
            pltpu.SemaphoreType.DMA(()),
        ],
    )
    return pl.pallas_call(
        functools.partial(_dispatch_kernel, tb=tb),
        grid_spec=grid_spec,
        out_shape=jax.ShapeDtypeStruct((rows, D_MODEL), F32),
        compiler_params=pltpu.CompilerParams(
            dimension_semantics=("arbitrary",), vmem_limit_bytes=VMEM_LIMIT),
        name="moe_dispatch",
    )(dest1, dest2, last_tile, x1)


def _experts_kernel(te_ref, nt_ref, xs_ref, wg_ref, wu_ref, wd_ref, ys_ref, wgu_s, wd_s):
    t = pl.program_id(0)
    changed = (t == 0) | (te_ref[t] != te_ref[jnp.maximum(t - 1, 0)])

    @pl.when(changed)
    def _():
        wgu_s[:, 0:D_EXPERT] = wg_ref[...].astype(BF16)
        wgu_s[:, D_EXPERT:2 * D_EXPERT] = wu_ref[...].astype(BF16)
        wd_s[...] = wd_ref[...].astype(BF16)

    @pl.when(t < nt_ref[0])
    def _():
        gu = _dot(xs_ref[...].astype(BF16), wgu_s[...])
        gate = gu[:, 0:D_EXPERT]
        up = gu[:, D_EXPERT:2 * D_EXPERT]
        hid = (gate * (1.0 / (1.0 + jnp.exp(-gate)))) * up
        ys_ref[...] = _dot(hid.astype(BF16), wd_s[...])

    @pl.when(t >= nt_ref[0])
    def _():
        ys_ref[...] = jnp.zeros_like(ys_ref)


def _moe_experts(xs, tile_expert, n_tiles_used, w_gate, w_up, w_down, layer):
    rows = xs.shape[0]
    first = layer * N_EXPERTS
    grid_spec = pltpu.PrefetchScalarGridSpec(
        num_scalar_prefetch=2,
        grid=(rows // MOE_TILE,),
        in_specs=[
            pl.BlockSpec((MOE_TILE, D_MODEL), lambda t, te, nt: (jnp.minimum(t, nt[0] - 1), 0)),
            pl.BlockSpec((None, D_MODEL, D_EXPERT), lambda t, te, nt: (first + te[t], 0, 0)),
            pl.BlockSpec((None, D_MODEL, D_EXPERT), lambda t, te, nt: (first + te[t], 0, 0)),
            pl.BlockSpec((None, D_EXPERT, D_MODEL), lambda t, te, nt: (first + te[t], 0, 0)),
        ],
        out_specs=pl.BlockSpec((MOE_TILE, D_MODEL), lambda t, te, nt: (t, 0)),
        scratch_shapes=[
            pltpu.VMEM((D_MODEL, 2 * D_EXPERT), BF16),
            pltpu.VMEM((D_EXPERT, D_MODEL), BF16),
        ],
    )
    return pl.pallas_call(
        _experts_kernel,
        grid_spec=grid_spec,
        out_shape=jax.ShapeDtypeStruct((rows, D_MODEL), F32),
        compiler_params=pltpu.CompilerParams(
            dimension_semantics=("arbitrary",), vmem_limit_bytes=VMEM_LIMIT),
        name="moe_experts",
    )(tile_expert, n_tiles_used, xs, w_gate, w_up, w_down)


def _combine_kernel(d1_ref, d2_ref, x1_ref, info_ref, ys_hbm, g_ref, b_ref, out_ref, y_ref, sems, *, tb):
    i = pl.program_id(0)
    last_step = pl.num_programs(0) - 1

    def start_gather(tile, slot):
        base = tile * tb
        for r in range(tb):
            _row_copy(ys_hbm, d1_ref[base + r], y_ref.at[slot, 0], r, sems.at[slot]).start(priority=0)
            _row_copy(ys_hbm, d2_ref[base + r], y_ref.at[slot, 1], r, sems.at[slot]).start(priority=1)

    def wait_gather(slot):
        for k in range(TOP_K):
            pltpu.make_async_copy(ys_hbm.at[pl.ds(0, tb), :], y_ref.at[slot, k], sems.at[slot]).wait()

    def finish(slot):
        rows = pl.ds(slot * tb, tb)
        info = info_ref[rows, :]
        lane = lax.broadcasted_iota(jnp.int32, info.shape, 1)
        w1 = jnp.sum(jnp.where(lane == ROUTE_W1, info, 0.0), axis=1, keepdims=True)
        w2 = jnp.sum(jnp.where(lane == ROUTE_W2, info, 0.0), axis=1, keepdims=True)
        ffn = w1 * y_ref[slot, 0] + w2 * y_ref[slot, 1]
        out_ref[rows, :] = _layer_norm(ALPHA * x1_ref[rows, :] + ffn, g_ref[...], b_ref[...])

    pl.when(i == 0)(lambda: start_gather(0, 0))
    wait_gather(0)
    start_gather(2 * i + 1, 1)
    finish(0)
    wait_gather(1)
    start_gather(jnp.minimum(2 * i + 2, 2 * last_step), 0)
    finish(1)
    pl.when(i == last_step)(lambda: wait_gather(0))


def _moe_combine(x1, info, ys, dest1, dest2, ln_g, ln_b, tb=256):
    n = x1.shape[0]
    grid_spec = pltpu.PrefetchScalarGridSpec(
        num_scalar_prefetch=2,
        grid=(n // (2 * tb),),
        in_specs=[
            pl.BlockSpec((2 * tb, D_MODEL), lambda i, d1, d2: (i, 0)),
            pl.BlockSpec((2 * tb, ROUTER_LANES), lambda i, d1, d2: (i, 0)),
            pl.BlockSpec(memory_space=pl.ANY),
            pl.BlockSpec((1, D_MODEL), lambda i, d1, d2: (0, 0)),
            pl.BlockSpec((1, D_MODEL), lambda i, d1, d2: (0, 0)),
        ],
        out_specs=pl.BlockSpec((2 * tb, D_MODEL), lambda i, d1, d2: (i, 0)),
        scratch_shapes=[
            pltpu.VMEM((2, TOP_K, tb, D_MODEL), F32),
            pltpu.SemaphoreType.DMA((2,)),
        ],
    )
    return pl.pallas_call(
        functools.partial(_combine_kernel, tb=tb),
        grid_spec=grid_spec,
        out_shape=jax.ShapeDtypeStruct((n, D_MODEL), F32),
        compiler_params=pltpu.CompilerParams(
            dimension_semantics=("arbitrary",), vmem_limit_bytes=VMEM_LIMIT),
        name="moe_combine",
    )(dest1, dest2, x1, info, ys, ln_g, ln_b)


def _moe(x1, info, pick, w_gate, w_up, w_down, layer, ln_g, ln_b):
    n = x1.shape[0]
    dest, counts = _moe_rank(pick, info)
    dest1 = dest[:, 0]
    dest2 = dest[:, 1]
    tiles = jnp.ceil(counts[0, N_GROUPS:N_GROUPS + N_EXPERTS] * (1.0 / MOE_TILE)).astype(jnp.int32)
    upto = np.arange(N_EXPERTS)[None, :] <= np.arange(N_EXPERTS)[:, None]
    last_tile = jnp.sum(jnp.where(upto, tiles[None, :], 0), axis=1)
    tile_ids = jnp.arange(_moe_rows(n) // MOE_TILE, dtype=jnp.int32)
    tile_expert = jnp.minimum(jnp.sum(tile_ids[:, None] >= last_tile[None, :], axis=1), N_EXPERTS - 1)
    xs = _moe_dispatch(x1, dest1, dest2, last_tile)
    ys = _moe_experts(xs, tile_expert.astype(jnp.int32), last_tile[N_EXPERTS - 1:], w_gate, w_up, w_down, layer)
    return _moe_combine(x1, info, ys, dest1, dest2, ln_g, ln_b)


def _layer(x2d, batch, w_in, b_gate, rel_bias, w_br_sb, w_br_ca, w_out, ln1_g, ln1_b,
           w_group, b_group, w_erouter, b_erouter, w_gate, w_up, w_down, layer, ln2_g, ln2_b):
    n = x2d.shape[0]
    seq = n // batch
    scale = HEAD_DIM ** -0.5
    qscale = np.ones((1, D_QKV), np.float32)
    qscale[:, 0:W_SB] = scale * np.log2(np.e)
    qscale[:, 3 * W_SB:3 * W_SB + W_CA] = scale * np.log2(np.e)
    qkv, gates = _in_proj(x2d, w_in.astype(BF16), jnp.asarray(qscale), b_gate.reshape(1, 2 * D_MODEL))
    y_sb = _sb_attn(qkv, batch)
    y_ca = _ca_attn(qkv, batch, _ca_bias_line(rel_bias))

    w_router = jnp.concatenate(
        [w_group, w_erouter.transpose(1, 0, 2).reshape(D_MODEL, N_EXPERTS)], axis=1)
    w_router = jnp.pad(w_router, ((0, 0), (0, ROUTER_LANES - N_GROUPS - N_EXPERTS)))
    b_router = jnp.pad(jnp.concatenate([b_group, b_erouter.reshape(N_EXPERTS)]),
                       (0, ROUTER_LANES - N_GROUPS - N_EXPERTS)).reshape(1, ROUTER_LANES)
    x1, info, pick = _post_attn(y_sb, y_ca, gates, x2d, w_br_sb.astype(BF16), w_br_ca.astype(BF16),
                        w_out.astype(BF16), w_router, b_router,
                        ln1_g.reshape(1, D_MODEL), ln1_b.reshape(1, D_MODEL))

    return _moe(x1, info, pick, w_gate, w_up, w_down, layer,
                ln2_g.reshape(1, D_MODEL), ln2_b.reshape(1, D_MODEL))


def kernel(x, w_in, b_gate, rel_bias, w_br_sb, w_br_ca, w_out, ln1_g, ln1_b, w_group, b_group,
           w_erouter, b_erouter, w_gate, w_up, w_down, ln2_g, ln2_b):
    batch, seq, d = x.shape
    h = x.reshape(batch * seq, d)
    w_gate = w_gate.reshape(DEPTH * N_EXPERTS, D_MODEL, D_EXPERT)
    w_up = w_up.reshape(DEPTH * N_EXPERTS, D_MODEL, D_EXPERT)
    w_down = w_down.reshape(DEPTH * N_EXPERTS, D_EXPERT, D_MODEL)
    for l in range(DEPTH):
        h = _layer(h, batch, w_in[l], b_gate[l], rel_bias[l], w_br_sb[l], w_br_ca[l], w_out[l],
                   ln1_g[l], ln1_b[l], w_group[l], b_group[l], w_erouter[l], b_erouter[l],
                   w_gate, w_up, w_down, l, ln2_g[l], ln2_b[l])
    return h.reshape(batch, seq, d)
```

```python
import functools

import jax
import jax.numpy as jnp
import numpy as np
from jax import lax
from jax.experimental import pallas as pl
from jax.experimental.pallas import tpu as pltpu

D_MODEL = 1024
DEPTH = 2
CHUNK = 64
HEAD_DIM = 64
H_SB = 8
H_CA = 8
W_SB = H_SB * HEAD_DIM
W_CA = H_CA * HEAD_DIM
N_PAST_CHUNKS = 8
REL_CLIP = 128
N_GROUPS = 4
EXPERTS_PER_GROUP = 8
N_EXPERTS = N_GROUPS * EXPERTS_PER_GROUP
D_EXPERT = 256
ALPHA = (2.0 * DEPTH) ** 0.25
LN_EPS = 1e-5
D_QKV = 3 * W_SB + 3 * W_CA
D_IN = D_QKV + 2 * D_MODEL
NEG_INF = -1e30

LANES = 128
HEADS_PER_TILE = LANES // HEAD_DIM
ROUTER_LANES = LANES
VMEM_LIMIT = 56 * 1024 * 1024

BF16 = jnp.bfloat16
F32 = jnp.float32

_NT = (((1,), (1,)), ((), ()))


def _dot(a, b):
    return jnp.dot(a, b, preferred_element_type=F32)


def _layer_norm(h, g, b):
    mu = jnp.mean(h, axis=-1, keepdims=True)
    hc = h - mu
    var = jnp.mean(hc * hc, axis=-1, keepdims=True)
    return hc * lax.rsqrt(var + LN_EPS) * g + b


def _split_bf16(a):
    hi = a.astype(BF16)
    lo = (a - hi.astype(F32)).astype(BF16)
    return hi, lo


def _in_proj_kernel(x_ref, w_ref, scale_ref, bg_ref, qkv_ref, gate_ref):
    xb = x_ref[...].astype(BF16)
    for c in range(D_QKV // D_MODEL):
        cols = slice(c * D_MODEL, (c + 1) * D_MODEL)
        acc = _dot(xb, w_ref[:, cols])
        qkv_ref[:, cols] = (acc * scale_ref[:, cols]).astype(BF16)
    for c in range(2):
        cols = slice(c * D_MODEL, (c + 1) * D_MODEL)
        wcols = slice(D_QKV + c * D_MODEL, D_QKV + (c + 1) * D_MODEL)
        logit = _dot(xb, w_ref[:, wcols]) + bg_ref[:, cols]
        gate_ref[:, cols] = 1.0 / (1.0 + jnp.exp(-logit))


def _in_proj(x2d, w_in_bf16, qscale, b_gate_row, tm=512):
    n = x2d.shape[0]
    return pl.pallas_call(
        _in_proj_kernel,
        grid=(n // tm,),
        in_specs=[
            pl.BlockSpec((tm, D_MODEL), lambda i: (i, 0)),
            pl.BlockSpec((D_MODEL, D_IN), lambda i: (0, 0)),
            pl.BlockSpec((1, D_QKV), lambda i: (0, 0)),
            pl.BlockSpec((1, 2 * D_MODEL), lambda i: (0, 0)),
        ],
        out_specs=[
            pl.BlockSpec((tm, D_QKV), lambda i: (i, 0)),
            pl.BlockSpec((tm, 2 * D_MODEL), lambda i: (i, 0)),
        ],
        out_shape=[
            jax.ShapeDtypeStruct((n, D_QKV), BF16),
            jax.ShapeDtypeStruct((n, 2 * D_MODEL), F32),
        ],
        compiler_params=pltpu.CompilerParams(
            dimension_semantics=("arbitrary",), vmem_limit_bytes=VMEM_LIMIT),
        name="in_proj",
    )(x2d, w_in_bf16, qscale, b_gate_row)


SB_DEAD_BITS = 160.0
SB_STEP_BLOCKS = 4


def _sb_kernel(q_ref, k_ref, v_ref, u_ref, o_ref, z_ref, arg_ref, rs_ref, acc_ref, car_ref, *, tq):
    step = pl.program_id(2)
    lane_q = lax.broadcasted_iota(jnp.int32, (tq, LANES), 1)

    def split_heads(x):
        zero = jnp.zeros_like(x)
        return jnp.where(lane_q < HEAD_DIM, x, zero), jnp.where(lane_q < HEAD_DIM, zero, x)

    blocks = [step * SB_STEP_BLOCKS + j for j in range(SB_STEP_BLOCKS)]
    q_heads = [split_heads(q_ref[pl.ds(j * tq, tq), :]) for j in range(SB_STEP_BLOCKS)]

    def key_rows(i, n):
        return pl.ds(pl.multiple_of(jnp.maximum(i - n, 0) * tq, tq), tq)

    def block_pairs(subs, n, diagonal):
        for j in subs:
            for b in range(2):
                k = k_ref[key_rows(blocks[j], n + b), :]
                for h in range(HEADS_PER_TILE):
                    z = lax.dot_general(q_heads[j][h], k, _NT, preferred_element_type=F32)
                    if diagonal and b == 0:
                        row = lax.broadcasted_iota(jnp.int32, (tq, tq), 0)
                        col = lax.broadcasted_iota(jnp.int32, (tq, tq), 1)
                        z = jnp.where(col < row, z, NEG_INF)
                    z_ref[j, b, h] = z

        def terms(j):
            for b in range(2):
                for h in range(HEADS_PER_TILE):
                    z = z_ref[j, b, h]
                    sp = jnp.maximum(z, 0.0) + jnp.log2(1.0 + jnp.exp2(-jnp.abs(z)))
                    sums = _dot(sp.astype(BF16), u_ref[...])
                    arg_ref[j, b, h] = (z - sp) - sums[:, 0:tq]
                    rs_ref[j, b, h] = sums[:, tq:tq + LANES]

        def apply(j):
            carries = [car_ref[j]]
            for b in range(2):
                carries.append(carries[b] + rs_ref[j, b])
            car_ref[j] = carries[2]
            v_parts = []
            for b in range(2):
                v = v_ref[key_rows(blocks[j], n + b), :]
                if b == 1:
                    v = jnp.where(n + b <= blocks[j], v, jnp.zeros_like(v))
                v_parts += split_heads(v)
            ws = []
            for b in range(2):
                for h in range(HEADS_PER_TILE):
                    carry = jnp.concatenate([carries[b][h]] * (tq // LANES), axis=1)
                    ws.append(jnp.exp2(arg_ref[j, b, h] - carry).astype(BF16))
            acc_ref[j] += _dot(jnp.concatenate(ws, axis=1), jnp.concatenate(v_parts, axis=0))
            return jnp.min(carries[2])

        subs = list(subs)
        least = []
        terms(subs[0])
        for prev, j in zip(subs, subs[1:]):
            terms(j)
            least.append(apply(prev))
        least.append(apply(subs[-1]))
        return least

    acc_ref[...] = jnp.zeros_like(acc_ref)
    car_ref[...] = jnp.zeros_like(car_ref)
    first = block_pairs(range(SB_STEP_BLOCKS), 0, True)

    for j in range(SB_STEP_BLOCKS):
        def more(state, j=j):
            n, least = state
            return (n <= blocks[j]) & (least < SB_DEAD_BITS)

        def body(state, j=j):
            n, _ = state
            return n + 2, block_pairs([j], n, False)[0]

        lax.while_loop(more, body, (jnp.int32(2), first[j]))
        o_ref[pl.ds(j * tq, tq), :] = acc_ref[j].astype(o_ref.dtype)


def _sb_attn(qkv, b, tq=256):
    s = qkv.shape[0] // b
    n_tiles = W_SB // LANES
    step_rows = SB_STEP_BLOCKS * tq
    steps = s // step_rows
    u = jnp.asarray(np.concatenate([np.arange(tq)[:, None] > np.arange(tq)[None, :],
                                    np.ones((tq, LANES), bool)], axis=1), dtype=BF16)
    return pl.pallas_call(
        functools.partial(_sb_kernel, tq=tq),
        grid=(b, n_tiles, steps),
        in_specs=[
            pl.BlockSpec((step_rows, LANES), lambda bi, hp, i: (bi * steps + i, hp)),
            pl.BlockSpec((s, LANES), lambda bi, hp, i: (bi, n_tiles + hp)),
            pl.BlockSpec((s, LANES), lambda bi, hp, i: (bi, 2 * n_tiles + hp)),
            pl.BlockSpec((tq, tq + LANES), lambda bi, hp, i: (0, 0)),
        ],
        out_specs=pl.BlockSpec((step_rows, LANES), lambda bi, hp, i: (bi * steps + i, hp)),
        out_shape=jax.ShapeDtypeStruct((b * s, W_SB), BF16),
        scratch_shapes=[
            pltpu.VMEM((SB_STEP_BLOCKS, 2, HEADS_PER_TILE, tq, tq), F32),
            pltpu.VMEM((SB_STEP_BLOCKS, 2, HEADS_PER_TILE, tq, tq), F32),
            pltpu.VMEM((SB_STEP_BLOCKS, 2, HEADS_PER_TILE, tq, LANES), F32),
            pltpu.VMEM((SB_STEP_BLOCKS, tq, LANES), F32),
            pltpu.VMEM((SB_STEP_BLOCKS, HEADS_PER_TILE, tq, LANES), F32),
        ],
        compiler_params=pltpu.CompilerParams(
            dimension_semantics=("arbitrary", "arbitrary", "arbitrary"), vmem_limit_bytes=VMEM_LIMIT),
        name="sb_attn",
    )(qkv, qkv, qkv, u)


CA_GROUP = 4
CA_STEP_GROUPS = 8
CA_TQ = CA_GROUP * CHUNK
CA_BAND = (CA_GROUP + N_PAST_CHUNKS) * CHUNK
CA_PAD = N_PAST_CHUNKS * CHUNK


def _ca_kernel(q_ref, k_ref, v_ref, line_ref, ones_ref, o_ref, kp_ref, vp_ref, bias_ref):
    c = pl.program_id(2)
    s = k_ref.shape[0]

    @pl.when(c == 0)
    def _():
        kp_ref[0:CA_PAD, :] = jnp.zeros((CA_PAD, LANES), BF16)
        vp_ref[0:CA_PAD, :] = jnp.zeros((CA_PAD, LANES), BF16)
        kp_ref[CA_PAD:CA_PAD + s, :] = k_ref[...]
        vp_ref[CA_PAD:CA_PAD + s, :] = v_ref[...]
        r = lax.broadcasted_iota(jnp.int32, (CA_TQ, CA_BAND), 0)
        p = lax.broadcasted_iota(jnp.int32, (CA_TQ, CA_BAND), 1)
        shift = CHUNK.bit_length() - 1
        qc = lax.shift_right_logical(r, shift)
        kc = lax.shift_right_logical(p, shift)
        in_band = (kc >= qc) & (kc <= qc + N_PAST_CHUNKS)
        for h in range(HEADS_PER_TILE):
            rows = jnp.broadcast_to(line_ref[h], (CA_TQ, CA_LINE))
            skew = pltpu.roll(rows, 1, axis=1, stride=1, stride_axis=0)
            bias_ref[h] = jnp.where(in_band, skew[:, CA_TQ:CA_TQ + CA_BAND], NEG_INF)

    lane_q = lax.broadcasted_iota(jnp.int32, (CA_TQ, LANES), 1)
    lane_v = lax.broadcasted_iota(jnp.int32, (CA_BAND, LANES), 1)

    def band_start(g):
        return pl.multiple_of((c * CA_STEP_GROUPS + g) * CA_TQ, CA_TQ)

    def scores(g):
        kb = kp_ref[pl.ds(band_start(g), CA_BAND), :]
        q = q_ref[pl.ds(g * CA_TQ, CA_TQ), :]
        zero_q = jnp.zeros_like(q)
        q_heads = (jnp.where(lane_q < HEAD_DIM, q, zero_q), jnp.where(lane_q < HEAD_DIM, zero_q, q))
        return [lax.dot_general(q_heads[h], kb, _NT, preferred_element_type=F32) for h in range(HEADS_PER_TILE)]

    def attend(g, qk, masked):
        group = c * CA_STEP_GROUPS + g
        vb = vp_ref[pl.ds(band_start(g), CA_BAND), :]
        zero_v = jnp.zeros_like(vb)
        v_heads = jnp.concatenate(
            [jnp.where(lane_v < HEAD_DIM, vb, zero_v), jnp.where(lane_v < HEAD_DIM, zero_v, vb)], axis=0)
        v_and_ones = jnp.concatenate([v_heads, ones_ref[...]], axis=1)
        es = []
        for h in range(HEADS_PER_TILE):
            sc = qk[h] + bias_ref[h]
            if masked:
                pos = lax.broadcasted_iota(jnp.int32, (CA_TQ, CA_BAND), 1)
                sc = jnp.where(pos >= CA_PAD - group * CA_TQ, sc, NEG_INF)
            m = jnp.max(sc, axis=1, keepdims=True)
            es.append(jnp.exp2(sc - m).astype(BF16))
        both = _dot(jnp.concatenate(es, axis=1), v_and_ones)
        o_ref[pl.ds(g * CA_TQ, CA_TQ), :] = (both[:, 0:LANES] / both[:, LANES:2 * LANES]).astype(o_ref.dtype)

    def step(masked):
        qk = scores(0)
        for g in range(CA_STEP_GROUPS):
            qk_next = scores(g + 1) if g + 1 < CA_STEP_GROUPS else None
            attend(g, qk, masked)
            qk = qk_next

    assert CA_PAD // CA_TQ <= CA_STEP_GROUPS
    pl.when(c == 0)(lambda: step(True))
    pl.when(c > 0)(lambda: step(False))


CA_LINE = CA_TQ + CA_BAND


def _ca_bias_line(rel_bias):
    h = rel_bias.shape[0]
    n_far = CA_PAD + CA_TQ - 1 - REL_CLIP
    n_neg = CA_BAND - 1 - CA_PAD - REL_CLIP
    rb = rel_bias.astype(F32) * np.float32(np.log2(np.e))
    line = jnp.concatenate([jnp.broadcast_to(rb[:, 2 * REL_CLIP:], (h, n_far)), rb[:, ::-1],
                            jnp.broadcast_to(rb[:, :1], (h, n_neg + 1))], axis=1)
    assert line.shape[1] == CA_LINE
    return line.reshape(h, 1, CA_LINE)


def _ca_attn(qkv, b, bias_line):
    s = qkv.shape[0] // b
    n_tiles = W_CA // LANES
    base = 3 * W_SB // LANES
    step_rows = CA_STEP_GROUPS * CA_TQ
    steps = s // step_rows
    head_of_row = np.arange(HEADS_PER_TILE * CA_BAND)[:, None] // CA_BAND
    head_of_lane = np.arange(LANES)[None, :] // HEAD_DIM
    ones = jnp.asarray(head_of_row == head_of_lane, dtype=BF16)
    return pl.pallas_call(
        _ca_kernel,
        grid=(b, n_tiles, steps),
        in_specs=[
            pl.BlockSpec((step_rows, LANES), lambda bi, hp, c: (bi * steps + c, base + hp)),
            pl.BlockSpec((s, LANES), lambda bi, hp, c: (bi, base + n_tiles + hp)),
            pl.BlockSpec((s, LANES), lambda bi, hp, c: (bi, base + 2 * n_tiles + hp)),
            pl.BlockSpec((HEADS_PER_TILE, 1, CA_LINE), lambda bi, hp, c: (hp, 0, 0)),
            pl.BlockSpec((HEADS_PER_TILE * CA_BAND, LANES), lambda bi, hp, c: (0, 0)),
        ],
        out_specs=pl.BlockSpec((step_rows, LANES), lambda bi, hp, c: (bi * steps + c, hp)),
        out_shape=jax.ShapeDtypeStruct((b * s, W_CA), BF16),
        scratch_shapes=[
            pltpu.VMEM((CA_PAD + s, LANES), BF16),
            pltpu.VMEM((CA_PAD + s, LANES), BF16),
            pltpu.VMEM((HEADS_PER_TILE, CA_TQ, CA_BAND), F32),
        ],
        compiler_params=pltpu.CompilerParams(
            dimension_semantics=("arbitrary", "arbitrary", "arbitrary"), vmem_limit_bytes=VMEM_LIMIT),
        name="ca_attn",
    )(qkv, qkv, qkv, bias_line, ones)


ROUTE_E1, ROUTE_E2, ROUTE_W1, ROUTE_W2 = 0, 1, 2, 3


def _route(lg):
    lane = lax.broadcasted_iota(jnp.int32, lg.shape, 1)
    big = jnp.int32(ROUTER_LANES)
    is_group = lane < N_GROUPS
    g_max = jnp.max(jnp.where(is_group, lg, -jnp.inf), axis=1, keepdims=True)
    g_idx = jnp.min(jnp.where(is_group & (lg == g_max), lane, big), axis=1, keepdims=True)
    g_den = jnp.sum(jnp.where(is_group, jnp.exp(lg - g_max), 0.0), axis=1, keepdims=True)
    g_val = 1.0 / g_den
    lo = N_GROUPS + EXPERTS_PER_GROUP * g_idx
    in_group = (lane >= lo) & (lane < lo + EXPERTS_PER_GROUP)
    v1 = jnp.max(jnp.where(in_group, lg, -jnp.inf), axis=1, keepdims=True)
    i1 = jnp.min(jnp.where(in_group & (lg == v1), lane, big), axis=1, keepdims=True)
    rest = in_group & (lane != i1)
    v2 = jnp.max(jnp.where(rest, lg, -jnp.inf), axis=1, keepdims=True)
    i2 = jnp.min(jnp.where(rest & (lg == v2), lane, big), axis=1, keepdims=True)
    e2 = jnp.exp(v2 - v1)
    w1 = g_val / (1.0 + e2)
    w2 = g_val * e2 / (1.0 + e2)
    picked = ((lane == i1) | (lane == i2)).astype(BF16)
    info = (jnp.where(lane == ROUTE_E1, (i1 - N_GROUPS).astype(F32), 0.0)
            + jnp.where(lane == ROUTE_E2, (i2 - N_GROUPS).astype(F32), 0.0)
            + jnp.where(lane == ROUTE_W1, w1, 0.0) + jnp.where(lane == ROUTE_W2, w2, 0.0))
    return info, picked


def _post_kernel(ysb_ref, yca_ref, gate_ref, x_ref, wsb_ref, wca_ref, wout_ref, wr_ref, br_ref,
                 g_ref, b_ref, x1_ref, info_ref, pick_ref, cnt_ref):
    @pl.when(pl.program_id(0) == 0)
    def _():
        cnt_ref[...] = jnp.zeros_like(cnt_ref)

    w_hi, w_lo = _split_bf16(wr_ref[...])
    w_split = jnp.concatenate([w_hi, w_lo], axis=1)
    tm = x_ref.shape[0]
    subs = [pl.ds(s * POST_SUB, POST_SUB) for s in range(tm // POST_SUB)]
    for rows in subs:
        a = _dot(ysb_ref[rows, :], wsb_ref[...])
        c = _dot(yca_ref[rows, :], wca_ref[...])
        mix = gate_ref[rows, 0:D_MODEL] * a + gate_ref[rows, D_MODEL:2 * D_MODEL] * c
        x1_ref[rows, :] = ALPHA * x_ref[rows, :] + _dot(mix.astype(BF16), wout_ref[...])
    for rows in subs:
        x1 = _layer_norm(x1_ref[rows, :], g_ref[...], b_ref[...])
        x1_ref[rows, :] = x1
        x_hi, x_lo = _split_bf16(x1)
        parts = _dot(jnp.concatenate([x_hi, x_lo], axis=0), w_split)
        lg = (parts[0:POST_SUB, 0:ROUTER_LANES] + parts[0:POST_SUB, ROUTER_LANES:]
              + parts[POST_SUB:, 0:ROUTER_LANES] + parts[POST_SUB:, ROUTER_LANES:]) + br_ref[...]
        info, pick = _route(lg)
        info_ref[rows, :] = info
        pick_ref[rows, :] = pick
        cnt_ref[...] += jnp.sum(pick.astype(F32), axis=0, keepdims=True)


POST_SUB = 256


def _post_attn(y_sb, y_ca, gates, x2d, w_br_sb, w_br_ca, w_out, w_router, b_router, ln_g, ln_b, tm=1024):
    n = x2d.shape[0]
    row = lambda i: (i, 0)
    fixed = lambda i: (0, 0)
    return pl.pallas_call(
        _post_kernel,
        grid=(n // tm,),
        in_specs=[
            pl.BlockSpec((tm, W_SB), row),
            pl.BlockSpec((tm, W_CA), row),
            pl.BlockSpec((tm, 2 * D_MODEL), row),
            pl.BlockSpec((tm, D_MODEL), row),
            pl.BlockSpec((W_SB, D_MODEL), fixed),
            pl.BlockSpec((W_CA, D_MODEL), fixed),
            pl.BlockSpec((D_MODEL, D_MODEL), fixed),
            pl.BlockSpec((D_MODEL, ROUTER_LANES), fixed),
            pl.BlockSpec((1, ROUTER_LANES), fixed),
            pl.BlockSpec((1, D_MODEL), fixed),
            pl.BlockSpec((1, D_MODEL), fixed),
        ],
        out_specs=[
            pl.BlockSpec((tm, D_MODEL), row),
            pl.BlockSpec((tm, ROUTER_LANES), row),
            pl.BlockSpec((tm, ROUTER_LANES), row),
            pl.BlockSpec((8, ROUTER_LANES), fixed),
        ],
        out_shape=[
            jax.ShapeDtypeStruct((n, D_MODEL), F32),
            jax.ShapeDtypeStruct((n, ROUTER_LANES), F32),
            jax.ShapeDtypeStruct((n, ROUTER_LANES), BF16),
            jax.ShapeDtypeStruct((8, ROUTER_LANES), F32),
        ],
        compiler_params=pltpu.CompilerParams(
            dimension_semantics=("arbitrary",), vmem_limit_bytes=VMEM_LIMIT),
        name="post_attn",
    )(y_sb, y_ca, gates, x2d, w_br_sb, w_br_ca, w_out, w_router, b_router, ln_g, ln_b)


MOE_TILE = 512
TOP_K = 2


def _moe_rows(n):
    return n * TOP_K + N_EXPERTS * MOE_TILE


def _rank_kernel(pick_ref, info_ref, tri_ref, cnt_ref, d1_ref, d2_ref, run_ref, off_ref, *, tb):
    i = pl.program_id(0)
    pick = pick_ref[...]

    @pl.when(i == 0)
    def _():
        cnt = cnt_ref[...]
        padded = jnp.ceil(cnt * (1.0 / MOE_TILE)) * MOE_TILE
        lane = lax.broadcasted_iota(jnp.int32, padded.shape, 1)
        scan = padded
        step = 1
        while step < ROUTER_LANES:
            scan = scan + jnp.where(lane >= step, pltpu.roll(scan, step, axis=1), 0.0)
            step *= 2
        off_ref[...] = scan - padded
        run_ref[...] = jnp.zeros_like(run_ref)

    seen = run_ref[0:1, :]
    earlier = _dot(tri_ref[...], pick)
    row_of = earlier + seen + off_ref[0:1, :]
    info = info_ref[...]
    lane = lax.broadcasted_iota(jnp.int32, info.shape, 1)
    lane_f = lane.astype(F32)
    e1 = jnp.sum(jnp.where(lane == ROUTE_E1, info, 0.0), axis=1, keepdims=True)
    e2 = jnp.sum(jnp.where(lane == ROUTE_E2, info, 0.0), axis=1, keepdims=True)
    d1 = jnp.sum(jnp.where(lane_f == e1 + N_GROUPS, row_of, 0.0), axis=1, keepdims=True)
    d2 = jnp.sum(jnp.where(lane_f == e2 + N_GROUPS, row_of, 0.0), axis=1, keepdims=True)
    eye = (lax.broadcasted_iota(jnp.int32, (LANES, LANES), 0) == lax.broadcasted_iota(jnp.int32, (LANES, LANES), 1))
    for blk in range(tb // LANES):
        for d, d_ref in ((d1, d1_ref), (d2, d2_ref)):
            column = d[blk * LANES:(blk + 1) * LANES]
            d_ref[blk:blk + 1, :] = jnp.sum(jnp.where(eye, column, 0.0), axis=0, keepdims=True).astype(jnp.int32)
    run_ref[...] += jnp.sum(pick.astype(F32), axis=0, keepdims=True)


def _moe_rank(pick, info, counts, tb=1024):
    n = pick.shape[0]
    tri = jnp.asarray(np.arange(tb)[None, :] < np.arange(tb)[:, None], dtype=BF16)
    dest_spec = pl.BlockSpec((tb // LANES, LANES), lambda i: (i, 0))
    dest_shape = jax.ShapeDtypeStruct((n // LANES, LANES), jnp.int32)
    return pl.pallas_call(
        functools.partial(_rank_kernel, tb=tb),
        grid=(n // tb,),
        in_specs=[
            pl.BlockSpec((tb, ROUTER_LANES), lambda i: (i, 0)),
            pl.BlockSpec((tb, ROUTER_LANES), lambda i: (i, 0)),
            pl.BlockSpec((tb, tb), lambda i: (0, 0)),
            pl.BlockSpec((8, ROUTER_LANES), lambda i: (0, 0)),
        ],
        out_specs=[dest_spec, dest_spec],
        out_shape=[dest_shape, dest_shape],
        scratch_shapes=[
            pltpu.VMEM((8, ROUTER_LANES), F32),
            pltpu.VMEM((8, ROUTER_LANES), F32),
        ],
        compiler_params=pltpu.CompilerParams(
            dimension_semantics=("arbitrary",), vmem_limit_bytes=VMEM_LIMIT),
        name="moe_rank",
    )(pick, info, tri, counts)


def _row_copy(src_ref, src_row, dst_ref, dst_row, sem):
    return pltpu.make_async_copy(src_ref.at[pl.ds(src_row, 1), :], dst_ref.at[pl.ds(dst_row, 1), :], sem)


def _dispatch_kernel(d1_ref, d2_ref, last_ref, x_ref, xs_hbm, zero_ref, sem, zsem, *, tb):
    base = pl.program_id(0) * tb

    @pl.when(pl.program_id(0) == 0)
    def _():
        zero_ref[...] = jnp.zeros_like(zero_ref)

        def last_tile_copy(e):
            first_row = pl.multiple_of((last_ref[e] - 1) * MOE_TILE, MOE_TILE)
            return pltpu.make_async_copy(zero_ref, xs_hbm.at[pl.ds(first_row, MOE_TILE), :], zsem)

        def unused_tile_copy(t):
            return pltpu.make_async_copy(zero_ref, xs_hbm.at[pl.ds(t * MOE_TILE, MOE_TILE), :], zsem)

        n_tiles = xs_hbm.shape[0] // MOE_TILE
        min_used = n_tiles - N_EXPERTS
        for wait in (False, True):
            for e in range(N_EXPERTS):
                owns_tiles = last_ref[e] > (last_ref[e - 1] if e else 0)
                pl.when(owns_tiles)(
                    lambda e=e, wait=wait: last_tile_copy(e).wait() if wait else last_tile_copy(e).start())
            for t in range(min_used, n_tiles):
                pl.when(t >= last_ref[N_EXPERTS - 1])(
                    lambda t=t, wait=wait: unused_tile_copy(t).wait() if wait else unused_tile_copy(t).start())

    def issue(r, c):
        t = base + r
        hi, lo = lax.shift_right_logical(t, LANES.bit_length() - 1), jnp.bitwise_and(t, LANES - 1)
        _row_copy(x_ref, r, xs_hbm, d1_ref[hi, lo], sem).start(priority=0)
        _row_copy(x_ref, r, xs_hbm, d2_ref[hi, lo], sem).start(priority=1)
        return c

    lax.fori_loop(0, tb, issue, 0, unroll=8)
    for _ in range(TOP_K):
        pltpu.make_async_copy(x_ref, xs_hbm.at[pl.ds(0, tb), :], sem).wait()


def _moe_dispatch(x1, dest1, dest2, last_tile, tb=512):
    n = x1.shape[0]
    rows = _moe_rows(n)
    grid_spec = pltpu.PrefetchScalarGridSpec(
        num_scalar_prefetch=3,
        grid=(n // tb,),
        in_specs=[pl.BlockSpec((tb, D_MODEL), lambda i, d1, d2, last: (i, 0))],
        out_specs=pl.BlockSpec(memory_space=pl.ANY),
        scratch_shapes=[
            pltpu.VMEM((MOE_TILE, D_MODEL), F32),
            pltpu.SemaphoreType.DMA(()),
            pltpu.SemaphoreType.DMA(()),
        ],
    )
    return pl.pallas_call(
        functools.partial(_dispatch_kernel, tb=tb),
        grid_spec=grid_spec,
        out_shape=jax.ShapeDtypeStruct((rows, D_MODEL), F32),
        compiler_params=pltpu.CompilerParams(
            dimension_semantics=("arbitrary",), vmem_limit_bytes=VMEM_LIMIT),
        name="moe_dispatch",
    )(dest1, dest2, last_tile, x1)


def _experts_kernel(te_ref, nt_ref, xs_ref, wg_ref, wu_ref, wd_ref, ys_ref, wgu_s, wd_s):
    t = pl.program_id(0)
    changed = (t == 0) | (te_ref[t] != te_ref[jnp.maximum(t - 1, 0)])

    @pl.when(changed)
    def _():
        wgu_s[:, 0:D_EXPERT] = wg_ref[...].astype(BF16)
        wgu_s[:, D_EXPERT:2 * D_EXPERT] = wu_ref[...].astype(BF16)
        wd_s[...] = wd_ref[...].astype(BF16)

    @pl.when(t < nt_ref[0])
    def _():
        gu = _dot(xs_ref[...].astype(BF16), wgu_s[...])
        gate = gu[:, 0:D_EXPERT]
        up = gu[:, D_EXPERT:2 * D_EXPERT]
        hid = (gate * (1.0 / (1.0 + jnp.exp(-gate)))) * up
        ys_ref[...] = _dot(hid.astype(BF16), wd_s[...])

    @pl.when(t >= nt_ref[0])
    def _():
        ys_ref[...] = jnp.zeros_like(ys_ref)


def _moe_experts(xs, tile_expert, n_tiles_used, w_gate, w_up, w_down, layer):
    rows = xs.shape[0]
    first = layer * N_EXPERTS
    grid_spec = pltpu.PrefetchScalarGridSpec(
        num_scalar_prefetch=2,
        grid=(rows // MOE_TILE,),
        in_specs=[
            pl.BlockSpec((MOE_TILE, D_MODEL), lambda t, te, nt: (jnp.minimum(t, nt[0] - 1), 0)),
            pl.BlockSpec((None, D_MODEL, D_EXPERT), lambda t, te, nt: (first + te[t], 0, 0)),
            pl.BlockSpec((None, D_MODEL, D_EXPERT), lambda t, te, nt: (first + te[t], 0, 0)),
            pl.BlockSpec((None, D_EXPERT, D_MODEL), lambda t, te, nt: (first + te[t], 0, 0)),
        ],
        out_specs=pl.BlockSpec((MOE_TILE, D_MODEL), lambda t, te, nt: (t, 0)),
        scratch_shapes=[
            pltpu.VMEM((D_MODEL, 2 * D_EXPERT), BF16),
            pltpu.VMEM((D_EXPERT, D_MODEL), BF16),
        ],
    )
    return pl.pallas_call(
        _experts_kernel,
        grid_spec=grid_spec,
        out_shape=jax.ShapeDtypeStruct((rows, D_MODEL), F32),
        compiler_params=pltpu.CompilerParams(
            dimension_semantics=("arbitrary",), vmem_limit_bytes=VMEM_LIMIT),
        name="moe_experts",
    )(tile_expert, n_tiles_used, xs, w_gate, w_up, w_down)


def _combine_kernel(d1_ref, d2_ref, x1_ref, info_ref, ys_hbm, g_ref, b_ref, out_ref, y_ref, sems, *, tb):
    i = pl.program_id(0)
    last_step = pl.num_programs(0) - 1

    def start_gather(tile, slot):
        first = tile * (tb // LANES)
        for r in range(tb):
            hi, lo = first + r // LANES, r % LANES
            _row_copy(ys_hbm, d1_ref[hi, lo], y_ref.at[slot, 0], r, sems.at[slot]).start(priority=0)
            _row_copy(ys_hbm, d2_ref[hi, lo], y_ref.at[slot, 1], r, sems.at[slot]).start(priority=1)

    def wait_gather(slot):
        for k in range(TOP_K):
            pltpu.make_async_copy(ys_hbm.at[pl.ds(0, tb), :], y_ref.at[slot, k], sems.at[slot]).wait()

    def finish(slot):
        rows = pl.ds(slot * tb, tb)
        info = info_ref[rows, :]
        lane = lax.broadcasted_iota(jnp.int32, info.shape, 1)
        w1 = jnp.sum(jnp.where(lane == ROUTE_W1, info, 0.0), axis=1, keepdims=True)
        w2 = jnp.sum(jnp.where(lane == ROUTE_W2, info, 0.0), axis=1, keepdims=True)
        ffn = w1 * y_ref[slot, 0] + w2 * y_ref[slot, 1]
        out_ref[rows, :] = _layer_norm(ALPHA * x1_ref[rows, :] + ffn, g_ref[...], b_ref[...])

    pl.when(i == 0)(lambda: start_gather(0, 0))
    wait_gather(0)
    start_gather(2 * i + 1, 1)
    finish(0)
    wait_gather(1)
    start_gather(jnp.minimum(2 * i + 2, 2 * last_step), 0)
    finish(1)
    pl.when(i == last_step)(lambda: wait_gather(0))


def _moe_combine(x1, info, ys, dest1, dest2, ln_g, ln_b, tb=256):
    n = x1.shape[0]
    grid_spec = pltpu.PrefetchScalarGridSpec(
        num_scalar_prefetch=2,
        grid=(n // (2 * tb),),
        in_specs=[
            pl.BlockSpec((2 * tb, D_MODEL), lambda i, d1, d2: (i, 0)),
            pl.BlockSpec((2 * tb, ROUTER_LANES), lambda i, d1, d2: (i, 0)),
            pl.BlockSpec(memory_space=pl.ANY),
            pl.BlockSpec((1, D_MODEL), lambda i, d1, d2: (0, 0)),
            pl.BlockSpec((1, D_MODEL), lambda i, d1, d2: (0, 0)),
        ],
        out_specs=pl.BlockSpec((2 * tb, D_MODEL), lambda i, d1, d2: (i, 0)),
        scratch_shapes=[
            pltpu.VMEM((2, TOP_K, tb, D_MODEL), F32),
            pltpu.SemaphoreType.DMA((2,)),
        ],
    )
    return pl.pallas_call(
        functools.partial(_combine_kernel, tb=tb),
        grid_spec=grid_spec,
        out_shape=jax.ShapeDtypeStruct((n, D_MODEL), F32),
        compiler_params=pltpu.CompilerParams(
            dimension_semantics=("arbitrary",), vmem_limit_bytes=VMEM_LIMIT),
        name="moe_combine",
    )(dest1, dest2, x1, info, ys, ln_g, ln_b)


def _moe(x1, info, pick, counts, w_gate, w_up, w_down, layer, ln_g, ln_b):
    n = x1.shape[0]
    dest1, dest2 = _moe_rank(pick, info, counts)
    tiles = jnp.ceil(counts[0, N_GROUPS:N_GROUPS + N_EXPERTS] * (1.0 / MOE_TILE)).astype(jnp.int32)
    upto = np.arange(N_EXPERTS)[None, :] <= np.arange(N_EXPERTS)[:, None]
    last_tile = jnp.sum(jnp.where(upto, tiles[None, :], 0), axis=1)
    tile_ids = jnp.arange(_moe_rows(n) // MOE_TILE, dtype=jnp.int32)
    tile_expert = jnp.minimum(jnp.sum(tile_ids[:, None] >= last_tile[None, :], axis=1), N_EXPERTS - 1)
    xs = _moe_dispatch(x1, dest1, dest2, last_tile)
    ys = _moe_experts(xs, tile_expert.astype(jnp.int32), last_tile[N_EXPERTS - 1:], w_gate, w_up, w_down, layer)
    return _moe_combine(x1, info, ys, dest1, dest2, ln_g, ln_b)


def _layer(x2d, batch, w_in, b_gate, rel_bias, w_br_sb, w_br_ca, w_out, ln1_g, ln1_b,
           w_group, b_group, w_erouter, b_erouter, w_gate, w_up, w_down, layer, ln2_g, ln2_b):
    n = x2d.shape[0]
    seq = n // batch
    scale = HEAD_DIM ** -0.5
    qscale = np.ones((1, D_QKV), np.float32)
    qscale[:, 0:W_SB] = scale * np.log2(np.e)
    qscale[:, 3 * W_SB:3 * W_SB + W_CA] = scale * np.log2(np.e)
    qkv, gates = _in_proj(x2d, w_in.astype(BF16), jnp.asarray(qscale), b_gate.reshape(1, 2 * D_MODEL))
    y_sb = _sb_attn(qkv, batch)
    y_ca = _ca_attn(qkv, batch, _ca_bias_line(rel_bias))

    w_router = jnp.concatenate(
        [w_group, w_erouter.transpose(1, 0, 2).reshape(D_MODEL, N_EXPERTS)], axis=1)
    w_router = jnp.pad(w_router, ((0, 0), (0, ROUTER_LANES - N_GROUPS - N_EXPERTS)))
    b_router = jnp.pad(jnp.concatenate([b_group, b_erouter.reshape(N_EXPERTS)]),
                       (0, ROUTER_LANES - N_GROUPS - N_EXPERTS)).reshape(1, ROUTER_LANES)
    x1, info, pick, counts = _post_attn(y_sb, y_ca, gates, x2d, w_br_sb.astype(BF16), w_br_ca.astype(BF16),
                        w_out.astype(BF16), w_router, b_router,
                        ln1_g.reshape(1, D_MODEL), ln1_b.reshape(1, D_MODEL))

    return _moe(x1, info, pick, counts, w_gate, w_up, w_down, layer,
                ln2_g.reshape(1, D_MODEL), ln2_b.reshape(1, D_MODEL))


def kernel(x, w_in, b_gate, rel_bias, w_br_sb, w_br_ca, w_out, ln1_g, ln1_b, w_group, b_group,
           w_erouter, b_erouter, w_gate, w_up, w_down, ln2_g, ln2_b):
    batch, seq, d = x.shape
    h = x.reshape(batch * seq, d)
    w_gate = w_gate.reshape(DEPTH * N_EXPERTS, D_MODEL, D_EXPERT)
    w_up = w_up.reshape(DEPTH * N_EXPERTS, D_MODEL, D_EXPERT)
    w_down = w_down.reshape(DEPTH * N_EXPERTS, D_EXPERT, D_MODEL)
    for l in range(DEPTH):
        h = _layer(h, batch, w_in[l], b_gate[l], rel_bias[l], w_br_sb[l], w_br_ca[l], w_out[l],
                   ln1_g[l], ln1_b[l], w_group[l], b_group[l], w_erouter[l], b_erouter[l],
                   w_gate, w_up, w_down, l, ln2_g[l], ln2_b[l])
    return h.reshape(batch, seq, d)
```

```python
import functools

import jax
import jax.numpy as jnp
import numpy as np
from jax import lax
from jax.experimental import pallas as pl
from jax.experimental.pallas import tpu as pltpu

D_MODEL = 1024
DEPTH = 2
CHUNK = 64
HEAD_DIM = 64
H_SB = 8
H_CA = 8
W_SB = H_SB * HEAD_DIM
W_CA = H_CA * HEAD_DIM
N_PAST_CHUNKS = 8
REL_CLIP = 128
N_GROUPS = 4
EXPERTS_PER_GROUP = 8
N_EXPERTS = N_GROUPS * EXPERTS_PER_GROUP
D_EXPERT = 256
ALPHA = (2.0 * DEPTH) ** 0.25
LN_EPS = 1e-5
D_QKV = 3 * W_SB + 3 * W_CA
D_IN = D_QKV + 2 * D_MODEL
NEG_INF = -1e30

LANES = 128
HEADS_PER_TILE = LANES // HEAD_DIM
ROUTER_LANES = LANES
VMEM_LIMIT = 56 * 1024 * 1024

BF16 = jnp.bfloat16
F32 = jnp.float32

_NT = (((1,), (1,)), ((), ()))


def _dot(a, b):
    return jnp.dot(a, b, preferred_element_type=F32)


def _layer_norm(h, g, b):
    mu = jnp.mean(h, axis=-1, keepdims=True)
    hc = h - mu
    var = jnp.mean(hc * hc, axis=-1, keepdims=True)
    return hc * lax.rsqrt(var + LN_EPS) * g + b


def _split_bf16(a):
    hi = a.astype(BF16)
    lo = (a - hi.astype(F32)).astype(BF16)
    return hi, lo


def _in_proj_kernel(x_ref, w_ref, scale_ref, bg_ref, qkv_ref, gate_ref):
    xb = x_ref[...].astype(BF16)
    for c in range(D_QKV // D_MODEL):
        cols = slice(c * D_MODEL, (c + 1) * D_MODEL)
        acc = _dot(xb, w_ref[:, cols])
        qkv_ref[:, cols] = (acc * scale_ref[:, cols]).astype(BF16)
    for c in range(2):
        cols = slice(c * D_MODEL, (c + 1) * D_MODEL)
        wcols = slice(D_QKV + c * D_MODEL, D_QKV + (c + 1) * D_MODEL)
        logit = _dot(xb, w_ref[:, wcols]) + bg_ref[:, cols]
        gate_ref[:, cols] = 1.0 / (1.0 + jnp.exp(-logit))


def _in_proj(x2d, w_in_bf16, qscale, b_gate_row, tm=512):
    n = x2d.shape[0]
    return pl.pallas_call(
        _in_proj_kernel,
        grid=(n // tm,),
        in_specs=[
            pl.BlockSpec((tm, D_MODEL), lambda i: (i, 0)),
            pl.BlockSpec((D_MODEL, D_IN), lambda i: (0, 0)),
            pl.BlockSpec((1, D_QKV), lambda i: (0, 0)),
            pl.BlockSpec((1, 2 * D_MODEL), lambda i: (0, 0)),
        ],
        out_specs=[
            pl.BlockSpec((tm, D_QKV), lambda i: (i, 0)),
            pl.BlockSpec((tm, 2 * D_MODEL), lambda i: (i, 0)),
        ],
        out_shape=[
            jax.ShapeDtypeStruct((n, D_QKV), BF16),
            jax.ShapeDtypeStruct((n, 2 * D_MODEL), F32),
        ],
        compiler_params=pltpu.CompilerParams(
            dimension_semantics=("arbitrary",), vmem_limit_bytes=VMEM_LIMIT),
        name="in_proj",
    )(x2d, w_in_bf16, qscale, b_gate_row)


SB_DEAD_BITS = 160.0
SB_STEP_BLOCKS = 4


def _sb_kernel(q_ref, k_ref, v_ref, u_ref, o_ref, z_ref, arg_ref, rs_ref, acc_ref, car_ref, *, tq):
    step = pl.program_id(2)
    lane_q = lax.broadcasted_iota(jnp.int32, (tq, LANES), 1)

    def split_heads(x):
        zero = jnp.zeros_like(x)
        return jnp.where(lane_q < HEAD_DIM, x, zero), jnp.where(lane_q < HEAD_DIM, zero, x)

    blocks = [step * SB_STEP_BLOCKS + j for j in range(SB_STEP_BLOCKS)]
    q_heads = [split_heads(q_ref[pl.ds(j * tq, tq), :]) for j in range(SB_STEP_BLOCKS)]

    def key_rows(i, n):
        return pl.ds(pl.multiple_of(jnp.maximum(i - n, 0) * tq, tq), tq)

    def block_pairs(subs, n, diagonal):
        for j in subs:
            for b in range(2):
                k = k_ref[key_rows(blocks[j], n + b), :]
                for h in range(HEADS_PER_TILE):
                    z = lax.dot_general(q_heads[j][h], k, _NT, preferred_element_type=F32)
                    if diagonal and b == 0:
                        row = lax.broadcasted_iota(jnp.int32, (tq, tq), 0)
                        col = lax.broadcasted_iota(jnp.int32, (tq, tq), 1)
                        z = jnp.where(col < row, z, NEG_INF)
                    z_ref[j, b, h] = z

        def terms(j):
            for b in range(2):
                for h in range(HEADS_PER_TILE):
                    z = z_ref[j, b, h]
                    sp = jnp.maximum(z, 0.0) + jnp.log2(1.0 + jnp.exp2(-jnp.abs(z)))
                    sums = _dot(sp.astype(BF16), u_ref[...])
                    arg_ref[j, b, h] = (z - sp) - sums[:, 0:tq]
                    rs_ref[j, b, h] = sums[:, tq:tq + LANES]

        def apply(j):
            carries = [car_ref[j]]
            for b in range(2):
                carries.append(carries[b] + rs_ref[j, b])
            car_ref[j] = carries[2]
            v_parts = []
            for b in range(2):
                v = v_ref[key_rows(blocks[j], n + b), :]
                if b == 1:
                    v = jnp.where(n + b <= blocks[j], v, jnp.zeros_like(v))
                v_parts += split_heads(v)
            ws = []
            for b in range(2):
                for h in range(HEADS_PER_TILE):
                    carry = jnp.concatenate([carries[b][h]] * (tq // LANES), axis=1)
                    ws.append(jnp.exp2(arg_ref[j, b, h] - carry).astype(BF16))
            acc_ref[j] += _dot(jnp.concatenate(ws, axis=1), jnp.concatenate(v_parts, axis=0))
            return jnp.min(carries[2])

        subs = list(subs)
        least = []
        terms(subs[0])
        for prev, j in zip(subs, subs[1:]):
            terms(j)
            least.append(apply(prev))
        least.append(apply(subs[-1]))
        return least

    acc_ref[...] = jnp.zeros_like(acc_ref)
    car_ref[...] = jnp.zeros_like(car_ref)
    first = block_pairs(range(SB_STEP_BLOCKS), 0, True)

    for j in range(SB_STEP_BLOCKS):
        def more(state, j=j):
            n, least = state
            return (n <= blocks[j]) & (least < SB_DEAD_BITS)

        def body(state, j=j):
            n, _ = state
            return n + 2, block_pairs([j], n, False)[0]

        lax.while_loop(more, body, (jnp.int32(2), first[j]))
        o_ref[pl.ds(j * tq, tq), :] = acc_ref[j].astype(o_ref.dtype)


def _sb_attn(qkv, b, tq=256):
    s = qkv.shape[0] // b
    n_tiles = W_SB // LANES
    step_rows = SB_STEP_BLOCKS * tq
    steps = s // step_rows
    u = jnp.asarray(np.concatenate([np.arange(tq)[:, None] > np.arange(tq)[None, :],
                                    np.ones((tq, LANES), bool)], axis=1), dtype=BF16)
    return pl.pallas_call(
        functools.partial(_sb_kernel, tq=tq),
        grid=(b, n_tiles, steps),
        in_specs=[
            pl.BlockSpec((step_rows, LANES), lambda bi, hp, i: (bi * steps + i, hp)),
            pl.BlockSpec((s, LANES), lambda bi, hp, i: (bi, n_tiles + hp)),
            pl.BlockSpec((s, LANES), lambda bi, hp, i: (bi, 2 * n_tiles + hp)),
            pl.BlockSpec((tq, tq + LANES), lambda bi, hp, i: (0, 0)),
        ],
        out_specs=pl.BlockSpec((step_rows, LANES), lambda bi, hp, i: (bi * steps + i, hp)),
        out_shape=jax.ShapeDtypeStruct((b * s, W_SB), BF16),
        scratch_shapes=[
            pltpu.VMEM((SB_STEP_BLOCKS, 2, HEADS_PER_TILE, tq, tq), F32),
            pltpu.VMEM((SB_STEP_BLOCKS, 2, HEADS_PER_TILE, tq, tq), F32),
            pltpu.VMEM((SB_STEP_BLOCKS, 2, HEADS_PER_TILE, tq, LANES), F32),
            pltpu.VMEM((SB_STEP_BLOCKS, tq, LANES), F32),
            pltpu.VMEM((SB_STEP_BLOCKS, HEADS_PER_TILE, tq, LANES), F32),
        ],
        compiler_params=pltpu.CompilerParams(
            dimension_semantics=("arbitrary", "arbitrary", "arbitrary"), vmem_limit_bytes=VMEM_LIMIT),
        name="sb_attn",
    )(qkv, qkv, qkv, u)


CA_GROUP = 4
CA_STEP_GROUPS = 8
CA_TQ = CA_GROUP * CHUNK
CA_BAND = (CA_GROUP + N_PAST_CHUNKS) * CHUNK
CA_PAD = N_PAST_CHUNKS * CHUNK


def _ca_kernel(q_ref, k_ref, v_ref, line_ref, ones_ref, o_ref, kp_ref, vp_ref, bias_ref):
    c = pl.program_id(2)
    s = k_ref.shape[0]

    @pl.when(c == 0)
    def _():
        kp_ref[0:CA_PAD, :] = jnp.zeros((CA_PAD, LANES), BF16)
        vp_ref[0:CA_PAD, :] = jnp.zeros((CA_PAD, LANES), BF16)
        kp_ref[CA_PAD:CA_PAD + s, :] = k_ref[...]
        vp_ref[CA_PAD:CA_PAD + s, :] = v_ref[...]
        r = lax.broadcasted_iota(jnp.int32, (CA_TQ, CA_BAND), 0)
        p = lax.broadcasted_iota(jnp.int32, (CA_TQ, CA_BAND), 1)
        shift = CHUNK.bit_length() - 1
        qc = lax.shift_right_logical(r, shift)
        kc = lax.shift_right_logical(p, shift)
        in_band = (kc >= qc) & (kc <= qc + N_PAST_CHUNKS)
        for h in range(HEADS_PER_TILE):
            rows = jnp.broadcast_to(line_ref[h], (CA_TQ, CA_LINE))
            skew = pltpu.roll(rows, 1, axis=1, stride=1, stride_axis=0)
            bias_ref[h] = jnp.where(in_band, skew[:, CA_TQ:CA_TQ + CA_BAND], NEG_INF)

    lane_q = lax.broadcasted_iota(jnp.int32, (CA_TQ, LANES), 1)
    lane_v = lax.broadcasted_iota(jnp.int32, (CA_BAND, LANES), 1)

    def band_start(g):
        return pl.multiple_of((c * CA_STEP_GROUPS + g) * CA_TQ, CA_TQ)

    def scores(g):
        kb = kp_ref[pl.ds(band_start(g), CA_BAND), :]
        q = q_ref[pl.ds(g * CA_TQ, CA_TQ), :]
        zero_q = jnp.zeros_like(q)
        q_heads = (jnp.where(lane_q < HEAD_DIM, q, zero_q), jnp.where(lane_q < HEAD_DIM, zero_q, q))
        return [lax.dot_general(q_heads[h], kb, _NT, preferred_element_type=F32) for h in range(HEADS_PER_TILE)]

    def attend(g, qk, masked):
        group = c * CA_STEP_GROUPS + g
        vb = vp_ref[pl.ds(band_start(g), CA_BAND), :]
        zero_v = jnp.zeros_like(vb)
        v_heads = jnp.concatenate(
            [jnp.where(lane_v < HEAD_DIM, vb, zero_v), jnp.where(lane_v < HEAD_DIM, zero_v, vb)], axis=0)
        v_and_ones = jnp.concatenate([v_heads, ones_ref[...]], axis=1)
        es = []
        for h in range(HEADS_PER_TILE):
            sc = qk[h] + bias_ref[h]
            if masked:
                pos = lax.broadcasted_iota(jnp.int32, (CA_TQ, CA_BAND), 1)
                sc = jnp.where(pos >= CA_PAD - group * CA_TQ, sc, NEG_INF)
            m = jnp.max(sc, axis=1, keepdims=True)
            es.append(jnp.exp2(sc - m).astype(BF16))
        both = _dot(jnp.concatenate(es, axis=1), v_and_ones)
        o_ref[pl.ds(g * CA_TQ, CA_TQ), :] = (both[:, 0:LANES] / both[:, LANES:2 * LANES]).astype(o_ref.dtype)

    def step(masked):
        qk = scores(0)
        for g in range(CA_STEP_GROUPS):
            qk_next = scores(g + 1) if g + 1 < CA_STEP_GROUPS else None
            attend(g, qk, masked)
            qk = qk_next

    assert CA_PAD // CA_TQ <= CA_STEP_GROUPS
    pl.when(c == 0)(lambda: step(True))
    pl.when(c > 0)(lambda: step(False))


CA_LINE = CA_TQ + CA_BAND


def _ca_bias_line(rel_bias):
    h = rel_bias.shape[0]
    n_far = CA_PAD + CA_TQ - 1 - REL_CLIP
    n_neg = CA_BAND - 1 - CA_PAD - REL_CLIP
    rb = rel_bias.astype(F32) * np.float32(np.log2(np.e))
    line = jnp.concatenate([jnp.broadcast_to(rb[:, 2 * REL_CLIP:], (h, n_far)), rb[:, ::-1],
                            jnp.broadcast_to(rb[:, :1], (h, n_neg + 1))], axis=1)
    assert line.shape[1] == CA_LINE
    return line.reshape(h, 1, CA_LINE)


def _ca_attn(qkv, b, bias_line):
    s = qkv.shape[0] // b
    n_tiles = W_CA // LANES
    base = 3 * W_SB // LANES
    step_rows = CA_STEP_GROUPS * CA_TQ
    steps = s // step_rows
    head_of_row = np.arange(HEADS_PER_TILE * CA_BAND)[:, None] // CA_BAND
    head_of_lane = np.arange(LANES)[None, :] // HEAD_DIM
    ones = jnp.asarray(head_of_row == head_of_lane, dtype=BF16)
    return pl.pallas_call(
        _ca_kernel,
        grid=(b, n_tiles, steps),
        in_specs=[
            pl.BlockSpec((step_rows, LANES), lambda bi, hp, c: (bi * steps + c, base + hp)),
            pl.BlockSpec((s, LANES), lambda bi, hp, c: (bi, base + n_tiles + hp)),
            pl.BlockSpec((s, LANES), lambda bi, hp, c: (bi, base + 2 * n_tiles + hp)),
            pl.BlockSpec((HEADS_PER_TILE, 1, CA_LINE), lambda bi, hp, c: (hp, 0, 0)),
            pl.BlockSpec((HEADS_PER_TILE * CA_BAND, LANES), lambda bi, hp, c: (0, 0)),
        ],
        out_specs=pl.BlockSpec((step_rows, LANES), lambda bi, hp, c: (bi * steps + c, hp)),
        out_shape=jax.ShapeDtypeStruct((b * s, W_CA), BF16),
        scratch_shapes=[
            pltpu.VMEM((CA_PAD + s, LANES), BF16),
            pltpu.VMEM((CA_PAD + s, LANES), BF16),
            pltpu.VMEM((HEADS_PER_TILE, CA_TQ, CA_BAND), F32),
        ],
        compiler_params=pltpu.CompilerParams(
            dimension_semantics=("arbitrary", "arbitrary", "arbitrary"), vmem_limit_bytes=VMEM_LIMIT),
        name="ca_attn",
    )(qkv, qkv, qkv, bias_line, ones)


ROUTE_E1, ROUTE_E2, ROUTE_W1, ROUTE_W2 = 0, 1, 2, 3


def _route(lg):
    lane = lax.broadcasted_iota(jnp.int32, lg.shape, 1)
    big = jnp.int32(ROUTER_LANES)
    is_group = lane < N_GROUPS
    g_max = jnp.max(jnp.where(is_group, lg, -jnp.inf), axis=1, keepdims=True)
    g_idx = jnp.min(jnp.where(is_group & (lg == g_max), lane, big), axis=1, keepdims=True)
    g_den = jnp.sum(jnp.where(is_group, jnp.exp(lg - g_max), 0.0), axis=1, keepdims=True)
    g_val = 1.0 / g_den
    lo = N_GROUPS + EXPERTS_PER_GROUP * g_idx
    in_group = (lane >= lo) & (lane < lo + EXPERTS_PER_GROUP)
    v1 = jnp.max(jnp.where(in_group, lg, -jnp.inf), axis=1, keepdims=True)
    i1 = jnp.min(jnp.where(in_group & (lg == v1), lane, big), axis=1, keepdims=True)
    rest = in_group & (lane != i1)
    v2 = jnp.max(jnp.where(rest, lg, -jnp.inf), axis=1, keepdims=True)
    i2 = jnp.min(jnp.where(rest & (lg == v2), lane, big), axis=1, keepdims=True)
    e2 = jnp.exp(v2 - v1)
    w1 = g_val / (1.0 + e2)
    w2 = g_val * e2 / (1.0 + e2)
    picked = ((lane == i1) | (lane == i2)).astype(BF16)
    info = (jnp.where(lane == ROUTE_E1, (i1 - N_GROUPS).astype(F32), 0.0)
            + jnp.where(lane == ROUTE_E2, (i2 - N_GROUPS).astype(F32), 0.0)
            + jnp.where(lane == ROUTE_W1, w1, 0.0) + jnp.where(lane == ROUTE_W2, w2, 0.0))
    return info, picked


def _post_kernel(ysb_ref, yca_ref, gate_ref, x_ref, wsb_ref, wca_ref, wout_ref, wr_ref, br_ref,
                 g_ref, b_ref, x1_ref, info_ref, pick_ref, cnt_ref):
    @pl.when(pl.program_id(0) == 0)
    def _():
        cnt_ref[...] = jnp.zeros_like(cnt_ref)

    w_hi, w_lo = _split_bf16(wr_ref[...])
    w_split = jnp.concatenate([w_hi, w_lo], axis=1)
    tm = x_ref.shape[0]
    subs = [pl.ds(s * POST_SUB, POST_SUB) for s in range(tm // POST_SUB)]
    for rows in subs:
        a = _dot(ysb_ref[rows, :], wsb_ref[...])
        c = _dot(yca_ref[rows, :], wca_ref[...])
        mix = gate_ref[rows, 0:D_MODEL] * a + gate_ref[rows, D_MODEL:2 * D_MODEL] * c
        x1_ref[rows, :] = ALPHA * x_ref[rows, :] + _dot(mix.astype(BF16), wout_ref[...])
    for rows in subs:
        x1 = _layer_norm(x1_ref[rows, :], g_ref[...], b_ref[...])
        x1_ref[rows, :] = x1
        x_hi, x_lo = _split_bf16(x1)
        parts = _dot(jnp.concatenate([x_hi, x_lo], axis=0), w_split)
        lg = (parts[0:POST_SUB, 0:ROUTER_LANES] + parts[0:POST_SUB, ROUTER_LANES:]
              + parts[POST_SUB:, 0:ROUTER_LANES] + parts[POST_SUB:, ROUTER_LANES:]) + br_ref[...]
        info, pick = _route(lg)
        info_ref[rows, :] = info
        pick_ref[rows, :] = pick
        cnt_ref[...] += jnp.sum(pick.astype(F32), axis=0, keepdims=True)


POST_SUB = 256


def _post_attn(y_sb, y_ca, gates, x2d, w_br_sb, w_br_ca, w_out, w_router, b_router, ln_g, ln_b, tm=1024):
    n = x2d.shape[0]
    row = lambda i: (i, 0)
    fixed = lambda i: (0, 0)
    return pl.pallas_call(
        _post_kernel,
        grid=(n // tm,),
        in_specs=[
            pl.BlockSpec((tm, W_SB), row),
            pl.BlockSpec((tm, W_CA), row),
            pl.BlockSpec((tm, 2 * D_MODEL), row),
            pl.BlockSpec((tm, D_MODEL), row),
            pl.BlockSpec((W_SB, D_MODEL), fixed),
            pl.BlockSpec((W_CA, D_MODEL), fixed),
            pl.BlockSpec((D_MODEL, D_MODEL), fixed),
            pl.BlockSpec((D_MODEL, ROUTER_LANES), fixed),
            pl.BlockSpec((1, ROUTER_LANES), fixed),
            pl.BlockSpec((1, D_MODEL), fixed),
            pl.BlockSpec((1, D_MODEL), fixed),
        ],
        out_specs=[
            pl.BlockSpec((tm, D_MODEL), row),
            pl.BlockSpec((tm, ROUTER_LANES), row),
            pl.BlockSpec((tm, ROUTER_LANES), row),
            pl.BlockSpec((8, ROUTER_LANES), fixed),
        ],
        out_shape=[
            jax.ShapeDtypeStruct((n, D_MODEL), F32),
            jax.ShapeDtypeStruct((n, ROUTER_LANES), F32),
            jax.ShapeDtypeStruct((n, ROUTER_LANES), BF16),
            jax.ShapeDtypeStruct((8, ROUTER_LANES), F32),
        ],
        compiler_params=pltpu.CompilerParams(
            dimension_semantics=("arbitrary",), vmem_limit_bytes=VMEM_LIMIT),
        name="post_attn",
    )(y_sb, y_ca, gates, x2d, w_br_sb, w_br_ca, w_out, w_router, b_router, ln_g, ln_b)


MOE_TILE = 512
TOP_K = 2


def _moe_rows(n):
    return n * TOP_K + N_EXPERTS * MOE_TILE


def _rank_kernel(pick_ref, info_ref, tri_ref, cnt_ref, d1_ref, d2_ref, run_ref, off_ref, *, tb):
    i = pl.program_id(0)
    pick = pick_ref[...]

    @pl.when(i == 0)
    def _():
        cnt = cnt_ref[...]
        padded = jnp.ceil(cnt * (1.0 / MOE_TILE)) * MOE_TILE
        lane = lax.broadcasted_iota(jnp.int32, padded.shape, 1)
        scan = padded
        step = 1
        while step < ROUTER_LANES:
            scan = scan + jnp.where(lane >= step, pltpu.roll(scan, step, axis=1), 0.0)
            step *= 2
        off_ref[...] = scan - padded
        run_ref[...] = jnp.zeros_like(run_ref)

    seen = run_ref[0:1, :]
    earlier = _dot(tri_ref[...], pick)
    row_of = earlier + seen + off_ref[0:1, :]
    info = info_ref[...]
    lane = lax.broadcasted_iota(jnp.int32, info.shape, 1)
    lane_f = lane.astype(F32)
    e1 = jnp.sum(jnp.where(lane == ROUTE_E1, info, 0.0), axis=1, keepdims=True)
    e2 = jnp.sum(jnp.where(lane == ROUTE_E2, info, 0.0), axis=1, keepdims=True)
    d1 = jnp.sum(jnp.where(lane_f == e1 + N_GROUPS, row_of, 0.0), axis=1, keepdims=True)
    d2 = jnp.sum(jnp.where(lane_f == e2 + N_GROUPS, row_of, 0.0), axis=1, keepdims=True)
    eye = (lax.broadcasted_iota(jnp.int32, (LANES, LANES), 0) == lax.broadcasted_iota(jnp.int32, (LANES, LANES), 1))
    for blk in range(tb // LANES):
        for d, d_ref in ((d1, d1_ref), (d2, d2_ref)):
            column = d[blk * LANES:(blk + 1) * LANES]
            d_ref[blk:blk + 1, :] = jnp.sum(jnp.where(eye, column, 0.0), axis=0, keepdims=True).astype(jnp.int32)
    run_ref[...] += jnp.sum(pick.astype(F32), axis=0, keepdims=True)


def _moe_rank(pick, info, counts, tb=1024):
    n = pick.shape[0]
    tri = jnp.asarray(np.arange(tb)[None, :] < np.arange(tb)[:, None], dtype=BF16)
    dest_spec = pl.BlockSpec((tb // LANES, LANES), lambda i: (i, 0))
    dest_shape = jax.ShapeDtypeStruct((n // LANES, LANES), jnp.int32)
    return pl.pallas_call(
        functools.partial(_rank_kernel, tb=tb),
        grid=(n // tb,),
        in_specs=[
            pl.BlockSpec((tb, ROUTER_LANES), lambda i: (i, 0)),
            pl.BlockSpec((tb, ROUTER_LANES), lambda i: (i, 0)),
            pl.BlockSpec((tb, tb), lambda i: (0, 0)),
            pl.BlockSpec((8, ROUTER_LANES), lambda i: (0, 0)),
        ],
        out_specs=[dest_spec, dest_spec],
        out_shape=[dest_shape, dest_shape],
        scratch_shapes=[
            pltpu.VMEM((8, ROUTER_LANES), F32),
            pltpu.VMEM((8, ROUTER_LANES), F32),
        ],
        compiler_params=pltpu.CompilerParams(
            dimension_semantics=("arbitrary",), vmem_limit_bytes=VMEM_LIMIT),
        name="moe_rank",
    )(pick, info, tri, counts)


def _row_copy(src_ref, src_row, dst_ref, dst_row, sem):
    return pltpu.make_async_copy(src_ref.at[pl.ds(src_row, 1), :], dst_ref.at[pl.ds(dst_row, 1), :], sem)


def _dispatch_kernel(d1_ref, d2_ref, last_ref, x_ref, xs_hbm, zero_ref, sem, zsem, *, tb):
    @pl.when(pl.program_id(0) == 0)
    def _():
        zero_ref[...] = jnp.zeros_like(zero_ref)

        def last_tile_copy(e):
            first_row = pl.multiple_of((last_ref[e] - 1) * MOE_TILE, MOE_TILE)
            return pltpu.make_async_copy(zero_ref, xs_hbm.at[pl.ds(first_row, MOE_TILE), :], zsem)

        def unused_tile_copy(t):
            return pltpu.make_async_copy(zero_ref, xs_hbm.at[pl.ds(t * MOE_TILE, MOE_TILE), :], zsem)

        n_tiles = xs_hbm.shape[0] // MOE_TILE
        min_used = n_tiles - N_EXPERTS
        for wait in (False, True):
            for e in range(N_EXPERTS):
                owns_tiles = last_ref[e] > (last_ref[e - 1] if e else 0)
                pl.when(owns_tiles)(
                    lambda e=e, wait=wait: last_tile_copy(e).wait() if wait else last_tile_copy(e).start())
            for t in range(min_used, n_tiles):
                pl.when(t >= last_ref[N_EXPERTS - 1])(
                    lambda t=t, wait=wait: unused_tile_copy(t).wait() if wait else unused_tile_copy(t).start())

    first = pl.program_id(0) * (tb // LANES)
    for r in range(tb):
        hi, lo = first + r // LANES, r % LANES
        _row_copy(x_ref, r, xs_hbm, d1_ref[hi, lo], sem).start(priority=0)
        _row_copy(x_ref, r, xs_hbm, d2_ref[hi, lo], sem).start(priority=1)
    for _ in range(TOP_K):
        pltpu.make_async_copy(x_ref, xs_hbm.at[pl.ds(0, tb), :], sem).wait()


def _moe_dispatch(x1, dest1, dest2, last_tile, tb=512):
    n = x1.shape[0]
    rows = _moe_rows(n)
    grid_spec = pltpu.PrefetchScalarGridSpec(
        num_scalar_prefetch=3,
        grid=(n // tb,),
        in_specs=[pl.BlockSpec((tb, D_MODEL), lambda i, d1, d2, last: (i, 0))],
        out_specs=pl.BlockSpec(memory_space=pl.ANY),
        scratch_shapes=[
            pltpu.VMEM((MOE_TILE, D_MODEL), F32),
            pltpu.SemaphoreType.DMA(()),
            pltpu.SemaphoreType.DMA(()),
        ],
    )
    return pl.pallas_call(
        functools.partial(_dispatch_kernel, tb=tb),
        grid_spec=grid_spec,
        out_shape=jax.ShapeDtypeStruct((rows, D_MODEL), F32),
        compiler_params=pltpu.CompilerParams(
            dimension_semantics=("arbitrary",), vmem_limit_bytes=VMEM_LIMIT),
        name="moe_dispatch",
    )(dest1, dest2, last_tile, x1)


def _experts_kernel(te_ref, nt_ref, xs_ref, wg_ref, wu_ref, wd_ref, ys_ref, wgu_s, wd_s):
    t = pl.program_id(0)
    changed = (t == 0) | (te_ref[t] != te_ref[jnp.maximum(t - 1, 0)])

    @pl.when(changed)
    def _():
        wgu_s[:, 0:D_EXPERT] = wg_ref[...].astype(BF16)
        wgu_s[:, D_EXPERT:2 * D_EXPERT] = wu_ref[...].astype(BF16)
        wd_s[...] = wd_ref[...].astype(BF16)

    @pl.when(t < nt_ref[0])
    def _():
        gu = _dot(xs_ref[...].astype(BF16), wgu_s[...])
        gate = gu[:, 0:D_EXPERT]
        up = gu[:, D_EXPERT:2 * D_EXPERT]
        hid = (gate * (1.0 / (1.0 + jnp.exp(-gate)))) * up
        ys_ref[...] = _dot(hid.astype(BF16), wd_s[...])

    @pl.when(t >= nt_ref[0])
    def _():
        ys_ref[...] = jnp.zeros_like(ys_ref)


def _moe_experts(xs, tile_expert, n_tiles_used, w_gate, w_up, w_down, layer):
    rows = xs.shape[0]
    first = layer * N_EXPERTS
    grid_spec = pltpu.PrefetchScalarGridSpec(
        num_scalar_prefetch=2,
        grid=(rows // MOE_TILE,),
        in_specs=[
            pl.BlockSpec((MOE_TILE, D_MODEL), lambda t, te, nt: (jnp.minimum(t, nt[0] - 1), 0)),
            pl.BlockSpec((None, D_MODEL, D_EXPERT), lambda t, te, nt: (first + te[t], 0, 0)),
            pl.BlockSpec((None, D_MODEL, D_EXPERT), lambda t, te, nt: (first + te[t], 0, 0)),
            pl.BlockSpec((None, D_EXPERT, D_MODEL), lambda t, te, nt: (first + te[t], 0, 0)),
        ],
        out_specs=pl.BlockSpec((MOE_TILE, D_MODEL), lambda t, te, nt: (t, 0)),
        scratch_shapes=[
            pltpu.VMEM((D_MODEL, 2 * D_EXPERT), BF16),
            pltpu.VMEM((D_EXPERT, D_MODEL), BF16),
        ],
    )
    return pl.pallas_call(
        _experts_kernel,
        grid_spec=grid_spec,
        out_shape=jax.ShapeDtypeStruct((rows, D_MODEL), F32),
        compiler_params=pltpu.CompilerParams(
            dimension_semantics=("arbitrary",), vmem_limit_bytes=VMEM_LIMIT),
        name="moe_experts",
    )(tile_expert, n_tiles_used, xs, w_gate, w_up, w_down)


def _combine_kernel(d1_ref, d2_ref, x1_ref, info_ref, ys_hbm, g_ref, b_ref, out_ref, y_ref, sems, *, tb):
    i = pl.program_id(0)
    last_step = pl.num_programs(0) - 1

    def start_gather(tile, slot):
        first = tile * (tb // LANES)
        for r in range(tb):
            hi, lo = first + r // LANES, r % LANES
            _row_copy(ys_hbm, d1_ref[hi, lo], y_ref.at[slot, 0], r, sems.at[slot]).start(priority=0)
            _row_copy(ys_hbm, d2_ref[hi, lo], y_ref.at[slot, 1], r, sems.at[slot]).start(priority=1)

    def wait_gather(slot):
        for k in range(TOP_K):
            pltpu.make_async_copy(ys_hbm.at[pl.ds(0, tb), :], y_ref.at[slot, k], sems.at[slot]).wait()

    def finish(slot):
        rows = pl.ds(slot * tb, tb)
        info = info_ref[rows, :]
        lane = lax.broadcasted_iota(jnp.int32, info.shape, 1)
        w1 = jnp.sum(jnp.where(lane == ROUTE_W1, info, 0.0), axis=1, keepdims=True)
        w2 = jnp.sum(jnp.where(lane == ROUTE_W2, info, 0.0), axis=1, keepdims=True)
        ffn = w1 * y_ref[slot, 0] + w2 * y_ref[slot, 1]
        out_ref[rows, :] = _layer_norm(ALPHA * x1_ref[rows, :] + ffn, g_ref[...], b_ref[...])

    pl.when(i == 0)(lambda: start_gather(0, 0))
    wait_gather(0)
    start_gather(2 * i + 1, 1)
    finish(0)
    wait_gather(1)
    start_gather(jnp.minimum(2 * i + 2, 2 * last_step), 0)
    finish(1)
    pl.when(i == last_step)(lambda: wait_gather(0))


def _moe_combine(x1, info, ys, dest1, dest2, ln_g, ln_b, tb=256):
    n = x1.shape[0]
    grid_spec = pltpu.PrefetchScalarGridSpec(
        num_scalar_prefetch=2,
        grid=(n // (2 * tb),),
        in_specs=[
            pl.BlockSpec((2 * tb, D_MODEL), lambda i, d1, d2: (i, 0)),
            pl.BlockSpec((2 * tb, ROUTER_LANES), lambda i, d1, d2: (i, 0)),
            pl.BlockSpec(memory_space=pl.ANY),
            pl.BlockSpec((1, D_MODEL), lambda i, d1, d2: (0, 0)),
            pl.BlockSpec((1, D_MODEL), lambda i, d1, d2: (0, 0)),
        ],
        out_specs=pl.BlockSpec((2 * tb, D_MODEL), lambda i, d1, d2: (i, 0)),
        scratch_shapes=[
            pltpu.VMEM((2, TOP_K, tb, D_MODEL), F32),
            pltpu.SemaphoreType.DMA((2,)),
        ],
    )
    return pl.pallas_call(
        functools.partial(_combine_kernel, tb=tb),
        grid_spec=grid_spec,
        out_shape=jax.ShapeDtypeStruct((n, D_MODEL), F32),
        compiler_params=pltpu.CompilerParams(
            dimension_semantics=("arbitrary",), vmem_limit_bytes=VMEM_LIMIT),
        name="moe_combine",
    )(dest1, dest2, x1, info, ys, ln_g, ln_b)


def _moe(x1, info, pick, counts, w_gate, w_up, w_down, layer, ln_g, ln_b):
    n = x1.shape[0]
    dest1, dest2 = _moe_rank(pick, info, counts)
    tiles = jnp.ceil(counts[0, N_GROUPS:N_GROUPS + N_EXPERTS] * (1.0 / MOE_TILE)).astype(jnp.int32)
    upto = np.arange(N_EXPERTS)[None, :] <= np.arange(N_EXPERTS)[:, None]
    last_tile = jnp.sum(jnp.where(upto, tiles[None, :], 0), axis=1)
    tile_ids = jnp.arange(_moe_rows(n) // MOE_TILE, dtype=jnp.int32)
    tile_expert = jnp.minimum(jnp.sum(tile_ids[:, None] >= last_tile[None, :], axis=1), N_EXPERTS - 1)
    xs = _moe_dispatch(x1, dest1, dest2, last_tile)
    ys = _moe_experts(xs, tile_expert.astype(jnp.int32), last_tile[N_EXPERTS - 1:], w_gate, w_up, w_down, layer)
    return _moe_combine(x1, info, ys, dest1, dest2, ln_g, ln_b)


def _layer(x2d, batch, w_in, b_gate, rel_bias, w_br_sb, w_br_ca, w_out, ln1_g, ln1_b,
           w_group, b_group, w_erouter, b_erouter, w_gate, w_up, w_down, layer, ln2_g, ln2_b):
    n = x2d.shape[0]
    seq = n // batch
    scale = HEAD_DIM ** -0.5
    qscale = np.ones((1, D_QKV), np.float32)
    qscale[:, 0:W_SB] = scale * np.log2(np.e)
    qscale[:, 3 * W_SB:3 * W_SB + W_CA] = scale * np.log2(np.e)
    qkv, gates = _in_proj(x2d, w_in.astype(BF16), jnp.asarray(qscale), b_gate.reshape(1, 2 * D_MODEL))
    y_sb = _sb_attn(qkv, batch)
    y_ca = _ca_attn(qkv, batch, _ca_bias_line(rel_bias))

    w_router = jnp.concatenate(
        [w_group, w_erouter.transpose(1, 0, 2).reshape(D_MODEL, N_EXPERTS)], axis=1)
    w_router = jnp.pad(w_router, ((0, 0), (0, ROUTER_LANES - N_GROUPS - N_EXPERTS)))
    b_router = jnp.pad(jnp.concatenate([b_group, b_erouter.reshape(N_EXPERTS)]),
                       (0, ROUTER_LANES - N_GROUPS - N_EXPERTS)).reshape(1, ROUTER_LANES)
    x1, info, pick, counts = _post_attn(y_sb, y_ca, gates, x2d, w_br_sb.astype(BF16), w_br_ca.astype(BF16),
                        w_out.astype(BF16), w_router, b_router,
                        ln1_g.reshape(1, D_MODEL), ln1_b.reshape(1, D_MODEL))

    return _moe(x1, info, pick, counts, w_gate, w_up, w_down, layer,
                ln2_g.reshape(1, D_MODEL), ln2_b.reshape(1, D_MODEL))


def kernel(x, w_in, b_gate, rel_bias, w_br_sb, w_br_ca, w_out, ln1_g, ln1_b, w_group, b_group,
           w_erouter, b_erouter, w_gate, w_up, w_down, ln2_g, ln2_b):
    batch, seq, d = x.shape
    h = x.reshape(batch * seq, d)
    w_gate = w_gate.reshape(DEPTH * N_EXPERTS, D_MODEL, D_EXPERT)
    w_up = w_up.reshape(DEPTH * N_EXPERTS, D_MODEL, D_EXPERT)
    w_down = w_down.reshape(DEPTH * N_EXPERTS, D_EXPERT, D_MODEL)
    for l in range(DEPTH):
        h = _layer(h, batch, w_in[l], b_gate[l], rel_bias[l], w_br_sb[l], w_br_ca[l], w_out[l],
                   ln1_g[l], ln1_b[l], w_group[l], b_group[l], w_erouter[l], b_erouter[l],
                   w_gate, w_up, w_down, l, ln2_g[l], ln2_b[l])
    return h.reshape(batch, seq, d)
```

```python
import functools

import jax
import jax.numpy as jnp
import numpy as np
from jax import lax
from jax.experimental import pallas as pl
from jax.experimental.pallas import tpu as pltpu

D_MODEL = 1024
DEPTH = 2
CHUNK = 64
HEAD_DIM = 64
H_SB = 8
H_CA = 8
W_SB = H_SB * HEAD_DIM
W_CA = H_CA * HEAD_DIM
N_PAST_CHUNKS = 8
REL_CLIP = 128
N_GROUPS = 4
EXPERTS_PER_GROUP = 8
N_EXPERTS = N_GROUPS * EXPERTS_PER_GROUP
D_EXPERT = 256
ALPHA = (2.0 * DEPTH) ** 0.25
LN_EPS = 1e-5
D_QKV = 3 * W_SB + 3 * W_CA
D_IN = D_QKV + 2 * D_MODEL
NEG_INF = -1e30

LANES = 128
HEADS_PER_TILE = LANES // HEAD_DIM
ROUTER_LANES = LANES
VMEM_LIMIT = 56 * 1024 * 1024

BF16 = jnp.bfloat16
F32 = jnp.float32

_NT = (((1,), (1,)), ((), ()))


def _dot(a, b):
    return jnp.dot(a, b, preferred_element_type=F32)


def _layer_norm(h, g, b):
    mu = jnp.mean(h, axis=-1, keepdims=True)
    hc = h - mu
    var = jnp.mean(hc * hc, axis=-1, keepdims=True)
    return hc * lax.rsqrt(var + LN_EPS) * g + b


def _split_bf16(a):
    hi = a.astype(BF16)
    lo = (a - hi.astype(F32)).astype(BF16)
    return hi, lo


def _in_proj_kernel(x_ref, w_ref, scale_ref, bg_ref, qkv_ref, gate_ref):
    xb = x_ref[...].astype(BF16)
    for c in range(D_QKV // D_MODEL):
        cols = slice(c * D_MODEL, (c + 1) * D_MODEL)
        acc = _dot(xb, w_ref[:, cols])
        qkv_ref[:, cols] = (acc * scale_ref[:, cols]).astype(BF16)
    for c in range(2):
        cols = slice(c * D_MODEL, (c + 1) * D_MODEL)
        wcols = slice(D_QKV + c * D_MODEL, D_QKV + (c + 1) * D_MODEL)
        logit = _dot(xb, w_ref[:, wcols]) + bg_ref[:, cols]
        gate_ref[:, cols] = 1.0 / (1.0 + jnp.exp(-logit))


def _in_proj(x2d, w_in_bf16, qscale, b_gate_row, tm=512):
    n = x2d.shape[0]
    return pl.pallas_call(
        _in_proj_kernel,
        grid=(n // tm,),
        in_specs=[
            pl.BlockSpec((tm, D_MODEL), lambda i: (i, 0)),
            pl.BlockSpec((D_MODEL, D_IN), lambda i: (0, 0)),
            pl.BlockSpec((1, D_QKV), lambda i: (0, 0)),
            pl.BlockSpec((1, 2 * D_MODEL), lambda i: (0, 0)),
        ],
        out_specs=[
            pl.BlockSpec((tm, D_QKV), lambda i: (i, 0)),
            pl.BlockSpec((tm, 2 * D_MODEL), lambda i: (i, 0)),
        ],
        out_shape=[
            jax.ShapeDtypeStruct((n, D_QKV), BF16),
            jax.ShapeDtypeStruct((n, 2 * D_MODEL), F32),
        ],
        compiler_params=pltpu.CompilerParams(
            dimension_semantics=("arbitrary",), vmem_limit_bytes=VMEM_LIMIT),
        name="in_proj",
    )(x2d, w_in_bf16, qscale, b_gate_row)


SB_DEAD_BITS = 160.0
SB_STEP_BLOCKS = 4


def _sb_kernel(q_ref, k_ref, v_ref, u_ref, o_ref, z_ref, arg_ref, rs_ref, acc_ref, car_ref, *, tq):
    step = pl.program_id(2)
    lane_q = lax.broadcasted_iota(jnp.int32, (tq, LANES), 1)

    def split_heads(x):
        zero = jnp.zeros_like(x)
        return jnp.where(lane_q < HEAD_DIM, x, zero), jnp.where(lane_q < HEAD_DIM, zero, x)

    blocks = [step * SB_STEP_BLOCKS + j for j in range(SB_STEP_BLOCKS)]
    q_heads = [split_heads(q_ref[pl.ds(j * tq, tq), :]) for j in range(SB_STEP_BLOCKS)]

    def key_rows(i, n):
        return pl.ds(pl.multiple_of(jnp.maximum(i - n, 0) * tq, tq), tq)

    def block_pairs(subs, n, diagonal):
        for j in subs:
            for b in range(2):
                k = k_ref[key_rows(blocks[j], n + b), :]
                for h in range(HEADS_PER_TILE):
                    z = lax.dot_general(q_heads[j][h], k, _NT, preferred_element_type=F32)
                    if diagonal and b == 0:
                        row = lax.broadcasted_iota(jnp.int32, (tq, tq), 0)
                        col = lax.broadcasted_iota(jnp.int32, (tq, tq), 1)
                        z = jnp.where(col < row, z, NEG_INF)
                    z_ref[j, b, h] = z

        def terms(j):
            for b in range(2):
                for h in range(HEADS_PER_TILE):
                    z = z_ref[j, b, h]
                    sp = jnp.maximum(z, 0.0) + jnp.log2(1.0 + jnp.exp2(-jnp.abs(z)))
                    sums = _dot(sp.astype(BF16), u_ref[...])
                    arg_ref[j, b, h] = (z - sp) - sums[:, 0:tq]
                    rs_ref[j, b, h] = sums[:, tq:tq + LANES]

        def apply(j):
            carries = [car_ref[j]]
            for b in range(2):
                carries.append(carries[b] + rs_ref[j, b])
            car_ref[j] = carries[2]
            v_parts = []
            for b in range(2):
                v = v_ref[key_rows(blocks[j], n + b), :]
                if b == 1:
                    v = jnp.where(n + b <= blocks[j], v, jnp.zeros_like(v))
                v_parts += split_heads(v)
            ws = []
            for b in range(2):
                for h in range(HEADS_PER_TILE):
                    carry = jnp.concatenate([carries[b][h]] * (tq // LANES), axis=1)
                    ws.append(jnp.exp2(arg_ref[j, b, h] - carry).astype(BF16))
            acc_ref[j] += _dot(jnp.concatenate(ws, axis=1), jnp.concatenate(v_parts, axis=0))
            return jnp.min(carries[2])

        subs = list(subs)
        least = []
        terms(subs[0])
        for prev, j in zip(subs, subs[1:]):
            terms(j)
            least.append(apply(prev))
        least.append(apply(subs[-1]))
        return least

    acc_ref[...] = jnp.zeros_like(acc_ref)
    car_ref[...] = jnp.zeros_like(car_ref)
    first = block_pairs(range(SB_STEP_BLOCKS), 0, True)

    for j in range(SB_STEP_BLOCKS):
        def more(state, j=j):
            n, least = state
            return (n <= blocks[j]) & (least < SB_DEAD_BITS)

        def body(state, j=j):
            n, _ = state
            return n + 2, block_pairs([j], n, False)[0]

        lax.while_loop(more, body, (jnp.int32(2), first[j]))
        o_ref[pl.ds(j * tq, tq), :] = acc_ref[j].astype(o_ref.dtype)


def _sb_attn(qkv, b, tq=256):
    s = qkv.shape[0] // b
    n_tiles = W_SB // LANES
    step_rows = SB_STEP_BLOCKS * tq
    steps = s // step_rows
    u = jnp.asarray(np.concatenate([np.arange(tq)[:, None] > np.arange(tq)[None, :],
                                    np.ones((tq, LANES), bool)], axis=1), dtype=BF16)
    return pl.pallas_call(
        functools.partial(_sb_kernel, tq=tq),
        grid=(b, n_tiles, steps),
        in_specs=[
            pl.BlockSpec((step_rows, LANES), lambda bi, hp, i: (bi * steps + i, hp)),
            pl.BlockSpec((s, LANES), lambda bi, hp, i: (bi, n_tiles + hp)),
            pl.BlockSpec((s, LANES), lambda bi, hp, i: (bi, 2 * n_tiles + hp)),
            pl.BlockSpec((tq, tq + LANES), lambda bi, hp, i: (0, 0)),
        ],
        out_specs=pl.BlockSpec((step_rows, LANES), lambda bi, hp, i: (bi * steps + i, hp)),
        out_shape=jax.ShapeDtypeStruct((b * s, W_SB), BF16),
        scratch_shapes=[
            pltpu.VMEM((SB_STEP_BLOCKS, 2, HEADS_PER_TILE, tq, tq), F32),
            pltpu.VMEM((SB_STEP_BLOCKS, 2, HEADS_PER_TILE, tq, tq), F32),
            pltpu.VMEM((SB_STEP_BLOCKS, 2, HEADS_PER_TILE, tq, LANES), F32),
            pltpu.VMEM((SB_STEP_BLOCKS, tq, LANES), F32),
            pltpu.VMEM((SB_STEP_BLOCKS, HEADS_PER_TILE, tq, LANES), F32),
        ],
        compiler_params=pltpu.CompilerParams(
            dimension_semantics=("arbitrary", "arbitrary", "arbitrary"), vmem_limit_bytes=VMEM_LIMIT),
        name="sb_attn",
    )(qkv, qkv, qkv, u)


CA_GROUP = 4
CA_STEP_GROUPS = 8
CA_TQ = CA_GROUP * CHUNK
CA_BAND = (CA_GROUP + N_PAST_CHUNKS) * CHUNK
CA_PAD = N_PAST_CHUNKS * CHUNK


def _ca_kernel(q_ref, k_ref, v_ref, line_ref, ones_ref, o_ref, kp_ref, vp_ref, bias_ref):
    c = pl.program_id(2)
    s = k_ref.shape[0]

    @pl.when(c == 0)
    def _():
        kp_ref[0:CA_PAD, :] = jnp.zeros((CA_PAD, LANES), BF16)
        vp_ref[0:CA_PAD, :] = jnp.zeros((CA_PAD, LANES), BF16)
        kp_ref[CA_PAD:CA_PAD + s, :] = k_ref[...]
        vp_ref[CA_PAD:CA_PAD + s, :] = v_ref[...]
        r = lax.broadcasted_iota(jnp.int32, (CA_TQ, CA_BAND), 0)
        p = lax.broadcasted_iota(jnp.int32, (CA_TQ, CA_BAND), 1)
        shift = CHUNK.bit_length() - 1
        qc = lax.shift_right_logical(r, shift)
        kc = lax.shift_right_logical(p, shift)
        in_band = (kc >= qc) & (kc <= qc + N_PAST_CHUNKS)
        for h in range(HEADS_PER_TILE):
            rows = jnp.broadcast_to(line_ref[h], (CA_TQ, CA_LINE))
            skew = pltpu.roll(rows, 1, axis=1, stride=1, stride_axis=0)
            bias_ref[h] = jnp.where(in_band, skew[:, CA_TQ:CA_TQ + CA_BAND], NEG_INF)

    lane_q = lax.broadcasted_iota(jnp.int32, (CA_TQ, LANES), 1)
    lane_v = lax.broadcasted_iota(jnp.int32, (CA_BAND, LANES), 1)

    def band_start(g):
        return pl.multiple_of((c * CA_STEP_GROUPS + g) * CA_TQ, CA_TQ)

    def scores(g):
        kb = kp_ref[pl.ds(band_start(g), CA_BAND), :]
        q = q_ref[pl.ds(g * CA_TQ, CA_TQ), :]
        zero_q = jnp.zeros_like(q)
        q_heads = (jnp.where(lane_q < HEAD_DIM, q, zero_q), jnp.where(lane_q < HEAD_DIM, zero_q, q))
        return [lax.dot_general(q_heads[h], kb, _NT, preferred_element_type=F32) for h in range(HEADS_PER_TILE)]

    def attend(g, qk, masked):
        group = c * CA_STEP_GROUPS + g
        vb = vp_ref[pl.ds(band_start(g), CA_BAND), :]
        zero_v = jnp.zeros_like(vb)
        v_heads = jnp.concatenate(
            [jnp.where(lane_v < HEAD_DIM, vb, zero_v), jnp.where(lane_v < HEAD_DIM, zero_v, vb)], axis=0)
        v_and_ones = jnp.concatenate([v_heads, ones_ref[...]], axis=1)
        es = []
        for h in range(HEADS_PER_TILE):
            sc = qk[h] + bias_ref[h]
            if masked:
                pos = lax.broadcasted_iota(jnp.int32, (CA_TQ, CA_BAND), 1)
                sc = jnp.where(pos >= CA_PAD - group * CA_TQ, sc, NEG_INF)
            m = jnp.max(sc, axis=1, keepdims=True)
            es.append(jnp.exp2(sc - m).astype(BF16))
        both = _dot(jnp.concatenate(es, axis=1), v_and_ones)
        o_ref[pl.ds(g * CA_TQ, CA_TQ), :] = (both[:, 0:LANES] / both[:, LANES:2 * LANES]).astype(o_ref.dtype)

    def step(masked):
        qk = scores(0)
        for g in range(CA_STEP_GROUPS):
            qk_next = scores(g + 1) if g + 1 < CA_STEP_GROUPS else None
            attend(g, qk, masked)
            qk = qk_next

    assert CA_PAD // CA_TQ <= CA_STEP_GROUPS
    pl.when(c == 0)(lambda: step(True))
    pl.when(c > 0)(lambda: step(False))


CA_LINE = CA_TQ + CA_BAND


def _ca_bias_line(rel_bias):
    h = rel_bias.shape[0]
    n_far = CA_PAD + CA_TQ - 1 - REL_CLIP
    n_neg = CA_BAND - 1 - CA_PAD - REL_CLIP
    rb = rel_bias.astype(F32) * np.float32(np.log2(np.e))
    line = jnp.concatenate([jnp.broadcast_to(rb[:, 2 * REL_CLIP:], (h, n_far)), rb[:, ::-1],
                            jnp.broadcast_to(rb[:, :1], (h, n_neg + 1))], axis=1)
    assert line.shape[1] == CA_LINE
    return line.reshape(h, 1, CA_LINE)


def _ca_attn(qkv, b, bias_line):
    s = qkv.shape[0] // b
    n_tiles = W_CA // LANES
    base = 3 * W_SB // LANES
    step_rows = CA_STEP_GROUPS * CA_TQ
    steps = s // step_rows
    head_of_row = np.arange(HEADS_PER_TILE * CA_BAND)[:, None] // CA_BAND
    head_of_lane = np.arange(LANES)[None, :] // HEAD_DIM
    ones = jnp.asarray(head_of_row == head_of_lane, dtype=BF16)
    return pl.pallas_call(
        _ca_kernel,
        grid=(b, n_tiles, steps),
        in_specs=[
            pl.BlockSpec((step_rows, LANES), lambda bi, hp, c: (bi * steps + c, base + hp)),
            pl.BlockSpec((s, LANES), lambda bi, hp, c: (bi, base + n_tiles + hp)),
            pl.BlockSpec((s, LANES), lambda bi, hp, c: (bi, base + 2 * n_tiles + hp)),
            pl.BlockSpec((HEADS_PER_TILE, 1, CA_LINE), lambda bi, hp, c: (hp, 0, 0)),
            pl.BlockSpec((HEADS_PER_TILE * CA_BAND, LANES), lambda bi, hp, c: (0, 0)),
        ],
        out_specs=pl.BlockSpec((step_rows, LANES), lambda bi, hp, c: (bi * steps + c, hp)),
        out_shape=jax.ShapeDtypeStruct((b * s, W_CA), BF16),
        scratch_shapes=[
            pltpu.VMEM((CA_PAD + s, LANES), BF16),
            pltpu.VMEM((CA_PAD + s, LANES), BF16),
            pltpu.VMEM((HEADS_PER_TILE, CA_TQ, CA_BAND), F32),
        ],
        compiler_params=pltpu.CompilerParams(
            dimension_semantics=("arbitrary", "arbitrary", "arbitrary"), vmem_limit_bytes=VMEM_LIMIT),
        name="ca_attn",
    )(qkv, qkv, qkv, bias_line, ones)


ROUTE_E1, ROUTE_E2, ROUTE_W1, ROUTE_W2 = 0, 1, 2, 3


def _route(lg):
    lane = lax.broadcasted_iota(jnp.int32, lg.shape, 1)
    big = jnp.int32(ROUTER_LANES)
    is_group = lane < N_GROUPS
    g_max = jnp.max(jnp.where(is_group, lg, -jnp.inf), axis=1, keepdims=True)
    g_idx = jnp.min(jnp.where(is_group & (lg == g_max), lane, big), axis=1, keepdims=True)
    g_den = jnp.sum(jnp.where(is_group, jnp.exp(lg - g_max), 0.0), axis=1, keepdims=True)
    g_val = 1.0 / g_den
    lo = N_GROUPS + EXPERTS_PER_GROUP * g_idx
    in_group = (lane >= lo) & (lane < lo + EXPERTS_PER_GROUP)
    v1 = jnp.max(jnp.where(in_group, lg, -jnp.inf), axis=1, keepdims=True)
    i1 = jnp.min(jnp.where(in_group & (lg == v1), lane, big), axis=1, keepdims=True)
    rest = in_group & (lane != i1)
    v2 = jnp.max(jnp.where(rest, lg, -jnp.inf), axis=1, keepdims=True)
    i2 = jnp.min(jnp.where(rest & (lg == v2), lane, big), axis=1, keepdims=True)
    e2 = jnp.exp(v2 - v1)
    w1 = g_val / (1.0 + e2)
    w2 = g_val * e2 / (1.0 + e2)
    picked = ((lane == i1) | (lane == i2)).astype(BF16)
    info = (jnp.where(lane == ROUTE_E1, (i1 - N_GROUPS).astype(F32), 0.0)
            + jnp.where(lane == ROUTE_E2, (i2 - N_GROUPS).astype(F32), 0.0)
            + jnp.where(lane == ROUTE_W1, w1, 0.0) + jnp.where(lane == ROUTE_W2, w2, 0.0))
    return info, picked


def _post_kernel(ysb_ref, yca_ref, gate_ref, x_ref, wsb_ref, wca_ref, wout_ref, wr_ref, br_ref,
                 g_ref, b_ref, x1_ref, info_ref, pick_ref, cnt_ref):
    @pl.when(pl.program_id(0) == 0)
    def _():
        cnt_ref[...] = jnp.zeros_like(cnt_ref)

    w_hi, w_lo = _split_bf16(wr_ref[...])
    w_split = jnp.concatenate([w_hi, w_lo], axis=1)
    tm = x_ref.shape[0]
    subs = [pl.ds(s * POST_SUB, POST_SUB) for s in range(tm // POST_SUB)]
    for rows in subs:
        a = _dot(ysb_ref[rows, :], wsb_ref[...])
        c = _dot(yca_ref[rows, :], wca_ref[...])
        mix = gate_ref[rows, 0:D_MODEL] * a + gate_ref[rows, D_MODEL:2 * D_MODEL] * c
        x1_ref[rows, :] = ALPHA * x_ref[rows, :] + _dot(mix.astype(BF16), wout_ref[...])
    for rows in subs:
        x1 = _layer_norm(x1_ref[rows, :], g_ref[...], b_ref[...])
        x1_ref[rows, :] = x1
        x_hi, x_lo = _split_bf16(x1)
        parts = _dot(jnp.concatenate([x_hi, x_lo], axis=0), w_split)
        lg = (parts[0:POST_SUB, 0:ROUTER_LANES] + parts[0:POST_SUB, ROUTER_LANES:]
              + parts[POST_SUB:, 0:ROUTER_LANES] + parts[POST_SUB:, ROUTER_LANES:]) + br_ref[...]
        info, pick = _route(lg)
        info_ref[rows, :] = info
        pick_ref[rows, :] = pick
        cnt_ref[...] += jnp.sum(pick.astype(F32), axis=0, keepdims=True)


POST_SUB = 256


def _post_attn(y_sb, y_ca, gates, x2d, w_br_sb, w_br_ca, w_out, w_router, b_router, ln_g, ln_b, tm=1024):
    n = x2d.shape[0]
    row = lambda i: (i, 0)
    fixed = lambda i: (0, 0)
    return pl.pallas_call(
        _post_kernel,
        grid=(n // tm,),
        in_specs=[
            pl.BlockSpec((tm, W_SB), row),
            pl.BlockSpec((tm, W_CA), row),
            pl.BlockSpec((tm, 2 * D_MODEL), row),
            pl.BlockSpec((tm, D_MODEL), row),
            pl.BlockSpec((W_SB, D_MODEL), fixed),
            pl.BlockSpec((W_CA, D_MODEL), fixed),
            pl.BlockSpec((D_MODEL, D_MODEL), fixed),
            pl.BlockSpec((D_MODEL, ROUTER_LANES), fixed),
            pl.BlockSpec((1, ROUTER_LANES), fixed),
            pl.BlockSpec((1, D_MODEL), fixed),
            pl.BlockSpec((1, D_MODEL), fixed),
        ],
        out_specs=[
            pl.BlockSpec((tm, D_MODEL), row),
            pl.BlockSpec((tm, ROUTER_LANES), row),
            pl.BlockSpec((tm, ROUTER_LANES), row),
            pl.BlockSpec((8, ROUTER_LANES), fixed),
        ],
        out_shape=[
            jax.ShapeDtypeStruct((n, D_MODEL), F32),
            jax.ShapeDtypeStruct((n, ROUTER_LANES), F32),
            jax.ShapeDtypeStruct((n, ROUTER_LANES), BF16),
            jax.ShapeDtypeStruct((8, ROUTER_LANES), F32),
        ],
        compiler_params=pltpu.CompilerParams(
            dimension_semantics=("arbitrary",), vmem_limit_bytes=VMEM_LIMIT),
        name="post_attn",
    )(y_sb, y_ca, gates, x2d, w_br_sb, w_br_ca, w_out, w_router, b_router, ln_g, ln_b)


MOE_TILE = 512
TOP_K = 2


def _moe_rows(n):
    return n * TOP_K + N_EXPERTS * MOE_TILE


def _rank_kernel(pick_ref, info_ref, tri_ref, cnt_ref, d1_ref, d2_ref, run_ref, off_ref, *, tb):
    i = pl.program_id(0)
    pick = pick_ref[...]

    @pl.when(i == 0)
    def _():
        cnt = cnt_ref[...]
        padded = jnp.ceil(cnt * (1.0 / MOE_TILE)) * MOE_TILE
        lane = lax.broadcasted_iota(jnp.int32, padded.shape, 1)
        scan = padded
        step = 1
        while step < ROUTER_LANES:
            scan = scan + jnp.where(lane >= step, pltpu.roll(scan, step, axis=1), 0.0)
            step *= 2
        off_ref[...] = scan - padded
        run_ref[...] = jnp.zeros_like(run_ref)

    seen = run_ref[0:1, :]
    earlier = _dot(tri_ref[...], pick)
    row_of = earlier + seen + off_ref[0:1, :]
    info = info_ref[...]
    lane = lax.broadcasted_iota(jnp.int32, info.shape, 1)
    lane_f = lane.astype(F32)
    e1 = jnp.sum(jnp.where(lane == ROUTE_E1, info, 0.0), axis=1, keepdims=True)
    e2 = jnp.sum(jnp.where(lane == ROUTE_E2, info, 0.0), axis=1, keepdims=True)
    d1 = jnp.sum(jnp.where(lane_f == e1 + N_GROUPS, row_of, 0.0), axis=1, keepdims=True)
    d2 = jnp.sum(jnp.where(lane_f == e2 + N_GROUPS, row_of, 0.0), axis=1, keepdims=True)
    eye = (lax.broadcasted_iota(jnp.int32, (LANES, LANES), 0) == lax.broadcasted_iota(jnp.int32, (LANES, LANES), 1))
    for blk in range(tb // LANES):
        for d, d_ref in ((d1, d1_ref), (d2, d2_ref)):
            column = d[blk * LANES:(blk + 1) * LANES]
            d_ref[blk:blk + 1, :] = jnp.sum(jnp.where(eye, column, 0.0), axis=0, keepdims=True).astype(jnp.int32)
    run_ref[...] += jnp.sum(pick.astype(F32), axis=0, keepdims=True)


def _moe_rank(pick, info, counts, tb=1024):
    n = pick.shape[0]
    tri = jnp.asarray(np.arange(tb)[None, :] < np.arange(tb)[:, None], dtype=BF16)
    dest_spec = pl.BlockSpec((tb // LANES, LANES), lambda i: (i, 0))
    dest_shape = jax.ShapeDtypeStruct((n // LANES, LANES), jnp.int32)
    return pl.pallas_call(
        functools.partial(_rank_kernel, tb=tb),
        grid=(n // tb,),
        in_specs=[
            pl.BlockSpec((tb, ROUTER_LANES), lambda i: (i, 0)),
            pl.BlockSpec((tb, ROUTER_LANES), lambda i: (i, 0)),
            pl.BlockSpec((tb, tb), lambda i: (0, 0)),
            pl.BlockSpec((8, ROUTER_LANES), lambda i: (0, 0)),
        ],
        out_specs=[dest_spec, dest_spec],
        out_shape=[dest_shape, dest_shape],
        scratch_shapes=[
            pltpu.VMEM((8, ROUTER_LANES), F32),
            pltpu.VMEM((8, ROUTER_LANES), F32),
        ],
        compiler_params=pltpu.CompilerParams(
            dimension_semantics=("arbitrary",), vmem_limit_bytes=VMEM_LIMIT),
        name="moe_rank",
    )(pick, info, tri, counts)


def _row_copy(src_ref, src_row, dst_ref, dst_row, sem):
    return pltpu.make_async_copy(src_ref.at[pl.ds(src_row, 1), :], dst_ref.at[pl.ds(dst_row, 1), :], sem)


def _dispatch_kernel(d1_ref, d2_ref, last_ref, x_ref, xs_hbm, zero_ref, sem, zsem, *, tb):
    @pl.when(pl.program_id(0) == 0)
    def _():
        zero_ref[...] = jnp.zeros_like(zero_ref)

        def last_tile_copy(e):
            first_row = pl.multiple_of((last_ref[e] - 1) * MOE_TILE, MOE_TILE)
            return pltpu.make_async_copy(zero_ref, xs_hbm.at[pl.ds(first_row, MOE_TILE), :], zsem)

        def unused_tile_copy(t):
            return pltpu.make_async_copy(zero_ref, xs_hbm.at[pl.ds(t * MOE_TILE, MOE_TILE), :], zsem)

        n_tiles = xs_hbm.shape[0] // MOE_TILE
        min_used = n_tiles - N_EXPERTS
        for wait in (False, True):
            for e in range(N_EXPERTS):
                owns_tiles = last_ref[e] > (last_ref[e - 1] if e else 0)
                pl.when(owns_tiles)(
                    lambda e=e, wait=wait: last_tile_copy(e).wait() if wait else last_tile_copy(e).start())
            for t in range(min_used, n_tiles):
                pl.when(t >= last_ref[N_EXPERTS - 1])(
                    lambda t=t, wait=wait: unused_tile_copy(t).wait() if wait else unused_tile_copy(t).start())

    first = pl.program_id(0) * (tb // LANES)
    for r in range(tb):
        hi, lo = first + r // LANES, r % LANES
        _row_copy(x_ref, r, xs_hbm, d1_ref[hi, lo], sem).start(priority=0)
        _row_copy(x_ref, r, xs_hbm, d2_ref[hi, lo], sem).start(priority=1)
    for _ in range(TOP_K):
        pltpu.make_async_copy(x_ref, xs_hbm.at[pl.ds(0, tb), :], sem).wait()


def _moe_dispatch(x1, dest1, dest2, last_tile, tb=512):
    n = x1.shape[0]
    rows = _moe_rows(n)
    grid_spec = pltpu.PrefetchScalarGridSpec(
        num_scalar_prefetch=3,
        grid=(n // tb,),
        in_specs=[pl.BlockSpec((tb, D_MODEL), lambda i, d1, d2, last: (i, 0))],
        out_specs=pl.BlockSpec(memory_space=pl.ANY),
        scratch_shapes=[
            pltpu.VMEM((MOE_TILE, D_MODEL), F32),
            pltpu.SemaphoreType.DMA(()),
            pltpu.SemaphoreType.DMA(()),
        ],
    )
    return pl.pallas_call(
        functools.partial(_dispatch_kernel, tb=tb),
        grid_spec=grid_spec,
        out_shape=jax.ShapeDtypeStruct((rows, D_MODEL), F32),
        compiler_params=pltpu.CompilerParams(
            dimension_semantics=("arbitrary",), vmem_limit_bytes=VMEM_LIMIT),
        name="moe_dispatch",
    )(dest1, dest2, last_tile, x1)


def _experts_kernel(te_ref, nt_ref, xs_ref, wg_ref, wu_ref, wd_ref, ys_ref, wgu_s, wd_s):
    t = pl.program_id(0)
    changed = (t == 0) | (te_ref[t] != te_ref[jnp.maximum(t - 1, 0)])

    @pl.when(changed)
    def _():
        wgu_s[:, 0:D_EXPERT] = wg_ref[...].astype(BF16)
        wgu_s[:, D_EXPERT:2 * D_EXPERT] = wu_ref[...].astype(BF16)
        wd_s[...] = wd_ref[...].astype(BF16)

    @pl.when(t < nt_ref[0])
    def _():
        gu = _dot(xs_ref[...].astype(BF16), wgu_s[...])
        gate = gu[:, 0:D_EXPERT]
        up = gu[:, D_EXPERT:2 * D_EXPERT]
        hid = (gate * (1.0 / (1.0 + jnp.exp(-gate)))) * up
        ys_ref[...] = _dot(hid.astype(BF16), wd_s[...])

    @pl.when(t >= nt_ref[0])
    def _():
        ys_ref[...] = jnp.zeros_like(ys_ref)


def _moe_experts(xs, tile_expert, n_tiles_used, w_gate, w_up, w_down, layer):
    rows = xs.shape[0]
    first = layer * N_EXPERTS
    grid_spec = pltpu.PrefetchScalarGridSpec(
        num_scalar_prefetch=2,
        grid=(rows // MOE_TILE,),
        in_specs=[
            pl.BlockSpec((MOE_TILE, D_MODEL), lambda t, te, nt: (jnp.minimum(t, nt[0] - 1), 0)),
            pl.BlockSpec((None, D_MODEL, D_EXPERT), lambda t, te, nt: (first + te[t], 0, 0)),
            pl.BlockSpec((None, D_MODEL, D_EXPERT), lambda t, te, nt: (first + te[t], 0, 0)),
            pl.BlockSpec((None, D_EXPERT, D_MODEL), lambda t, te, nt: (first + te[t], 0, 0)),
        ],
        out_specs=pl.BlockSpec((MOE_TILE, D_MODEL), lambda t, te, nt: (t, 0)),
        scratch_shapes=[
            pltpu.VMEM((D_MODEL, 2 * D_EXPERT), BF16),
            pltpu.VMEM((D_EXPERT, D_MODEL), BF16),
        ],
    )
    return pl.pallas_call(
        _experts_kernel,
        grid_spec=grid_spec,
        out_shape=jax.ShapeDtypeStruct((rows, D_MODEL), F32),
        compiler_params=pltpu.CompilerParams(
            dimension_semantics=("arbitrary",), vmem_limit_bytes=VMEM_LIMIT),
        name="moe_experts",
    )(tile_expert, n_tiles_used, xs, w_gate, w_up, w_down)


def _combine_kernel(d1_ref, d2_ref, x1_ref, info_ref, ys_hbm, g_ref, b_ref, out_ref, y_ref, sems, *, tb):
    i = pl.program_id(0)
    last_tile = COMBINE_SLOTS * pl.num_programs(0) - 1

    def start_gather(tile, slot):
        first = tile * (tb // LANES)
        for r in range(tb):
            hi, lo = first + r // LANES, r % LANES
            _row_copy(ys_hbm, d1_ref[hi, lo], y_ref.at[slot, 0], r, sems.at[slot]).start(priority=0)
            _row_copy(ys_hbm, d2_ref[hi, lo], y_ref.at[slot, 1], r, sems.at[slot]).start(priority=1)

    def wait_gather(slot):
        for k in range(TOP_K):
            pltpu.make_async_copy(ys_hbm.at[pl.ds(0, tb), :], y_ref.at[slot, k], sems.at[slot]).wait()

    def finish(slot):
        rows = pl.ds(slot * tb, tb)
        info = info_ref[rows, :]
        lane = lax.broadcasted_iota(jnp.int32, info.shape, 1)
        w1 = jnp.sum(jnp.where(lane == ROUTE_W1, info, 0.0), axis=1, keepdims=True)
        w2 = jnp.sum(jnp.where(lane == ROUTE_W2, info, 0.0), axis=1, keepdims=True)
        ffn = w1 * y_ref[slot, 0] + w2 * y_ref[slot, 1]
        out_ref[rows, :] = _layer_norm(ALPHA * x1_ref[rows, :] + ffn, g_ref[...], b_ref[...])

    @pl.when(i == 0)
    def _():
        for s in range(COMBINE_AHEAD):
            start_gather(s, s)

    for s in range(COMBINE_SLOTS):
        wait_gather(s)
        ahead = jnp.minimum(COMBINE_SLOTS * i + s + COMBINE_AHEAD, last_tile)
        start_gather(ahead, (s + COMBINE_AHEAD) % COMBINE_SLOTS)
        finish(s)

    @pl.when(i == pl.num_programs(0) - 1)
    def _():
        for s in range(COMBINE_AHEAD):
            wait_gather(s)


COMBINE_SLOTS = 4
COMBINE_AHEAD = 2


def _moe_combine(x1, info, ys, dest1, dest2, ln_g, ln_b, tb=256):
    n = x1.shape[0]
    step_rows = COMBINE_SLOTS * tb
    grid_spec = pltpu.PrefetchScalarGridSpec(
        num_scalar_prefetch=2,
        grid=(n // step_rows,),
        in_specs=[
            pl.BlockSpec((step_rows, D_MODEL), lambda i, d1, d2: (i, 0)),
            pl.BlockSpec((step_rows, ROUTER_LANES), lambda i, d1, d2: (i, 0)),
            pl.BlockSpec(memory_space=pl.ANY),
            pl.BlockSpec((1, D_MODEL), lambda i, d1, d2: (0, 0)),
            pl.BlockSpec((1, D_MODEL), lambda i, d1, d2: (0, 0)),
        ],
        out_specs=pl.BlockSpec((step_rows, D_MODEL), lambda i, d1, d2: (i, 0)),
        scratch_shapes=[
            pltpu.VMEM((COMBINE_SLOTS, TOP_K, tb, D_MODEL), F32),
            pltpu.SemaphoreType.DMA((COMBINE_SLOTS,)),
        ],
    )
    return pl.pallas_call(
        functools.partial(_combine_kernel, tb=tb),
        grid_spec=grid_spec,
        out_shape=jax.ShapeDtypeStruct((n, D_MODEL), F32),
        compiler_params=pltpu.CompilerParams(
            dimension_semantics=("arbitrary",), vmem_limit_bytes=VMEM_LIMIT),
        name="moe_combine",
    )(dest1, dest2, x1, info, ys, ln_g, ln_b)


def _moe(x1, info, pick, counts, w_gate, w_up, w_down, layer, ln_g, ln_b):
    n = x1.shape[0]
    dest1, dest2 = _moe_rank(pick, info, counts)
    tiles = jnp.ceil(counts[0, N_GROUPS:N_GROUPS + N_EXPERTS] * (1.0 / MOE_TILE)).astype(jnp.int32)
    upto = np.arange(N_EXPERTS)[None, :] <= np.arange(N_EXPERTS)[:, None]
    last_tile = jnp.sum(jnp.where(upto, tiles[None, :], 0), axis=1)
    tile_ids = jnp.arange(_moe_rows(n) // MOE_TILE, dtype=jnp.int32)
    tile_expert = jnp.minimum(jnp.sum(tile_ids[:, None] >= last_tile[None, :], axis=1), N_EXPERTS - 1)
    xs = _moe_dispatch(x1, dest1, dest2, last_tile)
    ys = _moe_experts(xs, tile_expert.astype(jnp.int32), last_tile[N_EXPERTS - 1:], w_gate, w_up, w_down, layer)
    return _moe_combine(x1, info, ys, dest1, dest2, ln_g, ln_b)


def _layer(x2d, batch, w_in, b_gate, rel_bias, w_br_sb, w_br_ca, w_out, ln1_g, ln1_b,
           w_group, b_group, w_erouter, b_erouter, w_gate, w_up, w_down, layer, ln2_g, ln2_b):
    n = x2d.shape[0]
    seq = n // batch
    scale = HEAD_DIM ** -0.5
    qscale = np.ones((1, D_QKV), np.float32)
    qscale[:, 0:W_SB] = scale * np.log2(np.e)
    qscale[:, 3 * W_SB:3 * W_SB + W_CA] = scale * np.log2(np.e)
    qkv, gates = _in_proj(x2d, w_in.astype(BF16), jnp.asarray(qscale), b_gate.reshape(1, 2 * D_MODEL))
    y_sb = _sb_attn(qkv, batch)
    y_ca = _ca_attn(qkv, batch, _ca_bias_line(rel_bias))

    w_router = jnp.concatenate(
        [w_group, w_erouter.transpose(1, 0, 2).reshape(D_MODEL, N_EXPERTS)], axis=1)
    w_router = jnp.pad(w_router, ((0, 0), (0, ROUTER_LANES - N_GROUPS - N_EXPERTS)))
    b_router = jnp.pad(jnp.concatenate([b_group, b_erouter.reshape(N_EXPERTS)]),
                       (0, ROUTER_LANES - N_GROUPS - N_EXPERTS)).reshape(1, ROUTER_LANES)
    x1, info, pick, counts = _post_attn(y_sb, y_ca, gates, x2d, w_br_sb.astype(BF16), w_br_ca.astype(BF16),
                        w_out.astype(BF16), w_router, b_router,
                        ln1_g.reshape(1, D_MODEL), ln1_b.reshape(1, D_MODEL))

    return _moe(x1, info, pick, counts, w_gate, w_up, w_down, layer,
                ln2_g.reshape(1, D_MODEL), ln2_b.reshape(1, D_MODEL))


def kernel(x, w_in, b_gate, rel_bias, w_br_sb, w_br_ca, w_out, ln1_g, ln1_b, w_group, b_group,
           w_erouter, b_erouter, w_gate, w_up, w_down, ln2_g, ln2_b):
    batch, seq, d = x.shape
    h = x.reshape(batch * seq, d)
    w_gate = w_gate.reshape(DEPTH * N_EXPERTS, D_MODEL, D_EXPERT)
    w_up = w_up.reshape(DEPTH * N_EXPERTS, D_MODEL, D_EXPERT)
    w_down = w_down.reshape(DEPTH * N_EXPERTS, D_EXPERT, D_MODEL)
    for l in range(DEPTH):
        h = _layer(h, batch, w_in[l], b_gate[l], rel_bias[l], w_br_sb[l], w_br_ca[l], w_out[l],
                   ln1_g[l], ln1_b[l], w_group[l], b_group[l], w_erouter[l], b_erouter[l],
                   w_gate, w_up, w_down, l, ln2_g[l], ln2_b[l])
    return h.reshape(batch, seq, d)
```

```python
import functools

import jax
import jax.numpy as jnp
import numpy as np
from jax import lax
from jax.experimental import pallas as pl
from jax.experimental.pallas import tpu as pltpu

D_MODEL = 1024
DEPTH = 2
CHUNK = 64
HEAD_DIM = 64
H_SB = 8
H_CA = 8
W_SB = H_SB * HEAD_DIM
W_CA = H_CA * HEAD_DIM
N_PAST_CHUNKS = 8
REL_CLIP = 128
N_GROUPS = 4
EXPERTS_PER_GROUP = 8
N_EXPERTS = N_GROUPS * EXPERTS_PER_GROUP
D_EXPERT = 256
ALPHA = (2.0 * DEPTH) ** 0.25
LN_EPS = 1e-5
D_QKV = 3 * W_SB + 3 * W_CA
D_IN = D_QKV + 2 * D_MODEL
NEG_INF = -1e30

LANES = 128
HEADS_PER_TILE = LANES // HEAD_DIM
ROUTER_LANES = LANES
VMEM_LIMIT = 56 * 1024 * 1024

BF16 = jnp.bfloat16
F32 = jnp.float32

_NT = (((1,), (1,)), ((), ()))


def _dot(a, b):
    return jnp.dot(a, b, preferred_element_type=F32)


def _layer_norm(h, g, b):
    mu = jnp.mean(h, axis=-1, keepdims=True)
    hc = h - mu
    var = jnp.mean(hc * hc, axis=-1, keepdims=True)
    return hc * lax.rsqrt(var + LN_EPS) * g + b


def _split_bf16(a):
    hi = a.astype(BF16)
    lo = (a - hi.astype(F32)).astype(BF16)
    return hi, lo


def _in_proj_kernel(x_ref, w_ref, scale_ref, bg_ref, qkv_ref, gate_ref):
    xb = x_ref[...].astype(BF16)
    for c in range(D_QKV // D_MODEL):
        cols = slice(c * D_MODEL, (c + 1) * D_MODEL)
        acc = _dot(xb, w_ref[:, cols])
        qkv_ref[:, cols] = (acc * scale_ref[:, cols]).astype(BF16)
    for c in range(2):
        cols = slice(c * D_MODEL, (c + 1) * D_MODEL)
        wcols = slice(D_QKV + c * D_MODEL, D_QKV + (c + 1) * D_MODEL)
        logit = _dot(xb, w_ref[:, wcols]) + bg_ref[:, cols]
        gate_ref[:, cols] = 1.0 / (1.0 + jnp.exp(-logit))


def _in_proj(x2d, w_in_bf16, qscale, b_gate_row, tm=512):
    n = x2d.shape[0]
    return pl.pallas_call(
        _in_proj_kernel,
        grid=(n // tm,),
        in_specs=[
            pl.BlockSpec((tm, D_MODEL), lambda i: (i, 0)),
            pl.BlockSpec((D_MODEL, D_IN), lambda i: (0, 0)),
            pl.BlockSpec((1, D_QKV), lambda i: (0, 0)),
            pl.BlockSpec((1, 2 * D_MODEL), lambda i: (0, 0)),
        ],
        out_specs=[
            pl.BlockSpec((tm, D_QKV), lambda i: (i, 0)),
            pl.BlockSpec((tm, 2 * D_MODEL), lambda i: (i, 0)),
        ],
        out_shape=[
            jax.ShapeDtypeStruct((n, D_QKV), BF16),
            jax.ShapeDtypeStruct((n, 2 * D_MODEL), F32),
        ],
        compiler_params=pltpu.CompilerParams(
            dimension_semantics=("arbitrary",), vmem_limit_bytes=VMEM_LIMIT),
        name="in_proj",
    )(x2d, w_in_bf16, qscale, b_gate_row)


SB_DEAD_BITS = 160.0
SB_STEP_BLOCKS = 4


def _sb_kernel(q_ref, k_ref, v_ref, u_ref, o_ref, z_ref, arg_ref, rs_ref, acc_ref, car_ref, *, tq):
    step = pl.program_id(2)
    lane_q = lax.broadcasted_iota(jnp.int32, (tq, LANES), 1)

    def split_heads(x):
        zero = jnp.zeros_like(x)
        return jnp.where(lane_q < HEAD_DIM, x, zero), jnp.where(lane_q < HEAD_DIM, zero, x)

    blocks = [step * SB_STEP_BLOCKS + j for j in range(SB_STEP_BLOCKS)]
    q_heads = [split_heads(q_ref[pl.ds(j * tq, tq), :]) for j in range(SB_STEP_BLOCKS)]

    def key_rows(i, n):
        return pl.ds(pl.multiple_of(jnp.maximum(i - n, 0) * tq, tq), tq)

    def block_pairs(subs, n, diagonal):
        for j in subs:
            for b in range(2):
                k = k_ref[key_rows(blocks[j], n + b), :]
                for h in range(HEADS_PER_TILE):
                    z = lax.dot_general(q_heads[j][h], k, _NT, preferred_element_type=F32)
                    if diagonal and b == 0:
                        row = lax.broadcasted_iota(jnp.int32, (tq, tq), 0)
                        col = lax.broadcasted_iota(jnp.int32, (tq, tq), 1)
                        z = jnp.where(col < row, z, NEG_INF)
                    z_ref[j, b, h] = z

        def terms(j):
            for b in range(2):
                for h in range(HEADS_PER_TILE):
                    z = z_ref[j, b, h]
                    sp = jnp.maximum(z, 0.0) + jnp.log2(1.0 + jnp.exp2(-jnp.abs(z)))
                    sums = _dot(sp.astype(BF16), u_ref[...])
                    arg_ref[j, b, h] = (z - sp) - sums[:, 0:tq]
                    rs_ref[j, b, h] = sums[:, tq:tq + LANES]

        def apply(j):
            carries = [car_ref[j]]
            for b in range(2):
                carries.append(carries[b] + rs_ref[j, b])
            car_ref[j] = carries[2]
            v_parts = []
            for b in range(2):
                v = v_ref[key_rows(blocks[j], n + b), :]
                if b == 1:
                    v = jnp.where(n + b <= blocks[j], v, jnp.zeros_like(v))
                v_parts += split_heads(v)
            ws = []
            for b in range(2):
                for h in range(HEADS_PER_TILE):
                    carry = jnp.concatenate([carries[b][h]] * (tq // LANES), axis=1)
                    ws.append(jnp.exp2(arg_ref[j, b, h] - carry).astype(BF16))
            acc_ref[j] += _dot(jnp.concatenate(ws, axis=1), jnp.concatenate(v_parts, axis=0))
            return jnp.min(carries[2])

        subs = list(subs)
        least = []
        terms(subs[0])
        for prev, j in zip(subs, subs[1:]):
            terms(j)
            least.append(apply(prev))
        least.append(apply(subs[-1]))
        return least

    acc_ref[...] = jnp.zeros_like(acc_ref)
    car_ref[...] = jnp.zeros_like(car_ref)
    first = block_pairs(range(SB_STEP_BLOCKS), 0, True)

    for j in range(SB_STEP_BLOCKS):
        def more(state, j=j):
            n, least = state
            return (n <= blocks[j]) & (least < SB_DEAD_BITS)

        def body(state, j=j):
            n, _ = state
            return n + 2, block_pairs([j], n, False)[0]

        lax.while_loop(more, body, (jnp.int32(2), first[j]))
        o_ref[pl.ds(j * tq, tq), :] = acc_ref[j].astype(o_ref.dtype)


def _sb_attn(qkv, b, tq=256):
    s = qkv.shape[0] // b
    n_tiles = W_SB // LANES
    step_rows = SB_STEP_BLOCKS * tq
    steps = s // step_rows
    u = jnp.asarray(np.concatenate([np.arange(tq)[:, None] > np.arange(tq)[None, :],
                                    np.ones((tq, LANES), bool)], axis=1), dtype=BF16)
    return pl.pallas_call(
        functools.partial(_sb_kernel, tq=tq),
        grid=(b, n_tiles, steps),
        in_specs=[
            pl.BlockSpec((step_rows, LANES), lambda bi, hp, i: (bi * steps + i, hp)),
            pl.BlockSpec((s, LANES), lambda bi, hp, i: (bi, n_tiles + hp)),
            pl.BlockSpec((s, LANES), lambda bi, hp, i: (bi, 2 * n_tiles + hp)),
            pl.BlockSpec((tq, tq + LANES), lambda bi, hp, i: (0, 0)),
        ],
        out_specs=pl.BlockSpec((step_rows, LANES), lambda bi, hp, i: (bi * steps + i, hp)),
        out_shape=jax.ShapeDtypeStruct((b * s, W_SB), BF16),
        scratch_shapes=[
            pltpu.VMEM((SB_STEP_BLOCKS, 2, HEADS_PER_TILE, tq, tq), F32),
            pltpu.VMEM((SB_STEP_BLOCKS, 2, HEADS_PER_TILE, tq, tq), F32),
            pltpu.VMEM((SB_STEP_BLOCKS, 2, HEADS_PER_TILE, tq, LANES), F32),
            pltpu.VMEM((SB_STEP_BLOCKS, tq, LANES), F32),
            pltpu.VMEM((SB_STEP_BLOCKS, HEADS_PER_TILE, tq, LANES), F32),
        ],
        compiler_params=pltpu.CompilerParams(
            dimension_semantics=("arbitrary", "arbitrary", "arbitrary"), vmem_limit_bytes=VMEM_LIMIT),
        name="sb_attn",
    )(qkv, qkv, qkv, u)


CA_GROUP = 4
CA_STEP_GROUPS = 8
CA_TQ = CA_GROUP * CHUNK
CA_BAND = (CA_GROUP + N_PAST_CHUNKS) * CHUNK
CA_PAD = N_PAST_CHUNKS * CHUNK


def _ca_kernel(q_ref, k_ref, v_ref, line_ref, ones_ref, o_ref, kp_ref, vp_ref, bias_ref):
    c = pl.program_id(2)
    s = k_ref.shape[0]

    @pl.when(c == 0)
    def _():
        kp_ref[0:CA_PAD, :] = jnp.zeros((CA_PAD, LANES), BF16)
        vp_ref[0:CA_PAD, :] = jnp.zeros((CA_PAD, LANES), BF16)
        kp_ref[CA_PAD:CA_PAD + s, :] = k_ref[...]
        vp_ref[CA_PAD:CA_PAD + s, :] = v_ref[...]
        r = lax.broadcasted_iota(jnp.int32, (CA_TQ, CA_BAND), 0)
        p = lax.broadcasted_iota(jnp.int32, (CA_TQ, CA_BAND), 1)
        shift = CHUNK.bit_length() - 1
        qc = lax.shift_right_logical(r, shift)
        kc = lax.shift_right_logical(p, shift)
        in_band = (kc >= qc) & (kc <= qc + N_PAST_CHUNKS)
        for h in range(HEADS_PER_TILE):
            rows = jnp.broadcast_to(line_ref[h], (CA_TQ, CA_LINE))
            skew = pltpu.roll(rows, 1, axis=1, stride=1, stride_axis=0)
            bias_ref[h] = jnp.where(in_band, skew[:, CA_TQ:CA_TQ + CA_BAND], NEG_INF)

    lane_q = lax.broadcasted_iota(jnp.int32, (CA_TQ, LANES), 1)
    lane_v = lax.broadcasted_iota(jnp.int32, (CA_BAND, LANES), 1)

    def band_start(g):
        return pl.multiple_of((c * CA_STEP_GROUPS + g) * CA_TQ, CA_TQ)

    def scores(g):
        kb = kp_ref[pl.ds(band_start(g), CA_BAND), :]
        q = q_ref[pl.ds(g * CA_TQ, CA_TQ), :]
        zero_q = jnp.zeros_like(q)
        q_heads = (jnp.where(lane_q < HEAD_DIM, q, zero_q), jnp.where(lane_q < HEAD_DIM, zero_q, q))
        return [lax.dot_general(q_heads[h], kb, _NT, preferred_element_type=F32) for h in range(HEADS_PER_TILE)]

    def attend(g, qk, masked):
        group = c * CA_STEP_GROUPS + g
        vb = vp_ref[pl.ds(band_start(g), CA_BAND), :]
        zero_v = jnp.zeros_like(vb)
        v_heads = jnp.concatenate(
            [jnp.where(lane_v < HEAD_DIM, vb, zero_v), jnp.where(lane_v < HEAD_DIM, zero_v, vb)], axis=0)
        v_and_ones = jnp.concatenate([v_heads, ones_ref[...]], axis=1)
        es = []
        for h in range(HEADS_PER_TILE):
            sc = qk[h] + bias_ref[h]
            if masked:
                pos = lax.broadcasted_iota(jnp.int32, (CA_TQ, CA_BAND), 1)
                sc = jnp.where(pos >= CA_PAD - group * CA_TQ, sc, NEG_INF)
            m = jnp.max(sc, axis=1, keepdims=True)
            es.append(jnp.exp2(sc - m).astype(BF16))
        both = _dot(jnp.concatenate(es, axis=1), v_and_ones)
        o_ref[pl.ds(g * CA_TQ, CA_TQ), :] = (both[:, 0:LANES] / both[:, LANES:2 * LANES]).astype(o_ref.dtype)

    def step(masked):
        qk = scores(0)
        for g in range(CA_STEP_GROUPS):
            qk_next = scores(g + 1) if g + 1 < CA_STEP_GROUPS else None
            attend(g, qk, masked)
            qk = qk_next

    assert CA_PAD // CA_TQ <= CA_STEP_GROUPS
    pl.when(c == 0)(lambda: step(True))
    pl.when(c > 0)(lambda: step(False))


CA_LINE = CA_TQ + CA_BAND


def _ca_bias_line(rel_bias):
    h = rel_bias.shape[0]
    n_far = CA_PAD + CA_TQ - 1 - REL_CLIP
    n_neg = CA_BAND - 1 - CA_PAD - REL_CLIP
    rb = rel_bias.astype(F32) * np.float32(np.log2(np.e))
    line = jnp.concatenate([jnp.broadcast_to(rb[:, 2 * REL_CLIP:], (h, n_far)), rb[:, ::-1],
                            jnp.broadcast_to(rb[:, :1], (h, n_neg + 1))], axis=1)
    assert line.shape[1] == CA_LINE
    return line.reshape(h, 1, CA_LINE)


def _ca_attn(qkv, b, bias_line):
    s = qkv.shape[0] // b
    n_tiles = W_CA // LANES
    base = 3 * W_SB // LANES
    step_rows = CA_STEP_GROUPS * CA_TQ
    steps = s // step_rows
    head_of_row = np.arange(HEADS_PER_TILE * CA_BAND)[:, None] // CA_BAND
    head_of_lane = np.arange(LANES)[None, :] // HEAD_DIM
    ones = jnp.asarray(head_of_row == head_of_lane, dtype=BF16)
    return pl.pallas_call(
        _ca_kernel,
        grid=(b, n_tiles, steps),
        in_specs=[
            pl.BlockSpec((step_rows, LANES), lambda bi, hp, c: (bi * steps + c, base + hp)),
            pl.BlockSpec((s, LANES), lambda bi, hp, c: (bi, base + n_tiles + hp)),
            pl.BlockSpec((s, LANES), lambda bi, hp, c: (bi, base + 2 * n_tiles + hp)),
            pl.BlockSpec((HEADS_PER_TILE, 1, CA_LINE), lambda bi, hp, c: (hp, 0, 0)),
            pl.BlockSpec((HEADS_PER_TILE * CA_BAND, LANES), lambda bi, hp, c: (0, 0)),
        ],
        out_specs=pl.BlockSpec((step_rows, LANES), lambda bi, hp, c: (bi * steps + c, hp)),
        out_shape=jax.ShapeDtypeStruct((b * s, W_CA), BF16),
        scratch_shapes=[
            pltpu.VMEM((CA_PAD + s, LANES), BF16),
            pltpu.VMEM((CA_PAD + s, LANES), BF16),
            pltpu.VMEM((HEADS_PER_TILE, CA_TQ, CA_BAND), F32),
        ],
        compiler_params=pltpu.CompilerParams(
            dimension_semantics=("arbitrary", "arbitrary", "arbitrary"), vmem_limit_bytes=VMEM_LIMIT),
        name="ca_attn",
    )(qkv, qkv, qkv, bias_line, ones)


ROUTE_E1, ROUTE_E2, ROUTE_W1, ROUTE_W2 = 0, 1, 2, 3


def _route(lg):
    lane = lax.broadcasted_iota(jnp.int32, lg.shape, 1)
    big = jnp.int32(ROUTER_LANES)
    is_group = lane < N_GROUPS
    g_max = jnp.max(jnp.where(is_group, lg, -jnp.inf), axis=1, keepdims=True)
    g_idx = jnp.min(jnp.where(is_group & (lg == g_max), lane, big), axis=1, keepdims=True)
    g_den = jnp.sum(jnp.where(is_group, jnp.exp(lg - g_max), 0.0), axis=1, keepdims=True)
    g_val = 1.0 / g_den
    lo = N_GROUPS + EXPERTS_PER_GROUP * g_idx
    in_group = (lane >= lo) & (lane < lo + EXPERTS_PER_GROUP)
    v1 = jnp.max(jnp.where(in_group, lg, -jnp.inf), axis=1, keepdims=True)
    i1 = jnp.min(jnp.where(in_group & (lg == v1), lane, big), axis=1, keepdims=True)
    rest = in_group & (lane != i1)
    v2 = jnp.max(jnp.where(rest, lg, -jnp.inf), axis=1, keepdims=True)
    i2 = jnp.min(jnp.where(rest & (lg == v2), lane, big), axis=1, keepdims=True)
    e2 = jnp.exp(v2 - v1)
    w1 = g_val / (1.0 + e2)
    w2 = g_val * e2 / (1.0 + e2)
    picked = ((lane == i1) | (lane == i2)).astype(BF16)
    info = (jnp.where(lane == ROUTE_E1, (i1 - N_GROUPS).astype(F32), 0.0)
            + jnp.where(lane == ROUTE_E2, (i2 - N_GROUPS).astype(F32), 0.0)
            + jnp.where(lane == ROUTE_W1, w1, 0.0) + jnp.where(lane == ROUTE_W2, w2, 0.0))
    return info, picked


def _post_kernel(ysb_ref, yca_ref, gate_ref, x_ref, wsb_ref, wca_ref, wout_ref, wr_ref, br_ref,
                 g_ref, b_ref, x1_ref, info_ref, pick_ref, cnt_ref):
    @pl.when(pl.program_id(0) == 0)
    def _():
        cnt_ref[...] = jnp.zeros_like(cnt_ref)

    w_hi, w_lo = _split_bf16(wr_ref[...])
    w_split = jnp.concatenate([w_hi, w_lo], axis=1)
    tm = x_ref.shape[0]
    subs = [pl.ds(s * POST_SUB, POST_SUB) for s in range(tm // POST_SUB)]
    for rows in subs:
        a = _dot(ysb_ref[rows, :], wsb_ref[...])
        c = _dot(yca_ref[rows, :], wca_ref[...])
        mix = gate_ref[rows, 0:D_MODEL] * a + gate_ref[rows, D_MODEL:2 * D_MODEL] * c
        x1_ref[rows, :] = ALPHA * x_ref[rows, :] + _dot(mix.astype(BF16), wout_ref[...])
    for rows in subs:
        x1 = _layer_norm(x1_ref[rows, :], g_ref[...], b_ref[...])
        x1_ref[rows, :] = x1
        x_hi, x_lo = _split_bf16(x1)
        parts = _dot(jnp.concatenate([x_hi, x_lo], axis=0), w_split)
        lg = (parts[0:POST_SUB, 0:ROUTER_LANES] + parts[0:POST_SUB, ROUTER_LANES:]
              + parts[POST_SUB:, 0:ROUTER_LANES] + parts[POST_SUB:, ROUTER_LANES:]) + br_ref[...]
        info, pick = _route(lg)
        info_ref[rows, :] = info
        pick_ref[rows, :] = pick
        cnt_ref[...] += jnp.sum(pick.astype(F32), axis=0, keepdims=True)


POST_SUB = 256


def _post_attn(y_sb, y_ca, gates, x2d, w_br_sb, w_br_ca, w_out, w_router, b_router, ln_g, ln_b, tm=1024):
    n = x2d.shape[0]
    row = lambda i: (i, 0)
    fixed = lambda i: (0, 0)
    return pl.pallas_call(
        _post_kernel,
        grid=(n // tm,),
        in_specs=[
            pl.BlockSpec((tm, W_SB), row),
            pl.BlockSpec((tm, W_CA), row),
            pl.BlockSpec((tm, 2 * D_MODEL), row),
            pl.BlockSpec((tm, D_MODEL), row),
            pl.BlockSpec((W_SB, D_MODEL), fixed),
            pl.BlockSpec((W_CA, D_MODEL), fixed),
            pl.BlockSpec((D_MODEL, D_MODEL), fixed),
            pl.BlockSpec((D_MODEL, ROUTER_LANES), fixed),
            pl.BlockSpec((1, ROUTER_LANES), fixed),
            pl.BlockSpec((1, D_MODEL), fixed),
            pl.BlockSpec((1, D_MODEL), fixed),
        ],
        out_specs=[
            pl.BlockSpec((tm, D_MODEL), row),
            pl.BlockSpec((tm, ROUTER_LANES), row),
            pl.BlockSpec((tm, ROUTER_LANES), row),
            pl.BlockSpec((8, ROUTER_LANES), fixed),
        ],
        out_shape=[
            jax.ShapeDtypeStruct((n, D_MODEL), F32),
            jax.ShapeDtypeStruct((n, ROUTER_LANES), F32),
            jax.ShapeDtypeStruct((n, ROUTER_LANES), BF16),
            jax.ShapeDtypeStruct((8, ROUTER_LANES), F32),
        ],
        compiler_params=pltpu.CompilerParams(
            dimension_semantics=("arbitrary",), vmem_limit_bytes=VMEM_LIMIT),
        name="post_attn",
    )(y_sb, y_ca, gates, x2d, w_br_sb, w_br_ca, w_out, w_router, b_router, ln_g, ln_b)


MOE_TILE = 512
TOP_K = 2


def _moe_rows(n):
    return n * TOP_K + N_EXPERTS * MOE_TILE


def _rank_kernel(pick_ref, info_ref, tri_ref, cnt_ref, d1_ref, d2_ref, run_ref, off_ref, *, tb):
    i = pl.program_id(0)
    pick = pick_ref[...]

    @pl.when(i == 0)
    def _():
        cnt = cnt_ref[...]
        padded = jnp.ceil(cnt * (1.0 / MOE_TILE)) * MOE_TILE
        lane = lax.broadcasted_iota(jnp.int32, padded.shape, 1)
        scan = padded
        step = 1
        while step < ROUTER_LANES:
            scan = scan + jnp.where(lane >= step, pltpu.roll(scan, step, axis=1), 0.0)
            step *= 2
        off_ref[...] = scan - padded
        run_ref[...] = jnp.zeros_like(run_ref)

    seen = run_ref[0:1, :]
    earlier = _dot(tri_ref[...], pick)
    row_of = earlier + seen + off_ref[0:1, :]
    info = info_ref[...]
    lane = lax.broadcasted_iota(jnp.int32, info.shape, 1)
    lane_f = lane.astype(F32)
    e1 = jnp.sum(jnp.where(lane == ROUTE_E1, info, 0.0), axis=1, keepdims=True)
    e2 = jnp.sum(jnp.where(lane == ROUTE_E2, info, 0.0), axis=1, keepdims=True)
    d1 = jnp.sum(jnp.where(lane_f == e1 + N_GROUPS, row_of, 0.0), axis=1, keepdims=True)
    d2 = jnp.sum(jnp.where(lane_f == e2 + N_GROUPS, row_of, 0.0), axis=1, keepdims=True)
    eye = (lax.broadcasted_iota(jnp.int32, (LANES, LANES), 0) == lax.broadcasted_iota(jnp.int32, (LANES, LANES), 1))
    for blk in range(tb // LANES):
        for d, d_ref in ((d1, d1_ref), (d2, d2_ref)):
            column = d[blk * LANES:(blk + 1) * LANES]
            d_ref[blk:blk + 1, :] = jnp.sum(jnp.where(eye, column, 0.0), axis=0, keepdims=True).astype(jnp.int32)
    run_ref[...] += jnp.sum(pick.astype(F32), axis=0, keepdims=True)


def _moe_rank(pick, info, counts, tb=1024):
    n = pick.shape[0]
    tri = jnp.asarray(np.arange(tb)[None, :] < np.arange(tb)[:, None], dtype=BF16)
    dest_spec = pl.BlockSpec((tb // LANES, LANES), lambda i: (i, 0))
    dest_shape = jax.ShapeDtypeStruct((n // LANES, LANES), jnp.int32)
    return pl.pallas_call(
        functools.partial(_rank_kernel, tb=tb),
        grid=(n // tb,),
        in_specs=[
            pl.BlockSpec((tb, ROUTER_LANES), lambda i: (i, 0)),
            pl.BlockSpec((tb, ROUTER_LANES), lambda i: (i, 0)),
            pl.BlockSpec((tb, tb), lambda i: (0, 0)),
            pl.BlockSpec((8, ROUTER_LANES), lambda i: (0, 0)),
        ],
        out_specs=[dest_spec, dest_spec],
        out_shape=[dest_shape, dest_shape],
        scratch_shapes=[
            pltpu.VMEM((8, ROUTER_LANES), F32),
            pltpu.VMEM((8, ROUTER_LANES), F32),
        ],
        compiler_params=pltpu.CompilerParams(
            dimension_semantics=("arbitrary",), vmem_limit_bytes=VMEM_LIMIT),
        name="moe_rank",
    )(pick, info, tri, counts)


def _row_copy(src_ref, src_row, dst_ref, dst_row, sem):
    return pltpu.make_async_copy(src_ref.at[pl.ds(src_row, 1), :], dst_ref.at[pl.ds(dst_row, 1), :], sem)


def _dispatch_kernel(d1_ref, d2_ref, last_ref, x_ref, xs_hbm, zero_ref, stage_ref, sems, zsem, *, tb):
    @pl.when(pl.program_id(0) == 0)
    def _():
        zero_ref[...] = jnp.zeros_like(zero_ref)

        def last_tile_copy(e):
            first_row = pl.multiple_of((last_ref[e] - 1) * MOE_TILE, MOE_TILE)
            return pltpu.make_async_copy(zero_ref, xs_hbm.at[pl.ds(first_row, MOE_TILE), :], zsem)

        def unused_tile_copy(t):
            return pltpu.make_async_copy(zero_ref, xs_hbm.at[pl.ds(t * MOE_TILE, MOE_TILE), :], zsem)

        n_tiles = xs_hbm.shape[0] // MOE_TILE
        min_used = n_tiles - N_EXPERTS
        for wait in (False, True):
            for e in range(N_EXPERTS):
                owns_tiles = last_ref[e] > (last_ref[e - 1] if e else 0)
                pl.when(owns_tiles)(
                    lambda e=e, wait=wait: last_tile_copy(e).wait() if wait else last_tile_copy(e).start())
            for t in range(min_used, n_tiles):
                pl.when(t >= last_ref[N_EXPERTS - 1])(
                    lambda t=t, wait=wait: unused_tile_copy(t).wait() if wait else unused_tile_copy(t).start())

    def wait_rows(slot):
        for _ in range(TOP_K):
            pltpu.make_async_copy(stage_ref.at[slot], xs_hbm.at[pl.ds(0, tb), :], sems.at[slot]).wait()

    i = pl.program_id(0)
    for slot in range(DISPATCH_SLOTS):
        pl.when(i > 0)(lambda slot=slot: wait_rows(slot))
        stage_ref[slot] = x_ref[pl.ds(slot * tb, tb), :]
        first = (i * DISPATCH_SLOTS + slot) * (tb // LANES)
        for r in range(tb):
            hi, lo = first + r // LANES, r % LANES
            _row_copy(stage_ref.at[slot], r, xs_hbm, d1_ref[hi, lo], sems.at[slot]).start(priority=0)
            _row_copy(stage_ref.at[slot], r, xs_hbm, d2_ref[hi, lo], sems.at[slot]).start(priority=1)

    @pl.when(i == pl.num_programs(0) - 1)
    def _():
        for slot in range(DISPATCH_SLOTS):
            wait_rows(slot)


DISPATCH_SLOTS = 2


def _moe_dispatch(x1, dest1, dest2, last_tile, tb=256):
    n = x1.shape[0]
    rows = _moe_rows(n)
    grid_spec = pltpu.PrefetchScalarGridSpec(
        num_scalar_prefetch=3,
        grid=(n // (DISPATCH_SLOTS * tb),),
        in_specs=[pl.BlockSpec((DISPATCH_SLOTS * tb, D_MODEL), lambda i, d1, d2, last: (i, 0))],
        out_specs=pl.BlockSpec(memory_space=pl.ANY),
        scratch_shapes=[
            pltpu.VMEM((MOE_TILE, D_MODEL), F32),
            pltpu.VMEM((DISPATCH_SLOTS, tb, D_MODEL), F32),
            pltpu.SemaphoreType.DMA((DISPATCH_SLOTS,)),
            pltpu.SemaphoreType.DMA(()),
        ],
    )
    return pl.pallas_call(
        functools.partial(_dispatch_kernel, tb=tb),
        grid_spec=grid_spec,
        out_shape=jax.ShapeDtypeStruct((rows, D_MODEL), F32),
        compiler_params=pltpu.CompilerParams(
            dimension_semantics=("arbitrary",), vmem_limit_bytes=VMEM_LIMIT),
        name="moe_dispatch",
    )(dest1, dest2, last_tile, x1)


def _experts_kernel(te_ref, nt_ref, xs_ref, wg_ref, wu_ref, wd_ref, ys_ref, wgu_s, wd_s):
    t = pl.program_id(0)
    changed = (t == 0) | (te_ref[t] != te_ref[jnp.maximum(t - 1, 0)])

    @pl.when(changed)
    def _():
        wgu_s[:, 0:D_EXPERT] = wg_ref[...].astype(BF16)
        wgu_s[:, D_EXPERT:2 * D_EXPERT] = wu_ref[...].astype(BF16)
        wd_s[...] = wd_ref[...].astype(BF16)

    @pl.when(t < nt_ref[0])
    def _():
        gu = _dot(xs_ref[...].astype(BF16), wgu_s[...])
        gate = gu[:, 0:D_EXPERT]
        up = gu[:, D_EXPERT:2 * D_EXPERT]
        hid = (gate * (1.0 / (1.0 + jnp.exp(-gate)))) * up
        ys_ref[...] = _dot(hid.astype(BF16), wd_s[...])

    @pl.when(t >= nt_ref[0])
    def _():
        ys_ref[...] = jnp.zeros_like(ys_ref)


def _moe_experts(xs, tile_expert, n_tiles_used, w_gate, w_up, w_down, layer):
    rows = xs.shape[0]
    first = layer * N_EXPERTS
    grid_spec = pltpu.PrefetchScalarGridSpec(
        num_scalar_prefetch=2,
        grid=(rows // MOE_TILE,),
        in_specs=[
            pl.BlockSpec((MOE_TILE, D_MODEL), lambda t, te, nt: (jnp.minimum(t, nt[0] - 1), 0)),
            pl.BlockSpec((None, D_MODEL, D_EXPERT), lambda t, te, nt: (first + te[t], 0, 0)),
            pl.BlockSpec((None, D_MODEL, D_EXPERT), lambda t, te, nt: (first + te[t], 0, 0)),
            pl.BlockSpec((None, D_EXPERT, D_MODEL), lambda t, te, nt: (first + te[t], 0, 0)),
        ],
        out_specs=pl.BlockSpec((MOE_TILE, D_MODEL), lambda t, te, nt: (t, 0)),
        scratch_shapes=[
            pltpu.VMEM((D_MODEL, 2 * D_EXPERT), BF16),
            pltpu.VMEM((D_EXPERT, D_MODEL), BF16),
        ],
    )
    return pl.pallas_call(
        _experts_kernel,
        grid_spec=grid_spec,
        out_shape=jax.ShapeDtypeStruct((rows, D_MODEL), F32),
        compiler_params=pltpu.CompilerParams(
            dimension_semantics=("arbitrary",), vmem_limit_bytes=VMEM_LIMIT),
        name="moe_experts",
    )(tile_expert, n_tiles_used, xs, w_gate, w_up, w_down)


def _combine_kernel(d1_ref, d2_ref, x1_ref, info_ref, ys_hbm, g_ref, b_ref, out_ref, y_ref, sems, *, tb):
    i = pl.program_id(0)
    last_tile = COMBINE_SLOTS * pl.num_programs(0) - 1

    def start_gather(tile, slot):
        first = tile * (tb // LANES)
        for r in range(tb):
            hi, lo = first + r // LANES, r % LANES
            _row_copy(ys_hbm, d1_ref[hi, lo], y_ref.at[slot, 0], r, sems.at[slot]).start(priority=0)
            _row_copy(ys_hbm, d2_ref[hi, lo], y_ref.at[slot, 1], r, sems.at[slot]).start(priority=1)

    def wait_gather(slot):
        for k in range(TOP_K):
            pltpu.make_async_copy(ys_hbm.at[pl.ds(0, tb), :], y_ref.at[slot, k], sems.at[slot]).wait()

    def finish(slot):
        rows = pl.ds(slot * tb, tb)
        info = info_ref[rows, :]
        lane = lax.broadcasted_iota(jnp.int32, info.shape, 1)
        w1 = jnp.sum(jnp.where(lane == ROUTE_W1, info, 0.0), axis=1, keepdims=True)
        w2 = jnp.sum(jnp.where(lane == ROUTE_W2, info, 0.0), axis=1, keepdims=True)
        ffn = w1 * y_ref[slot, 0] + w2 * y_ref[slot, 1]
        out_ref[rows, :] = _layer_norm(ALPHA * x1_ref[rows, :] + ffn, g_ref[...], b_ref[...])

    @pl.when(i == 0)
    def _():
        for s in range(COMBINE_AHEAD):
            start_gather(s, s)

    for s in range(COMBINE_SLOTS):
        wait_gather(s)
        ahead = jnp.minimum(COMBINE_SLOTS * i + s + COMBINE_AHEAD, last_tile)
        start_gather(ahead, (s + COMBINE_AHEAD) % COMBINE_SLOTS)
        finish(s)

    @pl.when(i == pl.num_programs(0) - 1)
    def _():
        for s in range(COMBINE_AHEAD):
            wait_gather(s)


COMBINE_SLOTS = 4
COMBINE_AHEAD = 2


def _moe_combine(x1, info, ys, dest1, dest2, ln_g, ln_b, tb=256):
    n = x1.shape[0]
    step_rows = COMBINE_SLOTS * tb
    grid_spec = pltpu.PrefetchScalarGridSpec(
        num_scalar_prefetch=2,
        grid=(n // step_rows,),
        in_specs=[
            pl.BlockSpec((step_rows, D_MODEL), lambda i, d1, d2: (i, 0)),
            pl.BlockSpec((step_rows, ROUTER_LANES), lambda i, d1, d2: (i, 0)),
            pl.BlockSpec(memory_space=pl.ANY),
            pl.BlockSpec((1, D_MODEL), lambda i, d1, d2: (0, 0)),
            pl.BlockSpec((1, D_MODEL), lambda i, d1, d2: (0, 0)),
        ],
        out_specs=pl.BlockSpec((step_rows, D_MODEL), lambda i, d1, d2: (i, 0)),
        scratch_shapes=[
            pltpu.VMEM((COMBINE_SLOTS, TOP_K, tb, D_MODEL), F32),
            pltpu.SemaphoreType.DMA((COMBINE_SLOTS,)),
        ],
    )
    return pl.pallas_call(
        functools.partial(_combine_kernel, tb=tb),
        grid_spec=grid_spec,
        out_shape=jax.ShapeDtypeStruct((n, D_MODEL), F32),
        compiler_params=pltpu.CompilerParams(
            dimension_semantics=("arbitrary",), vmem_limit_bytes=VMEM_LIMIT),
        name="moe_combine",
    )(dest1, dest2, x1, info, ys, ln_g, ln_b)


def _moe(x1, info, pick, counts, w_gate, w_up, w_down, layer, ln_g, ln_b):
    n = x1.shape[0]
    dest1, dest2 = _moe_rank(pick, info, counts)
    tiles = jnp.ceil(counts[0, N_GROUPS:N_GROUPS + N_EXPERTS] * (1.0 / MOE_TILE)).astype(jnp.int32)
    upto = np.arange(N_EXPERTS)[None, :] <= np.arange(N_EXPERTS)[:, None]
    last_tile = jnp.sum(jnp.where(upto, tiles[None, :], 0), axis=1)
    tile_ids = jnp.arange(_moe_rows(n) // MOE_TILE, dtype=jnp.int32)
    tile_expert = jnp.minimum(jnp.sum(tile_ids[:, None] >= last_tile[None, :], axis=1), N_EXPERTS - 1)
    xs = _moe_dispatch(x1, dest1, dest2, last_tile)
    ys = _moe_experts(xs, tile_expert.astype(jnp.int32), last_tile[N_EXPERTS - 1:], w_gate, w_up, w_down, layer)
    return _moe_combine(x1, info, ys, dest1, dest2, ln_g, ln_b)


def _layer(x2d, batch, w_in, b_gate, rel_bias, w_br_sb, w_br_ca, w_out, ln1_g, ln1_b,
           w_group, b_group, w_erouter, b_erouter, w_gate, w_up, w_down, layer, ln2_g, ln2_b):
    n = x2d.shape[0]
    seq = n // batch
    scale = HEAD_DIM ** -0.5
    qscale = np.ones((1, D_QKV), np.float32)
    qscale[:, 0:W_SB] = scale * np.log2(np.e)
    qscale[:, 3 * W_SB:3 * W_SB + W_CA] = scale * np.log2(np.e)
    qkv, gates = _in_proj(x2d, w_in.astype(BF16), jnp.asarray(qscale), b_gate.reshape(1, 2 * D_MODEL))
    y_sb = _sb_attn(qkv, batch)
    y_ca = _ca_attn(qkv, batch, _ca_bias_line(rel_bias))

    w_router = jnp.concatenate(
        [w_group, w_erouter.transpose(1, 0, 2).reshape(D_MODEL, N_EXPERTS)], axis=1)
    w_router = jnp.pad(w_router, ((0, 0), (0, ROUTER_LANES - N_GROUPS - N_EXPERTS)))
    b_router = jnp.pad(jnp.concatenate([b_group, b_erouter.reshape(N_EXPERTS)]),
                       (0, ROUTER_LANES - N_GROUPS - N_EXPERTS)).reshape(1, ROUTER_LANES)
    x1, info, pick, counts = _post_attn(y_sb, y_ca, gates, x2d, w_br_sb.astype(BF16), w_br_ca.astype(BF16),
                        w_out.astype(BF16), w_router, b_router,
                        ln1_g.reshape(1, D_MODEL), ln1_b.reshape(1, D_MODEL))

    return _moe(x1, info, pick, counts, w_gate, w_up, w_down, layer,
                ln2_g.reshape(1, D_MODEL), ln2_b.reshape(1, D_MODEL))


def kernel(x, w_in, b_gate, rel_bias, w_br_sb, w_br_ca, w_out, ln1_g, ln1_b, w_group, b_group,
           w_erouter, b_erouter, w_gate, w_up, w_down, ln2_g, ln2_b):
    batch, seq, d = x.shape
    h = x.reshape(batch * seq, d)
    w_gate = w_gate.reshape(DEPTH * N_EXPERTS, D_MODEL, D_EXPERT)
    w_up = w_up.reshape(DEPTH * N_EXPERTS, D_MODEL, D_EXPERT)
    w_down = w_down.reshape(DEPTH * N_EXPERTS, D_EXPERT, D_MODEL)
    for l in range(DEPTH):
        h = _layer(h, batch, w_in[l], b_gate[l], rel_bias[l], w_br_sb[l], w_br_ca[l], w_out[l],
                   ln1_g[l], ln1_b[l], w_group[l], b_group[l], w_erouter[l], b_erouter[l],
                   w_gate, w_up, w_down, l, ln2_g[l], ln2_b[l])
    return h.reshape(batch, seq, d)
```
